```python
import math
import jax, jax.numpy as jnp
from jax import lax
import numpy as np

D_MODEL = 2048
BATCH = 1
SEQ = 16384
DEPTH = 4
DEC_BATCH = 2
DEC_SEQ = 4096
PAST_LEN = 128

A_HEADS = 16
A_KV_HEADS = 2
A_HEAD_DIM = 64
A_WIDTH = A_HEADS * A_HEAD_DIM
A_KV_WIDTH = A_KV_HEADS * A_HEAD_DIM
A_RADIUS = 128
B_CH = D_MODEL - A_WIDTH
B_ORDER = 2
B_SHORT = 3
B_EMB = 33
B_BANDS = (B_EMB - 1) // 2
B_FILTER_HIDDEN = 64
B_DECAY_TARGET = 1e-2
B_FAST_DECAY_PCT = 0.3
B_SLOW_DECAY_PCT = 1.5
B_N_DIR = 2
C_HEADS = 16
C_HEAD_DIM = D_MODEL // C_HEADS
C_PATTERNS = ((128, 1), (512, 4), (2048, 16))
C_GROUPS = len(C_PATTERNS)
E_PROJ = A_WIDTH + 2 * A_KV_WIDTH + 3 * B_CH
O_PROJ = 3 * C_GROUPS * C_HEADS * C_HEAD_DIM
D_FF = -(-8 * D_MODEL // 768) * 256
ROPE_THETA = 500000.0
ROPE_FRACTION = 4
ALPHA = (2 * DEPTH) ** 0.25
BETA = (8 * DEPTH) ** -0.25
LN_EPS = 1e-5
BLOCK = 128
N_EVEN = (DEPTH + 1) // 2
N_ODD = DEPTH // 2

kernel_name = "hybrid_bidir_swa_hyena_dilated_encoder"


def layer_norm(x, g, b):
    xf = x.astype(jnp.float32)
    mu = xf.mean(-1, keepdims=True)
    var = jnp.square(xf - mu).mean(-1, keepdims=True)
    y = (xf - mu) * lax.rsqrt(var + LN_EPS) * g.astype(jnp.float32) + b.astype(jnp.float32)
    return y.astype(x.dtype)


def partial_rope(x):
    L, hd = x.shape[1], x.shape[-1]
    rot = hd // ROPE_FRACTION
    half = rot // 2
    inv = ROPE_THETA ** (-(jnp.arange(half, dtype=jnp.float32) * 2.0 / rot))
    ang = jnp.arange(L, dtype=jnp.float32)[:, None] * inv[None, :]
    cos = jnp.cos(ang)[None, :, None, :]
    sin = jnp.sin(ang)[None, :, None, :]
    xf = x.astype(jnp.float32)
    x1, x2, xp = xf[..., :half], xf[..., half:rot], xf[..., rot:]
    out = jnp.concatenate([x1 * cos - x2 * sin, x2 * cos + x1 * sin, xp], axis=-1)
    return out.astype(x.dtype)


def banded_attention(q, k, v, radius, sink=None):
    n, L, hq, hd = q.shape
    hkv = k.shape[2]
    g = hq // hkv
    qb = math.gcd(L, BLOCK)
    nb = L // qb
    span = qb + 2 * radius
    pad = ((0, 0), (radius, radius), (0, 0), (0, 0))
    kp = jnp.pad(k, pad)
    vp = jnp.pad(v, pad)
    idx = jnp.arange(nb)[:, None] * qb + jnp.arange(span)[None, :]
    kb = kp[:, idx]
    vb = vp[:, idx]
    qr = q.reshape(n, nb, qb, hkv, g, hd)
    s = jnp.einsum('nbqhgd,nbkhd->nbhgqk', qr, kb, preferred_element_type=jnp.float32) * (hd ** -0.5)
    qpos = jnp.arange(nb)[:, None] * qb + jnp.arange(qb)[None, :]
    kpos = idx - radius
    rel = kpos[:, None, :] - qpos[:, :, None]
    valid = (jnp.abs(rel) <= radius) & (kpos[:, None, :] >= 0) & (kpos[:, None, :] < L)
    s = jnp.where(valid[None, :, None, None], s, -jnp.inf)
    m = s.max(-1)
    if sink is not None:
        sk = sink.astype(jnp.float32).reshape(hkv, g)[None, None, :, :, None]
        m = jnp.maximum(m, sk)
    p = jnp.exp(s - m[..., None])
    den = p.sum(-1)
    if sink is not None:
        den = den + jnp.exp(sk - m)
    o = jnp.einsum('nbhgqk,nbkhd->nbqhgd', p.astype(v.dtype), vb, preferred_element_type=jnp.float32)
    den_t = jnp.transpose(den, (0, 1, 4, 2, 3))
    o = o / den_t[..., None]
    lse = jnp.transpose(m + jnp.log(den), (0, 1, 4, 2, 3)).reshape(n, L, hq)
    return o.reshape(n, L, hq, hd).astype(q.dtype), lse


def dilated_attention(q, k, v, window, dilation):
    b, L, h, hd = q.shape
    radius = window // (2 * dilation)
    ls = L // dilation

    def split(t):
        return t.reshape(b, ls, dilation, h, hd).transpose(0, 2, 1, 3, 4).reshape(b * dilation, ls, h, hd)

    o, lse = banded_attention(split(q), split(k), split(v), radius)
    o = o.reshape(b, dilation, ls, h, hd).transpose(0, 2, 1, 3, 4).reshape(b, L, h, hd)
    lse = lse.reshape(b, dilation, ls, h).transpose(0, 2, 1, 3).reshape(b, L, h)
    return o, lse


def hyena_filters(L, w1, b1, f1, w2, b2, f2, w3, b3):
    f32 = jnp.float32
    t = jnp.linspace(0.0, 1.0, L, dtype=f32)[:, None]
    bands = jnp.linspace(1e-4, B_BANDS - 1, B_BANDS, dtype=f32)[None, :]
    w = 2.0 * math.pi * jnp.arange(L, dtype=f32)[:, None] / L
    z = jnp.concatenate([t, jnp.cos(bands * w), -jnp.sin(bands * w)], axis=-1)
    h = jnp.sin(f1.astype(f32) * (z @ w1.astype(f32) + b1.astype(f32)))
    h = jnp.sin(f2.astype(f32) * (h @ w2.astype(f32) + b2.astype(f32)))
    h = (h @ w3.astype(f32) + b3.astype(f32)).reshape(L, B_N_DIR, B_ORDER, B_CH)
    max_decay = math.log(B_DECAY_TARGET) / B_FAST_DECAY_PCT
    min_decay = math.log(B_DECAY_TARGET) / B_SLOW_DECAY_PCT
    deltas = jnp.abs(jnp.linspace(min_decay, max_decay, B_CH, dtype=f32))
    decay = jnp.exp(-t * deltas[None, :])
    h = h * decay[:, None, None, :]
    fwd, bwd = h[:, 0], h[:, 1]
    kern = jnp.concatenate([fwd, jnp.zeros_like(fwd[:1]), bwd[:0:-1]], axis=0)
    kern = kern / jnp.sum(jnp.abs(kern), axis=0, keepdims=True)
    return jnp.fft.rfft(kern, axis=0)


def fft_long_conv(u, kf, bias):
    L = u.shape[1]
    uf32 = u.astype(jnp.float32)
    uf = jnp.fft.rfft(uf32, n=2 * L, axis=1)
    y = jnp.fft.irfft(uf * kf[None], n=2 * L, axis=1)[:, :L]
    return (y + uf32 * bias.astype(jnp.float32)).astype(u.dtype)


def short_conv(x, w, b):
    L = x.shape[1]
    r = B_SHORT // 2
    xp = jnp.pad(x, ((0, 0), (r, r), (0, 0)))
    y = b
    for j in range(B_SHORT):
        y = y + xp[:, j:j + L] * w[j]
    return y


def hyena_mixer(u, conv_w, conv_b, w1, b1, f1, w2, b2, f2, w3, b3, hy_bias):
    L = u.shape[1]
    u = short_conv(u, conv_w, conv_b)
    v, x1, x2 = jnp.split(u, 3, axis=-1)
    kf = hyena_filters(L, w1, b1, f1, w2, b2, f2, w3, b3)
    z = x1 * fft_long_conv(v, kf[:, 0], hy_bias[0])
    z = x2 * fft_long_conv(z, kf[:, 1], hy_bias[1])
    return z


def even_mixer(x, w_in, sink, conv_w, conv_b, w1, b1, f1, w2, b2, f2, w3, b3, hy_bias, w_out):
    b, L, _ = x.shape
    proj = x @ w_in
    q, k, v, hy = jnp.split(proj, [A_WIDTH, A_WIDTH + A_KV_WIDTH, A_WIDTH + 2 * A_KV_WIDTH], axis=-1)
    q = partial_rope(q.reshape(b, L, A_HEADS, A_HEAD_DIM))
    k = partial_rope(k.reshape(b, L, A_KV_HEADS, A_HEAD_DIM))
    v = v.reshape(b, L, A_KV_HEADS, A_HEAD_DIM)
    a_out, _ = banded_attention(q, k, v, A_RADIUS, sink)
    h_out = hyena_mixer(hy, conv_w, conv_b, w1, b1, f1, w2, b2, f2, w3, b3, hy_bias)
    return jnp.concatenate([a_out.reshape(b, L, A_WIDTH), h_out.astype(x.dtype)], axis=-1) @ w_out


def odd_mixer(x, w_in, w_out):
    b, L, _ = x.shape
    proj = (x @ w_in).reshape(b, L, 3, C_GROUPS, C_HEADS, C_HEAD_DIM)
    outs, lses = [], []
    for gi, (win, dil) in enumerate(C_PATTERNS):
        q = partial_rope(proj[:, :, 0, gi])
        k = partial_rope(proj[:, :, 1, gi])
        v = proj[:, :, 2, gi]
        o, lse = dilated_attention(q, k, v, win, dil)
        outs.append(o)
        lses.append(lse)
    wts = jax.nn.softmax(jnp.stack(lses, 0), axis=0)
    o = jnp.einsum('gblh,gblhd->blhd', wts, jnp.stack(outs, 0).astype(jnp.float32))
    return o.reshape(b, L, C_HEADS * C_HEAD_DIM).astype(x.dtype) @ w_out


def swiglu(x, wg, wu, wd):
    return (jax.nn.silu(x @ wg) * (x @ wu)) @ wd


def trunk(x, p):
    for i in range(DEPTH):
        if i % 2 == 0:
            j = i // 2
            mix = even_mixer(x, p['mix_e_w_in'][j], p['a_sink'][j], p['hy_conv_w'][j], p['hy_conv_b'][j],
                             p['hy_w1'][j], p['hy_b1'][j], p['hy_f1'][j], p['hy_w2'][j], p['hy_b2'][j],
                             p['hy_f2'][j], p['hy_w3'][j], p['hy_b3'][j], p['hy_bias'][j], p['mix_e_w_out'][j])
        else:
            j = i // 2
            mix = odd_mixer(x, p['mix_o_w_in'][j], p['mix_o_w_out'][j])
        x = layer_norm(ALPHA * x + mix, p['ln1_g'][i], p['ln1_b'][i])
        f = swiglu(x, p['ffn_w_gate'][i], p['ffn_w_up'][i], p['ffn_w_down'][i])
        x = layer_norm(ALPHA * x + f, p['ln2_g'][i], p['ln2_b'][i])
    return x


def setup_inputs(seed: int = 0) -> dict:
    key = jax.random.key(seed)
    ks = iter(jax.random.split(key, 32))
    f32 = jnp.float32

    def nrm(shape, scale):
        return jax.random.normal(next(ks), shape, f32) * scale

    D, H = D_MODEL, B_FILTER_HIDDEN
    return {
        'x_prompt': nrm((BATCH, SEQ, D), 1.0),
        'x_sample': nrm((DEC_BATCH, DEC_SEQ, D), 1.0),
        'mix_e_w_in': nrm((N_EVEN, D, E_PROJ), D ** -0.5),
        'a_sink': nrm((N_EVEN, A_HEADS), 0.5),
        'hy_conv_w': nrm((N_EVEN, B_SHORT, 3 * B_CH), B_SHORT ** -0.5),
        'hy_conv_b': nrm((N_EVEN, 3 * B_CH), 0.02),
        'hy_w1': nrm((N_EVEN, B_EMB, H), B_EMB ** -0.5),
        'hy_b1': nrm((N_EVEN, H), 0.02),
        'hy_f1': 1.0 + nrm((N_EVEN, H), 0.02),
        'hy_w2': nrm((N_EVEN, H, H), H ** -0.5),
        'hy_b2': nrm((N_EVEN, H), 0.02),
        'hy_f2': 1.0 + nrm((N_EVEN, H), 0.02),
        'hy_w3': nrm((N_EVEN, H, B_N_DIR * B_ORDER * B_CH), H ** -0.5),
        'hy_b3': nrm((N_EVEN, B_N_DIR * B_ORDER * B_CH), 0.02),
        'hy_bias': nrm((N_EVEN, B_ORDER, B_CH), 0.5),
        'mix_e_w_out': nrm((N_EVEN, D, D), D ** -0.5 * BETA),
        'mix_o_w_in': nrm((N_ODD, D, O_PROJ), D ** -0.5),
        'mix_o_w_out': nrm((N_ODD, C_HEADS * C_HEAD_DIM, D), (C_HEADS * C_HEAD_DIM) ** -0.5 * BETA),
        'ffn_w_gate': nrm((DEPTH, D, D_FF), D ** -0.5),
        'ffn_w_up': nrm((DEPTH, D, D_FF), D ** -0.5),
        'ffn_w_down': nrm((DEPTH, D_FF, D), D_FF ** -0.5 * BETA),
        'ln1_g': 1.0 + nrm((DEPTH, D), 0.02),
        'ln1_b': nrm((DEPTH, D), 0.02),
        'ln2_g': 1.0 + nrm((DEPTH, D), 0.02),
        'ln2_b': nrm((DEPTH, D), 0.02),
    }


def reference(x_prompt, x_sample, mix_e_w_in, a_sink, hy_conv_w, hy_conv_b, hy_w1, hy_b1, hy_f1,
              hy_w2, hy_b2, hy_f2, hy_w3, hy_b3, hy_bias, mix_e_w_out, mix_o_w_in, mix_o_w_out,
              ffn_w_gate, ffn_w_up, ffn_w_down, ln1_g, ln1_b, ln2_g, ln2_b):
    params = dict(mix_e_w_in=mix_e_w_in, a_sink=a_sink, hy_conv_w=hy_conv_w, hy_conv_b=hy_conv_b,
                  hy_w1=hy_w1, hy_b1=hy_b1, hy_f1=hy_f1, hy_w2=hy_w2, hy_b2=hy_b2, hy_f2=hy_f2,
                  hy_w3=hy_w3, hy_b3=hy_b3, hy_bias=hy_bias, mix_e_w_out=mix_e_w_out,
                  mix_o_w_in=mix_o_w_in, mix_o_w_out=mix_o_w_out, ffn_w_gate=ffn_w_gate,
                  ffn_w_up=ffn_w_up, ffn_w_down=ffn_w_down, ln1_g=ln1_g, ln1_b=ln1_b,
                  ln2_g=ln2_g, ln2_b=ln2_b)
    y_prompt = trunk(x_prompt, params)
    y_sample = trunk(x_sample, params)
    return (y_prompt, y_sample)
```

```python
import functools
import math

import numpy as np
import jax
import jax.numpy as jnp
from jax import lax
from jax.experimental import pallas as pl
from jax.experimental.pallas import tpu as pltpu

F32 = jnp.float32
BF16 = jnp.bfloat16

DEPTH = 4
A_HEADS, A_KV_HEADS, A_HEAD_DIM, A_RADIUS = 16, 2, 64, 128
A_WIDTH = A_HEADS * A_HEAD_DIM
A_KV_WIDTH = A_KV_HEADS * A_HEAD_DIM
B_SHORT, B_EMB = 3, 33
B_BANDS = (B_EMB - 1) // 2
B_DECAY_TARGET, B_FAST_DECAY_PCT, B_SLOW_DECAY_PCT = 1e-2, 0.3, 1.5
C_HEADS, C_HEAD_DIM = 16, 128
C_DILATIONS = (1, 4, 16)
C_RADIUS = 64
ROPE_THETA, ROPE_FRACTION = 500000.0, 4
ALPHA = (2 * DEPTH) ** 0.25
LN_EPS = 1e-5

LANES = 128
VMEM_LIMIT = 56 * 1024 * 1024
FFT_N2 = 256

ODD_CHUNK = 2048
ATT_TQ = 256


def _cparams(sem):
    return pltpu.CompilerParams(dimension_semantics=sem, vmem_limit_bytes=VMEM_LIMIT)


def _seq_bounds(row, bounds):
    start = jnp.int32(bounds[0])
    end = jnp.int32(bounds[1])
    for b0, b1 in zip(bounds[1:-1], bounds[2:]):
        inside = row >= b0
        start = jnp.where(inside, jnp.int32(b0), start)
        end = jnp.where(inside, jnp.int32(b1), end)
    return start, end


def _rope(a, c, s1, s2, half):
    w = a.shape[-1]
    return a * c + pltpu.roll(a, w - half, 1) * s1 + pltpu.roll(a, half, 1) * s2


def _mm_kernel(x_ref, w_ref, o_ref):
    o_ref[...] = jnp.dot(x_ref[...], w_ref[...], preferred_element_type=F32).astype(o_ref.dtype)


def _matmul(x, w, tn, out_dtype, tm=1024):
    m, k = x.shape
    n = w.shape[1]
    return pl.pallas_call(
        _mm_kernel,
        grid=(m // tm, n // tn),
        in_specs=[pl.BlockSpec((tm, k), lambda i, j: (i, 0)),
                  pl.BlockSpec((k, tn), lambda i, j: (0, j))],
        out_specs=pl.BlockSpec((tm, tn), lambda i, j: (i, j)),
        out_shape=jax.ShapeDtypeStruct((m, n), out_dtype),
        compiler_params=_cparams(("parallel", "arbitrary")),
        name="matmul",
    )(x, w)


def _mm_rope_kernel(x_ref, w_ref, c_ref, s1_ref, s2_ref, o_ref, *, half):
    acc = jnp.dot(x_ref[...], w_ref[...], preferred_element_type=F32)
    tw = c_ref.shape[1]
    for c in range(acc.shape[1] // tw):
        a = acc[:, c * tw:(c + 1) * tw]
        o_ref[:, c * tw:(c + 1) * tw] = _rope(a, c_ref[...], s1_ref[...], s2_ref[...], half).astype(o_ref.dtype)


def _matmul_rope(x, w, tabs, half, tn, tm=1024):
    m, k = x.shape
    n = w.shape[1]
    tw = tabs[0].shape[1]
    tab_spec = pl.BlockSpec((tm, tw), lambda i, j: (i, 0))
    return pl.pallas_call(
        functools.partial(_mm_rope_kernel, half=half),
        grid=(m // tm, n // tn),
        in_specs=[pl.BlockSpec((tm, k), lambda i, j: (i, 0)),
                  pl.BlockSpec((k, tn), lambda i, j: (0, j)),
                  tab_spec, tab_spec, tab_spec],
        out_specs=pl.BlockSpec((tm, tn), lambda i, j: (i, j)),
        out_shape=jax.ShapeDtypeStruct((m, n), BF16),
        compiler_params=_cparams(("parallel", "arbitrary")),
        name="matmul_rope",
    )(x, w, *tabs)


def _odd_proj_kernel(x_ref, w_ref, c_ref, s1_ref, s2_ref, o_ref, acc_ref, *, d, rope, half):
    acc = jnp.dot(x_ref[...], w_ref[...], preferred_element_type=F32)
    hps = acc.shape[1] // LANES
    tm = acc.shape[0]
    t = tm // d
    for hh in range(hps):
        a = acc[:, hh * LANES:(hh + 1) * LANES]
        if rope:
            a = _rope(a, c_ref[...], s1_ref[...], s2_ref[...], half)
        if d == 1:
            o_ref[hh, 0, :, :] = a.astype(BF16)
        else:
            acc_ref[hh, :, :] = a
    if d > 1:
        for hh in range(hps):
            for r in range(d):
                o_ref[hh, r, :, :] = acc_ref[hh, pl.ds(r, t, stride=d), :].astype(BF16)


def _odd_proj(x, w, tabs, half, d, rope, hps=4):
    m, k = x.shape
    tm = ODD_CHUNK
    tab_spec = pl.BlockSpec((tm, LANES), lambda i, j: (i, 0))
    return pl.pallas_call(
        functools.partial(_odd_proj_kernel, d=d, rope=rope, half=half),
        grid=(m // tm, C_HEADS // hps),
        in_specs=[pl.BlockSpec((tm, k), lambda i, j: (i, 0)),
                  pl.BlockSpec((k, hps * LANES), lambda i, j: (0, j)),
                  tab_spec, tab_spec, tab_spec],
        out_specs=pl.BlockSpec((hps, d, tm // d, LANES), lambda i, j: (j, 0, i, 0)),
        out_shape=jax.ShapeDtypeStruct((C_HEADS, d, m // d, LANES), BF16),
        scratch_shapes=[pltpu.VMEM((hps, tm, LANES), F32)],
        compiler_params=_cparams(("parallel", "arbitrary")),
        name="odd_proj",
    )(x, w, *tabs)


LN_ROWS = 128


def _layer_norm_store(x_ref, acc_ref, g_ref, b_ref, of_ref, ob_ref):
    for c in range(x_ref.shape[0] // LN_ROWS):
        rows = pl.ds(c * LN_ROWS, LN_ROWS)
        r = ALPHA * x_ref[rows, :] + acc_ref[rows, :]
        mu = jnp.mean(r, axis=-1, keepdims=True)
        xc = r - mu
        var = jnp.mean(xc * xc, axis=-1, keepdims=True)
        y = xc * lax.rsqrt(var + LN_EPS) * g_ref[...] + b_ref[...]
        of_ref[rows, :] = y
        ob_ref[rows, :] = y.astype(BF16)


def _mm_ln_kernel(*refs, n_in):
    ys = refs[:n_in]
    ws = refs[n_in:2 * n_in]
    x_ref, g_ref, b_ref, of_ref, ob_ref, acc_ref = refs[2 * n_in:]
    acc = jnp.dot(ys[0][...], ws[0][...], preferred_element_type=F32)
    for y_ref, w_ref in zip(ys[1:], ws[1:]):
        acc = acc + jnp.dot(y_ref[...], w_ref[...], preferred_element_type=F32)
    acc_ref[...] = acc
    _layer_norm_store(x_ref, acc_ref, g_ref, b_ref, of_ref, ob_ref)


def _matmul_ln(ys, ws, x, g, b, tm=256):
    m, dm = x.shape
    n_in = len(ys)
    in_specs = [pl.BlockSpec((tm, y.shape[1]), lambda i: (i, 0)) for y in ys]
    in_specs += [pl.BlockSpec(w.shape, lambda i: (0, 0)) for w in ws]
    in_specs += [pl.BlockSpec((tm, dm), lambda i: (i, 0)),
                 pl.BlockSpec((1, dm), lambda i: (0, 0)),
                 pl.BlockSpec((1, dm), lambda i: (0, 0))]
    return pl.pallas_call(
        functools.partial(_mm_ln_kernel, n_in=n_in),
        grid=(m // tm,),
        in_specs=in_specs,
        out_specs=[pl.BlockSpec((tm, dm), lambda i: (i, 0)), pl.BlockSpec((tm, dm), lambda i: (i, 0))],
        out_shape=[jax.ShapeDtypeStruct((m, dm), F32), jax.ShapeDtypeStruct((m, dm), BF16)],
        scratch_shapes=[pltpu.VMEM((tm, dm), F32)],
        compiler_params=_cparams(("parallel",)),
        name="matmul_ln",
    )(*ys, *ws, x, g.reshape(1, dm), b.reshape(1, dm))


def _ffn_kernel(xb_ref, xf_ref, wg_ref, wu_ref, wd_ref, g_ref, b_ref, of_ref, ob_ref, acc_ref):
    j = pl.program_id(1)

    @pl.when(j == 0)
    def _():
        acc_ref[...] = jnp.zeros_like(acc_ref)

    xb = xb_ref[...]
    gate = jnp.dot(xb, wg_ref[...], preferred_element_type=F32)
    up = jnp.dot(xb, wu_ref[...], preferred_element_type=F32)
    h = (gate * jax.nn.sigmoid(gate)) * up
    acc_ref[...] += jnp.dot(h.astype(BF16), wd_ref[...], preferred_element_type=F32)

    @pl.when(j == pl.num_programs(1) - 1)
    def _():
        _layer_norm_store(xf_ref, acc_ref, g_ref, b_ref, of_ref, ob_ref)


def _ffn_ln(xb, xf, wg, wu, wd, g, b, tm=512, tf=512):
    m, dm = xf.shape
    dff = wg.shape[1]
    row = lambda i, j: (i, 0)
    return pl.pallas_call(
        _ffn_kernel,
        grid=(m // tm, dff // tf),
        in_specs=[pl.BlockSpec((tm, dm), row), pl.BlockSpec((tm, dm), row),
                  pl.BlockSpec((dm, tf), lambda i, j: (0, j)),
                  pl.BlockSpec((dm, tf), lambda i, j: (0, j)),
                  pl.BlockSpec((tf, dm), lambda i, j: (j, 0)),
                  pl.BlockSpec((1, dm), lambda i, j: (0, 0)),
                  pl.BlockSpec((1, dm), lambda i, j: (0, 0))],
        out_specs=[pl.BlockSpec((tm, dm), row), pl.BlockSpec((tm, dm), row)],
        out_shape=[jax.ShapeDtypeStruct((m, dm), F32), jax.ShapeDtypeStruct((m, dm), BF16)],
        scratch_shapes=[pltpu.VMEM((tm, dm), F32)],
        compiler_params=_cparams(("parallel", "arbitrary")),
        name="ffn_ln",
    )(xb, xf, wg, wu, wd, g.reshape(1, dm), b.reshape(1, dm))


def _even_attn_kernel(sink_ref, q_ref, kp_ref, km_ref, kn_ref, o_ref, *, bounds):
    i = pl.program_id(0)
    tq = q_ref.shape[0]
    nk = tq + 2 * A_RADIUS
    row0 = i * tq
    start, end = _seq_bounds(row0, bounds)
    kv = jnp.concatenate([kp_ref[...], km_ref[...], kn_ref[...]], axis=0)
    rq = row0 + lax.broadcasted_iota(jnp.int32, (tq, nk), 0)
    rk = row0 - A_RADIUS + lax.broadcasted_iota(jnp.int32, (tq, nk), 1)
    mask = (jnp.abs(rk - rq) <= A_RADIUS) & (rk >= start) & (rk < end)
    group = A_HEADS // A_KV_HEADS
    scale = A_HEAD_DIM ** -0.5
    for j in range(A_KV_HEADS):
        k = kv[:, j * A_HEAD_DIM:(j + 1) * A_HEAD_DIM]
        v = kv[:, A_KV_WIDTH + j * A_HEAD_DIM:A_KV_WIDTH + (j + 1) * A_HEAD_DIM]
        for gq in range(group):
            h = j * group + gq
            qh = q_ref[:, h * A_HEAD_DIM:(h + 1) * A_HEAD_DIM]
            s = lax.dot_general(qh, k, (((1,), (1,)), ((), ())), preferred_element_type=F32) * scale
            s = jnp.where(mask, s, -jnp.inf)
            sk = sink_ref[h]
            m = jnp.maximum(jnp.max(s, axis=-1, keepdims=True), sk)
            p = jnp.exp(s - m)
            den = jnp.sum(p, axis=-1, keepdims=True) + jnp.exp(sk - m)
            o = jnp.dot(p.astype(BF16), v, preferred_element_type=F32) / den
            o_ref[:, h * A_HEAD_DIM:(h + 1) * A_HEAD_DIM] = o.astype(o_ref.dtype)


def _even_attention(q, kv, sink, bounds):
    m = q.shape[0]
    tq = ATT_TQ
    hb = A_RADIUS
    per = tq // hb
    last = m // hb - 1
    kvw = kv.shape[1]
    return pl.pallas_call(
        functools.partial(_even_attn_kernel, bounds=bounds),
        grid=(m // tq,),
        in_specs=[pl.BlockSpec(memory_space=pltpu.SMEM),
                  pl.BlockSpec((tq, A_WIDTH), lambda i: (i, 0)),
                  pl.BlockSpec((hb, kvw), lambda i: (jnp.maximum(i * per - 1, 0), 0)),
                  pl.BlockSpec((tq, kvw), lambda i: (i, 0)),
                  pl.BlockSpec((hb, kvw), lambda i: (jnp.minimum((i + 1) * per, last), 0))],
        out_specs=pl.BlockSpec((tq, A_WIDTH), lambda i: (i, 0)),
        out_shape=jax.ShapeDtypeStruct((m, A_WIDTH), BF16),
        compiler_params=_cparams(("parallel",)),
        name="even_attention",
    )(sink, q, kv, kv, kv)


def _odd_attn_kernel(*refs, bounds):
    ng = len(C_DILATIONS)
    o_ref, oacc, lacc = refs[7 * ng:]
    i = pl.program_id(0)
    chunk = ODD_CHUNK
    qb = 128
    nk = qb + 2 * C_RADIUS
    row0 = i * chunk
    start, end = _seq_bounds(row0, bounds)
    rr = lax.broadcasted_iota(jnp.int32, (qb, nk), 0)
    cc = lax.broadcasted_iota(jnp.int32, (qb, nk), 1)
    band = jnp.abs(cc - C_RADIUS - rr) <= C_RADIUS
    scale = C_HEAD_DIM ** -0.5
    for g, d in enumerate(C_DILATIONS):
        q_ref, kp_ref, km_ref, kn_ref, vp_ref, vm_ref, vn_ref = refs[7 * g:7 * g + 7]
        tg = chunk // d
        t_lo, t_hi, t_c0 = start // d, end // d, row0 // d
        for sb in range(tg // qb):
            lo, hi = qb * sb - C_RADIUS, qb * sb + qb + C_RADIUS
            tk = t_c0 + lo + cc
            mask = band & (tk >= t_lo) & (tk < t_hi)
            for r in range(d):
                def window(p_ref, m_ref, n_ref):
                    parts = []
                    if lo < 0:
                        parts.append(p_ref[0, r, :, :])
                    parts.append(m_ref[0, r, max(lo, 0):min(hi, tg), :])
                    if hi > tg:
                        parts.append(n_ref[0, r, :, :])
                    return parts[0] if len(parts) == 1 else jnp.concatenate(parts, axis=0)

                q = q_ref[0, r, qb * sb:qb * (sb + 1), :]
                k = window(kp_ref, km_ref, kn_ref)
                v = window(vp_ref, vm_ref, vn_ref)
                s = lax.dot_general(q, k, (((1,), (1,)), ((), ())), preferred_element_type=F32) * scale
                s = jnp.where(mask, s, -jnp.inf)
                m = jnp.max(s, axis=-1, keepdims=True)
                p = jnp.exp(s - m)
                den = jnp.sum(p, axis=-1, keepdims=True)
                o = jnp.dot(p.astype(BF16), v, preferred_element_type=F32) / den
                lse = jnp.broadcast_to(m + jnp.log(den), (qb, LANES))
                if d == 1:
                    rows = pl.ds(qb * sb, qb)
                else:
                    rows = pl.ds(r + d * qb * sb, qb, stride=d)
                oacc[g, rows, :] = o
                lacc[g, rows, :] = lse
    ls = [lacc[g] for g in range(ng)]
    mx = functools.reduce(jnp.maximum, ls)
    ws = [jnp.exp(l - mx) for l in ls]
    tot = functools.reduce(lambda a, b: a + b, ws)
    out = functools.reduce(lambda a, b: a + b, [(ws[g] / tot) * oacc[g] for g in range(ng)])
    o_ref[...] = out.astype(o_ref.dtype)


def _odd_attention(qkv, bounds, m):
    chunk = ODD_CHUNK
    hb = C_RADIUS
    operands, in_specs = [], []
    for (q, k, v), d in zip(qkv, C_DILATIONS):
        tg = chunk // d
        per = tg // hb
        last = m // d // hb - 1
        main = pl.BlockSpec((1, d, tg, LANES), lambda i, h: (h, 0, i, 0))
        prev = pl.BlockSpec((1, d, hb, LANES), lambda i, h, per=per: (h, 0, jnp.maximum(i * per - 1, 0), 0))
        nxt = pl.BlockSpec((1, d, hb, LANES), lambda i, h, per=per, last=last: (h, 0, jnp.minimum((i + 1) * per, last), 0))
        operands += [q, k, k, k, v, v, v]
        in_specs += [main, prev, main, nxt, prev, main, nxt]
    ng = len(C_DILATIONS)
    return pl.pallas_call(
        functools.partial(_odd_attn_kernel, bounds=bounds),
        grid=(m // chunk, C_HEADS),
        in_specs=in_specs,
        out_specs=pl.BlockSpec((chunk, LANES), lambda i, h: (i, h)),
        out_shape=jax.ShapeDtypeStruct((m, C_HEADS * C_HEAD_DIM), BF16),
        scratch_shapes=[pltpu.VMEM((ng, chunk, LANES), F32), pltpu.VMEM((ng, chunk, LANES), F32)],
        compiler_params=_cparams(("parallel", "arbitrary")),
        name="odd_attention",
    )(*operands)


def _short_conv_kernel(xp_ref, xm_ref, xn_ref, w_ref, b_ref, o_ref, *, bounds):
    i = pl.program_id(0)
    tr = xm_ref.shape[0]
    row0 = i * tr
    start, end = _seq_bounds(row0, bounds)
    x = xm_ref[...]
    hp = xp_ref.shape[0]
    before = jnp.where(row0 > start, xp_ref[hp - 1:hp, :], 0.0)
    after = jnp.where(row0 + tr < end, xn_ref[0:1, :], 0.0)
    ridx = lax.broadcasted_iota(jnp.int32, x.shape, 0)
    xl = jnp.where(ridx == 0, before, pltpu.roll(x, 1, 0))
    xr = jnp.where(ridx == tr - 1, after, pltpu.roll(x, tr - 1, 0))
    y = b_ref[...] + xl * w_ref[0:1, :]
    y = y + x * w_ref[1:2, :]
    y = y + xr * w_ref[2:3, :]
    o_ref[...] = y


def _short_conv(x, w, b, bounds, tr=512, tc=512):
    m, n = x.shape
    hb = 8
    per = tr // hb
    last = m // hb - 1
    return pl.pallas_call(
        functools.partial(_short_conv_kernel, bounds=bounds),
        grid=(m // tr, n // tc),
        in_specs=[pl.BlockSpec((hb, tc), lambda i, j: (jnp.maximum(i * per - 1, 0), j)),
                  pl.BlockSpec((tr, tc), lambda i, j: (i, j)),
                  pl.BlockSpec((hb, tc), lambda i, j: (jnp.minimum((i + 1) * per, last), j)),
                  pl.BlockSpec((B_SHORT, tc), lambda i, j: (0, j)),
                  pl.BlockSpec((1, tc), lambda i, j: (0, j))],
        out_specs=pl.BlockSpec((tr, tc), lambda i, j: (i, j)),
        out_shape=jax.ShapeDtypeStruct((m, n), F32),
        compiler_params=_cparams(("parallel", "arbitrary")),
        name="short_conv",
    )(x, x, x, w, b.reshape(1, n))


def _filter_mlp_kernel(z_ref, w1_ref, b1_ref, f1_ref, w2_ref, b2_ref, f2_ref, w3_ref, b3_ref, dl_ref,
                       h_ref, nrm_ref):
    i = pl.program_id(0)
    z = z_ref[...]
    h = jnp.sin(f1_ref[...] * (jnp.dot(z.astype(BF16), w1_ref[...], preferred_element_type=F32) + b1_ref[...]))
    h = jnp.sin(f2_ref[...] * (jnp.dot(h.astype(BF16), w2_ref[...], preferred_element_type=F32) + b2_ref[...]))
    h = jnp.dot(h.astype(BF16), w3_ref[...], preferred_element_type=F32) + b3_ref[...]
    decay = jnp.exp(-z[:, 0:1] * dl_ref[...])
    nrep = h.shape[1] // decay.shape[1]
    h = h * jnp.concatenate([decay] * nrep, axis=1)
    h_ref[...] = h

    @pl.when(i == 0)
    def _():
        nrm_ref[...] = jnp.zeros_like(nrm_ref)

    half = h.shape[1] // 2
    col = lax.broadcasted_iota(jnp.int32, h.shape, 1)
    row = lax.broadcasted_iota(jnp.int32, h.shape, 0) + i * h.shape[0]
    a = jnp.where((col >= half) & (row == 0), 0.0, jnp.abs(h))
    nrm_ref[...] += jnp.sum(a, axis=0, keepdims=True)


def _filter_mlp(z, w1, b1, f1, w2, b2, f2, w3, b3, deltas, tl=512):
    l, e = z.shape
    hid = w1.shape[1]
    n = w3.shape[1]
    c = deltas.shape[0]
    full = lambda shape: pl.BlockSpec(shape, lambda i: (0, 0))
    return pl.pallas_call(
        _filter_mlp_kernel,
        grid=(l // tl,),
        in_specs=[pl.BlockSpec((tl, e), lambda i: (i, 0)),
                  full((e, hid)), full((1, hid)), full((1, hid)),
                  full((hid, hid)), full((1, hid)), full((1, hid)),
                  full((hid, n)), full((1, n)), full((1, c))],
        out_specs=[pl.BlockSpec((tl, n), lambda i: (i, 0)), full((1, n))],
        out_shape=[jax.ShapeDtypeStruct((l, n), F32), jax.ShapeDtypeStruct((1, n), F32)],
        compiler_params=_cparams(("arbitrary",)),
        name="filter_mlp",
    )(z, w1, b1.reshape(1, hid), f1.reshape(1, hid), w2, b2.reshape(1, hid), f2.reshape(1, hid),
      w3, b3.reshape(1, n), deltas.reshape(1, c))


def _split(x):
    hi = x.astype(BF16)
    lo = (x - hi.astype(F32)).astype(BF16)
    return hi, lo


def _dot3(ch, cl, x):
    xh, xl = _split(x)
    r = jnp.dot(ch, xh, preferred_element_type=F32)
    r = r + jnp.dot(ch, xl, preferred_element_type=F32)
    return r + jnp.dot(cl, xh, preferred_element_type=F32)


def _cdot3(ch, cl, xr, xi):
    n = ch.shape[0] // 2
    p = _dot3(ch, cl, xr)
    q = _dot3(ch, cl, xi)
    return p[:n] - q[n:], q[:n] + p[n:]


def _fft1_kernel(x_ref, fh_ref, fl_ref, ar_ref, ai_ref):
    rp = ar_ref.shape[0]
    out = _dot3(fh_ref[...], fl_ref[...], x_ref[...])
    ar_ref[...] = out[:rp]
    ai_ref[...] = out[rp:]


def _fft_stage1(x, fh, fl, ncol, cstride, coff, tc):
    b, kn1, _ = x.shape
    rp = fh.shape[0] // 2
    out = jax.ShapeDtypeStruct((b, rp, ncol * tc), F32)
    return pl.pallas_call(
        _fft1_kernel,
        grid=(b, ncol),
        in_specs=[pl.BlockSpec((None, kn1, tc), lambda bb, j: (bb, 0, j * cstride + coff)),
                  pl.BlockSpec(fh.shape, lambda bb, j: (0, 0)),
                  pl.BlockSpec(fl.shape, lambda bb, j: (0, 0))],
        out_specs=[pl.BlockSpec((None, rp, tc), lambda bb, j: (bb, 0, j)),
                   pl.BlockSpec((None, rp, tc), lambda bb, j: (bb, 0, j))],
        out_shape=[out, out],
        compiler_params=_cparams(("parallel", "arbitrary")),
        name="fft_stage1",
    )(x, fh, fl)


def _filter_mid_kernel(ar_ref, ai_ref, wh_ref, wl_ref, inv_ref, kr_ref, ki_ref, *, nslab):
    k1 = pl.program_id(0)

    @pl.when(k1 < nslab)
    def _():
        xr, xi = _cdot3(wh_ref[...], wl_ref[...], ar_ref[...], ai_ref[...])
        kr_ref[...] = xr * inv_ref[...]
        ki_ref[...] = xi * inv_ref[...]

    @pl.when(k1 >= nslab)
    def _():
        kr_ref[...] = jnp.zeros_like(kr_ref)
        ki_ref[...] = jnp.zeros_like(ki_ref)


def _filter_mid(ar, ai, wh, wl, inv, nslab, ct=512):
    rp, n2, c = ar.shape
    slab = pl.BlockSpec((None, n2, ct), lambda k, j: (k, 0, j))
    wspec = pl.BlockSpec((None, 2 * n2, n2), lambda k, j: (k, 0, 0))
    out = jax.ShapeDtypeStruct((rp, n2, c), F32)
    return pl.pallas_call(
        functools.partial(_filter_mid_kernel, nslab=nslab),
        grid=(rp, c // ct),
        in_specs=[slab, slab, wspec, wspec, pl.BlockSpec((1, ct), lambda k, j: (0, j))],
        out_specs=[slab, slab],
        out_shape=[out, out],
        compiler_params=_cparams(("parallel", "arbitrary")),
        name="filter_mid",
    )(ar, ai, wh, wl, inv)


def _conv_mid_kernel(ar_ref, ai_ref, wfh_ref, wfl_ref, wih_ref, wil_ref, kr_ref, ki_ref, dr_ref, di_ref, *, nslab):
    k1 = pl.program_id(1)

    @pl.when(k1 < nslab)
    def _():
        xr, xi = _cdot3(wfh_ref[...], wfl_ref[...], ar_ref[...], ai_ref[...])
        kr, ki = kr_ref[...], ki_ref[...]
        yr = xr * kr - xi * ki
        yi = xr * ki + xi * kr
        dr, di = _cdot3(wih_ref[...], wil_ref[...], yr, yi)
        dr_ref[...] = dr
        di_ref[...] = di

    @pl.when(k1 >= nslab)
    def _():
        dr_ref[...] = jnp.zeros_like(dr_ref)
        di_ref[...] = jnp.zeros_like(di_ref)


def _conv_mid(ar, ai, wfh, wfl, wih, wil, kr, ki, order, nslab, ct=512):
    b, rp, n2, c = ar.shape
    koff = order * (c // ct)
    slab = pl.BlockSpec((None, None, n2, ct), lambda bb, k, j: (bb, k, 0, j))
    wspec = pl.BlockSpec((None, 2 * n2, n2), lambda bb, k, j: (k, 0, 0))
    kspec = pl.BlockSpec((None, n2, ct), lambda bb, k, j: (k, 0, j + koff))
    out = jax.ShapeDtypeStruct((b, rp, n2, c), F32)
    return pl.pallas_call(
        functools.partial(_conv_mid_kernel, nslab=nslab),
        grid=(b, rp, c // ct),
        in_specs=[slab, slab, wspec, wspec, wspec, wspec, kspec, kspec],
        out_specs=[slab, slab],
        out_shape=[out, out],
        compiler_params=_cparams(("parallel", "parallel", "arbitrary")),
        name="conv_mid",
    )(ar, ai, wfh, wfl, wih, wil, kr, ki)


def _fft_out_kernel(dr_ref, di_ref, grh_ref, grl_ref, gih_ref, gil_ref, u_ref, gate_ref, bias_ref, o_ref):
    y = _dot3(grh_ref[...], grl_ref[...], dr_ref[...]) + _dot3(gih_ref[...], gil_ref[...], di_ref[...])
    u = u_ref[...]
    o_ref[...] = (gate_ref[...] * (y + u * bias_ref[...])).astype(o_ref.dtype)


def _fft_out(dr, di, g4, u, ucs, uoff, gate, gcs, goff, bias, out_dtype, tc):
    b, rp, cols = dr.shape
    nh = g4[0].shape[0]
    ncol = cols // tc
    dspec = pl.BlockSpec((None, rp, tc), lambda bb, j: (bb, 0, j))
    gspec = pl.BlockSpec(g4[0].shape, lambda bb, j: (0, 0))
    return pl.pallas_call(
        _fft_out_kernel,
        grid=(b, ncol),
        in_specs=[dspec, dspec, gspec, gspec, gspec, gspec,
                  pl.BlockSpec((None, nh, tc), lambda bb, j: (bb, 0, j * ucs + uoff)),
                  pl.BlockSpec((None, nh, tc), lambda bb, j: (bb, 0, j * gcs + goff)),
                  pl.BlockSpec((1, tc), lambda bb, j: (0, 0))],
        out_specs=pl.BlockSpec((None, nh, tc), lambda bb, j: (bb, 0, j)),
        out_shape=jax.ShapeDtypeStruct((b, nh, cols), out_dtype),
        compiler_params=_cparams(("parallel", "arbitrary")),
        name="fft_out",
    )(dr, di, *g4, u, gate, bias)


def _np_split(a):
    a32 = np.asarray(a, np.float32)
    hi = a32.astype(BF16)
    lo = (a32 - hi.astype(np.float32)).astype(BF16)
    return hi, lo


def _fft_plan(l):
    n = 2 * l
    n2 = FFT_N2
    n1 = n // n2
    r = n1 // 2 + 1
    rp = -(-r // 16) * 16
    k1 = np.arange(rp, dtype=np.float64)[:, None]
    live = (k1 < r).astype(np.float64)

    def f1(kn1):
        ang = 2 * np.pi * k1 * np.arange(kn1)[None, :] / n1
        return np.concatenate([np.cos(ang) * live, -np.sin(ang) * live], axis=0)

    kk = np.arange(rp, dtype=np.float64)[None, :]
    wgt = np.where((kk == 0) | (kk == n1 // 2), 1.0, 2.0) * (kk < r) / n
    ango = 2 * np.pi * np.arange(n1 // 2)[:, None] * kk / n1
    gre = np.cos(ango) * wgt
    gim = -np.sin(ango) * wgt

    ik1 = jnp.arange(rp, dtype=jnp.int32)[:, None, None]
    ia = jnp.arange(n2, dtype=jnp.int32)[None, :, None]
    ib = jnp.arange(n2, dtype=jnp.int32)[None, None, :]
    livej = (ik1 < r).astype(F32)

    def slab_matrices(idx, sign):
        ang = (idx % n).astype(F32) * F32(2.0 * math.pi / n)
        w = jnp.concatenate([jnp.cos(ang), sign * jnp.sin(ang)], axis=1) * livej
        hi = w.astype(BF16)
        return hi, (w - hi.astype(F32)).astype(BF16)

    wf = slab_matrices(n1 * ia * ib + ik1 * ib, -1.0)
    wi = slab_matrices(n1 * ia * ib + ik1 * ia, 1.0)
    return dict(n1=n1, n2=n2, r=r, rp=rp,
                f1_half=_np_split(f1(n1 // 2)), f1_full=_np_split(f1(n1)),
                wf=wf, wi=wi, g4=_np_split(gre) + _np_split(gim))


def _filter_features(l):
    t = jnp.linspace(0.0, 1.0, l, dtype=F32)[:, None]
    bands = jnp.linspace(1e-4, B_BANDS - 1, B_BANDS, dtype=F32)[None, :]
    w = 2.0 * math.pi * jnp.arange(l, dtype=F32)[:, None] / l
    return jnp.concatenate([t, jnp.cos(bands * w), -jnp.sin(bands * w)], axis=-1)


def _hyena_filters(plan, l, c, w1, b1, f1, w2, b2, f2, w3, b3):
    n1, n2 = plan["n1"], plan["n2"]
    z = _filter_features(l)
    e = z.shape[1]
    ep = -(-e // 16) * 16
    z = jnp.pad(z, ((0, 0), (0, ep - e)))
    w1p = jnp.pad(w1, ((0, ep - e), (0, 0))).astype(BF16)
    max_decay = math.log(B_DECAY_TARGET) / B_FAST_DECAY_PCT
    min_decay = math.log(B_DECAY_TARGET) / B_SLOW_DECAY_PCT
    deltas = jnp.abs(jnp.linspace(min_decay, max_decay, c, dtype=F32))
    h, nrm = _filter_mlp(z, w1p, b1, f1, w2.astype(BF16), b2, f2, w3.astype(BF16), b3, deltas,
                         tl=min(512, l))
    oc = h.shape[1] // 2
    fwd, bwd = h[:, :oc], h[:, oc:]
    kern = jnp.concatenate([fwd, jnp.zeros((1, oc), F32), bwd[:0:-1]], axis=0)
    inv = 1.0 / (nrm[:, :oc] + nrm[:, oc:])
    tc = 2048
    ar, ai = _fft_stage1(kern.reshape(1, n1, n2 * oc), *plan["f1_full"], ncol=n2 * oc // tc,
                         cstride=1, coff=0, tc=tc)
    rp = plan["rp"]
    return _filter_mid(ar.reshape(rp, n2, oc), ai.reshape(rp, n2, oc), *plan["wf"], inv, plan["r"])


def _hyena_conv(plan, kf, order, u, ucs, uoff, gate, gcs, goff, bias, c, out_dtype):
    n1, n2, rp = plan["n1"], plan["n2"], plan["rp"]
    b = u.shape[0]
    uv = u.reshape(b, n1 // 2, n2 * ucs * c)
    gv = gate.reshape(b, n1 // 2, n2 * gcs * c)
    ar, ai = _fft_stage1(uv, *plan["f1_half"], ncol=n2, cstride=ucs, coff=uoff, tc=c)
    dr, di = _conv_mid(ar.reshape(b, rp, n2, c), ai.reshape(b, rp, n2, c), *plan["wf"], *plan["wi"],
                       kf[0], kf[1], order, plan["r"])
    out = _fft_out(dr.reshape(b, rp, n2 * c), di.reshape(b, rp, n2 * c), plan["g4"],
                   uv, ucs, uoff, gv, gcs, goff, bias.reshape(1, c), out_dtype, tc=c)
    return out.reshape(b, n1 // 2 * n2, c)


def _hyena_mixer(u, seqs, plans, c, fw, hy_bias):
    outs = []
    for (row0, nb, l) in seqs:
        plan = plans[l]
        kf = _hyena_filters(plan, l, c, *fw)
        ub = u[row0:row0 + nb * l].reshape(nb, l, 3 * c)
        z = _hyena_conv(plan, kf, 0, ub, 3, 0, ub, 3, 1, hy_bias[0], c, F32)
        o = _hyena_conv(plan, kf, 1, z, 1, 0, ub, 3, 2, hy_bias[1], c, BF16)
        outs.append(o.reshape(nb * l, c))
    return jnp.concatenate(outs, axis=0)


def _rope_tables(pos, hd):
    rot = hd // ROPE_FRACTION
    half = rot // 2
    inv = ROPE_THETA ** (-(jnp.arange(half, dtype=F32) * 2.0 / rot))
    ang = pos[:, None] * inv[None, :]
    cos, sin = jnp.cos(ang), jnp.sin(ang)
    m = pos.shape[0]
    one = jnp.ones((m, hd - rot), F32)
    zero = jnp.zeros((m, hd - rot), F32)
    zh = jnp.zeros((m, half), F32)
    c = jnp.concatenate([cos, cos, one], axis=1)
    s1 = jnp.concatenate([-sin, zh, zero], axis=1)
    s2 = jnp.concatenate([zh, sin, zero], axis=1)
    rep = LANES // hd
    return tuple(jnp.tile(t, (1, rep)) for t in (c, s1, s2)), half


def _trunk(x, bounds, seqs, p):
    m, dm = x.shape
    pos = jnp.concatenate([jnp.tile(jnp.arange(l, dtype=F32), nb) for (_, nb, l) in seqs])
    tabs_a, half_a = _rope_tables(pos, A_HEAD_DIM)
    tabs_c, half_c = _rope_tables(pos, C_HEAD_DIM)
    ident = (jnp.ones((m, LANES), F32), jnp.zeros((m, LANES), F32), jnp.zeros((m, LANES), F32))
    tabs_kv = tuple(jnp.concatenate([a, b], axis=1) for a, b in zip(tabs_a, ident))
    c_hy = dm - A_WIDTH
    plans = {l: _fft_plan(l) for l in sorted({l for (_, _, l) in seqs})}
    xf = x
    xb = x.astype(BF16)
    for i in range(DEPTH):
        j = i // 2
        if i % 2 == 0:
            w_in = p['mix_e_w_in'][j].astype(BF16)
            kv0 = A_WIDTH
            hy0 = A_WIDTH + 2 * A_KV_WIDTH
            q = _matmul_rope(xb, w_in[:, :kv0], tabs_a, half_a, tn=512)
            kv = _matmul_rope(xb, w_in[:, kv0:hy0], tabs_kv, half_a, tn=2 * A_KV_WIDTH)
            hy = _matmul(xb, w_in[:, hy0:], tn=512, out_dtype=F32)
            a_out = _even_attention(q, kv, p['a_sink'][j], bounds)
            u = _short_conv(hy, p['hy_conv_w'][j], p['hy_conv_b'][j], bounds)
            fw = (p['hy_w1'][j], p['hy_b1'][j], p['hy_f1'][j], p['hy_w2'][j], p['hy_b2'][j],
                  p['hy_f2'][j], p['hy_w3'][j], p['hy_b3'][j])
            h_out = _hyena_mixer(u, seqs, plans, c_hy, fw, p['hy_bias'][j])
            w_out = p['mix_e_w_out'][j].astype(BF16)
            xf, xb = _matmul_ln([a_out, h_out], [w_out[:A_WIDTH], w_out[A_WIDTH:]], xf,
                                p['ln1_g'][i], p['ln1_b'][i])
        else:
            w_in = p['mix_o_w_in'][j].astype(BF16)
            gw = C_HEADS * C_HEAD_DIM
            ng = len(C_DILATIONS)
            qkv = []
            for g, d in enumerate(C_DILATIONS):
                trio = []
                for part in range(3):
                    c0 = (part * ng + g) * gw
                    trio.append(_odd_proj(xb, w_in[:, c0:c0 + gw], tabs_c, half_c, d, rope=part < 2))
                qkv.append(tuple(trio))
            o = _odd_attention(qkv, bounds, m)
            xf, xb = _matmul_ln([o], [p['mix_o_w_out'][j].astype(BF16)], xf, p['ln1_g'][i], p['ln1_b'][i])
        xf, xb = _ffn_ln(xb, xf, p['ffn_w_gate'][i].astype(BF16), p['ffn_w_up'][i].astype(BF16),
                         p['ffn_w_down'][i].astype(BF16), p['ln2_g'][i], p['ln2_b'][i])
    return xf


def kernel(x_prompt, x_sample, mix_e_w_in, a_sink, hy_conv_w, hy_conv_b, hy_w1, hy_b1, hy_f1, hy_w2, hy_b2,
           hy_f2, hy_w3, hy_b3, hy_bias, mix_e_w_out, mix_o_w_in, mix_o_w_out, ffn_w_gate, ffn_w_up,
           ffn_w_down, ln1_g, ln1_b, ln2_g, ln2_b):
    p = dict(mix_e_w_in=mix_e_w_in, a_sink=a_sink, hy_conv_w=hy_conv_w, hy_conv_b=hy_conv_b,
             hy_w1=hy_w1, hy_b1=hy_b1, hy_f1=hy_f1, hy_w2=hy_w2, hy_b2=hy_b2, hy_f2=hy_f2,
             hy_w3=hy_w3, hy_b3=hy_b3, hy_bias=hy_bias, mix_e_w_out=mix_e_w_out,
             mix_o_w_in=mix_o_w_in, mix_o_w_out=mix_o_w_out, ffn_w_gate=ffn_w_gate,
             ffn_w_up=ffn_w_up, ffn_w_down=ffn_w_down, ln1_g=ln1_g, ln1_b=ln1_b,
             ln2_g=ln2_g, ln2_b=ln2_b)
    dm = x_prompt.shape[-1]
    seqs, bounds, row = [], [0], 0
    for xs in (x_prompt, x_sample):
        nb, l = xs.shape[0], xs.shape[1]
        seqs.append((row, nb, l))
        for _ in range(nb):
            row += l
            bounds.append(row)
    x = jnp.concatenate([x_prompt.reshape(-1, dm), x_sample.reshape(-1, dm)], axis=0)
    y = _trunk(x, tuple(bounds), tuple(seqs), p)
    n_p = x_prompt.shape[0] * x_prompt.shape[1]
    return (y[:n_p].reshape(x_prompt.shape), y[n_p:].reshape(x_sample.shape))
```

```python
import functools
import math

import numpy as np
import jax
import jax.numpy as jnp
from jax import lax
from jax.experimental import pallas as pl
from jax.experimental.pallas import tpu as pltpu

F32 = jnp.float32
BF16 = jnp.bfloat16

DEPTH = 4
A_HEADS, A_KV_HEADS, A_HEAD_DIM, A_RADIUS = 16, 2, 64, 128
A_WIDTH = A_HEADS * A_HEAD_DIM
A_KV_WIDTH = A_KV_HEADS * A_HEAD_DIM
B_SHORT, B_EMB = 3, 33
B_BANDS = (B_EMB - 1) // 2
B_DECAY_TARGET, B_FAST_DECAY_PCT, B_SLOW_DECAY_PCT = 1e-2, 0.3, 1.5
C_HEADS, C_HEAD_DIM = 16, 128
C_DILATIONS = (1, 4, 16)
C_RADIUS = 64
ROPE_THETA, ROPE_FRACTION = 500000.0, 4
ALPHA = (2 * DEPTH) ** 0.25
LN_EPS = 1e-5

LANES = 128
VMEM_LIMIT = 56 * 1024 * 1024
FFT_N2 = 256

ODD_CHUNK = 2048
ATT_TQ = 256


def _cparams(sem):
    return pltpu.CompilerParams(dimension_semantics=sem, vmem_limit_bytes=VMEM_LIMIT)


def _seq_bounds(row, bounds):
    start = jnp.int32(bounds[0])
    end = jnp.int32(bounds[1])
    for b0, b1 in zip(bounds[1:-1], bounds[2:]):
        inside = row >= b0
        start = jnp.where(inside, jnp.int32(b0), start)
        end = jnp.where(inside, jnp.int32(b1), end)
    return start, end


def _rope(a, c, s1, s2, half):
    w = a.shape[-1]
    return a * c + pltpu.roll(a, w - half, 1) * s1 + pltpu.roll(a, half, 1) * s2


def _mm_kernel(x_ref, w_ref, o_ref):
    o_ref[...] = jnp.dot(x_ref[...], w_ref[...], preferred_element_type=F32).astype(o_ref.dtype)


def _matmul(x, w, tn, out_dtype, tm=1024):
    m, k = x.shape
    n = w.shape[1]
    return pl.pallas_call(
        _mm_kernel,
        grid=(m // tm, n // tn),
        in_specs=[pl.BlockSpec((tm, k), lambda i, j: (i, 0)),
                  pl.BlockSpec((k, tn), lambda i, j: (0, j))],
        out_specs=pl.BlockSpec((tm, tn), lambda i, j: (i, j)),
        out_shape=jax.ShapeDtypeStruct((m, n), out_dtype),
        compiler_params=_cparams(("parallel", "arbitrary")),
        name="matmul",
    )(x, w)


def _mm_rope_kernel(x_ref, w_ref, c_ref, s1_ref, s2_ref, o_ref, *, half):
    acc = jnp.dot(x_ref[...], w_ref[...], preferred_element_type=F32)
    tw = c_ref.shape[1]
    rc = 256
    for r0 in range(0, acc.shape[0], rc):
        rows = slice(r0, r0 + rc)
        for c in range(acc.shape[1] // tw):
            cols = slice(c * tw, (c + 1) * tw)
            o_ref[rows, cols] = _rope(acc[rows, cols], c_ref[rows, :], s1_ref[rows, :], s2_ref[rows, :],
                                      half).astype(o_ref.dtype)


def _matmul_rope(x, w, tabs, half, tn, tm=1024):
    m, k = x.shape
    n = w.shape[1]
    tw = tabs[0].shape[1]
    tab_spec = pl.BlockSpec((tm, tw), lambda i, j: (i, 0))
    return pl.pallas_call(
        functools.partial(_mm_rope_kernel, half=half),
        grid=(m // tm, n // tn),
        in_specs=[pl.BlockSpec((tm, k), lambda i, j: (i, 0)),
                  pl.BlockSpec((k, tn), lambda i, j: (0, j)),
                  tab_spec, tab_spec, tab_spec],
        out_specs=pl.BlockSpec((tm, tn), lambda i, j: (i, j)),
        out_shape=jax.ShapeDtypeStruct((m, n), BF16),
        compiler_params=_cparams(("parallel", "arbitrary")),
        name="matmul_rope",
    )(x, w, *tabs)


def _odd_proj_kernel(x_ref, w_ref, c_ref, s1_ref, s2_ref, o_ref, acc_ref, *, d, rope, half):
    hps = w_ref.shape[1] // LANES
    tm = x_ref.shape[0]
    t = tm // d
    pair = 2
    rc = 256
    for p in range(hps // pair):
        acc = jnp.dot(x_ref[...], w_ref[:, p * pair * LANES:(p + 1) * pair * LANES], preferred_element_type=F32)
        for h2 in range(pair):
            hh = p * pair + h2
            slot = (p % 2) * pair + h2
            acc_ref[slot, :, :] = acc[:, h2 * LANES:(h2 + 1) * LANES]
            for c0 in range(0, tm, rc):
                a = acc_ref[slot, c0:c0 + rc, :]
                if rope:
                    a = _rope(a, c_ref[c0:c0 + rc, :], s1_ref[c0:c0 + rc, :], s2_ref[c0:c0 + rc, :], half)
                if d == 1:
                    o_ref[hh, 0, c0:c0 + rc, :] = a.astype(BF16)
                elif rope:
                    acc_ref[slot, c0:c0 + rc, :] = a
            if d > 1:
                for r in range(d):
                    o_ref[hh, r, :, :] = acc_ref[slot, pl.ds(r, t, stride=d), :].astype(BF16)


def _odd_proj(x, w, tabs, half, d, rope, hps=8):
    m, k = x.shape
    tm = ODD_CHUNK
    tab_spec = pl.BlockSpec((tm, LANES), lambda i, j: (i, 0))
    return pl.pallas_call(
        functools.partial(_odd_proj_kernel, d=d, rope=rope, half=half),
        grid=(m // tm, C_HEADS // hps),
        in_specs=[pl.BlockSpec((tm, k), lambda i, j: (i, 0)),
                  pl.BlockSpec((k, hps * LANES), lambda i, j: (0, j)),
                  tab_spec, tab_spec, tab_spec],
        out_specs=pl.BlockSpec((hps, d, tm // d, LANES), lambda i, j: (j, 0, i, 0)),
        out_shape=jax.ShapeDtypeStruct((C_HEADS, d, m // d, LANES), BF16),
        scratch_shapes=[pltpu.VMEM((4, tm, LANES), F32)],
        compiler_params=_cparams(("parallel", "arbitrary")),
        name="odd_proj",
    )(x, w, *tabs)


LN_ROWS = 128


def _layer_norm_store(x_ref, acc_ref, g_ref, b_ref, of_ref, ob_ref):
    for c in range(x_ref.shape[0] // LN_ROWS):
        rows = pl.ds(c * LN_ROWS, LN_ROWS)
        r = ALPHA * x_ref[rows, :] + acc_ref[rows, :]
        mu = jnp.mean(r, axis=-1, keepdims=True)
        xc = r - mu
        var = jnp.mean(xc * xc, axis=-1, keepdims=True)
        y = xc * lax.rsqrt(var + LN_EPS) * g_ref[...] + b_ref[...]
        of_ref[rows, :] = y
        ob_ref[rows, :] = y.astype(BF16)


def _mm_ln_kernel(*refs, n_in):
    ys = refs[:n_in]
    ws = refs[n_in:2 * n_in]
    x_ref, g_ref, b_ref, of_ref, ob_ref, acc_ref = refs[2 * n_in:]
    acc = jnp.dot(ys[0][...], ws[0][...], preferred_element_type=F32)
    for y_ref, w_ref in zip(ys[1:], ws[1:]):
        acc = acc + jnp.dot(y_ref[...], w_ref[...], preferred_element_type=F32)
    acc_ref[...] = acc
    _layer_norm_store(x_ref, acc_ref, g_ref, b_ref, of_ref, ob_ref)


def _matmul_ln(ys, ws, x, g, b, tm=256):
    m, dm = x.shape
    n_in = len(ys)
    in_specs = [pl.BlockSpec((tm, y.shape[1]), lambda i: (i, 0)) for y in ys]
    in_specs += [pl.BlockSpec(w.shape, lambda i: (0, 0)) for w in ws]
    in_specs += [pl.BlockSpec((tm, dm), lambda i: (i, 0)),
                 pl.BlockSpec((1, dm), lambda i: (0, 0)),
                 pl.BlockSpec((1, dm), lambda i: (0, 0))]
    return pl.pallas_call(
        functools.partial(_mm_ln_kernel, n_in=n_in),
        grid=(m // tm,),
        in_specs=in_specs,
        out_specs=[pl.BlockSpec((tm, dm), lambda i: (i, 0)), pl.BlockSpec((tm, dm), lambda i: (i, 0))],
        out_shape=[jax.ShapeDtypeStruct((m, dm), F32), jax.ShapeDtypeStruct((m, dm), BF16)],
        scratch_shapes=[pltpu.VMEM((tm, dm), F32)],
        compiler_params=_cparams(("parallel",)),
        name="matmul_ln",
    )(*ys, *ws, x, g.reshape(1, dm), b.reshape(1, dm))


def _ffn_kernel(xb_ref, xf_ref, wg_ref, wu_ref, wd_ref, g_ref, b_ref, of_ref, ob_ref, acc_ref):
    j = pl.program_id(1)

    @pl.when(j == 0)
    def _():
        acc_ref[...] = jnp.zeros_like(acc_ref)

    xb = xb_ref[...]
    gate = jnp.dot(xb, wg_ref[...], preferred_element_type=F32)
    up = jnp.dot(xb, wu_ref[...], preferred_element_type=F32)
    h = (gate * jax.nn.sigmoid(gate)) * up
    acc_ref[...] += jnp.dot(h.astype(BF16), wd_ref[...], preferred_element_type=F32)

    @pl.when(j == pl.num_programs(1) - 1)
    def _():
        _layer_norm_store(xf_ref, acc_ref, g_ref, b_ref, of_ref, ob_ref)


def _ffn_ln(xb, xf, wg, wu, wd, g, b, tm=512, tf=512):
    m, dm = xf.shape
    dff = wg.shape[1]
    row = lambda i, j: (i, 0)
    return pl.pallas_call(
        _ffn_kernel,
        grid=(m // tm, dff // tf),
        in_specs=[pl.BlockSpec((tm, dm), row), pl.BlockSpec((tm, dm), row),
                  pl.BlockSpec((dm, tf), lambda i, j: (0, j)),
                  pl.BlockSpec((dm, tf), lambda i, j: (0, j)),
                  pl.BlockSpec((tf, dm), lambda i, j: (j, 0)),
                  pl.BlockSpec((1, dm), lambda i, j: (0, 0)),
                  pl.BlockSpec((1, dm), lambda i, j: (0, 0))],
        out_specs=[pl.BlockSpec((tm, dm), row), pl.BlockSpec((tm, dm), row)],
        out_shape=[jax.ShapeDtypeStruct((m, dm), F32), jax.ShapeDtypeStruct((m, dm), BF16)],
        scratch_shapes=[pltpu.VMEM((tm, dm), F32)],
        compiler_params=_cparams(("parallel", "arbitrary")),
        name="ffn_ln",
    )(xb, xf, wg, wu, wd, g.reshape(1, dm), b.reshape(1, dm))


def _even_attn_kernel(sink_ref, q_ref, kp_ref, km_ref, kn_ref, o_ref, *, bounds):
    i = pl.program_id(0)
    tq = q_ref.shape[0]
    nk = tq + 2 * A_RADIUS
    row0 = i * tq
    start, end = _seq_bounds(row0, bounds)
    kv = jnp.concatenate([kp_ref[...], km_ref[...], kn_ref[...]], axis=0)
    rq = row0 + lax.broadcasted_iota(jnp.int32, (tq, nk), 0)
    rk = row0 - A_RADIUS + lax.broadcasted_iota(jnp.int32, (tq, nk), 1)
    mask = (jnp.abs(rk - rq) <= A_RADIUS) & (rk >= start) & (rk < end)
    group = A_HEADS // A_KV_HEADS
    scale = A_HEAD_DIM ** -0.5
    for j in range(A_KV_HEADS):
        k = kv[:, j * A_HEAD_DIM:(j + 1) * A_HEAD_DIM]
        v = kv[:, A_KV_WIDTH + j * A_HEAD_DIM:A_KV_WIDTH + (j + 1) * A_HEAD_DIM]
        for gq in range(group):
            h = j * group + gq
            qh = q_ref[:, h * A_HEAD_DIM:(h + 1) * A_HEAD_DIM]
            s = lax.dot_general(qh, k, (((1,), (1,)), ((), ())), preferred_element_type=F32) * scale
            s = jnp.where(mask, s, -jnp.inf)
            sk = sink_ref[h]
            m = jnp.maximum(jnp.max(s, axis=-1, keepdims=True), sk)
            p = jnp.exp(s - m)
            den = jnp.sum(p, axis=-1, keepdims=True) + jnp.exp(sk - m)
            o = jnp.dot(p.astype(BF16), v, preferred_element_type=F32) / den
            o_ref[:, h * A_HEAD_DIM:(h + 1) * A_HEAD_DIM] = o.astype(o_ref.dtype)


def _even_attention(q, kv, sink, bounds):
    m = q.shape[0]
    tq = ATT_TQ
    hb = A_RADIUS
    per = tq // hb
    last = m // hb - 1
    kvw = kv.shape[1]
    return pl.pallas_call(
        functools.partial(_even_attn_kernel, bounds=bounds),
        grid=(m // tq,),
        in_specs=[pl.BlockSpec(memory_space=pltpu.SMEM),
                  pl.BlockSpec((tq, A_WIDTH), lambda i: (i, 0)),
                  pl.BlockSpec((hb, kvw), lambda i: (jnp.maximum(i * per - 1, 0), 0)),
                  pl.BlockSpec((tq, kvw), lambda i: (i, 0)),
                  pl.BlockSpec((hb, kvw), lambda i: (jnp.minimum((i + 1) * per, last), 0))],
        out_specs=pl.BlockSpec((tq, A_WIDTH), lambda i: (i, 0)),
        out_shape=jax.ShapeDtypeStruct((m, A_WIDTH), BF16),
        compiler_params=_cparams(("parallel",)),
        name="even_attention",
    )(sink, q, kv, kv, kv)


def _odd_attn_kernel(*refs, bounds):
    ng = len(C_DILATIONS)
    o_ref, oacc, lacc = refs[7 * ng:]
    i = pl.program_id(0)
    chunk = ODD_CHUNK
    qb = 128
    nk = qb + 2 * C_RADIUS
    row0 = i * chunk
    start, end = _seq_bounds(row0, bounds)
    rr = lax.broadcasted_iota(jnp.int32, (qb, nk), 0)
    cc = lax.broadcasted_iota(jnp.int32, (qb, nk), 1)
    band = jnp.abs(cc - C_RADIUS - rr) <= C_RADIUS
    scale = C_HEAD_DIM ** -0.5
    for g, d in enumerate(C_DILATIONS):
        q_ref, kp_ref, km_ref, kn_ref, vp_ref, vm_ref, vn_ref = refs[7 * g:7 * g + 7]
        tg = chunk // d
        t_lo, t_hi, t_c0 = start // d, end // d, row0 // d
        for sb in range(tg // qb):
            lo, hi = qb * sb - C_RADIUS, qb * sb + qb + C_RADIUS
            tk = t_c0 + lo + cc
            mask = band & (tk >= t_lo) & (tk < t_hi)
            for r in range(d):
                def window(p_ref, m_ref, n_ref):
                    parts = []
                    if lo < 0:
                        parts.append(p_ref[0, r, :, :])
                    parts.append(m_ref[0, r, max(lo, 0):min(hi, tg), :])
                    if hi > tg:
                        parts.append(n_ref[0, r, :, :])
                    return parts[0] if len(parts) == 1 else jnp.concatenate(parts, axis=0)

                q = q_ref[0, r, qb * sb:qb * (sb + 1), :]
                k = window(kp_ref, km_ref, kn_ref)
                v = window(vp_ref, vm_ref, vn_ref)
                s = lax.dot_general(q, k, (((1,), (1,)), ((), ())), preferred_element_type=F32) * scale
                s = jnp.where(mask, s, -jnp.inf)
                m = jnp.max(s, axis=-1, keepdims=True)
                p = jnp.exp(s - m)
                den = jnp.sum(p, axis=-1, keepdims=True)
                o = jnp.dot(p.astype(BF16), v, preferred_element_type=F32) / den
                lse = jnp.broadcast_to(m + jnp.log(den), (qb, LANES))
                if d == 1:
                    rows = pl.ds(qb * sb, qb)
                else:
                    rows = pl.ds(r + d * qb * sb, qb, stride=d)
                oacc[g, rows, :] = o
                lacc[g, rows, :] = lse
    ls = [lacc[g] for g in range(ng)]
    mx = functools.reduce(jnp.maximum, ls)
    ws = [jnp.exp(l - mx) for l in ls]
    tot = functools.reduce(lambda a, b: a + b, ws)
    out = functools.reduce(lambda a, b: a + b, [(ws[g] / tot) * oacc[g] for g in range(ng)])
    o_ref[...] = out.astype(o_ref.dtype)


def _odd_attention(qkv, bounds, m):
    chunk = ODD_CHUNK
    hb = C_RADIUS
    operands, in_specs = [], []
    for (q, k, v), d in zip(qkv, C_DILATIONS):
        tg = chunk // d
        per = tg // hb
        last = m // d // hb - 1
        main = pl.BlockSpec((1, d, tg, LANES), lambda i, h: (h, 0, i, 0))
        prev = pl.BlockSpec((1, d, hb, LANES), lambda i, h, per=per: (h, 0, jnp.maximum(i * per - 1, 0), 0))
        nxt = pl.BlockSpec((1, d, hb, LANES), lambda i, h, per=per, last=last: (h, 0, jnp.minimum((i + 1) * per, last), 0))
        operands += [q, k, k, k, v, v, v]
        in_specs += [main, prev, main, nxt, prev, main, nxt]
    ng = len(C_DILATIONS)
    return pl.pallas_call(
        functools.partial(_odd_attn_kernel, bounds=bounds),
        grid=(m // chunk, C_HEADS),
        in_specs=in_specs,
        out_specs=pl.BlockSpec((chunk, LANES), lambda i, h: (i, h)),
        out_shape=jax.ShapeDtypeStruct((m, C_HEADS * C_HEAD_DIM), BF16),
        scratch_shapes=[pltpu.VMEM((ng, chunk, LANES), F32), pltpu.VMEM((ng, chunk, LANES), F32)],
        compiler_params=_cparams(("parallel", "arbitrary")),
        name="odd_attention",
    )(*operands)


def _short_conv_kernel(xp_ref, xm_ref, xn_ref, w_ref, b_ref, o_ref, *, bounds):
    i = pl.program_id(0)
    tr = xm_ref.shape[0]
    row0 = i * tr
    start, end = _seq_bounds(row0, bounds)
    x = xm_ref[...]
    hp = xp_ref.shape[0]
    before = jnp.where(row0 > start, xp_ref[hp - 1:hp, :], 0.0)
    after = jnp.where(row0 + tr < end, xn_ref[0:1, :], 0.0)
    ridx = lax.broadcasted_iota(jnp.int32, x.shape, 0)
    xl = jnp.where(ridx == 0, before, pltpu.roll(x, 1, 0))
    xr = jnp.where(ridx == tr - 1, after, pltpu.roll(x, tr - 1, 0))
    y = b_ref[...] + xl * w_ref[0:1, :]
    y = y + x * w_ref[1:2, :]
    y = y + xr * w_ref[2:3, :]
    o_ref[...] = y


def _short_conv(x, w, b, bounds, tr=512, tc=512):
    m, n = x.shape
    hb = 8
    per = tr // hb
    last = m // hb - 1
    return pl.pallas_call(
        functools.partial(_short_conv_kernel, bounds=bounds),
        grid=(m // tr, n // tc),
        in_specs=[pl.BlockSpec((hb, tc), lambda i, j: (jnp.maximum(i * per - 1, 0), j)),
                  pl.BlockSpec((tr, tc), lambda i, j: (i, j)),
                  pl.BlockSpec((hb, tc), lambda i, j: (jnp.minimum((i + 1) * per, last), j)),
                  pl.BlockSpec((B_SHORT, tc), lambda i, j: (0, j)),
                  pl.BlockSpec((1, tc), lambda i, j: (0, j))],
        out_specs=pl.BlockSpec((tr, tc), lambda i, j: (i, j)),
        out_shape=jax.ShapeDtypeStruct((m, n), F32),
        compiler_params=_cparams(("parallel", "arbitrary")),
        name="short_conv",
    )(x, x, x, w, b.reshape(1, n))


def _filter_mlp_kernel(z_ref, w1_ref, b1_ref, f1_ref, w2_ref, b2_ref, f2_ref, w3_ref, b3_ref, dl_ref,
                       h_ref, nrm_ref):
    i = pl.program_id(0)
    z = z_ref[...]
    h = jnp.sin(f1_ref[...] * (jnp.dot(z.astype(BF16), w1_ref[...], preferred_element_type=F32) + b1_ref[...]))
    h = jnp.sin(f2_ref[...] * (jnp.dot(h.astype(BF16), w2_ref[...], preferred_element_type=F32) + b2_ref[...]))
    h = jnp.dot(h.astype(BF16), w3_ref[...], preferred_element_type=F32) + b3_ref[...]
    decay = jnp.exp(-z[:, 0:1] * dl_ref[...])
    nrep = h.shape[1] // decay.shape[1]
    h = h * jnp.concatenate([decay] * nrep, axis=1)
    h_ref[...] = h

    @pl.when(i == 0)
    def _():
        nrm_ref[...] = jnp.zeros_like(nrm_ref)

    half = h.shape[1] // 2
    col = lax.broadcasted_iota(jnp.int32, h.shape, 1)
    row = lax.broadcasted_iota(jnp.int32, h.shape, 0) + i * h.shape[0]
    a = jnp.where((col >= half) & (row == 0), 0.0, jnp.abs(h))
    nrm_ref[...] += jnp.sum(a, axis=0, keepdims=True)


def _filter_mlp(z, w1, b1, f1, w2, b2, f2, w3, b3, deltas, tl=512):
    l, e = z.shape
    hid = w1.shape[1]
    n = w3.shape[1]
    c = deltas.shape[0]
    full = lambda shape: pl.BlockSpec(shape, lambda i: (0, 0))
    return pl.pallas_call(
        _filter_mlp_kernel,
        grid=(l // tl,),
        in_specs=[pl.BlockSpec((tl, e), lambda i: (i, 0)),
                  full((e, hid)), full((1, hid)), full((1, hid)),
                  full((hid, hid)), full((1, hid)), full((1, hid)),
                  full((hid, n)), full((1, n)), full((1, c))],
        out_specs=[pl.BlockSpec((tl, n), lambda i: (i, 0)), full((1, n))],
        out_shape=[jax.ShapeDtypeStruct((l, n), F32), jax.ShapeDtypeStruct((1, n), F32)],
        compiler_params=_cparams(("arbitrary",)),
        name="filter_mlp",
    )(z, w1, b1.reshape(1, hid), f1.reshape(1, hid), w2, b2.reshape(1, hid), f2.reshape(1, hid),
      w3, b3.reshape(1, n), deltas.reshape(1, c))


def _split(x):
    hi = x.astype(BF16)
    lo = (x - hi.astype(F32)).astype(BF16)
    return hi, lo


def _dot3(ch, cl, x):
    xh, xl = _split(x)
    r = jnp.dot(ch, xh, preferred_element_type=F32)
    r = r + jnp.dot(ch, xl, preferred_element_type=F32)
    return r + jnp.dot(cl, xh, preferred_element_type=F32)


def _cdot3(ch, cl, xr, xi):
    n = ch.shape[0] // 2
    p = _dot3(ch, cl, xr)
    q = _dot3(ch, cl, xi)
    return p[:n] - q[n:], q[:n] + p[n:]


FFT_GROUP = 8


def _fft1_kernel(x_ref, fh_ref, fl_ref, ar_ref, ai_ref, *, kn1, n2):
    rh = ar_ref.shape[0] // n2
    for g in range(n2 // FFT_GROUP):
        cols = [x_ref[pl.ds(g * FFT_GROUP + s, kn1, stride=n2), :] for s in range(FFT_GROUP)]
        out = _dot3(fh_ref[...], fl_ref[...], jnp.concatenate(cols, axis=1))
        for s in range(FFT_GROUP):
            rows = pl.ds(g * FFT_GROUP + s, rh, stride=n2)
            ar_ref[rows, :] = out[:rh, s * LANES:(s + 1) * LANES]
            ai_ref[rows, :] = out[rh:, s * LANES:(s + 1) * LANES]


def _fft_stage1(x, f1h, f1l, plan, l, nb, row_blk0, col_blk0, c):
    n2, rp, kh = plan["n2"], plan["rp"], plan["kh"]
    rh = rp // kh
    kn1 = f1h.shape[2]
    out = jax.ShapeDtypeStruct((nb, rp * n2, c), F32)
    fspec = pl.BlockSpec((None, 2 * rh, kn1), lambda b, j, k: (k, 0, 0))
    ospec = pl.BlockSpec((None, rh * n2, LANES), lambda b, j, k: (b, k, j))
    return pl.pallas_call(
        functools.partial(_fft1_kernel, kn1=kn1, n2=n2),
        grid=(nb, c // LANES, kh),
        in_specs=[pl.BlockSpec((l, LANES), lambda b, j, k: (row_blk0 + b, col_blk0 + j)), fspec, fspec],
        out_specs=[ospec, ospec],
        out_shape=[out, out],
        compiler_params=_cparams(("parallel", "parallel", "arbitrary")),
        name="fft_stage1",
    )(x, f1h, f1l)


def _filter_mid_kernel(fr_ref, fi_ref, br_ref, bi_ref, wh_ref, wl_ref, inv_ref, b0_ref, kr_ref, ki_ref, *, nslab):
    k1 = pl.program_id(0)

    @pl.when(k1 < nslab)
    def _():
        fr, fi = _cdot3(wh_ref[...], wl_ref[...], fr_ref[...], fi_ref[...])
        br, bi = _cdot3(wh_ref[...], wl_ref[...], br_ref[...], bi_ref[...])
        kr_ref[...] = (fr + (br - b0_ref[...])) * inv_ref[...]
        ki_ref[...] = (fi - bi) * inv_ref[...]

    @pl.when(k1 >= nslab)
    def _():
        kr_ref[...] = jnp.zeros_like(kr_ref)
        ki_ref[...] = jnp.zeros_like(ki_ref)


def _filter_mid(ar, ai, wh, wl, inv, b0, nslab, ct=512):
    rp, n2, c2 = ar.shape
    oc = c2 // 2
    fwd = pl.BlockSpec((None, n2, ct), lambda k, j: (k, 0, j))
    bwd = pl.BlockSpec((None, n2, ct), lambda k, j: (k, 0, j + oc // ct))
    wspec = pl.BlockSpec((None, 2 * n2, n2), lambda k, j: (k, 0, 0))
    vec = pl.BlockSpec((1, ct), lambda k, j: (0, j))
    out = jax.ShapeDtypeStruct((rp, n2, oc), F32)
    return pl.pallas_call(
        functools.partial(_filter_mid_kernel, nslab=nslab),
        grid=(rp, oc // ct),
        in_specs=[fwd, fwd, bwd, bwd, wspec, wspec, vec, vec],
        out_specs=[fwd, fwd],
        out_shape=[out, out],
        compiler_params=_cparams(("parallel", "arbitrary")),
        name="filter_mid",
    )(ar, ai, ar, ai, wh, wl, inv, b0)


def _conv_mid_kernel(ar_ref, ai_ref, wfh_ref, wfl_ref, wih_ref, wil_ref, kr_ref, ki_ref, dr_ref, di_ref, *, nslab):
    k1 = pl.program_id(1)

    @pl.when(k1 < nslab)
    def _():
        xr, xi = _cdot3(wfh_ref[...], wfl_ref[...], ar_ref[...], ai_ref[...])
        kr, ki = kr_ref[...], ki_ref[...]
        yr = xr * kr - xi * ki
        yi = xr * ki + xi * kr
        dr, di = _cdot3(wih_ref[...], wil_ref[...], yr, yi)
        dr_ref[...] = dr
        di_ref[...] = di

    @pl.when(k1 >= nslab)
    def _():
        dr_ref[...] = jnp.zeros_like(dr_ref)
        di_ref[...] = jnp.zeros_like(di_ref)


def _conv_mid(ar, ai, wfh, wfl, wih, wil, kr, ki, order, nslab, ct=512):
    b, rp, n2, c = ar.shape
    koff = order * (c // ct)
    slab = pl.BlockSpec((None, None, n2, ct), lambda bb, k, j: (bb, k, 0, j))
    wspec = pl.BlockSpec((None, 2 * n2, n2), lambda bb, k, j: (k, 0, 0))
    kspec = pl.BlockSpec((None, n2, ct), lambda bb, k, j: (k, 0, j + koff))
    out = jax.ShapeDtypeStruct((b, rp, n2, c), F32)
    return pl.pallas_call(
        functools.partial(_conv_mid_kernel, nslab=nslab),
        grid=(b, rp, c // ct),
        in_specs=[slab, slab, wspec, wspec, wspec, wspec, kspec, kspec],
        out_specs=[slab, slab],
        out_shape=[out, out],
        compiler_params=_cparams(("parallel", "parallel", "arbitrary")),
        name="conv_mid",
    )(ar, ai, wfh, wfl, wih, wil, kr, ki)


def _fft_out_kernel(dr_ref, di_ref, gh_ref, gl_ref, o_ref, *, nh, n2):
    rh = dr_ref.shape[0] // n2

    @pl.when(pl.program_id(2) == 0)
    def _():
        o_ref[...] = jnp.zeros_like(o_ref)

    for g in range(n2 // FFT_GROUP):
        cols = []
        for s in range(FFT_GROUP):
            rows = pl.ds(g * FFT_GROUP + s, rh, stride=n2)
            cols.append(jnp.concatenate([dr_ref[rows, :], di_ref[rows, :]], axis=0))
        y = _dot3(gh_ref[...], gl_ref[...], jnp.concatenate(cols, axis=1))
        for s in range(FFT_GROUP):
            rows = pl.ds(g * FFT_GROUP + s, nh, stride=n2)
            o_ref[rows, :] += y[:, s * LANES:(s + 1) * LANES]


def _fft_out(dr, di, gh, gl, plan, l):
    nb, _, c = dr.shape
    n2, rp, kh = plan["n2"], plan["rp"], plan["kh"]
    rh = rp // kh
    nh = gh.shape[1]
    dspec = pl.BlockSpec((None, rh * n2, LANES), lambda b, j, k: (b, k, j))
    gspec = pl.BlockSpec((None, nh, 2 * rh), lambda b, j, k: (k, 0, 0))
    return pl.pallas_call(
        functools.partial(_fft_out_kernel, nh=nh, n2=n2),
        grid=(nb, c // LANES, kh),
        in_specs=[dspec, dspec, gspec, gspec],
        out_specs=pl.BlockSpec((l, LANES), lambda b, j, k: (b, j)),
        out_shape=jax.ShapeDtypeStruct((nb * l, c), F32),
        compiler_params=_cparams(("parallel", "parallel", "arbitrary")),
        name="fft_out",
    )(dr, di, gh, gl)


def _gate_kernel(y_ref, u_ref, g_ref, bias_ref, o_ref):
    o_ref[...] = (g_ref[...] * (y_ref[...] + u_ref[...] * bias_ref[...])).astype(o_ref.dtype)


def _gate(y, u, u_row0, u_col0, gate, g_row0, g_col0, bias, out_dtype, tr=512, tc=512):
    m, c = y.shape
    return pl.pallas_call(
        _gate_kernel,
        grid=(m // tr, c // tc),
        in_specs=[pl.BlockSpec((tr, tc), lambda i, j: (i, j)),
                  pl.BlockSpec((tr, tc), lambda i, j: (i + u_row0, j + u_col0)),
                  pl.BlockSpec((tr, tc), lambda i, j: (i + g_row0, j + g_col0)),
                  pl.BlockSpec((1, tc), lambda i, j: (0, j))],
        out_specs=pl.BlockSpec((tr, tc), lambda i, j: (i, j)),
        out_shape=jax.ShapeDtypeStruct((m, c), out_dtype),
        compiler_params=_cparams(("parallel", "arbitrary")),
        name="hyena_gate",
    )(y, u, gate, bias)


def _np_split(a):
    a32 = np.asarray(a, np.float32)
    hi = a32.astype(BF16)
    lo = (a32 - hi.astype(np.float32)).astype(BF16)
    return hi, lo


def _fft_plan(l):
    n = 2 * l
    n2 = FFT_N2
    n1 = n // n2
    r = n1 // 2 + 1
    rp = -(-r // 16) * 16
    kh = -(-rp // 48)
    rh = rp // kh
    kn1 = n1 // 2
    k1 = np.arange(rp, dtype=np.float64)[:, None]
    live = (k1 < r).astype(np.float64)

    ang = 2 * np.pi * k1 * np.arange(kn1)[None, :] / n1
    f1 = np.concatenate([(np.cos(ang) * live).reshape(kh, rh, kn1),
                         (-np.sin(ang) * live).reshape(kh, rh, kn1)], axis=1)

    kk = np.arange(rp, dtype=np.float64)[None, :]
    wgt = np.where((kk == 0) | (kk == n1 // 2), 1.0, 2.0) * (kk < r) / n
    ango = 2 * np.pi * np.arange(n1 // 2)[:, None] * kk / n1
    gre = (np.cos(ango) * wgt).reshape(n1 // 2, kh, rh).transpose(1, 0, 2)
    gim = (-np.sin(ango) * wgt).reshape(n1 // 2, kh, rh).transpose(1, 0, 2)
    g = np.concatenate([gre, gim], axis=2)

    ik1 = jnp.arange(rp, dtype=jnp.int32)[:, None, None]
    ia = jnp.arange(n2, dtype=jnp.int32)[None, :, None]
    ib = jnp.arange(n2, dtype=jnp.int32)[None, None, :]
    livej = (ik1 < r).astype(F32)

    def slab_matrices(idx, sign):
        ang = (idx % n).astype(F32) * F32(2.0 * math.pi / n)
        w = jnp.concatenate([jnp.cos(ang), sign * jnp.sin(ang)], axis=1) * livej
        hi = w.astype(BF16)
        return hi, (w - hi.astype(F32)).astype(BF16)

    wf = slab_matrices(n1 * ia * ib + ik1 * ib, -1.0)
    wi = slab_matrices(n1 * ia * ib + ik1 * ia, 1.0)
    return dict(n1=n1, n2=n2, r=r, rp=rp, kh=kh, f1=_np_split(f1), g=_np_split(g), wf=wf, wi=wi)


def _filter_features(l):
    t = jnp.linspace(0.0, 1.0, l, dtype=F32)[:, None]
    bands = jnp.linspace(1e-4, B_BANDS - 1, B_BANDS, dtype=F32)[None, :]
    w = 2.0 * math.pi * jnp.arange(l, dtype=F32)[:, None] / l
    return jnp.concatenate([t, jnp.cos(bands * w), -jnp.sin(bands * w)], axis=-1)


def _hyena_filters(plan, l, c, w1, b1, f1, w2, b2, f2, w3, b3):
    n1, n2 = plan["n1"], plan["n2"]
    z = _filter_features(l)
    e = z.shape[1]
    ep = -(-e // 16) * 16
    z = jnp.pad(z, ((0, 0), (0, ep - e)))
    w1p = jnp.pad(w1, ((0, ep - e), (0, 0))).astype(BF16)
    max_decay = math.log(B_DECAY_TARGET) / B_FAST_DECAY_PCT
    min_decay = math.log(B_DECAY_TARGET) / B_SLOW_DECAY_PCT
    deltas = jnp.abs(jnp.linspace(min_decay, max_decay, c, dtype=F32))
    h, nrm = _filter_mlp(z, w1p, b1, f1, w2.astype(BF16), b2, f2, w3.astype(BF16), b3, deltas,
                         tl=min(512, l))
    oc = h.shape[1] // 2
    inv = 1.0 / (nrm[:, :oc] + nrm[:, oc:])
    b0 = h[0:1, oc:]
    rp = plan["rp"]
    ar, ai = _fft_stage1(h, *plan["f1"], plan, l, 1, 0, 0, 2 * oc)
    return _filter_mid(ar.reshape(rp, n2, 2 * oc), ai.reshape(rp, n2, 2 * oc), *plan["wf"], inv, b0, plan["r"])


def _hyena_conv(plan, kf, order, l, nb, u, u_row0, u_col0, gate, g_row0, g_col0, bias, c, out_dtype):
    n2, rp = plan["n2"], plan["rp"]
    cb = c // LANES
    ar, ai = _fft_stage1(u, *plan["f1"], plan, l, nb, u_row0 // l, u_col0 * cb, c)
    dr, di = _conv_mid(ar.reshape(nb, rp, n2, c), ai.reshape(nb, rp, n2, c), *plan["wf"], *plan["wi"],
                       kf[0], kf[1], order, plan["r"])
    y = _fft_out(dr.reshape(nb, rp * n2, c), di.reshape(nb, rp * n2, c), *plan["g"], plan, l)
    tr = tc = 512
    return _gate(y, u, u_row0 // tr, u_col0 * (c // tc), gate, g_row0 // tr, g_col0 * (c // tc),
                 bias.reshape(1, c), out_dtype, tr=tr, tc=tc)


def _hyena_mixer(u, seqs, plans, c, fw, hy_bias):
    outs = []
    for (row0, nb, l) in seqs:
        plan = plans[l]
        kf = _hyena_filters(plan, l, c, *fw)
        z = _hyena_conv(plan, kf, 0, l, nb, u, row0, 0, u, row0, 1, hy_bias[0], c, F32)
        o = _hyena_conv(plan, kf, 1, l, nb, z, 0, 0, u, row0, 2, hy_bias[1], c, BF16)
        outs.append(o)
    return jnp.concatenate(outs, axis=0)


def _rope_tables(pos, hd):
    rot = hd // ROPE_FRACTION
    half = rot // 2
    inv = ROPE_THETA ** (-(jnp.arange(half, dtype=F32) * 2.0 / rot))
    ang = pos[:, None] * inv[None, :]
    cos, sin = jnp.cos(ang), jnp.sin(ang)
    m = pos.shape[0]
    one = jnp.ones((m, hd - rot), F32)
    zero = jnp.zeros((m, hd - rot), F32)
    zh = jnp.zeros((m, half), F32)
    c = jnp.concatenate([cos, cos, one], axis=1)
    s1 = jnp.concatenate([-sin, zh, zero], axis=1)
    s2 = jnp.concatenate([zh, sin, zero], axis=1)
    rep = LANES // hd
    return tuple(jnp.tile(t, (1, rep)) for t in (c, s1, s2)), half


def _trunk(x, bounds, seqs, p):
    m, dm = x.shape
    pos = jnp.concatenate([jnp.tile(jnp.arange(l, dtype=F32), nb) for (_, nb, l) in seqs])
    tabs_a, half_a = _rope_tables(pos, A_HEAD_DIM)
    tabs_c, half_c = _rope_tables(pos, C_HEAD_DIM)
    ident = (jnp.ones((m, LANES), F32), jnp.zeros((m, LANES), F32), jnp.zeros((m, LANES), F32))
    tabs_kv = tuple(jnp.concatenate([a, b], axis=1) for a, b in zip(tabs_a, ident))
    c_hy = dm - A_WIDTH
    plans = {l: _fft_plan(l) for l in sorted({l for (_, _, l) in seqs})}
    xf = x
    xb = x.astype(BF16)
    for i in range(DEPTH):
        j = i // 2
        if i % 2 == 0:
            w_in = p['mix_e_w_in'][j].astype(BF16)
            kv0 = A_WIDTH
            hy0 = A_WIDTH + 2 * A_KV_WIDTH
            q = _matmul_rope(xb, w_in[:, :kv0], tabs_a, half_a, tn=512)
            kv = _matmul_rope(xb, w_in[:, kv0:hy0], tabs_kv, half_a, tn=2 * A_KV_WIDTH)
            hy = _matmul(xb, w_in[:, hy0:], tn=512, out_dtype=F32)
            a_out = _even_attention(q, kv, p['a_sink'][j], bounds)
            u = _short_conv(hy, p['hy_conv_w'][j], p['hy_conv_b'][j], bounds)
            fw = (p['hy_w1'][j], p['hy_b1'][j], p['hy_f1'][j], p['hy_w2'][j], p['hy_b2'][j],
                  p['hy_f2'][j], p['hy_w3'][j], p['hy_b3'][j])
            h_out = _hyena_mixer(u, seqs, plans, c_hy, fw, p['hy_bias'][j])
            w_out = p['mix_e_w_out'][j].astype(BF16)
            xf, xb = _matmul_ln([a_out, h_out], [w_out[:A_WIDTH], w_out[A_WIDTH:]], xf,
                                p['ln1_g'][i], p['ln1_b'][i])
        else:
            w_in = p['mix_o_w_in'][j].astype(BF16)
            gw = C_HEADS * C_HEAD_DIM
            ng = len(C_DILATIONS)
            qkv = []
            for g, d in enumerate(C_DILATIONS):
                trio = []
                for part in range(3):
                    c0 = (part * ng + g) * gw
                    trio.append(_odd_proj(xb, w_in[:, c0:c0 + gw], tabs_c, half_c, d, rope=part < 2))
                qkv.append(tuple(trio))
            o = _odd_attention(qkv, bounds, m)
            xf, xb = _matmul_ln([o], [p['mix_o_w_out'][j].astype(BF16)], xf, p['ln1_g'][i], p['ln1_b'][i])
        xf, xb = _ffn_ln(xb, xf, p['ffn_w_gate'][i].astype(BF16), p['ffn_w_up'][i].astype(BF16),
                         p['ffn_w_down'][i].astype(BF16), p['ln2_g'][i], p['ln2_b'][i])
    return xf


def kernel(x_prompt, x_sample, mix_e_w_in, a_sink, hy_conv_w, hy_conv_b, hy_w1, hy_b1, hy_f1, hy_w2, hy_b2,
           hy_f2, hy_w3, hy_b3, hy_bias, mix_e_w_out, mix_o_w_in, mix_o_w_out, ffn_w_gate, ffn_w_up,
           ffn_w_down, ln1_g, ln1_b, ln2_g, ln2_b):
    p = dict(mix_e_w_in=mix_e_w_in, a_sink=a_sink, hy_conv_w=hy_conv_w, hy_conv_b=hy_conv_b,
             hy_w1=hy_w1, hy_b1=hy_b1, hy_f1=hy_f1, hy_w2=hy_w2, hy_b2=hy_b2, hy_f2=hy_f2,
             hy_w3=hy_w3, hy_b3=hy_b3, hy_bias=hy_bias, mix_e_w_out=mix_e_w_out,
             mix_o_w_in=mix_o_w_in, mix_o_w_out=mix_o_w_out, ffn_w_gate=ffn_w_gate,
             ffn_w_up=ffn_w_up, ffn_w_down=ffn_w_down, ln1_g=ln1_g, ln1_b=ln1_b,
             ln2_g=ln2_g, ln2_b=ln2_b)
    dm = x_prompt.shape[-1]
    seqs, bounds, row = [], [0], 0
    for xs in (x_prompt, x_sample):
        nb, l = xs.shape[0], xs.shape[1]
        seqs.append((row, nb, l))
        for _ in range(nb):
            row += l
            bounds.append(row)
    x = jnp.concatenate([x_prompt.reshape(-1, dm), x_sample.reshape(-1, dm)], axis=0)
    y = _trunk(x, tuple(bounds), tuple(seqs), p)
    n_p = x_prompt.shape[0] * x_prompt.shape[1]
    return (y[:n_p].reshape(x_prompt.shape), y[n_p:].reshape(x_sample.shape))
```

```python
import functools
import math

import numpy as np
import jax
import jax.numpy as jnp
from jax import lax
from jax.experimental import pallas as pl
from jax.experimental.pallas import tpu as pltpu

F32 = jnp.float32
BF16 = jnp.bfloat16

DEPTH = 4
A_HEADS, A_KV_HEADS, A_HEAD_DIM, A_RADIUS = 16, 2, 64, 128
A_WIDTH = A_HEADS * A_HEAD_DIM
A_KV_WIDTH = A_KV_HEADS * A_HEAD_DIM
B_SHORT, B_EMB = 3, 33
B_BANDS = (B_EMB - 1) // 2
B_DECAY_TARGET, B_FAST_DECAY_PCT, B_SLOW_DECAY_PCT = 1e-2, 0.3, 1.5
C_HEADS, C_HEAD_DIM = 16, 128
C_DILATIONS = (1, 4, 16)
C_RADIUS = 64
ROPE_THETA, ROPE_FRACTION = 500000.0, 4
ALPHA = (2 * DEPTH) ** 0.25
LN_EPS = 1e-5

LANES = 128
VMEM_LIMIT = 56 * 1024 * 1024
FFT_N2 = 256

ODD_CHUNK = 2048
ATT_TQ = 256


def _cparams(sem):
    return pltpu.CompilerParams(dimension_semantics=sem, vmem_limit_bytes=VMEM_LIMIT)


def _seq_bounds(row, bounds):
    start = jnp.int32(bounds[0])
    end = jnp.int32(bounds[1])
    for b0, b1 in zip(bounds[1:-1], bounds[2:]):
        inside = row >= b0
        start = jnp.where(inside, jnp.int32(b0), start)
        end = jnp.where(inside, jnp.int32(b1), end)
    return start, end


def _rope(a, c, s1, s2, half):
    w = a.shape[-1]
    return a * c + pltpu.roll(a, w - half, 1) * s1 + pltpu.roll(a, half, 1) * s2


def _mm_kernel(x_ref, w_ref, o_ref):
    o_ref[...] = jnp.dot(x_ref[...], w_ref[...], preferred_element_type=F32).astype(o_ref.dtype)


def _matmul(x, w, tn, out_dtype, tm=1024):
    m, k = x.shape
    n = w.shape[1]
    return pl.pallas_call(
        _mm_kernel,
        grid=(m // tm, n // tn),
        in_specs=[pl.BlockSpec((tm, k), lambda i, j: (i, 0)),
                  pl.BlockSpec((k, tn), lambda i, j: (0, j))],
        out_specs=pl.BlockSpec((tm, tn), lambda i, j: (i, j)),
        out_shape=jax.ShapeDtypeStruct((m, n), out_dtype),
        compiler_params=_cparams(("parallel", "arbitrary")),
        name="matmul",
    )(x, w)


def _mm_rope_kernel(x_ref, w_ref, c_ref, s1_ref, s2_ref, o_ref, *, half):
    acc = jnp.dot(x_ref[...], w_ref[...], preferred_element_type=F32)
    tw = c_ref.shape[1]
    rc = 256
    for r0 in range(0, acc.shape[0], rc):
        rows = slice(r0, r0 + rc)
        for c in range(acc.shape[1] // tw):
            cols = slice(c * tw, (c + 1) * tw)
            o_ref[rows, cols] = _rope(acc[rows, cols], c_ref[rows, :], s1_ref[rows, :], s2_ref[rows, :],
                                      half).astype(o_ref.dtype)


def _matmul_rope(x, w, tabs, half, tn, tm=1024):
    m, k = x.shape
    n = w.shape[1]
    tw = tabs[0].shape[1]
    tab_spec = pl.BlockSpec((tm, tw), lambda i, j: (i, 0))
    return pl.pallas_call(
        functools.partial(_mm_rope_kernel, half=half),
        grid=(m // tm, n // tn),
        in_specs=[pl.BlockSpec((tm, k), lambda i, j: (i, 0)),
                  pl.BlockSpec((k, tn), lambda i, j: (0, j)),
                  tab_spec, tab_spec, tab_spec],
        out_specs=pl.BlockSpec((tm, tn), lambda i, j: (i, j)),
        out_shape=jax.ShapeDtypeStruct((m, n), BF16),
        compiler_params=_cparams(("parallel", "arbitrary")),
        name="matmul_rope",
    )(x, w, *tabs)


def _odd_proj_kernel(x_ref, w_ref, c_ref, s1_ref, s2_ref, o_ref, acc_ref, *, d, rope, half):
    hps = w_ref.shape[1] // LANES
    tm = x_ref.shape[0]
    t = tm // d
    pair = 2
    rc = 256
    for p in range(hps // pair):
        acc = jnp.dot(x_ref[...], w_ref[:, p * pair * LANES:(p + 1) * pair * LANES], preferred_element_type=F32)
        for h2 in range(pair):
            hh = p * pair + h2
            slot = (p % 2) * pair + h2
            acc_ref[slot, :, :] = acc[:, h2 * LANES:(h2 + 1) * LANES]
            for c0 in range(0, tm, rc):
                a = acc_ref[slot, c0:c0 + rc, :]
                if rope:
                    a = _rope(a, c_ref[c0:c0 + rc, :], s1_ref[c0:c0 + rc, :], s2_ref[c0:c0 + rc, :], half)
                if d == 1:
                    o_ref[hh, 0, c0:c0 + rc, :] = a.astype(BF16)
                elif rope:
                    acc_ref[slot, c0:c0 + rc, :] = a
            if d > 1:
                for r in range(d):
                    o_ref[hh, r, :, :] = acc_ref[slot, pl.ds(r, t, stride=d), :].astype(BF16)


def _odd_proj(x, w, col0, tabs, half, d, rope, hps=8):
    m, k = x.shape
    tm = ODD_CHUNK
    cb0 = col0 // (hps * LANES)
    tab_spec = pl.BlockSpec((tm, LANES), lambda i, j: (i, 0))
    return pl.pallas_call(
        functools.partial(_odd_proj_kernel, d=d, rope=rope, half=half),
        grid=(m // tm, C_HEADS // hps),
        in_specs=[pl.BlockSpec((tm, k), lambda i, j: (i, 0)),
                  pl.BlockSpec((k, hps * LANES), lambda i, j: (0, cb0 + j)),
                  tab_spec, tab_spec, tab_spec],
        out_specs=pl.BlockSpec((hps, d, tm // d, LANES), lambda i, j: (j, 0, i, 0)),
        out_shape=jax.ShapeDtypeStruct((C_HEADS, d, m // d, LANES), BF16),
        scratch_shapes=[pltpu.VMEM((4, tm, LANES), F32)],
        compiler_params=_cparams(("parallel", "arbitrary")),
        name="odd_proj",
    )(x, w, *tabs)


LN_ROWS = 128


def _layer_norm_store(x_ref, acc_ref, g_ref, b_ref, of_ref, ob_ref):
    for c in range(x_ref.shape[0] // LN_ROWS):
        rows = pl.ds(c * LN_ROWS, LN_ROWS)
        r = ALPHA * x_ref[rows, :] + acc_ref[rows, :]
        mu = jnp.mean(r, axis=-1, keepdims=True)
        xc = r - mu
        var = jnp.mean(xc * xc, axis=-1, keepdims=True)
        y = xc * lax.rsqrt(var + LN_EPS) * g_ref[...] + b_ref[...]
        of_ref[rows, :] = y
        ob_ref[rows, :] = y.astype(BF16)


def _mm_ln_kernel(*refs, n_in):
    ys = refs[:n_in]
    ws = refs[n_in:2 * n_in]
    x_ref, g_ref, b_ref, of_ref, ob_ref, acc_ref = refs[2 * n_in:]
    acc = jnp.dot(ys[0][...], ws[0][...], preferred_element_type=F32)
    for y_ref, w_ref in zip(ys[1:], ws[1:]):
        acc = acc + jnp.dot(y_ref[...], w_ref[...], preferred_element_type=F32)
    acc_ref[...] = acc
    _layer_norm_store(x_ref, acc_ref, g_ref, b_ref, of_ref, ob_ref)


def _matmul_ln(ys, ws, x, g, b, tm=256):
    m, dm = x.shape
    n_in = len(ys)
    in_specs = [pl.BlockSpec((tm, y.shape[1]), lambda i: (i, 0)) for y in ys]
    in_specs += [pl.BlockSpec(w.shape, lambda i: (0, 0)) for w in ws]
    in_specs += [pl.BlockSpec((tm, dm), lambda i: (i, 0)),
                 pl.BlockSpec((1, dm), lambda i: (0, 0)),
                 pl.BlockSpec((1, dm), lambda i: (0, 0))]
    return pl.pallas_call(
        functools.partial(_mm_ln_kernel, n_in=n_in),
        grid=(m // tm,),
        in_specs=in_specs,
        out_specs=[pl.BlockSpec((tm, dm), lambda i: (i, 0)), pl.BlockSpec((tm, dm), lambda i: (i, 0))],
        out_shape=[jax.ShapeDtypeStruct((m, dm), F32), jax.ShapeDtypeStruct((m, dm), BF16)],
        scratch_shapes=[pltpu.VMEM((tm, dm), F32)],
        compiler_params=_cparams(("parallel",)),
        name="matmul_ln",
    )(*ys, *ws, x, g.reshape(1, dm), b.reshape(1, dm))


def _ffn_kernel(xb_ref, xf_ref, wg_ref, wu_ref, wd_ref, g_ref, b_ref, of_ref, ob_ref, acc_ref):
    j = pl.program_id(1)

    @pl.when(j == 0)
    def _():
        acc_ref[...] = jnp.zeros_like(acc_ref)

    xb = xb_ref[...]
    gate = jnp.dot(xb, wg_ref[...], preferred_element_type=F32)
    up = jnp.dot(xb, wu_ref[...], preferred_element_type=F32)
    h = (gate * jax.nn.sigmoid(gate)) * up
    acc_ref[...] += jnp.dot(h.astype(BF16), wd_ref[...], preferred_element_type=F32)

    @pl.when(j == pl.num_programs(1) - 1)
    def _():
        _layer_norm_store(xf_ref, acc_ref, g_ref, b_ref, of_ref, ob_ref)


def _ffn_ln(xb, xf, wg, wu, wd, g, b, tm=512, tf=512):
    m, dm = xf.shape
    dff = wg.shape[1]
    row = lambda i, j: (i, 0)
    return pl.pallas_call(
        _ffn_kernel,
        grid=(m // tm, dff // tf),
        in_specs=[pl.BlockSpec((tm, dm), row), pl.BlockSpec((tm, dm), row),
                  pl.BlockSpec((dm, tf), lambda i, j: (0, j)),
                  pl.BlockSpec((dm, tf), lambda i, j: (0, j)),
                  pl.BlockSpec((tf, dm), lambda i, j: (j, 0)),
                  pl.BlockSpec((1, dm), lambda i, j: (0, 0)),
                  pl.BlockSpec((1, dm), lambda i, j: (0, 0))],
        out_specs=[pl.BlockSpec((tm, dm), row), pl.BlockSpec((tm, dm), row)],
        out_shape=[jax.ShapeDtypeStruct((m, dm), F32), jax.ShapeDtypeStruct((m, dm), BF16)],
        scratch_shapes=[pltpu.VMEM((tm, dm), F32)],
        compiler_params=_cparams(("parallel", "arbitrary")),
        name="ffn_ln",
    )(xb, xf, wg, wu, wd, g.reshape(1, dm), b.reshape(1, dm))


def _even_attn_kernel(sink_ref, q_ref, kp_ref, km_ref, kn_ref, o_ref, *, bounds):
    i = pl.program_id(0)
    tq = q_ref.shape[0]
    nk = tq + 2 * A_RADIUS
    row0 = i * tq
    start, end = _seq_bounds(row0, bounds)
    kv = jnp.concatenate([kp_ref[...], km_ref[...], kn_ref[...]], axis=0)
    rq = row0 + lax.broadcasted_iota(jnp.int32, (tq, nk), 0)
    rk = row0 - A_RADIUS + lax.broadcasted_iota(jnp.int32, (tq, nk), 1)
    mask = (jnp.abs(rk - rq) <= A_RADIUS) & (rk >= start) & (rk < end)
    group = A_HEADS // A_KV_HEADS
    scale = A_HEAD_DIM ** -0.5
    for j in range(A_KV_HEADS):
        k = kv[:, j * A_HEAD_DIM:(j + 1) * A_HEAD_DIM]
        v = kv[:, A_KV_WIDTH + j * A_HEAD_DIM:A_KV_WIDTH + (j + 1) * A_HEAD_DIM]
        for gq in range(group):
            h = j * group + gq
            qh = q_ref[:, h * A_HEAD_DIM:(h + 1) * A_HEAD_DIM]
            s = lax.dot_general(qh, k, (((1,), (1,)), ((), ())), preferred_element_type=F32) * scale
            s = jnp.where(mask, s, -jnp.inf)
            sk = sink_ref[h]
            m = jnp.maximum(jnp.max(s, axis=-1, keepdims=True), sk)
            p = jnp.exp(s - m)
            den = jnp.sum(p, axis=-1, keepdims=True) + jnp.exp(sk - m)
            o = jnp.dot(p.astype(BF16), v, preferred_element_type=F32) / den
            o_ref[:, h * A_HEAD_DIM:(h + 1) * A_HEAD_DIM] = o.astype(o_ref.dtype)


def _even_attention(q, kv, sink, bounds):
    m = q.shape[0]
    tq = ATT_TQ
    hb = A_RADIUS
    per = tq // hb
    last = m // hb - 1
    kvw = kv.shape[1]
    return pl.pallas_call(
        functools.partial(_even_attn_kernel, bounds=bounds),
        grid=(m // tq,),
        in_specs=[pl.BlockSpec(memory_space=pltpu.SMEM),
                  pl.BlockSpec((tq, A_WIDTH), lambda i: (i, 0)),
                  pl.BlockSpec((hb, kvw), lambda i: (jnp.maximum(i * per - 1, 0), 0)),
                  pl.BlockSpec((tq, kvw), lambda i: (i, 0)),
                  pl.BlockSpec((hb, kvw), lambda i: (jnp.minimum((i + 1) * per, last), 0))],
        out_specs=pl.BlockSpec((tq, A_WIDTH), lambda i: (i, 0)),
        out_shape=jax.ShapeDtypeStruct((m, A_WIDTH), BF16),
        compiler_params=_cparams(("parallel",)),
        name="even_attention",
    )(sink, q, kv, kv, kv)


def _odd_attn_kernel(*refs, bounds):
    ng = len(C_DILATIONS)
    o_ref, oacc, lacc = refs[7 * ng:]
    i = pl.program_id(0)
    chunk = ODD_CHUNK
    qb = 128
    nk = qb + 2 * C_RADIUS
    row0 = i * chunk
    start, end = _seq_bounds(row0, bounds)
    rr = lax.broadcasted_iota(jnp.int32, (qb, nk), 0)
    cc = lax.broadcasted_iota(jnp.int32, (qb, nk), 1)
    band = jnp.abs(cc - C_RADIUS - rr) <= C_RADIUS
    scale = C_HEAD_DIM ** -0.5
    for g, d in enumerate(C_DILATIONS):
        q_ref, kp_ref, km_ref, kn_ref, vp_ref, vm_ref, vn_ref = refs[7 * g:7 * g + 7]
        tg = chunk // d
        t_lo, t_hi, t_c0 = start // d, end // d, row0 // d
        for sb in range(tg // qb):
            lo, hi = qb * sb - C_RADIUS, qb * sb + qb + C_RADIUS
            tk = t_c0 + lo + cc
            mask = band & (tk >= t_lo) & (tk < t_hi)
            for r in range(d):
                def window(p_ref, m_ref, n_ref):
                    parts = []
                    if lo < 0:
                        parts.append(p_ref[0, r, :, :])
                    parts.append(m_ref[0, r, max(lo, 0):min(hi, tg), :])
                    if hi > tg:
                        parts.append(n_ref[0, r, :, :])
                    return parts[0] if len(parts) == 1 else jnp.concatenate(parts, axis=0)

                q = q_ref[0, r, qb * sb:qb * (sb + 1), :]
                k = window(kp_ref, km_ref, kn_ref)
                v = window(vp_ref, vm_ref, vn_ref)
                s = lax.dot_general(q, k, (((1,), (1,)), ((), ())), preferred_element_type=F32) * scale
                s = jnp.where(mask, s, -jnp.inf)
                m = jnp.max(s, axis=-1, keepdims=True)
                p = jnp.exp(s - m)
                den = jnp.sum(p, axis=-1, keepdims=True)
                o = jnp.dot(p.astype(BF16), v, preferred_element_type=F32) / den
                lse = jnp.broadcast_to(m + jnp.log(den), (qb, LANES))
                if d == 1:
                    rows = pl.ds(qb * sb, qb)
                else:
                    rows = pl.ds(r + d * qb * sb, qb, stride=d)
                oacc[g, rows, :] = o
                lacc[g, rows, :] = lse
    ls = [lacc[g] for g in range(ng)]
    mx = functools.reduce(jnp.maximum, ls)
    ws = [jnp.exp(l - mx) for l in ls]
    tot = functools.reduce(lambda a, b: a + b, ws)
    out = functools.reduce(lambda a, b: a + b, [(ws[g] / tot) * oacc[g] for g in range(ng)])
    o_ref[...] = out.astype(o_ref.dtype)


def _odd_attention(qkv, bounds, m):
    chunk = ODD_CHUNK
    hb = C_RADIUS
    operands, in_specs = [], []
    for (q, k, v), d in zip(qkv, C_DILATIONS):
        tg = chunk // d
        per = tg // hb
        last = m // d // hb - 1
        main = pl.BlockSpec((1, d, tg, LANES), lambda i, h: (h, 0, i, 0))
        prev = pl.BlockSpec((1, d, hb, LANES), lambda i, h, per=per: (h, 0, jnp.maximum(i * per - 1, 0), 0))
        nxt = pl.BlockSpec((1, d, hb, LANES), lambda i, h, per=per, last=last: (h, 0, jnp.minimum((i + 1) * per, last), 0))
        operands += [q, k, k, k, v, v, v]
        in_specs += [main, prev, main, nxt, prev, main, nxt]
    ng = len(C_DILATIONS)
    return pl.pallas_call(
        functools.partial(_odd_attn_kernel, bounds=bounds),
        grid=(m // chunk, C_HEADS),
        in_specs=in_specs,
        out_specs=pl.BlockSpec((chunk, LANES), lambda i, h: (i, h)),
        out_shape=jax.ShapeDtypeStruct((m, C_HEADS * C_HEAD_DIM), BF16),
        scratch_shapes=[pltpu.VMEM((ng, chunk, LANES), F32), pltpu.VMEM((ng, chunk, LANES), F32)],
        compiler_params=_cparams(("parallel", "arbitrary")),
        name="odd_attention",
    )(*operands)


def _short_conv_kernel(xp_ref, xm_ref, xn_ref, w_ref, b_ref, o_ref, *, bounds):
    i = pl.program_id(0)
    tr = xm_ref.shape[0]
    row0 = i * tr
    start, end = _seq_bounds(row0, bounds)
    x = xm_ref[...]
    hp = xp_ref.shape[0]
    before = jnp.where(row0 > start, xp_ref[hp - 1:hp, :], 0.0)
    after = jnp.where(row0 + tr < end, xn_ref[0:1, :], 0.0)
    ridx = lax.broadcasted_iota(jnp.int32, x.shape, 0)
    xl = jnp.where(ridx == 0, before, pltpu.roll(x, 1, 0))
    xr = jnp.where(ridx == tr - 1, after, pltpu.roll(x, tr - 1, 0))
    y = b_ref[...] + xl * w_ref[0:1, :]
    y = y + x * w_ref[1:2, :]
    y = y + xr * w_ref[2:3, :]
    o_ref[...] = y


def _short_conv(x, w, b, bounds, tr=512, tc=512):
    m, n = x.shape
    hb = 8
    per = tr // hb
    last = m // hb - 1
    return pl.pallas_call(
        functools.partial(_short_conv_kernel, bounds=bounds),
        grid=(m // tr, n // tc),
        in_specs=[pl.BlockSpec((hb, tc), lambda i, j: (jnp.maximum(i * per - 1, 0), j)),
                  pl.BlockSpec((tr, tc), lambda i, j: (i, j)),
                  pl.BlockSpec((hb, tc), lambda i, j: (jnp.minimum((i + 1) * per, last), j)),
                  pl.BlockSpec((B_SHORT, tc), lambda i, j: (0, j)),
                  pl.BlockSpec((1, tc), lambda i, j: (0, j))],
        out_specs=pl.BlockSpec((tr, tc), lambda i, j: (i, j)),
        out_shape=jax.ShapeDtypeStruct((m, n), F32),
        compiler_params=_cparams(("parallel", "arbitrary")),
        name="short_conv",
    )(x, x, x, w, b.reshape(1, n))


def _filter_mlp_kernel(z_ref, w1_ref, b1_ref, f1_ref, w2_ref, b2_ref, f2_ref, w3_ref, b3_ref, dl_ref,
                       h_ref, nrm_ref):
    i = pl.program_id(0)
    z = z_ref[...]
    h = jnp.sin(f1_ref[...] * (jnp.dot(z.astype(BF16), w1_ref[...], preferred_element_type=F32) + b1_ref[...]))
    h = jnp.sin(f2_ref[...] * (jnp.dot(h.astype(BF16), w2_ref[...], preferred_element_type=F32) + b2_ref[...]))
    h = jnp.dot(h.astype(BF16), w3_ref[...], preferred_element_type=F32) + b3_ref[...]
    decay = jnp.exp(-z[:, 0:1] * dl_ref[...])
    nrep = h.shape[1] // decay.shape[1]
    h = h * jnp.concatenate([decay] * nrep, axis=1)
    h_ref[...] = h

    @pl.when(i == 0)
    def _():
        nrm_ref[...] = jnp.zeros_like(nrm_ref)

    half = h.shape[1] // 2
    col = lax.broadcasted_iota(jnp.int32, h.shape, 1)
    row = lax.broadcasted_iota(jnp.int32, h.shape, 0) + i * h.shape[0]
    a = jnp.where((col >= half) & (row == 0), 0.0, jnp.abs(h))
    nrm_ref[...] += jnp.sum(a, axis=0, keepdims=True)


def _filter_mlp(z, w1, b1, f1, w2, b2, f2, w3, b3, deltas, tl=512):
    l, e = z.shape
    hid = w1.shape[1]
    n = w3.shape[1]
    c = deltas.shape[0]
    full = lambda shape: pl.BlockSpec(shape, lambda i: (0, 0))
    return pl.pallas_call(
        _filter_mlp_kernel,
        grid=(l // tl,),
        in_specs=[pl.BlockSpec((tl, e), lambda i: (i, 0)),
                  full((e, hid)), full((1, hid)), full((1, hid)),
                  full((hid, hid)), full((1, hid)), full((1, hid)),
                  full((hid, n)), full((1, n)), full((1, c))],
        out_specs=[pl.BlockSpec((tl, n), lambda i: (i, 0)), full((1, n))],
        out_shape=[jax.ShapeDtypeStruct((l, n), F32), jax.ShapeDtypeStruct((1, n), F32)],
        compiler_params=_cparams(("arbitrary",)),
        name="filter_mlp",
    )(z, w1, b1.reshape(1, hid), f1.reshape(1, hid), w2, b2.reshape(1, hid), f2.reshape(1, hid),
      w3, b3.reshape(1, n), deltas.reshape(1, c))


def _split(x):
    hi = x.astype(BF16)
    lo = (x - hi.astype(F32)).astype(BF16)
    return hi, lo


def _dot3(ch, cl, x):
    xh, xl = _split(x)
    r = jnp.dot(ch, xh, preferred_element_type=F32)
    r = r + jnp.dot(ch, xl, preferred_element_type=F32)
    return r + jnp.dot(cl, xh, preferred_element_type=F32)


FFT_GROUP = 8


FFT_N2C = 4
FFT_RH = 24
FFT_SPB = 8


def _fft1_kernel(x_ref, fh_ref, fl_ref, ar_ref, ai_ref, *, kn1, n2):
    kh, rows, _ = ar_ref.shape
    n2q = n2 // FFT_N2C
    rh = rows // n2q
    rp = kh * rh
    c0 = pl.program_id(2) * n2q
    for g in range(n2q // FFT_GROUP):
        cols = [x_ref[pl.ds(c0 + g * FFT_GROUP + s, kn1, stride=n2), :] for s in range(FFT_GROUP)]
        out = _dot3(fh_ref[...], fl_ref[...], jnp.concatenate(cols, axis=1))
        for k in range(kh):
            for s in range(FFT_GROUP):
                r0 = (g * FFT_GROUP + s) * rh
                ar_ref[k, r0:r0 + rh, :] = out[k * rh:(k + 1) * rh, s * LANES:(s + 1) * LANES]
                ai_ref[k, r0:r0 + rh, :] = out[rp + k * rh:rp + (k + 1) * rh, s * LANES:(s + 1) * LANES]


def _fft_stage1(x, f1h, f1l, plan, l, nb, row_blk0, col_blk0, c):
    n2, kh, rh = plan["n2"], plan["kh"], plan["rh"]
    kn1 = f1h.shape[1]
    rows = n2 // FFT_N2C * rh
    out = jax.ShapeDtypeStruct((nb, kh, FFT_N2C, rows, c), F32)
    fspec = pl.BlockSpec(f1h.shape, lambda b, j, q: (0, 0))
    ospec = pl.BlockSpec((None, kh, None, rows, LANES), lambda b, j, q: (b, 0, q, 0, j))
    return pl.pallas_call(
        functools.partial(_fft1_kernel, kn1=kn1, n2=n2),
        grid=(nb, c // LANES, FFT_N2C),
        in_specs=[pl.BlockSpec((l, LANES), lambda b, j, q: (row_blk0 + b, col_blk0 + j)), fspec, fspec],
        out_specs=[ospec, ospec],
        out_shape=[out, out],
        compiler_params=_cparams(("parallel", "parallel", "arbitrary")),
        name="fft_stage1",
    )(x, f1h, f1l)


def _cpair(p, n):
    return p[:n, :LANES] - p[n:, LANES:], p[:n, LANES:] + p[n:, :LANES]


def _filter_mid_kernel(fr_ref, fi_ref, br_ref, bi_ref, wh_ref, wl_ref, inv_ref, b0_ref, kr_ref, ki_ref,
                       *, nslab, rh, n2):
    k0 = pl.program_id(2) * FFT_SPB

    @pl.when(pl.program_id(1) * rh + k0 < nslab)
    def _():
        for t in range(FFT_SPB):
            rows = pl.ds(k0 + t, n2, stride=rh)
            x4 = jnp.concatenate([fr_ref[rows, :], fi_ref[rows, :], br_ref[rows, :], bi_ref[rows, :]], axis=1)
            p = _dot3(wh_ref[t], wl_ref[t], x4)
            fr, fi = _cpair(p[:, :2 * LANES], n2)
            br, bi = _cpair(p[:, 2 * LANES:], n2)
            kr_ref[t] = (fr + (br - b0_ref[...])) * inv_ref[...]
            ki_ref[t] = (fi - bi) * inv_ref[...]

    @pl.when(pl.program_id(1) * rh + k0 >= nslab)
    def _():
        kr_ref[...] = jnp.zeros_like(kr_ref)
        ki_ref[...] = jnp.zeros_like(ki_ref)


def _filter_mid(ar, ai, wh, wl, inv, b0, plan):
    n2, kh, rh, rp = plan["n2"], plan["kh"], plan["rh"], plan["rp"]
    oc = ar.shape[-1] // 2
    nj = oc // LANES
    fwd = pl.BlockSpec((None, None, n2 * rh, LANES), lambda j, kk, k: (0, kk, 0, j))
    bwd = pl.BlockSpec((None, None, n2 * rh, LANES), lambda j, kk, k: (0, kk, 0, j + nj))
    steps = rh // FFT_SPB
    wspec = pl.BlockSpec((FFT_SPB, 2 * n2, n2), lambda j, kk, k: (kk * steps + k, 0, 0))
    vec = pl.BlockSpec((1, LANES), lambda j, kk, k: (0, j))
    ospec = pl.BlockSpec((FFT_SPB, n2, LANES), lambda j, kk, k: (kk * steps + k, 0, j))
    out = jax.ShapeDtypeStruct((rp, n2, oc), F32)
    return pl.pallas_call(
        functools.partial(_filter_mid_kernel, nslab=plan["r"], rh=rh, n2=n2),
        grid=(nj, kh, steps),
        in_specs=[fwd, fwd, bwd, bwd, wspec, wspec, vec, vec],
        out_specs=[ospec, ospec],
        out_shape=[out, out],
        compiler_params=_cparams(("parallel", "arbitrary", "arbitrary")),
        name="filter_mid",
    )(ar, ai, ar, ai, wh, wl, inv, b0)


def _conv_mid_kernel(ar_ref, ai_ref, wfh_ref, wfl_ref, wih_ref, wil_ref, kr_ref, ki_ref, dr_ref, di_ref,
                     *, nslab, rh, n2):
    k0 = pl.program_id(3) * FFT_SPB

    @pl.when(pl.program_id(2) * rh + k0 < nslab)
    def _():
        for t in range(FFT_SPB):
            rows = pl.ds(k0 + t, n2, stride=rh)
            x2 = jnp.concatenate([ar_ref[rows, :], ai_ref[rows, :]], axis=1)
            xr, xi = _cpair(_dot3(wfh_ref[t], wfl_ref[t], x2), n2)
            kr, ki = kr_ref[t], ki_ref[t]
            y2 = jnp.concatenate([xr * kr - xi * ki, xr * ki + xi * kr], axis=1)
            dr, di = _cpair(_dot3(wih_ref[t], wil_ref[t], y2), n2)
            dr_ref[rows, :] = dr
            di_ref[rows, :] = di

    @pl.when(pl.program_id(2) * rh + k0 >= nslab)
    def _():
        for t in range(FFT_SPB):
            rows = pl.ds(k0 + t, n2, stride=rh)
            dr_ref[rows, :] = jnp.zeros((n2, LANES), F32)
            di_ref[rows, :] = jnp.zeros((n2, LANES), F32)


def _conv_mid(ar, ai, wfh, wfl, wih, wil, kr, ki, order, plan):
    nb, kh, _, c = ar.shape
    n2, rh = plan["n2"], plan["rh"]
    koff = order * (c // LANES)
    blk = pl.BlockSpec((None, None, n2 * rh, LANES), lambda b, j, kk, k: (b, kk, 0, j))
    steps = rh // FFT_SPB
    wspec = pl.BlockSpec((FFT_SPB, 2 * n2, n2), lambda b, j, kk, k: (kk * steps + k, 0, 0))
    kspec = pl.BlockSpec((FFT_SPB, n2, LANES), lambda b, j, kk, k: (kk * steps + k, 0, j + koff))
    out = jax.ShapeDtypeStruct(ar.shape, F32)
    return pl.pallas_call(
        functools.partial(_conv_mid_kernel, nslab=plan["r"], rh=rh, n2=n2),
        grid=(nb, c // LANES, kh, steps),
        in_specs=[blk, blk, wspec, wspec, wspec, wspec, kspec, kspec],
        out_specs=[blk, blk],
        out_shape=[out, out],
        compiler_params=_cparams(("parallel", "parallel", "arbitrary", "arbitrary")),
        name="conv_mid",
    )(ar, ai, wfh, wfl, wih, wil, kr, ki)


def _fft_out_kernel(dr_ref, di_ref, gh_ref, gl_ref, y_ref, *, n2):
    kh, rows, _ = dr_ref.shape
    n2q = n2 // FFT_N2C
    rh = rows // n2q
    n1c = y_ref.shape[0]
    n1r = y_ref.shape[1] // n2q
    for g in range(n2q // FFT_GROUP):
        cols = []
        for s in range(FFT_GROUP):
            r0 = (g * FFT_GROUP + s) * rh
            parts = [dr_ref[k, r0:r0 + rh, :] for k in range(kh)] + [di_ref[k, r0:r0 + rh, :] for k in range(kh)]
            cols.append(jnp.concatenate(parts, axis=0))
        y = _dot3(gh_ref[...], gl_ref[...], jnp.concatenate(cols, axis=1))
        for c in range(n1c):
            for s in range(FFT_GROUP):
                t0 = (g * FFT_GROUP + s) * n1r
                y_ref[c, t0:t0 + n1r, :] = y[c * n1r:(c + 1) * n1r, s * LANES:(s + 1) * LANES]


def _fft_out(dr, di, gh, gl, plan):
    nb, kh, _, rows, c = dr.shape
    n2 = plan["n2"]
    nh = gh.shape[0]
    n1r = min(16, nh)
    n1c = nh // n1r
    dspec = pl.BlockSpec((None, kh, None, rows, LANES), lambda b, j, q: (b, 0, q, 0, j))
    gspec = pl.BlockSpec(gh.shape, lambda b, j, q: (0, 0))
    yrows = n2 // FFT_N2C * n1r
    return pl.pallas_call(
        functools.partial(_fft_out_kernel, n2=n2),
        grid=(nb, c // LANES, FFT_N2C),
        in_specs=[dspec, dspec, gspec, gspec],
        out_specs=pl.BlockSpec((None, n1c, yrows, LANES), lambda b, j, q: (b, 0, q, j)),
        out_shape=jax.ShapeDtypeStruct((nb, n1c, n2 * n1r, c), F32),
        compiler_params=_cparams(("parallel", "parallel", "arbitrary")),
        name="fft_out",
    )(dr, di, gh, gl)


def _gate_kernel(y_ref, u_ref, g_ref, bias_ref, o_ref, *, n2):
    n1r = y_ref.shape[0] // n2
    for a in range(n1r):
        rows = slice(a * n2, (a + 1) * n2)
        yt = y_ref[pl.ds(a, n2, stride=n1r), :]
        o_ref[rows, :] = (g_ref[rows, :] * (yt + u_ref[rows, :] * bias_ref[...])).astype(o_ref.dtype)


def _gate(y, n2, u, u_row0, u_col0, gate, g_row0, g_col0, bias, out_dtype):
    nb, n1c, yr, c = y.shape
    tr = yr
    return pl.pallas_call(
        functools.partial(_gate_kernel, n2=n2),
        grid=(nb, c // LANES, n1c),
        in_specs=[pl.BlockSpec((None, None, yr, LANES), lambda b, j, q: (b, q, 0, j)),
                  pl.BlockSpec((tr, LANES), lambda b, j, q: (u_row0 // tr + b * n1c + q, u_col0 + j)),
                  pl.BlockSpec((tr, LANES), lambda b, j, q: (g_row0 // tr + b * n1c + q, g_col0 + j)),
                  pl.BlockSpec((1, LANES), lambda b, j, q: (0, j))],
        out_specs=pl.BlockSpec((tr, LANES), lambda b, j, q: (b * n1c + q, j)),
        out_shape=jax.ShapeDtypeStruct((nb * n1c * tr, c), out_dtype),
        compiler_params=_cparams(("parallel", "parallel", "arbitrary")),
        name="hyena_gate",
    )(y, u, gate, bias)


def _np_split(a):
    a32 = np.asarray(a, np.float32)
    hi = a32.astype(BF16)
    lo = (a32 - hi.astype(np.float32)).astype(BF16)
    return hi, lo


def _fft_plan(l):
    n = 2 * l
    n2 = FFT_N2
    n1 = n // n2
    r = n1 // 2 + 1
    rh = FFT_RH
    kh = -(-r // rh)
    rp = kh * rh
    kn1 = n1 // 2
    k1 = np.arange(rp, dtype=np.float64)[:, None]
    live = (k1 < r).astype(np.float64)

    ang = 2 * np.pi * k1 * np.arange(kn1)[None, :] / n1
    f1 = np.concatenate([np.cos(ang) * live, -np.sin(ang) * live], axis=0)

    kk = np.arange(rp, dtype=np.float64)[None, :]
    wgt = np.where((kk == 0) | (kk == n1 // 2), 1.0, 2.0) * (kk < r) / n
    ango = 2 * np.pi * np.arange(n1 // 2)[:, None] * kk / n1
    g = np.concatenate([np.cos(ango) * wgt, -np.sin(ango) * wgt], axis=1)

    ik1 = jnp.arange(rp, dtype=jnp.int32)[:, None, None]
    ia = jnp.arange(n2, dtype=jnp.int32)[None, :, None]
    ib = jnp.arange(n2, dtype=jnp.int32)[None, None, :]
    livej = (ik1 < r).astype(F32)

    def slab_matrices(idx, sign):
        ang = (idx % n).astype(F32) * F32(2.0 * math.pi / n)
        w = jnp.concatenate([jnp.cos(ang), sign * jnp.sin(ang)], axis=1) * livej
        hi = w.astype(BF16)
        return hi, (w - hi.astype(F32)).astype(BF16)

    wf = slab_matrices(n1 * ia * ib + ik1 * ib, -1.0)
    wi = slab_matrices(n1 * ia * ib + ik1 * ia, 1.0)
    return dict(n1=n1, n2=n2, r=r, rp=rp, kh=kh, rh=rh, f1=_np_split(f1), g=_np_split(g), wf=wf, wi=wi)


def _filter_features(l):
    t = jnp.linspace(0.0, 1.0, l, dtype=F32)[:, None]
    bands = jnp.linspace(1e-4, B_BANDS - 1, B_BANDS, dtype=F32)[None, :]
    w = 2.0 * math.pi * jnp.arange(l, dtype=F32)[:, None] / l
    return jnp.concatenate([t, jnp.cos(bands * w), -jnp.sin(bands * w)], axis=-1)


def _hyena_filters(plan, l, c, w1, b1, f1, w2, b2, f2, w3, b3):
    n1, n2 = plan["n1"], plan["n2"]
    z = _filter_features(l)
    e = z.shape[1]
    ep = -(-e // 16) * 16
    z = jnp.pad(z, ((0, 0), (0, ep - e)))
    w1p = jnp.pad(w1, ((0, ep - e), (0, 0))).astype(BF16)
    max_decay = math.log(B_DECAY_TARGET) / B_FAST_DECAY_PCT
    min_decay = math.log(B_DECAY_TARGET) / B_SLOW_DECAY_PCT
    deltas = jnp.abs(jnp.linspace(min_decay, max_decay, c, dtype=F32))
    h, nrm = _filter_mlp(z, w1p, b1, f1, w2.astype(BF16), b2, f2, w3.astype(BF16), b3, deltas,
                         tl=min(512, l))
    oc = h.shape[1] // 2
    inv = 1.0 / (nrm[:, :oc] + nrm[:, oc:])
    b0 = h[0:1, oc:]
    ar, ai = _fft_stage1(h, *plan["f1"], plan, l, 1, 0, 0, 2 * oc)
    mid_shape = (1, plan["kh"], n2 * plan["rh"], 2 * oc)
    return _filter_mid(ar.reshape(mid_shape), ai.reshape(mid_shape), *plan["wf"], inv, b0, plan)


def _hyena_conv(plan, kf, order, l, nb, u, u_row0, u_col0, gate, g_row0, g_col0, bias, c, out_dtype):
    n2 = plan["n2"]
    cb = c // LANES
    ar, ai = _fft_stage1(u, *plan["f1"], plan, l, nb, u_row0 // l, u_col0 * cb, c)
    mid_shape = (nb, plan["kh"], n2 * plan["rh"], c)
    dr, di = _conv_mid(ar.reshape(mid_shape), ai.reshape(mid_shape), *plan["wf"], *plan["wi"],
                       kf[0], kf[1], order, plan)
    y = _fft_out(dr.reshape(ar.shape), di.reshape(ar.shape), *plan["g"], plan)
    return _gate(y, n2, u, u_row0, u_col0 * cb, gate, g_row0, g_col0 * cb, bias.reshape(1, c), out_dtype)


def _hyena_mixer(u, seqs, plans, c, fw, hy_bias):
    outs = []
    for (row0, nb, l) in seqs:
        plan = plans[l]
        kf = _hyena_filters(plan, l, c, *fw)
        z = _hyena_conv(plan, kf, 0, l, nb, u, row0, 0, u, row0, 1, hy_bias[0], c, F32)
        o = _hyena_conv(plan, kf, 1, l, nb, z, 0, 0, u, row0, 2, hy_bias[1], c, BF16)
        outs.append(o)
    return jnp.concatenate(outs, axis=0)


def _rope_tables(pos, hd):
    rot = hd // ROPE_FRACTION
    half = rot // 2
    inv = ROPE_THETA ** (-(jnp.arange(half, dtype=F32) * 2.0 / rot))
    ang = pos[:, None] * inv[None, :]
    cos, sin = jnp.cos(ang), jnp.sin(ang)
    m = pos.shape[0]
    one = jnp.ones((m, hd - rot), F32)
    zero = jnp.zeros((m, hd - rot), F32)
    zh = jnp.zeros((m, half), F32)
    c = jnp.concatenate([cos, cos, one], axis=1)
    s1 = jnp.concatenate([-sin, zh, zero], axis=1)
    s2 = jnp.concatenate([zh, sin, zero], axis=1)
    rep = LANES // hd
    return tuple(jnp.tile(t, (1, rep)) for t in (c, s1, s2)), half


def _trunk(x, bounds, seqs, p):
    m, dm = x.shape
    pos = jnp.concatenate([jnp.tile(jnp.arange(l, dtype=F32), nb) for (_, nb, l) in seqs])
    tabs_a, half_a = _rope_tables(pos, A_HEAD_DIM)
    tabs_c, half_c = _rope_tables(pos, C_HEAD_DIM)
    ident = (jnp.ones((m, LANES), F32), jnp.zeros((m, LANES), F32), jnp.zeros((m, LANES), F32))
    tabs_kv = tuple(jnp.concatenate([a, b], axis=1) for a, b in zip(tabs_a, ident))
    c_hy = dm - A_WIDTH
    plans = {l: _fft_plan(l) for l in sorted({l for (_, _, l) in seqs})}
    xf = x
    xb = x.astype(BF16)
    for i in range(DEPTH):
        j = i // 2
        if i % 2 == 0:
            w_in = p['mix_e_w_in'][j].astype(BF16)
            kv0 = A_WIDTH
            hy0 = A_WIDTH + 2 * A_KV_WIDTH
            q = _matmul_rope(xb, w_in[:, :kv0], tabs_a, half_a, tn=512)
            kv = _matmul_rope(xb, w_in[:, kv0:hy0], tabs_kv, half_a, tn=2 * A_KV_WIDTH)
            hy = _matmul(xb, w_in[:, hy0:], tn=512, out_dtype=F32)
            a_out = _even_attention(q, kv, p['a_sink'][j], bounds)
            u = _short_conv(hy, p['hy_conv_w'][j], p['hy_conv_b'][j], bounds)
            fw = (p['hy_w1'][j], p['hy_b1'][j], p['hy_f1'][j], p['hy_w2'][j], p['hy_b2'][j],
                  p['hy_f2'][j], p['hy_w3'][j], p['hy_b3'][j])
            h_out = _hyena_mixer(u, seqs, plans, c_hy, fw, p['hy_bias'][j])
            w_out = p['mix_e_w_out'][j].astype(BF16)
            xf, xb = _matmul_ln([a_out, h_out], [w_out[:A_WIDTH], w_out[A_WIDTH:]], xf,
                                p['ln1_g'][i], p['ln1_b'][i])
        else:
            w_in = p['mix_o_w_in'][j].astype(BF16)
            gw = C_HEADS * C_HEAD_DIM
            ng = len(C_DILATIONS)
            qkv = []
            for g, d in enumerate(C_DILATIONS):
                trio = []
                for part in range(3):
                    c0 = (part * ng + g) * gw
                    trio.append(_odd_proj(xb, w_in, c0, tabs_c, half_c, d, rope=part < 2))
                qkv.append(tuple(trio))
            o = _odd_attention(qkv, bounds, m)
            xf, xb = _matmul_ln([o], [p['mix_o_w_out'][j].astype(BF16)], xf, p['ln1_g'][i], p['ln1_b'][i])
        xf, xb = _ffn_ln(xb, xf, p['ffn_w_gate'][i].astype(BF16), p['ffn_w_up'][i].astype(BF16),
                         p['ffn_w_down'][i].astype(BF16), p['ln2_g'][i], p['ln2_b'][i])
    return xf


def kernel(x_prompt, x_sample, mix_e_w_in, a_sink, hy_conv_w, hy_conv_b, hy_w1, hy_b1, hy_f1, hy_w2, hy_b2,
           hy_f2, hy_w3, hy_b3, hy_bias, mix_e_w_out, mix_o_w_in, mix_o_w_out, ffn_w_gate, ffn_w_up,
           ffn_w_down, ln1_g, ln1_b, ln2_g, ln2_b):
    p = dict(mix_e_w_in=mix_e_w_in, a_sink=a_sink, hy_conv_w=hy_conv_w, hy_conv_b=hy_conv_b,
             hy_w1=hy_w1, hy_b1=hy_b1, hy_f1=hy_f1, hy_w2=hy_w2, hy_b2=hy_b2, hy_f2=hy_f2,
             hy_w3=hy_w3, hy_b3=hy_b3, hy_bias=hy_bias, mix_e_w_out=mix_e_w_out,
             mix_o_w_in=mix_o_w_in, mix_o_w_out=mix_o_w_out, ffn_w_gate=ffn_w_gate,
             ffn_w_up=ffn_w_up, ffn_w_down=ffn_w_down, ln1_g=ln1_g, ln1_b=ln1_b,
             ln2_g=ln2_g, ln2_b=ln2_b)
    dm = x_prompt.shape[-1]
    seqs, bounds, row = [], [0], 0
    for xs in (x_prompt, x_sample):
        nb, l = xs.shape[0], xs.shape[1]
        seqs.append((row, nb, l))
        for _ in range(nb):
            row += l
            bounds.append(row)
    x = jnp.concatenate([x_prompt.reshape(-1, dm), x_sample.reshape(-1, dm)], axis=0)
    y = _trunk(x, tuple(bounds), tuple(seqs), p)
    n_p = x_prompt.shape[0] * x_prompt.shape[1]
    return (y[:n_p].reshape(x_prompt.shape), y[n_p:].reshape(x_sample.shape))
```

```python
import functools
import math

import numpy as np
import jax
import jax.numpy as jnp
from jax import lax
from jax.experimental import pallas as pl
from jax.experimental.pallas import tpu as pltpu

F32 = jnp.float32
BF16 = jnp.bfloat16

DEPTH = 4
A_HEADS, A_KV_HEADS, A_HEAD_DIM, A_RADIUS = 16, 2, 64, 128
A_WIDTH = A_HEADS * A_HEAD_DIM
A_KV_WIDTH = A_KV_HEADS * A_HEAD_DIM
B_SHORT, B_EMB = 3, 33
B_BANDS = (B_EMB - 1) // 2
B_DECAY_TARGET, B_FAST_DECAY_PCT, B_SLOW_DECAY_PCT = 1e-2, 0.3, 1.5
C_HEADS, C_HEAD_DIM = 16, 128
C_DILATIONS = (1, 4, 16)
C_RADIUS = 64
ROPE_THETA, ROPE_FRACTION = 500000.0, 4
ALPHA = (2 * DEPTH) ** 0.25
LN_EPS = 1e-5

LANES = 128
VMEM_LIMIT = 56 * 1024 * 1024
FFT_N2 = 256

ODD_CHUNK = 2048
ATT_TQ = 256


def _cparams(sem):
    return pltpu.CompilerParams(dimension_semantics=sem, vmem_limit_bytes=VMEM_LIMIT)


def _seq_bounds(row, bounds):
    start = jnp.int32(bounds[0])
    end = jnp.int32(bounds[1])
    for b0, b1 in zip(bounds[1:-1], bounds[2:]):
        inside = row >= b0
        start = jnp.where(inside, jnp.int32(b0), start)
        end = jnp.where(inside, jnp.int32(b1), end)
    return start, end


def _rope(a, c, s1, s2, half):
    w = a.shape[-1]
    return a * c + pltpu.roll(a, w - half, 1) * s1 + pltpu.roll(a, half, 1) * s2


def _mm_kernel(x_ref, w_ref, o_ref):
    o_ref[...] = jnp.dot(x_ref[...], w_ref[...], preferred_element_type=F32).astype(o_ref.dtype)


def _matmul(x, w, tn, out_dtype, tm=1024):
    m, k = x.shape
    n = w.shape[1]
    return pl.pallas_call(
        _mm_kernel,
        grid=(m // tm, n // tn),
        in_specs=[pl.BlockSpec((tm, k), lambda i, j: (i, 0)),
                  pl.BlockSpec((k, tn), lambda i, j: (0, j))],
        out_specs=pl.BlockSpec((tm, tn), lambda i, j: (i, j)),
        out_shape=jax.ShapeDtypeStruct((m, n), out_dtype),
        compiler_params=_cparams(("parallel", "arbitrary")),
        name="matmul",
    )(x, w)


def _mm_rope_kernel(x_ref, w_ref, c_ref, s1_ref, s2_ref, o_ref, *, half):
    acc = jnp.dot(x_ref[...], w_ref[...], preferred_element_type=F32)
    tw = c_ref.shape[1]
    rc = 256
    for r0 in range(0, acc.shape[0], rc):
        rows = slice(r0, r0 + rc)
        for c in range(acc.shape[1] // tw):
            cols = slice(c * tw, (c + 1) * tw)
            o_ref[rows, cols] = _rope(acc[rows, cols], c_ref[rows, :], s1_ref[rows, :], s2_ref[rows, :],
                                      half).astype(o_ref.dtype)


def _matmul_rope(x, w, tabs, half, tn, tm=1024):
    m, k = x.shape
    n = w.shape[1]
    tw = tabs[0].shape[1]
    tab_spec = pl.BlockSpec((tm, tw), lambda i, j: (i, 0))
    return pl.pallas_call(
        functools.partial(_mm_rope_kernel, half=half),
        grid=(m // tm, n // tn),
        in_specs=[pl.BlockSpec((tm, k), lambda i, j: (i, 0)),
                  pl.BlockSpec((k, tn), lambda i, j: (0, j)),
                  tab_spec, tab_spec, tab_spec],
        out_specs=pl.BlockSpec((tm, tn), lambda i, j: (i, j)),
        out_shape=jax.ShapeDtypeStruct((m, n), BF16),
        compiler_params=_cparams(("parallel", "arbitrary")),
        name="matmul_rope",
    )(x, w, *tabs)


def _odd_proj_kernel(x_ref, w_ref, c_ref, s1_ref, s2_ref, o_ref, acc_ref, *, d, rope, half):
    hps = w_ref.shape[1] // LANES
    tm = x_ref.shape[0]
    t = tm // d
    pair = 2
    rc = 256
    for p in range(hps // pair):
        acc = jnp.dot(x_ref[...], w_ref[:, p * pair * LANES:(p + 1) * pair * LANES], preferred_element_type=F32)
        for h2 in range(pair):
            hh = p * pair + h2
            slot = (p % 2) * pair + h2
            acc_ref[slot, :, :] = acc[:, h2 * LANES:(h2 + 1) * LANES]
            for c0 in range(0, tm, rc):
                a = acc_ref[slot, c0:c0 + rc, :]
                if rope:
                    a = _rope(a, c_ref[c0:c0 + rc, :], s1_ref[c0:c0 + rc, :], s2_ref[c0:c0 + rc, :], half)
                if d == 1:
                    o_ref[hh, 0, c0:c0 + rc, :] = a.astype(BF16)
                elif rope:
                    acc_ref[slot, c0:c0 + rc, :] = a
            if d > 1:
                for r in range(d):
                    o_ref[hh, r, :, :] = acc_ref[slot, pl.ds(r, t, stride=d), :].astype(BF16)


def _odd_proj(x, w, col0, tabs, half, d, rope, hps=8):
    m, k = x.shape
    tm = ODD_CHUNK
    cb0 = col0 // (hps * LANES)
    tab_spec = pl.BlockSpec((tm, LANES), lambda i, j: (i, 0))
    return pl.pallas_call(
        functools.partial(_odd_proj_kernel, d=d, rope=rope, half=half),
        grid=(m // tm, C_HEADS // hps),
        in_specs=[pl.BlockSpec((tm, k), lambda i, j: (i, 0)),
                  pl.BlockSpec((k, hps * LANES), lambda i, j: (0, cb0 + j)),
                  tab_spec, tab_spec, tab_spec],
        out_specs=pl.BlockSpec((hps, d, tm // d, LANES), lambda i, j: (j, 0, i, 0)),
        out_shape=jax.ShapeDtypeStruct((C_HEADS, d, m // d, LANES), BF16),
        scratch_shapes=[pltpu.VMEM((4, tm, LANES), F32)],
        compiler_params=_cparams(("parallel", "arbitrary")),
        name="odd_proj",
    )(x, w, *tabs)


LN_ROWS = 128


def _layer_norm_store(x_ref, acc_ref, g_ref, b_ref, of_ref, ob_ref):
    for c in range(x_ref.shape[0] // LN_ROWS):
        rows = pl.ds(c * LN_ROWS, LN_ROWS)
        r = ALPHA * x_ref[rows, :] + acc_ref[rows, :]
        mu = jnp.mean(r, axis=-1, keepdims=True)
        xc = r - mu
        var = jnp.mean(xc * xc, axis=-1, keepdims=True)
        y = xc * lax.rsqrt(var + LN_EPS) * g_ref[...] + b_ref[...]
        of_ref[rows, :] = y
        ob_ref[rows, :] = y.astype(BF16)


def _mm_ln_kernel(*refs, n_in):
    ys = refs[:n_in]
    ws = refs[n_in:2 * n_in]
    x_ref, g_ref, b_ref, of_ref, ob_ref, acc_ref = refs[2 * n_in:]
    acc = jnp.dot(ys[0][...], ws[0][...], preferred_element_type=F32)
    for y_ref, w_ref in zip(ys[1:], ws[1:]):
        acc = acc + jnp.dot(y_ref[...], w_ref[...], preferred_element_type=F32)
    acc_ref[...] = acc
    _layer_norm_store(x_ref, acc_ref, g_ref, b_ref, of_ref, ob_ref)


def _matmul_ln(ys, ws, x, g, b, tm=256):
    m, dm = x.shape
    n_in = len(ys)
    in_specs = [pl.BlockSpec((tm, y.shape[1]), lambda i: (i, 0)) for y in ys]
    in_specs += [pl.BlockSpec(w.shape, lambda i: (0, 0)) for w in ws]
    in_specs += [pl.BlockSpec((tm, dm), lambda i: (i, 0)),
                 pl.BlockSpec((1, dm), lambda i: (0, 0)),
                 pl.BlockSpec((1, dm), lambda i: (0, 0))]
    return pl.pallas_call(
        functools.partial(_mm_ln_kernel, n_in=n_in),
        grid=(m // tm,),
        in_specs=in_specs,
        out_specs=[pl.BlockSpec((tm, dm), lambda i: (i, 0)), pl.BlockSpec((tm, dm), lambda i: (i, 0))],
        out_shape=[jax.ShapeDtypeStruct((m, dm), F32), jax.ShapeDtypeStruct((m, dm), BF16)],
        scratch_shapes=[pltpu.VMEM((tm, dm), F32)],
        compiler_params=_cparams(("parallel",)),
        name="matmul_ln",
    )(*ys, *ws, x, g.reshape(1, dm), b.reshape(1, dm))


def _ffn_kernel(xb_ref, xf_ref, wg_ref, wu_ref, wd_ref, g_ref, b_ref, of_ref, ob_ref, acc_ref):
    j = pl.program_id(1)

    @pl.when(j == 0)
    def _():
        acc_ref[...] = jnp.zeros_like(acc_ref)

    xb = xb_ref[...]
    gate = jnp.dot(xb, wg_ref[...], preferred_element_type=F32)
    up = jnp.dot(xb, wu_ref[...], preferred_element_type=F32)
    h = (gate * jax.nn.sigmoid(gate)) * up
    acc_ref[...] += jnp.dot(h.astype(BF16), wd_ref[...], preferred_element_type=F32)

    @pl.when(j == pl.num_programs(1) - 1)
    def _():
        _layer_norm_store(xf_ref, acc_ref, g_ref, b_ref, of_ref, ob_ref)


def _ffn_ln(xb, xf, wg, wu, wd, g, b, tm=512, tf=512):
    m, dm = xf.shape
    dff = wg.shape[1]
    row = lambda i, j: (i, 0)
    return pl.pallas_call(
        _ffn_kernel,
        grid=(m // tm, dff // tf),
        in_specs=[pl.BlockSpec((tm, dm), row), pl.BlockSpec((tm, dm), row),
                  pl.BlockSpec((dm, tf), lambda i, j: (0, j)),
                  pl.BlockSpec((dm, tf), lambda i, j: (0, j)),
                  pl.BlockSpec((tf, dm), lambda i, j: (j, 0)),
                  pl.BlockSpec((1, dm), lambda i, j: (0, 0)),
                  pl.BlockSpec((1, dm), lambda i, j: (0, 0))],
        out_specs=[pl.BlockSpec((tm, dm), row), pl.BlockSpec((tm, dm), row)],
        out_shape=[jax.ShapeDtypeStruct((m, dm), F32), jax.ShapeDtypeStruct((m, dm), BF16)],
        scratch_shapes=[pltpu.VMEM((tm, dm), F32)],
        compiler_params=_cparams(("parallel", "arbitrary")),
        name="ffn_ln",
    )(xb, xf, wg, wu, wd, g.reshape(1, dm), b.reshape(1, dm))


def _even_attn_kernel(sink_ref, q_ref, kp_ref, km_ref, kn_ref, o_ref, *, bounds):
    i = pl.program_id(0)
    tq = q_ref.shape[0]
    nk = tq + 2 * A_RADIUS
    row0 = i * tq
    start, end = _seq_bounds(row0, bounds)
    kv = jnp.concatenate([kp_ref[...], km_ref[...], kn_ref[...]], axis=0)
    rq = row0 + lax.broadcasted_iota(jnp.int32, (tq, nk), 0)
    rk = row0 - A_RADIUS + lax.broadcasted_iota(jnp.int32, (tq, nk), 1)
    mask = (jnp.abs(rk - rq) <= A_RADIUS) & (rk >= start) & (rk < end)
    group = A_HEADS // A_KV_HEADS
    scale = A_HEAD_DIM ** -0.5
    for j in range(A_KV_HEADS):
        k = kv[:, j * A_HEAD_DIM:(j + 1) * A_HEAD_DIM]
        v = kv[:, A_KV_WIDTH + j * A_HEAD_DIM:A_KV_WIDTH + (j + 1) * A_HEAD_DIM]
        for gq in range(group):
            h = j * group + gq
            qh = q_ref[:, h * A_HEAD_DIM:(h + 1) * A_HEAD_DIM]
            s = lax.dot_general(qh, k, (((1,), (1,)), ((), ())), preferred_element_type=F32) * scale
            s = jnp.where(mask, s, -jnp.inf)
            sk = sink_ref[h]
            m = jnp.maximum(jnp.max(s, axis=-1, keepdims=True), sk)
            p = jnp.exp(s - m)
            den = jnp.sum(p, axis=-1, keepdims=True) + jnp.exp(sk - m)
            o = jnp.dot(p.astype(BF16), v, preferred_element_type=F32) / den
            o_ref[:, h * A_HEAD_DIM:(h + 1) * A_HEAD_DIM] = o.astype(o_ref.dtype)


def _even_attention(q, kv, sink, bounds):
    m = q.shape[0]
    tq = ATT_TQ
    hb = A_RADIUS
    per = tq // hb
    last = m // hb - 1
    kvw = kv.shape[1]
    return pl.pallas_call(
        functools.partial(_even_attn_kernel, bounds=bounds),
        grid=(m // tq,),
        in_specs=[pl.BlockSpec(memory_space=pltpu.SMEM),
                  pl.BlockSpec((tq, A_WIDTH), lambda i: (i, 0)),
                  pl.BlockSpec((hb, kvw), lambda i: (jnp.maximum(i * per - 1, 0), 0)),
                  pl.BlockSpec((tq, kvw), lambda i: (i, 0)),
                  pl.BlockSpec((hb, kvw), lambda i: (jnp.minimum((i + 1) * per, last), 0))],
        out_specs=pl.BlockSpec((tq, A_WIDTH), lambda i: (i, 0)),
        out_shape=jax.ShapeDtypeStruct((m, A_WIDTH), BF16),
        compiler_params=_cparams(("parallel",)),
        name="even_attention",
    )(sink, q, kv, kv, kv)


def _odd_attn_kernel(*refs, bounds):
    ng = len(C_DILATIONS)
    o_ref, oacc, lacc = refs[7 * ng:]
    i = pl.program_id(0)
    chunk = ODD_CHUNK
    qb = 128
    nk = qb + 2 * C_RADIUS
    row0 = i * chunk
    start, end = _seq_bounds(row0, bounds)
    rr = lax.broadcasted_iota(jnp.int32, (qb, nk), 0)
    cc = lax.broadcasted_iota(jnp.int32, (qb, nk), 1)
    band = jnp.abs(cc - C_RADIUS - rr) <= C_RADIUS
    scale = C_HEAD_DIM ** -0.5
    for g, d in enumerate(C_DILATIONS):
        q_ref, kp_ref, km_ref, kn_ref, vp_ref, vm_ref, vn_ref = refs[7 * g:7 * g + 7]
        tg = chunk // d
        t_lo, t_hi, t_c0 = start // d, end // d, row0 // d
        for sb in range(tg // qb):
            lo, hi = qb * sb - C_RADIUS, qb * sb + qb + C_RADIUS
            tk = t_c0 + lo + cc
            mask = band & (tk >= t_lo) & (tk < t_hi)
            for r in range(d):
                def window(p_ref, m_ref, n_ref):
                    parts = []
                    if lo < 0:
                        parts.append(p_ref[0, r, :, :])
                    parts.append(m_ref[0, r, max(lo, 0):min(hi, tg), :])
                    if hi > tg:
                        parts.append(n_ref[0, r, :, :])
                    return parts[0] if len(parts) == 1 else jnp.concatenate(parts, axis=0)

                q = q_ref[0, r, qb * sb:qb * (sb + 1), :]
                k = window(kp_ref, km_ref, kn_ref)
                v = window(vp_ref, vm_ref, vn_ref)
                s = lax.dot_general(q, k, (((1,), (1,)), ((), ())), preferred_element_type=F32) * scale
                s = jnp.where(mask, s, -jnp.inf)
                m = jnp.max(s, axis=-1, keepdims=True)
                p = jnp.exp(s - m)
                den = jnp.sum(p, axis=-1, keepdims=True)
                o = jnp.dot(p.astype(BF16), v, preferred_element_type=F32) / den
                lse = jnp.broadcast_to(m + jnp.log(den), (qb, LANES))
                if d == 1:
                    rows = pl.ds(qb * sb, qb)
                else:
                    rows = pl.ds(r + d * qb * sb, qb, stride=d)
                oacc[g, rows, :] = o
                lacc[g, rows, :] = lse
    ls = [lacc[g] for g in range(ng)]
    mx = functools.reduce(jnp.maximum, ls)
    ws = [jnp.exp(l - mx) for l in ls]
    tot = functools.reduce(lambda a, b: a + b, ws)
    out = functools.reduce(lambda a, b: a + b, [(ws[g] / tot) * oacc[g] for g in range(ng)])
    o_ref[...] = out.astype(o_ref.dtype)


def _odd_attention(qkv, bounds, m):
    chunk = ODD_CHUNK
    hb = C_RADIUS
    operands, in_specs = [], []
    for (q, k, v), d in zip(qkv, C_DILATIONS):
        tg = chunk // d
        per = tg // hb
        last = m // d // hb - 1
        main = pl.BlockSpec((1, d, tg, LANES), lambda i, h: (h, 0, i, 0))
        prev = pl.BlockSpec((1, d, hb, LANES), lambda i, h, per=per: (h, 0, jnp.maximum(i * per - 1, 0), 0))
        nxt = pl.BlockSpec((1, d, hb, LANES), lambda i, h, per=per, last=last: (h, 0, jnp.minimum((i + 1) * per, last), 0))
        operands += [q, k, k, k, v, v, v]
        in_specs += [main, prev, main, nxt, prev, main, nxt]
    ng = len(C_DILATIONS)
    return pl.pallas_call(
        functools.partial(_odd_attn_kernel, bounds=bounds),
        grid=(m // chunk, C_HEADS),
        in_specs=in_specs,
        out_specs=pl.BlockSpec((chunk, LANES), lambda i, h: (i, h)),
        out_shape=jax.ShapeDtypeStruct((m, C_HEADS * C_HEAD_DIM), BF16),
        scratch_shapes=[pltpu.VMEM((ng, chunk, LANES), F32), pltpu.VMEM((ng, chunk, LANES), F32)],
        compiler_params=_cparams(("parallel", "arbitrary")),
        name="odd_attention",
    )(*operands)


def _short_conv_kernel(xp_ref, xm_ref, xn_ref, w_ref, b_ref, o_ref, *, bounds):
    i = pl.program_id(0)
    tr = xm_ref.shape[0]
    row0 = i * tr
    start, end = _seq_bounds(row0, bounds)
    x = xm_ref[...]
    hp = xp_ref.shape[0]
    before = jnp.where(row0 > start, xp_ref[hp - 1:hp, :], 0.0)
    after = jnp.where(row0 + tr < end, xn_ref[0:1, :], 0.0)
    ridx = lax.broadcasted_iota(jnp.int32, x.shape, 0)
    xl = jnp.where(ridx == 0, before, pltpu.roll(x, 1, 0))
    xr = jnp.where(ridx == tr - 1, after, pltpu.roll(x, tr - 1, 0))
    y = b_ref[...] + xl * w_ref[0:1, :]
    y = y + x * w_ref[1:2, :]
    y = y + xr * w_ref[2:3, :]
    o_ref[...] = y


def _short_conv(x, w, b, bounds, tr=512, tc=512):
    m, n = x.shape
    hb = 8
    per = tr // hb
    last = m // hb - 1
    return pl.pallas_call(
        functools.partial(_short_conv_kernel, bounds=bounds),
        grid=(m // tr, n // tc),
        in_specs=[pl.BlockSpec((hb, tc), lambda i, j: (jnp.maximum(i * per - 1, 0), j)),
                  pl.BlockSpec((tr, tc), lambda i, j: (i, j)),
                  pl.BlockSpec((hb, tc), lambda i, j: (jnp.minimum((i + 1) * per, last), j)),
                  pl.BlockSpec((B_SHORT, tc), lambda i, j: (0, j)),
                  pl.BlockSpec((1, tc), lambda i, j: (0, j))],
        out_specs=pl.BlockSpec((tr, tc), lambda i, j: (i, j)),
        out_shape=jax.ShapeDtypeStruct((m, n), F32),
        compiler_params=_cparams(("parallel", "arbitrary")),
        name="short_conv",
    )(x, x, x, w, b.reshape(1, n))


def _filter_mlp_kernel(z_ref, w1_ref, b1_ref, f1_ref, w2_ref, b2_ref, f2_ref, w3_ref, b3_ref, dl_ref,
                       h_ref, nrm_ref):
    i = pl.program_id(0)
    z = z_ref[...]
    h = jnp.sin(f1_ref[...] * (jnp.dot(z.astype(BF16), w1_ref[...], preferred_element_type=F32) + b1_ref[...]))
    h = jnp.sin(f2_ref[...] * (jnp.dot(h.astype(BF16), w2_ref[...], preferred_element_type=F32) + b2_ref[...]))
    h = jnp.dot(h.astype(BF16), w3_ref[...], preferred_element_type=F32) + b3_ref[...]
    decay = jnp.exp(-z[:, 0:1] * dl_ref[...])
    nrep = h.shape[1] // decay.shape[1]
    h = h * jnp.concatenate([decay] * nrep, axis=1)
    h_ref[...] = h

    @pl.when(i == 0)
    def _():
        nrm_ref[...] = jnp.zeros_like(nrm_ref)

    half = h.shape[1] // 2
    col = lax.broadcasted_iota(jnp.int32, h.shape, 1)
    row = lax.broadcasted_iota(jnp.int32, h.shape, 0) + i * h.shape[0]
    a = jnp.where((col >= half) & (row == 0), 0.0, jnp.abs(h))
    nrm_ref[...] += jnp.sum(a, axis=0, keepdims=True)


def _filter_mlp(z, w1, b1, f1, w2, b2, f2, w3, b3, deltas, tl=512):
    l, e = z.shape
    hid = w1.shape[1]
    n = w3.shape[1]
    c = deltas.shape[0]
    full = lambda shape: pl.BlockSpec(shape, lambda i: (0, 0))
    return pl.pallas_call(
        _filter_mlp_kernel,
        grid=(l // tl,),
        in_specs=[pl.BlockSpec((tl, e), lambda i: (i, 0)),
                  full((e, hid)), full((1, hid)), full((1, hid)),
                  full((hid, hid)), full((1, hid)), full((1, hid)),
                  full((hid, n)), full((1, n)), full((1, c))],
        out_specs=[pl.BlockSpec((tl, n), lambda i: (i, 0)), full((1, n))],
        out_shape=[jax.ShapeDtypeStruct((l, n), F32), jax.ShapeDtypeStruct((1, n), F32)],
        compiler_params=_cparams(("arbitrary",)),
        name="filter_mlp",
    )(z, w1, b1.reshape(1, hid), f1.reshape(1, hid), w2, b2.reshape(1, hid), f2.reshape(1, hid),
      w3, b3.reshape(1, n), deltas.reshape(1, c))


def _split(x):
    hi = x.astype(BF16)
    lo = (x - hi.astype(F32)).astype(BF16)
    return hi, lo


def _dot3(ch, cl, x):
    xh, xl = _split(x)
    r = jnp.dot(ch, xh, preferred_element_type=F32)
    r = r + jnp.dot(ch, xl, preferred_element_type=F32)
    return r + jnp.dot(cl, xh, preferred_element_type=F32)


FFT_GROUP = 8


FFT_N2C = 4
FFT_RH = 24
FFT_SPB = 6


def _fft1_kernel(x_ref, fh_ref, fl_ref, ar_ref, ai_ref, *, kn1, n2):
    kh, rows, _ = ar_ref.shape
    n2q = n2 // FFT_N2C
    rh = rows // n2q
    rp = kh * rh
    c0 = pl.program_id(2) * n2q
    for g in range(n2q // FFT_GROUP):
        cols = [x_ref[pl.ds(c0 + g * FFT_GROUP + s, kn1, stride=n2), :] for s in range(FFT_GROUP)]
        out = _dot3(fh_ref[...], fl_ref[...], jnp.concatenate(cols, axis=1))
        for k in range(kh):
            for s in range(FFT_GROUP):
                r0 = (g * FFT_GROUP + s) * rh
                ar_ref[k, r0:r0 + rh, :] = out[k * rh:(k + 1) * rh, s * LANES:(s + 1) * LANES]
                ai_ref[k, r0:r0 + rh, :] = out[rp + k * rh:rp + (k + 1) * rh, s * LANES:(s + 1) * LANES]


def _fft_stage1(x, f1h, f1l, plan, l, nb, row_blk0, col_blk0, c):
    n2, kh, rh = plan["n2"], plan["kh"], plan["rh"]
    kn1 = f1h.shape[1]
    rows = n2 // FFT_N2C * rh
    out = jax.ShapeDtypeStruct((nb, kh, FFT_N2C, rows, c), F32)
    fspec = pl.BlockSpec(f1h.shape, lambda b, j, q: (0, 0))
    ospec = pl.BlockSpec((None, kh, None, rows, LANES), lambda b, j, q: (b, 0, q, 0, j))
    return pl.pallas_call(
        functools.partial(_fft1_kernel, kn1=kn1, n2=n2),
        grid=(nb, c // LANES, FFT_N2C),
        in_specs=[pl.BlockSpec((l, LANES), lambda b, j, q: (row_blk0 + b, col_blk0 + j)), fspec, fspec],
        out_specs=[ospec, ospec],
        out_shape=[out, out],
        compiler_params=_cparams(("parallel", "parallel", "arbitrary")),
        name="fft_stage1",
    )(x, f1h, f1l)


def _cpair(p, n):
    return p[:n, :LANES] - p[n:, LANES:], p[:n, LANES:] + p[n:, :LANES]


def _twiddle(xr, xi, tr, ti):
    return xr * tr - xi * ti, xr * ti + xi * tr


def _filter_mid_kernel(fr_ref, fi_ref, br_ref, bi_ref, tr_ref, ti_ref, wh_ref, wl_ref, inv_ref, b0_ref,
                       kr_ref, ki_ref, *, nslab, rh, n2):
    k0 = pl.program_id(2) * FFT_SPB

    @pl.when(pl.program_id(1) * rh + k0 < nslab)
    def _():
        for t in range(FFT_SPB):
            rows = pl.ds(k0 + t, n2, stride=rh)
            tr, ti = tr_ref[t], ti_ref[t]
            x4 = jnp.concatenate(_twiddle(fr_ref[rows, :], fi_ref[rows, :], tr, ti)
                                 + _twiddle(br_ref[rows, :], bi_ref[rows, :], tr, ti), axis=1)
            p = _dot3(wh_ref[...], wl_ref[...], x4)
            fr, fi = _cpair(p[:, :2 * LANES], n2)
            br, bi = _cpair(p[:, 2 * LANES:], n2)
            kr_ref[t] = (fr + (br - b0_ref[...])) * inv_ref[...]
            ki_ref[t] = (fi - bi) * inv_ref[...]

    @pl.when(pl.program_id(1) * rh + k0 >= nslab)
    def _():
        kr_ref[...] = jnp.zeros_like(kr_ref)
        ki_ref[...] = jnp.zeros_like(ki_ref)


def _filter_mid(ar, ai, inv, b0, plan):
    n2, kh, rh, rp = plan["n2"], plan["kh"], plan["rh"], plan["rp"]
    oc = ar.shape[-1] // 2
    nj = oc // LANES
    fwd = pl.BlockSpec((None, None, n2 * rh, LANES), lambda j, kk, k: (0, kk, 0, j))
    bwd = pl.BlockSpec((None, None, n2 * rh, LANES), lambda j, kk, k: (0, kk, 0, j + nj))
    steps = rh // FFT_SPB
    tspec = pl.BlockSpec((FFT_SPB, n2, LANES), lambda j, kk, k: (kk * steps + k, 0, 0))
    wspec = pl.BlockSpec((2 * n2, n2), lambda j, kk, k: (0, 0))
    vec = pl.BlockSpec((1, LANES), lambda j, kk, k: (0, j))
    ospec = pl.BlockSpec((FFT_SPB, n2, LANES), lambda j, kk, k: (kk * steps + k, 0, j))
    out = jax.ShapeDtypeStruct((rp, n2, oc), F32)
    return pl.pallas_call(
        functools.partial(_filter_mid_kernel, nslab=plan["r"], rh=rh, n2=n2),
        grid=(nj, kh, steps),
        in_specs=[fwd, fwd, bwd, bwd, tspec, tspec, wspec, wspec, vec, vec],
        out_specs=[ospec, ospec],
        out_shape=[out, out],
        compiler_params=_cparams(("parallel", "arbitrary", "arbitrary")),
        name="filter_mid",
    )(ar, ai, ar, ai, *plan["tw"], *plan["f2"], inv, b0)


def _conv_mid_kernel(ar_ref, ai_ref, tr_ref, ti_ref, wfh_ref, wfl_ref, wih_ref, wil_ref, kr_ref, ki_ref,
                     dr_ref, di_ref, *, nslab, rh, n2):
    k0 = pl.program_id(3) * FFT_SPB

    @pl.when(pl.program_id(2) * rh + k0 < nslab)
    def _():
        for t in range(FFT_SPB):
            rows = pl.ds(k0 + t, n2, stride=rh)
            tr, ti = tr_ref[t], ti_ref[t]
            x2 = jnp.concatenate(_twiddle(ar_ref[rows, :], ai_ref[rows, :], tr, ti), axis=1)
            xr, xi = _cpair(_dot3(wfh_ref[...], wfl_ref[...], x2), n2)
            kr, ki = kr_ref[t], ki_ref[t]
            y2 = jnp.concatenate([xr * kr - xi * ki, xr * ki + xi * kr], axis=1)
            cr, ci = _cpair(_dot3(wih_ref[...], wil_ref[...], y2), n2)
            dr, di = _twiddle(cr, ci, tr, -ti)
            dr_ref[rows, :] = dr
            di_ref[rows, :] = di

    @pl.when(pl.program_id(2) * rh + k0 >= nslab)
    def _():
        for t in range(FFT_SPB):
            rows = pl.ds(k0 + t, n2, stride=rh)
            dr_ref[rows, :] = jnp.zeros((n2, LANES), F32)
            di_ref[rows, :] = jnp.zeros((n2, LANES), F32)


def _conv_mid(ar, ai, kr, ki, order, plan):
    nb, kh, _, c = ar.shape
    n2, rh = plan["n2"], plan["rh"]
    koff = order * (c // LANES)
    blk = pl.BlockSpec((None, None, n2 * rh, LANES), lambda b, j, kk, k: (b, kk, 0, j))
    steps = rh // FFT_SPB
    tspec = pl.BlockSpec((FFT_SPB, n2, LANES), lambda b, j, kk, k: (kk * steps + k, 0, 0))
    wspec = pl.BlockSpec((2 * n2, n2), lambda b, j, kk, k: (0, 0))
    kspec = pl.BlockSpec((FFT_SPB, n2, LANES), lambda b, j, kk, k: (kk * steps + k, 0, j + koff))
    out = jax.ShapeDtypeStruct(ar.shape, F32)
    return pl.pallas_call(
        functools.partial(_conv_mid_kernel, nslab=plan["r"], rh=rh, n2=n2),
        grid=(nb, c // LANES, kh, steps),
        in_specs=[blk, blk, tspec, tspec, wspec, wspec, wspec, wspec, kspec, kspec],
        out_specs=[blk, blk],
        out_shape=[out, out],
        compiler_params=_cparams(("parallel", "parallel", "arbitrary", "arbitrary")),
        name="conv_mid",
    )(ar, ai, *plan["tw"], *plan["f2"], *plan["f2i"], kr, ki)


def _fft_out_kernel(dr_ref, di_ref, gh_ref, gl_ref, y_ref, *, n2):
    kh, rows, _ = dr_ref.shape
    n2q = n2 // FFT_N2C
    rh = rows // n2q
    n1c = y_ref.shape[0]
    n1r = y_ref.shape[1] // n2q
    for g in range(n2q // FFT_GROUP):
        cols = []
        for s in range(FFT_GROUP):
            r0 = (g * FFT_GROUP + s) * rh
            parts = [dr_ref[k, r0:r0 + rh, :] for k in range(kh)] + [di_ref[k, r0:r0 + rh, :] for k in range(kh)]
            cols.append(jnp.concatenate(parts, axis=0))
        y = _dot3(gh_ref[...], gl_ref[...], jnp.concatenate(cols, axis=1))
        for c in range(n1c):
            for s in range(FFT_GROUP):
                t0 = (g * FFT_GROUP + s) * n1r
                y_ref[c, t0:t0 + n1r, :] = y[c * n1r:(c + 1) * n1r, s * LANES:(s + 1) * LANES]


def _fft_out(dr, di, gh, gl, plan):
    nb, kh, _, rows, c = dr.shape
    n2 = plan["n2"]
    nh = gh.shape[0]
    n1r = min(16, nh)
    n1c = nh // n1r
    dspec = pl.BlockSpec((None, kh, None, rows, LANES), lambda b, j, q: (b, 0, q, 0, j))
    gspec = pl.BlockSpec(gh.shape, lambda b, j, q: (0, 0))
    yrows = n2 // FFT_N2C * n1r
    return pl.pallas_call(
        functools.partial(_fft_out_kernel, n2=n2),
        grid=(nb, c // LANES, FFT_N2C),
        in_specs=[dspec, dspec, gspec, gspec],
        out_specs=pl.BlockSpec((None, n1c, yrows, LANES), lambda b, j, q: (b, 0, q, j)),
        out_shape=jax.ShapeDtypeStruct((nb, n1c, n2 * n1r, c), F32),
        compiler_params=_cparams(("parallel", "parallel", "arbitrary")),
        name="fft_out",
    )(dr, di, gh, gl)


def _gate_kernel(y_ref, u_ref, g_ref, bias_ref, o_ref, *, n2):
    n1r = y_ref.shape[0] // n2
    for a in range(n1r):
        rows = slice(a * n2, (a + 1) * n2)
        yt = y_ref[pl.ds(a, n2, stride=n1r), :]
        o_ref[rows, :] = (g_ref[rows, :] * (yt + u_ref[rows, :] * bias_ref[...])).astype(o_ref.dtype)


def _gate(y, n2, u, u_row0, u_col0, gate, g_row0, g_col0, bias, out_dtype):
    nb, n1c, yr, c = y.shape
    tr = yr
    return pl.pallas_call(
        functools.partial(_gate_kernel, n2=n2),
        grid=(nb, c // LANES, n1c),
        in_specs=[pl.BlockSpec((None, None, yr, LANES), lambda b, j, q: (b, q, 0, j)),
                  pl.BlockSpec((tr, LANES), lambda b, j, q: (u_row0 // tr + b * n1c + q, u_col0 + j)),
                  pl.BlockSpec((tr, LANES), lambda b, j, q: (g_row0 // tr + b * n1c + q, g_col0 + j)),
                  pl.BlockSpec((1, LANES), lambda b, j, q: (0, j))],
        out_specs=pl.BlockSpec((tr, LANES), lambda b, j, q: (b * n1c + q, j)),
        out_shape=jax.ShapeDtypeStruct((nb * n1c * tr, c), out_dtype),
        compiler_params=_cparams(("parallel", "parallel", "arbitrary")),
        name="hyena_gate",
    )(y, u, gate, bias)


def _np_split(a):
    a32 = np.asarray(a, np.float32)
    hi = a32.astype(BF16)
    lo = (a32 - hi.astype(np.float32)).astype(BF16)
    return hi, lo


def _fft_plan(l):
    n = 2 * l
    n2 = FFT_N2
    n1 = n // n2
    r = n1 // 2 + 1
    rh = FFT_RH
    kh = -(-r // rh)
    rp = kh * rh
    kn1 = n1 // 2
    k1 = np.arange(rp, dtype=np.float64)[:, None]
    live = (k1 < r).astype(np.float64)

    ang = 2 * np.pi * k1 * np.arange(kn1)[None, :] / n1
    f1 = np.concatenate([np.cos(ang) * live, -np.sin(ang) * live], axis=0)

    kk = np.arange(rp, dtype=np.float64)[None, :]
    wgt = np.where((kk == 0) | (kk == n1 // 2), 1.0, 2.0) * (kk < r) / n
    ango = 2 * np.pi * np.arange(n1 // 2)[:, None] * kk / n1
    g = np.concatenate([np.cos(ango) * wgt, -np.sin(ango) * wgt], axis=1)

    a2 = 2 * np.pi * np.outer(np.arange(n2), np.arange(n2)) / n2
    f2 = np.concatenate([np.cos(a2), -np.sin(a2)], axis=0)
    f2i = np.concatenate([np.cos(a2), np.sin(a2)], axis=0)

    idx = jnp.arange(rp, dtype=jnp.int32)[:, None] * jnp.arange(n2, dtype=jnp.int32)[None, :]
    ang = idx.astype(F32) * F32(2.0 * math.pi / n)
    tw = tuple(jnp.broadcast_to(t[:, :, None], (rp, n2, LANES)) for t in (jnp.cos(ang), -jnp.sin(ang)))
    return dict(n1=n1, n2=n2, r=r, rp=rp, kh=kh, rh=rh, f1=_np_split(f1), g=_np_split(g),
                f2=_np_split(f2), f2i=_np_split(f2i), tw=tw)


def _filter_features(l):
    t = jnp.linspace(0.0, 1.0, l, dtype=F32)[:, None]
    bands = jnp.linspace(1e-4, B_BANDS - 1, B_BANDS, dtype=F32)[None, :]
    w = 2.0 * math.pi * jnp.arange(l, dtype=F32)[:, None] / l
    return jnp.concatenate([t, jnp.cos(bands * w), -jnp.sin(bands * w)], axis=-1)


def _hyena_filters(plan, l, c, w1, b1, f1, w2, b2, f2, w3, b3):
    n1, n2 = plan["n1"], plan["n2"]
    z = _filter_features(l)
    e = z.shape[1]
    ep = -(-e // 16) * 16
    z = jnp.pad(z, ((0, 0), (0, ep - e)))
    w1p = jnp.pad(w1, ((0, ep - e), (0, 0))).astype(BF16)
    max_decay = math.log(B_DECAY_TARGET) / B_FAST_DECAY_PCT
    min_decay = math.log(B_DECAY_TARGET) / B_SLOW_DECAY_PCT
    deltas = jnp.abs(jnp.linspace(min_decay, max_decay, c, dtype=F32))
    h, nrm = _filter_mlp(z, w1p, b1, f1, w2.astype(BF16), b2, f2, w3.astype(BF16), b3, deltas,
                         tl=min(512, l))
    oc = h.shape[1] // 2
    inv = 1.0 / (nrm[:, :oc] + nrm[:, oc:])
    b0 = h[0:1, oc:]
    ar, ai = _fft_stage1(h, *plan["f1"], plan, l, 1, 0, 0, 2 * oc)
    mid_shape = (1, plan["kh"], n2 * plan["rh"], 2 * oc)
    return _filter_mid(ar.reshape(mid_shape), ai.reshape(mid_shape), inv, b0, plan)


def _hyena_conv(plan, kf, order, l, nb, u, u_row0, u_col0, gate, g_row0, g_col0, bias, c, out_dtype):
    n2 = plan["n2"]
    cb = c // LANES
    ar, ai = _fft_stage1(u, *plan["f1"], plan, l, nb, u_row0 // l, u_col0 * cb, c)
    mid_shape = (nb, plan["kh"], n2 * plan["rh"], c)
    dr, di = _conv_mid(ar.reshape(mid_shape), ai.reshape(mid_shape), kf[0], kf[1], order, plan)
    y = _fft_out(dr.reshape(ar.shape), di.reshape(ar.shape), *plan["g"], plan)
    return _gate(y, n2, u, u_row0, u_col0 * cb, gate, g_row0, g_col0 * cb, bias.reshape(1, c), out_dtype)


def _hyena_mixer(u, seqs, plans, c, fw, hy_bias):
    outs = []
    for (row0, nb, l) in seqs:
        plan = plans[l]
        kf = _hyena_filters(plan, l, c, *fw)
        z = _hyena_conv(plan, kf, 0, l, nb, u, row0, 0, u, row0, 1, hy_bias[0], c, F32)
        o = _hyena_conv(plan, kf, 1, l, nb, z, 0, 0, u, row0, 2, hy_bias[1], c, BF16)
        outs.append(o)
    return jnp.concatenate(outs, axis=0)


def _rope_tables(pos, hd):
    rot = hd // ROPE_FRACTION
    half = rot // 2
    inv = ROPE_THETA ** (-(jnp.arange(half, dtype=F32) * 2.0 / rot))
    ang = pos[:, None] * inv[None, :]
    cos, sin = jnp.cos(ang), jnp.sin(ang)
    m = pos.shape[0]
    one = jnp.ones((m, hd - rot), F32)
    zero = jnp.zeros((m, hd - rot), F32)
    zh = jnp.zeros((m, half), F32)
    c = jnp.concatenate([cos, cos, one], axis=1)
    s1 = jnp.concatenate([-sin, zh, zero], axis=1)
    s2 = jnp.concatenate([zh, sin, zero], axis=1)
    rep = LANES // hd
    return tuple(jnp.tile(t, (1, rep)) for t in (c, s1, s2)), half


def _trunk(x, bounds, seqs, p):
    m, dm = x.shape
    pos = jnp.concatenate([jnp.tile(jnp.arange(l, dtype=F32), nb) for (_, nb, l) in seqs])
    tabs_a, half_a = _rope_tables(pos, A_HEAD_DIM)
    tabs_c, half_c = _rope_tables(pos, C_HEAD_DIM)
    ident = (jnp.ones((m, LANES), F32), jnp.zeros((m, LANES), F32), jnp.zeros((m, LANES), F32))
    tabs_kv = tuple(jnp.concatenate([a, b], axis=1) for a, b in zip(tabs_a, ident))
    c_hy = dm - A_WIDTH
    plans = {l: _fft_plan(l) for l in sorted({l for (_, _, l) in seqs})}
    xf = x
    xb = x.astype(BF16)
    for i in range(DEPTH):
        j = i // 2
        if i % 2 == 0:
            w_in = p['mix_e_w_in'][j].astype(BF16)
            kv0 = A_WIDTH
            hy0 = A_WIDTH + 2 * A_KV_WIDTH
            q = _matmul_rope(xb, w_in[:, :kv0], tabs_a, half_a, tn=512)
            kv = _matmul_rope(xb, w_in[:, kv0:hy0], tabs_kv, half_a, tn=2 * A_KV_WIDTH)
            hy = _matmul(xb, w_in[:, hy0:], tn=512, out_dtype=F32)
            a_out = _even_attention(q, kv, p['a_sink'][j], bounds)
            u = _short_conv(hy, p['hy_conv_w'][j], p['hy_conv_b'][j], bounds)
            fw = (p['hy_w1'][j], p['hy_b1'][j], p['hy_f1'][j], p['hy_w2'][j], p['hy_b2'][j],
                  p['hy_f2'][j], p['hy_w3'][j], p['hy_b3'][j])
            h_out = _hyena_mixer(u, seqs, plans, c_hy, fw, p['hy_bias'][j])
            w_out = p['mix_e_w_out'][j].astype(BF16)
            xf, xb = _matmul_ln([a_out, h_out], [w_out[:A_WIDTH], w_out[A_WIDTH:]], xf,
                                p['ln1_g'][i], p['ln1_b'][i])
        else:
            w_in = p['mix_o_w_in'][j].astype(BF16)
            gw = C_HEADS * C_HEAD_DIM
            ng = len(C_DILATIONS)
            qkv = []
            for g, d in enumerate(C_DILATIONS):
                trio = []
                for part in range(3):
                    c0 = (part * ng + g) * gw
                    trio.append(_odd_proj(xb, w_in, c0, tabs_c, half_c, d, rope=part < 2))
                qkv.append(tuple(trio))
            o = _odd_attention(qkv, bounds, m)
            xf, xb = _matmul_ln([o], [p['mix_o_w_out'][j].astype(BF16)], xf, p['ln1_g'][i], p['ln1_b'][i])
        xf, xb = _ffn_ln(xb, xf, p['ffn_w_gate'][i].astype(BF16), p['ffn_w_up'][i].astype(BF16),
                         p['ffn_w_down'][i].astype(BF16), p['ln2_g'][i], p['ln2_b'][i])
    return xf


def kernel(x_prompt, x_sample, mix_e_w_in, a_sink, hy_conv_w, hy_conv_b, hy_w1, hy_b1, hy_f1, hy_w2, hy_b2,
           hy_f2, hy_w3, hy_b3, hy_bias, mix_e_w_out, mix_o_w_in, mix_o_w_out, ffn_w_gate, ffn_w_up,
           ffn_w_down, ln1_g, ln1_b, ln2_g, ln2_b):
    p = dict(mix_e_w_in=mix_e_w_in, a_sink=a_sink, hy_conv_w=hy_conv_w, hy_conv_b=hy_conv_b,
             hy_w1=hy_w1, hy_b1=hy_b1, hy_f1=hy_f1, hy_w2=hy_w2, hy_b2=hy_b2, hy_f2=hy_f2,
             hy_w3=hy_w3, hy_b3=hy_b3, hy_bias=hy_bias, mix_e_w_out=mix_e_w_out,
             mix_o_w_in=mix_o_w_in, mix_o_w_out=mix_o_w_out, ffn_w_gate=ffn_w_gate,
             ffn_w_up=ffn_w_up, ffn_w_down=ffn_w_down, ln1_g=ln1_g, ln1_b=ln1_b,
             ln2_g=ln2_g, ln2_b=ln2_b)
    dm = x_prompt.shape[-1]
    seqs, bounds, row = [], [0], 0
    for xs in (x_prompt, x_sample):
        nb, l = xs.shape[0], xs.shape[1]
        seqs.append((row, nb, l))
        for _ in range(nb):
            row += l
            bounds.append(row)
    x = jnp.concatenate([x_prompt.reshape(-1, dm), x_sample.reshape(-1, dm)], axis=0)
    y = _trunk(x, tuple(bounds), tuple(seqs), p)
    n_p = x_prompt.shape[0] * x_prompt.shape[1]
    return (y[:n_p].reshape(x_prompt.shape), y[n_p:].reshape(x_sample.shape))
```

```python
import functools
import math

import numpy as np
import jax
import jax.numpy as jnp
from jax import lax
from jax.experimental import pallas as pl
from jax.experimental.pallas import tpu as pltpu

F32 = jnp.float32
BF16 = jnp.bfloat16

DEPTH = 4
A_HEADS, A_KV_HEADS, A_HEAD_DIM, A_RADIUS = 16, 2, 64, 128
A_WIDTH = A_HEADS * A_HEAD_DIM
A_KV_WIDTH = A_KV_HEADS * A_HEAD_DIM
B_SHORT, B_EMB = 3, 33
B_BANDS = (B_EMB - 1) // 2
B_DECAY_TARGET, B_FAST_DECAY_PCT, B_SLOW_DECAY_PCT = 1e-2, 0.3, 1.5
C_HEADS, C_HEAD_DIM = 16, 128
C_DILATIONS = (1, 4, 16)
C_RADIUS = 64
ROPE_THETA, ROPE_FRACTION = 500000.0, 4
ALPHA = (2 * DEPTH) ** 0.25
LN_EPS = 1e-5

LANES = 128
VMEM_LIMIT = 56 * 1024 * 1024
FFT_N2 = 256

ODD_CHUNK = 2048
ATT_TQ = 256


def _cparams(sem):
    return pltpu.CompilerParams(dimension_semantics=sem, vmem_limit_bytes=VMEM_LIMIT)


def _seq_bounds(row, bounds):
    start = jnp.int32(bounds[0])
    end = jnp.int32(bounds[1])
    for b0, b1 in zip(bounds[1:-1], bounds[2:]):
        inside = row >= b0
        start = jnp.where(inside, jnp.int32(b0), start)
        end = jnp.where(inside, jnp.int32(b1), end)
    return start, end


def _rope(a, c, s1, s2, half):
    w = a.shape[-1]
    return a * c + pltpu.roll(a, w - half, 1) * s1 + pltpu.roll(a, half, 1) * s2


def _mm_kernel(x_ref, w_ref, o_ref):
    o_ref[...] = jnp.dot(x_ref[...], w_ref[...], preferred_element_type=F32).astype(o_ref.dtype)


def _matmul(x, w, tn, out_dtype, tm=1024):
    m, k = x.shape
    n = w.shape[1]
    return pl.pallas_call(
        _mm_kernel,
        grid=(m // tm, n // tn),
        in_specs=[pl.BlockSpec((tm, k), lambda i, j: (i, 0)),
                  pl.BlockSpec((k, tn), lambda i, j: (0, j))],
        out_specs=pl.BlockSpec((tm, tn), lambda i, j: (i, j)),
        out_shape=jax.ShapeDtypeStruct((m, n), out_dtype),
        compiler_params=_cparams(("parallel", "arbitrary")),
        name="matmul",
    )(x, w)


def _mm_rope_kernel(x_ref, w_ref, c_ref, s1_ref, s2_ref, o_ref, *, half):
    acc = jnp.dot(x_ref[...], w_ref[...], preferred_element_type=F32)
    tw = c_ref.shape[1]
    rc = 256
    for r0 in range(0, acc.shape[0], rc):
        rows = slice(r0, r0 + rc)
        for c in range(acc.shape[1] // tw):
            cols = slice(c * tw, (c + 1) * tw)
            o_ref[rows, cols] = _rope(acc[rows, cols], c_ref[rows, :], s1_ref[rows, :], s2_ref[rows, :],
                                      half).astype(o_ref.dtype)


def _matmul_rope(x, w, tabs, half, tn, tm=1024):
    m, k = x.shape
    n = w.shape[1]
    tw = tabs[0].shape[1]
    tab_spec = pl.BlockSpec((tm, tw), lambda i, j: (i, 0))
    return pl.pallas_call(
        functools.partial(_mm_rope_kernel, half=half),
        grid=(m // tm, n // tn),
        in_specs=[pl.BlockSpec((tm, k), lambda i, j: (i, 0)),
                  pl.BlockSpec((k, tn), lambda i, j: (0, j)),
                  tab_spec, tab_spec, tab_spec],
        out_specs=pl.BlockSpec((tm, tn), lambda i, j: (i, j)),
        out_shape=jax.ShapeDtypeStruct((m, n), BF16),
        compiler_params=_cparams(("parallel", "arbitrary")),
        name="matmul_rope",
    )(x, w, *tabs)


def _odd_proj_kernel(x_ref, w_ref, c_ref, s1_ref, s2_ref, o_ref, acc_ref, *, d, rope, half):
    hps = w_ref.shape[1] // LANES
    tm = x_ref.shape[0]
    t = tm // d
    pair = 2
    rc = 256
    for p in range(hps // pair):
        acc = jnp.dot(x_ref[...], w_ref[:, p * pair * LANES:(p + 1) * pair * LANES], preferred_element_type=F32)
        for h2 in range(pair):
            hh = p * pair + h2
            slot = (p % 2) * pair + h2
            acc_ref[slot, :, :] = acc[:, h2 * LANES:(h2 + 1) * LANES]
            for c0 in range(0, tm, rc):
                a = acc_ref[slot, c0:c0 + rc, :]
                if rope:
                    a = _rope(a, c_ref[c0:c0 + rc, :], s1_ref[c0:c0 + rc, :], s2_ref[c0:c0 + rc, :], half)
                if d == 1:
                    o_ref[hh, 0, c0:c0 + rc, :] = a.astype(BF16)
                elif rope:
                    acc_ref[slot, c0:c0 + rc, :] = a
            if d > 1:
                for r in range(d):
                    o_ref[hh, r, :, :] = acc_ref[slot, pl.ds(r, t, stride=d), :].astype(BF16)


def _odd_proj(x, w, col0, tabs, half, d, rope, hps=8):
    m, k = x.shape
    tm = ODD_CHUNK
    cb0 = col0 // (hps * LANES)
    tab_spec = pl.BlockSpec((tm, LANES), lambda i, j: (i, 0))
    return pl.pallas_call(
        functools.partial(_odd_proj_kernel, d=d, rope=rope, half=half),
        grid=(m // tm, C_HEADS // hps),
        in_specs=[pl.BlockSpec((tm, k), lambda i, j: (i, 0)),
                  pl.BlockSpec((k, hps * LANES), lambda i, j: (0, cb0 + j)),
                  tab_spec, tab_spec, tab_spec],
        out_specs=pl.BlockSpec((hps, d, tm // d, LANES), lambda i, j: (j, 0, i, 0)),
        out_shape=jax.ShapeDtypeStruct((C_HEADS, d, m // d, LANES), BF16),
        scratch_shapes=[pltpu.VMEM((4, tm, LANES), F32)],
        compiler_params=_cparams(("parallel", "arbitrary")),
        name="odd_proj",
    )(x, w, *tabs)


LN_ROWS = 128


def _layer_norm_store(x_ref, acc_ref, g_ref, b_ref, of_ref, ob_ref):
    for c in range(x_ref.shape[0] // LN_ROWS):
        rows = pl.ds(c * LN_ROWS, LN_ROWS)
        r = ALPHA * x_ref[rows, :] + acc_ref[rows, :]
        mu = jnp.mean(r, axis=-1, keepdims=True)
        xc = r - mu
        var = jnp.mean(xc * xc, axis=-1, keepdims=True)
        y = xc * lax.rsqrt(var + LN_EPS) * g_ref[...] + b_ref[...]
        of_ref[rows, :] = y
        ob_ref[rows, :] = y.astype(BF16)


def _mm_ln_kernel(*refs, n_in):
    ys = refs[:n_in]
    ws = refs[n_in:2 * n_in]
    x_ref, g_ref, b_ref, of_ref, ob_ref, acc_ref = refs[2 * n_in:]
    acc = jnp.dot(ys[0][...], ws[0][...], preferred_element_type=F32)
    for y_ref, w_ref in zip(ys[1:], ws[1:]):
        acc = acc + jnp.dot(y_ref[...], w_ref[...], preferred_element_type=F32)
    acc_ref[...] = acc
    _layer_norm_store(x_ref, acc_ref, g_ref, b_ref, of_ref, ob_ref)


def _matmul_ln(ys, ws, x, g, b, tm=256):
    m, dm = x.shape
    n_in = len(ys)
    in_specs = [pl.BlockSpec((tm, y.shape[1]), lambda i: (i, 0)) for y in ys]
    in_specs += [pl.BlockSpec(w.shape, lambda i: (0, 0)) for w in ws]
    in_specs += [pl.BlockSpec((tm, dm), lambda i: (i, 0)),
                 pl.BlockSpec((1, dm), lambda i: (0, 0)),
                 pl.BlockSpec((1, dm), lambda i: (0, 0))]
    return pl.pallas_call(
        functools.partial(_mm_ln_kernel, n_in=n_in),
        grid=(m // tm,),
        in_specs=in_specs,
        out_specs=[pl.BlockSpec((tm, dm), lambda i: (i, 0)), pl.BlockSpec((tm, dm), lambda i: (i, 0))],
        out_shape=[jax.ShapeDtypeStruct((m, dm), F32), jax.ShapeDtypeStruct((m, dm), BF16)],
        scratch_shapes=[pltpu.VMEM((tm, dm), F32)],
        compiler_params=_cparams(("parallel",)),
        name="matmul_ln",
    )(*ys, *ws, x, g.reshape(1, dm), b.reshape(1, dm))


def _ffn_kernel(xb_ref, xf_ref, wg_ref, wu_ref, wd_ref, g_ref, b_ref, of_ref, ob_ref, acc_ref):
    j = pl.program_id(1)

    @pl.when(j == 0)
    def _():
        acc_ref[...] = jnp.zeros_like(acc_ref)

    xb = xb_ref[...]
    gate = jnp.dot(xb, wg_ref[...], preferred_element_type=F32)
    up = jnp.dot(xb, wu_ref[...], preferred_element_type=F32)
    h = (gate * jax.nn.sigmoid(gate)) * up
    acc_ref[...] += jnp.dot(h.astype(BF16), wd_ref[...], preferred_element_type=F32)

    @pl.when(j == pl.num_programs(1) - 1)
    def _():
        _layer_norm_store(xf_ref, acc_ref, g_ref, b_ref, of_ref, ob_ref)


def _ffn_ln(xb, xf, wg, wu, wd, g, b, tm=512, tf=512):
    m, dm = xf.shape
    dff = wg.shape[1]
    row = lambda i, j: (i, 0)
    return pl.pallas_call(
        _ffn_kernel,
        grid=(m // tm, dff // tf),
        in_specs=[pl.BlockSpec((tm, dm), row), pl.BlockSpec((tm, dm), row),
                  pl.BlockSpec((dm, tf), lambda i, j: (0, j)),
                  pl.BlockSpec((dm, tf), lambda i, j: (0, j)),
                  pl.BlockSpec((tf, dm), lambda i, j: (j, 0)),
                  pl.BlockSpec((1, dm), lambda i, j: (0, 0)),
                  pl.BlockSpec((1, dm), lambda i, j: (0, 0))],
        out_specs=[pl.BlockSpec((tm, dm), row), pl.BlockSpec((tm, dm), row)],
        out_shape=[jax.ShapeDtypeStruct((m, dm), F32), jax.ShapeDtypeStruct((m, dm), BF16)],
        scratch_shapes=[pltpu.VMEM((tm, dm), F32)],
        compiler_params=_cparams(("parallel", "arbitrary")),
        name="ffn_ln",
    )(xb, xf, wg, wu, wd, g.reshape(1, dm), b.reshape(1, dm))


def _even_attn_kernel(sink_ref, q_ref, kp_ref, km_ref, kn_ref, o_ref, *, bounds):
    i = pl.program_id(0)
    tq = q_ref.shape[0]
    nk = tq + 2 * A_RADIUS
    row0 = i * tq
    start, end = _seq_bounds(row0, bounds)
    kv = jnp.concatenate([kp_ref[...], km_ref[...], kn_ref[...]], axis=0)
    rq = row0 + lax.broadcasted_iota(jnp.int32, (tq, nk), 0)
    rk = row0 - A_RADIUS + lax.broadcasted_iota(jnp.int32, (tq, nk), 1)
    mask = (jnp.abs(rk - rq) <= A_RADIUS) & (rk >= start) & (rk < end)
    group = A_HEADS // A_KV_HEADS
    scale = A_HEAD_DIM ** -0.5
    for j in range(A_KV_HEADS):
        k = kv[:, j * A_HEAD_DIM:(j + 1) * A_HEAD_DIM]
        v = kv[:, A_KV_WIDTH + j * A_HEAD_DIM:A_KV_WIDTH + (j + 1) * A_HEAD_DIM]
        for gq in range(group):
            h = j * group + gq
            qh = q_ref[:, h * A_HEAD_DIM:(h + 1) * A_HEAD_DIM]
            s = lax.dot_general(qh, k, (((1,), (1,)), ((), ())), preferred_element_type=F32) * scale
            s = jnp.where(mask, s, -jnp.inf)
            sk = sink_ref[h]
            m = jnp.maximum(jnp.max(s, axis=-1, keepdims=True), sk)
            p = jnp.exp(s - m)
            den = jnp.sum(p, axis=-1, keepdims=True) + jnp.exp(sk - m)
            o = jnp.dot(p.astype(BF16), v, preferred_element_type=F32) / den
            o_ref[:, h * A_HEAD_DIM:(h + 1) * A_HEAD_DIM] = o.astype(o_ref.dtype)


def _even_attention(q, kv, sink, bounds):
    m = q.shape[0]
    tq = ATT_TQ
    hb = A_RADIUS
    per = tq // hb
    last = m // hb - 1
    kvw = kv.shape[1]
    return pl.pallas_call(
        functools.partial(_even_attn_kernel, bounds=bounds),
        grid=(m // tq,),
        in_specs=[pl.BlockSpec(memory_space=pltpu.SMEM),
                  pl.BlockSpec((tq, A_WIDTH), lambda i: (i, 0)),
                  pl.BlockSpec((hb, kvw), lambda i: (jnp.maximum(i * per - 1, 0), 0)),
                  pl.BlockSpec((tq, kvw), lambda i: (i, 0)),
                  pl.BlockSpec((hb, kvw), lambda i: (jnp.minimum((i + 1) * per, last), 0))],
        out_specs=pl.BlockSpec((tq, A_WIDTH), lambda i: (i, 0)),
        out_shape=jax.ShapeDtypeStruct((m, A_WIDTH), BF16),
        compiler_params=_cparams(("parallel",)),
        name="even_attention",
    )(sink, q, kv, kv, kv)


def _odd_attn_kernel(*refs, bounds):
    ng = len(C_DILATIONS)
    o_ref, oacc, lacc = refs[7 * ng:]
    i = pl.program_id(0)
    chunk = ODD_CHUNK
    qb = 128
    nk = qb + 2 * C_RADIUS
    row0 = i * chunk
    start, end = _seq_bounds(row0, bounds)
    rr = lax.broadcasted_iota(jnp.int32, (qb, nk), 0)
    cc = lax.broadcasted_iota(jnp.int32, (qb, nk), 1)
    band = jnp.abs(cc - C_RADIUS - rr) <= C_RADIUS
    scale = C_HEAD_DIM ** -0.5
    for g, d in enumerate(C_DILATIONS):
        q_ref, kp_ref, km_ref, kn_ref, vp_ref, vm_ref, vn_ref = refs[7 * g:7 * g + 7]
        tg = chunk // d
        t_lo, t_hi, t_c0 = start // d, end // d, row0 // d
        for sb in range(tg // qb):
            lo, hi = qb * sb - C_RADIUS, qb * sb + qb + C_RADIUS
            tk = t_c0 + lo + cc
            mask = band & (tk >= t_lo) & (tk < t_hi)
            for r in range(d):
                def window(p_ref, m_ref, n_ref):
                    parts = []
                    if lo < 0:
                        parts.append(p_ref[0, r, :, :])
                    parts.append(m_ref[0, r, max(lo, 0):min(hi, tg), :])
                    if hi > tg:
                        parts.append(n_ref[0, r, :, :])
                    return parts[0] if len(parts) == 1 else jnp.concatenate(parts, axis=0)

                q = q_ref[0, r, qb * sb:qb * (sb + 1), :]
                k = window(kp_ref, km_ref, kn_ref)
                v = window(vp_ref, vm_ref, vn_ref)
                s = lax.dot_general(q, k, (((1,), (1,)), ((), ())), preferred_element_type=F32) * scale
                s = jnp.where(mask, s, -jnp.inf)
                m = jnp.max(s, axis=-1, keepdims=True)
                p = jnp.exp(s - m)
                den = jnp.sum(p, axis=-1, keepdims=True)
                o = jnp.dot(p.astype(BF16), v, preferred_element_type=F32) / den
                lse = jnp.broadcast_to(m + jnp.log(den), (qb, LANES))
                if d == 1:
                    rows = pl.ds(qb * sb, qb)
                else:
                    rows = pl.ds(r + d * qb * sb, qb, stride=d)
                oacc[g, rows, :] = o
                lacc[g, rows, :] = lse
    ls = [lacc[g] for g in range(ng)]
    mx = functools.reduce(jnp.maximum, ls)
    ws = [jnp.exp(l - mx) for l in ls]
    tot = functools.reduce(lambda a, b: a + b, ws)
    out = functools.reduce(lambda a, b: a + b, [(ws[g] / tot) * oacc[g] for g in range(ng)])
    o_ref[...] = out.astype(o_ref.dtype)


def _odd_attention(qkv, bounds, m):
    chunk = ODD_CHUNK
    hb = C_RADIUS
    operands, in_specs = [], []
    for (q, k, v), d in zip(qkv, C_DILATIONS):
        tg = chunk // d
        per = tg // hb
        last = m // d // hb - 1
        main = pl.BlockSpec((1, d, tg, LANES), lambda i, h: (h, 0, i, 0))
        prev = pl.BlockSpec((1, d, hb, LANES), lambda i, h, per=per: (h, 0, jnp.maximum(i * per - 1, 0), 0))
        nxt = pl.BlockSpec((1, d, hb, LANES), lambda i, h, per=per, last=last: (h, 0, jnp.minimum((i + 1) * per, last), 0))
        operands += [q, k, k, k, v, v, v]
        in_specs += [main, prev, main, nxt, prev, main, nxt]
    ng = len(C_DILATIONS)
    return pl.pallas_call(
        functools.partial(_odd_attn_kernel, bounds=bounds),
        grid=(m // chunk, C_HEADS),
        in_specs=in_specs,
        out_specs=pl.BlockSpec((chunk, LANES), lambda i, h: (i, h)),
        out_shape=jax.ShapeDtypeStruct((m, C_HEADS * C_HEAD_DIM), BF16),
        scratch_shapes=[pltpu.VMEM((ng, chunk, LANES), F32), pltpu.VMEM((ng, chunk, LANES), F32)],
        compiler_params=_cparams(("parallel", "arbitrary")),
        name="odd_attention",
    )(*operands)


def _short_conv_kernel(xp_ref, xm_ref, xn_ref, w_ref, b_ref, o_ref, *, bounds):
    i = pl.program_id(0)
    tr = xm_ref.shape[0]
    row0 = i * tr
    start, end = _seq_bounds(row0, bounds)
    x = xm_ref[...]
    hp = xp_ref.shape[0]
    before = jnp.where(row0 > start, xp_ref[hp - 1:hp, :], 0.0)
    after = jnp.where(row0 + tr < end, xn_ref[0:1, :], 0.0)
    ridx = lax.broadcasted_iota(jnp.int32, x.shape, 0)
    xl = jnp.where(ridx == 0, before, pltpu.roll(x, 1, 0))
    xr = jnp.where(ridx == tr - 1, after, pltpu.roll(x, tr - 1, 0))
    y = b_ref[...] + xl * w_ref[0:1, :]
    y = y + x * w_ref[1:2, :]
    y = y + xr * w_ref[2:3, :]
    o_ref[...] = y


def _short_conv(x, w, b, bounds, tr=512, tc=512):
    m, n = x.shape
    hb = 8
    per = tr // hb
    last = m // hb - 1
    return pl.pallas_call(
        functools.partial(_short_conv_kernel, bounds=bounds),
        grid=(m // tr, n // tc),
        in_specs=[pl.BlockSpec((hb, tc), lambda i, j: (jnp.maximum(i * per - 1, 0), j)),
                  pl.BlockSpec((tr, tc), lambda i, j: (i, j)),
                  pl.BlockSpec((hb, tc), lambda i, j: (jnp.minimum((i + 1) * per, last), j)),
                  pl.BlockSpec((B_SHORT, tc), lambda i, j: (0, j)),
                  pl.BlockSpec((1, tc), lambda i, j: (0, j))],
        out_specs=pl.BlockSpec((tr, tc), lambda i, j: (i, j)),
        out_shape=jax.ShapeDtypeStruct((m, n), F32),
        compiler_params=_cparams(("parallel", "arbitrary")),
        name="short_conv",
    )(x, x, x, w, b.reshape(1, n))


def _filter_mlp_kernel(z_ref, w1_ref, b1_ref, f1_ref, w2_ref, b2_ref, f2_ref, w3_ref, b3_ref, dl_ref,
                       h_ref, nrm_ref):
    i = pl.program_id(0)
    z = z_ref[...]
    h = jnp.sin(f1_ref[...] * (jnp.dot(z.astype(BF16), w1_ref[...], preferred_element_type=F32) + b1_ref[...]))
    h = jnp.sin(f2_ref[...] * (jnp.dot(h.astype(BF16), w2_ref[...], preferred_element_type=F32) + b2_ref[...]))
    h = jnp.dot(h.astype(BF16), w3_ref[...], preferred_element_type=F32) + b3_ref[...]
    decay = jnp.exp(-z[:, 0:1] * dl_ref[...])
    nrep = h.shape[1] // decay.shape[1]
    h = h * jnp.concatenate([decay] * nrep, axis=1)
    h_ref[...] = h

    @pl.when(i == 0)
    def _():
        nrm_ref[...] = jnp.zeros_like(nrm_ref)

    half = h.shape[1] // 2
    col = lax.broadcasted_iota(jnp.int32, h.shape, 1)
    row = lax.broadcasted_iota(jnp.int32, h.shape, 0) + i * h.shape[0]
    a = jnp.where((col >= half) & (row == 0), 0.0, jnp.abs(h))
    nrm_ref[...] += jnp.sum(a, axis=0, keepdims=True)


def _filter_mlp(z, w1, b1, f1, w2, b2, f2, w3, b3, deltas, tl=512):
    l, e = z.shape
    hid = w1.shape[1]
    n = w3.shape[1]
    c = deltas.shape[0]
    full = lambda shape: pl.BlockSpec(shape, lambda i: (0, 0))
    return pl.pallas_call(
        _filter_mlp_kernel,
        grid=(l // tl,),
        in_specs=[pl.BlockSpec((tl, e), lambda i: (i, 0)),
                  full((e, hid)), full((1, hid)), full((1, hid)),
                  full((hid, hid)), full((1, hid)), full((1, hid)),
                  full((hid, n)), full((1, n)), full((1, c))],
        out_specs=[pl.BlockSpec((tl, n), lambda i: (i, 0)), full((1, n))],
        out_shape=[jax.ShapeDtypeStruct((l, n), F32), jax.ShapeDtypeStruct((1, n), F32)],
        compiler_params=_cparams(("arbitrary",)),
        name="filter_mlp",
    )(z, w1, b1.reshape(1, hid), f1.reshape(1, hid), w2, b2.reshape(1, hid), f2.reshape(1, hid),
      w3, b3.reshape(1, n), deltas.reshape(1, c))


def _split(x):
    hi = x.astype(BF16)
    lo = (x - hi.astype(F32)).astype(BF16)
    return hi, lo


def _dot3(ch, cl, x):
    xh, xl = _split(x)
    r = jnp.dot(ch, xh, preferred_element_type=F32)
    r = r + jnp.dot(ch, xl, preferred_element_type=F32)
    return r + jnp.dot(cl, xh, preferred_element_type=F32)


def _dot1(ch, x):
    return jnp.dot(ch, x.astype(BF16), preferred_element_type=F32)


FFT_GROUP = 8


FFT_N2C = 4
FFT_RH = 24
FFT_SPB = 6


def _fft1_kernel(x_ref, fh_ref, fl_ref, ar_ref, ai_ref, *, kn1, n2):
    kh, rows, _ = ar_ref.shape
    n2q = n2 // FFT_N2C
    rh = rows // n2q
    rp = kh * rh
    c0 = pl.program_id(2) * n2q
    for g in range(n2q // FFT_GROUP):
        cols = [x_ref[pl.ds(c0 + g * FFT_GROUP + s, kn1, stride=n2), :] for s in range(FFT_GROUP)]
        out = _dot3(fh_ref[...], fl_ref[...], jnp.concatenate(cols, axis=1))
        for k in range(kh):
            for s in range(FFT_GROUP):
                r0 = (g * FFT_GROUP + s) * rh
                ar_ref[k, r0:r0 + rh, :] = out[k * rh:(k + 1) * rh, s * LANES:(s + 1) * LANES]
                ai_ref[k, r0:r0 + rh, :] = out[rp + k * rh:rp + (k + 1) * rh, s * LANES:(s + 1) * LANES]


def _fft_stage1(x, f1h, f1l, plan, l, nb, row_blk0, col_blk0, c):
    n2, kh, rh = plan["n2"], plan["kh"], plan["rh"]
    kn1 = f1h.shape[1]
    rows = n2 // FFT_N2C * rh
    out = jax.ShapeDtypeStruct((nb, kh, FFT_N2C, rows, c), F32)
    fspec = pl.BlockSpec(f1h.shape, lambda b, j, q: (0, 0))
    ospec = pl.BlockSpec((None, kh, None, rows, LANES), lambda b, j, q: (b, 0, q, 0, j))
    return pl.pallas_call(
        functools.partial(_fft1_kernel, kn1=kn1, n2=n2),
        grid=(nb, c // LANES, FFT_N2C),
        in_specs=[pl.BlockSpec((l, LANES), lambda b, j, q: (row_blk0 + b, col_blk0 + j)), fspec, fspec],
        out_specs=[ospec, ospec],
        out_shape=[out, out],
        compiler_params=_cparams(("parallel", "parallel", "arbitrary")),
        name="fft_stage1",
    )(x, f1h, f1l)


def _cpair(p, n):
    return p[:n, :LANES] - p[n:, LANES:], p[:n, LANES:] + p[n:, :LANES]


def _twiddle(xr, xi, tr, ti):
    return xr * tr - xi * ti, xr * ti + xi * tr


def _filter_mid_kernel(fr_ref, fi_ref, br_ref, bi_ref, tr_ref, ti_ref, w_ref, inv_ref, b0_ref,
                       kr_ref, ki_ref, *, nslab, rh, n2):
    k0 = pl.program_id(2) * FFT_SPB

    @pl.when(pl.program_id(1) * rh + k0 < nslab)
    def _():
        for t in range(FFT_SPB):
            rows = pl.ds(k0 + t, n2, stride=rh)
            tr, ti = tr_ref[t], ti_ref[t]
            x4 = jnp.concatenate(_twiddle(fr_ref[rows, :], fi_ref[rows, :], tr, ti)
                                 + _twiddle(br_ref[rows, :], bi_ref[rows, :], tr, ti), axis=1)
            p = _dot1(w_ref[...], x4)
            fr, fi = _cpair(p[:, :2 * LANES], n2)
            br, bi = _cpair(p[:, 2 * LANES:], n2)
            kr_ref[t] = (fr + (br - b0_ref[...])) * inv_ref[...]
            ki_ref[t] = (fi - bi) * inv_ref[...]

    @pl.when(pl.program_id(1) * rh + k0 >= nslab)
    def _():
        kr_ref[...] = jnp.zeros_like(kr_ref)
        ki_ref[...] = jnp.zeros_like(ki_ref)


def _filter_mid(ar, ai, inv, b0, plan):
    n2, kh, rh, rp = plan["n2"], plan["kh"], plan["rh"], plan["rp"]
    oc = ar.shape[-1] // 2
    nj = oc // LANES
    fwd = pl.BlockSpec((None, None, n2 * rh, LANES), lambda j, kk, k: (0, kk, 0, j))
    bwd = pl.BlockSpec((None, None, n2 * rh, LANES), lambda j, kk, k: (0, kk, 0, j + nj))
    steps = rh // FFT_SPB
    tspec = pl.BlockSpec((FFT_SPB, n2, LANES), lambda j, kk, k: (kk * steps + k, 0, 0))
    wspec = pl.BlockSpec((2 * n2, n2), lambda j, kk, k: (0, 0))
    vec = pl.BlockSpec((1, LANES), lambda j, kk, k: (0, j))
    ospec = pl.BlockSpec((FFT_SPB, n2, LANES), lambda j, kk, k: (kk * steps + k, 0, j))
    out = jax.ShapeDtypeStruct((rp, n2, oc), F32)
    return pl.pallas_call(
        functools.partial(_filter_mid_kernel, nslab=plan["r"], rh=rh, n2=n2),
        grid=(nj, kh, steps),
        in_specs=[fwd, fwd, bwd, bwd, tspec, tspec, wspec, vec, vec],
        out_specs=[ospec, ospec],
        out_shape=[out, out],
        compiler_params=_cparams(("parallel", "arbitrary", "arbitrary")),
        name="filter_mid",
    )(ar, ai, ar, ai, *plan["tw"], plan["f2"], inv, b0)


def _conv_mid_kernel(ar_ref, ai_ref, tr_ref, ti_ref, wf_ref, wi_ref, kr_ref, ki_ref,
                     dr_ref, di_ref, *, nslab, rh, n2):
    k0 = pl.program_id(3) * FFT_SPB

    @pl.when(pl.program_id(2) * rh + k0 < nslab)
    def _():
        for t in range(FFT_SPB):
            rows = pl.ds(k0 + t, n2, stride=rh)
            tr, ti = tr_ref[t], ti_ref[t]
            x2 = jnp.concatenate(_twiddle(ar_ref[rows, :], ai_ref[rows, :], tr, ti), axis=1)
            xr, xi = _cpair(_dot1(wf_ref[...], x2), n2)
            kr, ki = kr_ref[t], ki_ref[t]
            y2 = jnp.concatenate([xr * kr - xi * ki, xr * ki + xi * kr], axis=1)
            cr, ci = _cpair(_dot1(wi_ref[...], y2), n2)
            dr, di = _twiddle(cr, ci, tr, -ti)
            dr_ref[rows, :] = dr
            di_ref[rows, :] = di

    @pl.when(pl.program_id(2) * rh + k0 >= nslab)
    def _():
        for t in range(FFT_SPB):
            rows = pl.ds(k0 + t, n2, stride=rh)
            dr_ref[rows, :] = jnp.zeros((n2, LANES), F32)
            di_ref[rows, :] = jnp.zeros((n2, LANES), F32)


def _conv_mid(ar, ai, kr, ki, order, plan):
    nb, kh, _, c = ar.shape
    n2, rh = plan["n2"], plan["rh"]
    koff = order * (c // LANES)
    blk = pl.BlockSpec((None, None, n2 * rh, LANES), lambda b, j, kk, k: (b, kk, 0, j))
    steps = rh // FFT_SPB
    tspec = pl.BlockSpec((FFT_SPB, n2, LANES), lambda b, j, kk, k: (kk * steps + k, 0, 0))
    wspec = pl.BlockSpec((2 * n2, n2), lambda b, j, kk, k: (0, 0))
    kspec = pl.BlockSpec((FFT_SPB, n2, LANES), lambda b, j, kk, k: (kk * steps + k, 0, j + koff))
    out = jax.ShapeDtypeStruct(ar.shape, F32)
    return pl.pallas_call(
        functools.partial(_conv_mid_kernel, nslab=plan["r"], rh=rh, n2=n2),
        grid=(nb, c // LANES, kh, steps),
        in_specs=[blk, blk, tspec, tspec, wspec, wspec, kspec, kspec],
        out_specs=[blk, blk],
        out_shape=[out, out],
        compiler_params=_cparams(("parallel", "parallel", "arbitrary", "arbitrary")),
        name="conv_mid",
    )(ar, ai, *plan["tw"], plan["f2"], plan["f2i"], kr, ki)


def _fft_out_kernel(dr_ref, di_ref, gh_ref, gl_ref, y_ref, *, n2):
    kh, rows, _ = dr_ref.shape
    n2q = n2 // FFT_N2C
    rh = rows // n2q
    n1c = y_ref.shape[0]
    n1r = y_ref.shape[1] // n2q
    for g in range(n2q // FFT_GROUP):
        cols = []
        for s in range(FFT_GROUP):
            r0 = (g * FFT_GROUP + s) * rh
            parts = [dr_ref[k, r0:r0 + rh, :] for k in range(kh)] + [di_ref[k, r0:r0 + rh, :] for k in range(kh)]
            cols.append(jnp.concatenate(parts, axis=0))
        y = _dot3(gh_ref[...], gl_ref[...], jnp.concatenate(cols, axis=1))
        for c in range(n1c):
            for s in range(FFT_GROUP):
                t0 = (g * FFT_GROUP + s) * n1r
                y_ref[c, t0:t0 + n1r, :] = y[c * n1r:(c + 1) * n1r, s * LANES:(s + 1) * LANES]


def _fft_out(dr, di, gh, gl, plan):
    nb, kh, _, rows, c = dr.shape
    n2 = plan["n2"]
    nh = gh.shape[0]
    n1r = min(16, nh)
    n1c = nh // n1r
    dspec = pl.BlockSpec((None, kh, None, rows, LANES), lambda b, j, q: (b, 0, q, 0, j))
    gspec = pl.BlockSpec(gh.shape, lambda b, j, q: (0, 0))
    yrows = n2 // FFT_N2C * n1r
    return pl.pallas_call(
        functools.partial(_fft_out_kernel, n2=n2),
        grid=(nb, c // LANES, FFT_N2C),
        in_specs=[dspec, dspec, gspec, gspec],
        out_specs=pl.BlockSpec((None, n1c, yrows, LANES), lambda b, j, q: (b, 0, q, j)),
        out_shape=jax.ShapeDtypeStruct((nb, n1c, n2 * n1r, c), F32),
        compiler_params=_cparams(("parallel", "parallel", "arbitrary")),
        name="fft_out",
    )(dr, di, gh, gl)


def _gate_kernel(y_ref, u_ref, g_ref, bias_ref, o_ref, *, n2):
    n1r = y_ref.shape[0] // n2
    for a in range(n1r):
        rows = slice(a * n2, (a + 1) * n2)
        yt = y_ref[pl.ds(a, n2, stride=n1r), :]
        o_ref[rows, :] = (g_ref[rows, :] * (yt + u_ref[rows, :] * bias_ref[...])).astype(o_ref.dtype)


def _gate(y, n2, u, u_row0, u_col0, gate, g_row0, g_col0, bias, out_dtype):
    nb, n1c, yr, c = y.shape
    tr = yr
    return pl.pallas_call(
        functools.partial(_gate_kernel, n2=n2),
        grid=(nb, c // LANES, n1c),
        in_specs=[pl.BlockSpec((None, None, yr, LANES), lambda b, j, q: (b, q, 0, j)),
                  pl.BlockSpec((tr, LANES), lambda b, j, q: (u_row0 // tr + b * n1c + q, u_col0 + j)),
                  pl.BlockSpec((tr, LANES), lambda b, j, q: (g_row0 // tr + b * n1c + q, g_col0 + j)),
                  pl.BlockSpec((1, LANES), lambda b, j, q: (0, j))],
        out_specs=pl.BlockSpec((tr, LANES), lambda b, j, q: (b * n1c + q, j)),
        out_shape=jax.ShapeDtypeStruct((nb * n1c * tr, c), out_dtype),
        compiler_params=_cparams(("parallel", "parallel", "arbitrary")),
        name="hyena_gate",
    )(y, u, gate, bias)


def _np_split(a):
    a32 = np.asarray(a, np.float32)
    hi = a32.astype(BF16)
    lo = (a32 - hi.astype(np.float32)).astype(BF16)
    return hi, lo


def _fft_plan(l):
    n = 2 * l
    n2 = FFT_N2
    n1 = n // n2
    r = n1 // 2 + 1
    rh = FFT_RH
    kh = -(-r // rh)
    rp = kh * rh
    kn1 = n1 // 2
    k1 = np.arange(rp, dtype=np.float64)[:, None]
    live = (k1 < r).astype(np.float64)

    ang = 2 * np.pi * k1 * np.arange(kn1)[None, :] / n1
    f1 = np.concatenate([np.cos(ang) * live, -np.sin(ang) * live], axis=0)

    kk = np.arange(rp, dtype=np.float64)[None, :]
    wgt = np.where((kk == 0) | (kk == n1 // 2), 1.0, 2.0) * (kk < r) / n
    ango = 2 * np.pi * np.arange(n1 // 2)[:, None] * kk / n1
    g = np.concatenate([np.cos(ango) * wgt, -np.sin(ango) * wgt], axis=1)

    a2 = 2 * np.pi * np.outer(np.arange(n2), np.arange(n2)) / n2
    f2 = np.concatenate([np.cos(a2), -np.sin(a2)], axis=0)
    f2i = np.concatenate([np.cos(a2), np.sin(a2)], axis=0)

    idx = jnp.arange(rp, dtype=jnp.int32)[:, None] * jnp.arange(n2, dtype=jnp.int32)[None, :]
    ang = idx.astype(F32) * F32(2.0 * math.pi / n)
    tw = tuple(jnp.broadcast_to(t[:, :, None], (rp, n2, LANES)) for t in (jnp.cos(ang), -jnp.sin(ang)))
    return dict(n1=n1, n2=n2, r=r, rp=rp, kh=kh, rh=rh, f1=_np_split(f1), g=_np_split(g),
                f2=_np_split(f2)[0], f2i=_np_split(f2i)[0], tw=tw)


def _filter_features(l):
    t = jnp.linspace(0.0, 1.0, l, dtype=F32)[:, None]
    bands = jnp.linspace(1e-4, B_BANDS - 1, B_BANDS, dtype=F32)[None, :]
    w = 2.0 * math.pi * jnp.arange(l, dtype=F32)[:, None] / l
    return jnp.concatenate([t, jnp.cos(bands * w), -jnp.sin(bands * w)], axis=-1)


def _hyena_filters(plan, l, c, w1, b1, f1, w2, b2, f2, w3, b3):
    n1, n2 = plan["n1"], plan["n2"]
    z = _filter_features(l)
    e = z.shape[1]
    ep = -(-e // 16) * 16
    z = jnp.pad(z, ((0, 0), (0, ep - e)))
    w1p = jnp.pad(w1, ((0, ep - e), (0, 0))).astype(BF16)
    max_decay = math.log(B_DECAY_TARGET) / B_FAST_DECAY_PCT
    min_decay = math.log(B_DECAY_TARGET) / B_SLOW_DECAY_PCT
    deltas = jnp.abs(jnp.linspace(min_decay, max_decay, c, dtype=F32))
    h, nrm = _filter_mlp(z, w1p, b1, f1, w2.astype(BF16), b2, f2, w3.astype(BF16), b3, deltas,
                         tl=min(512, l))
    oc = h.shape[1] // 2
    inv = 1.0 / (nrm[:, :oc] + nrm[:, oc:])
    b0 = h[0:1, oc:]
    ar, ai = _fft_stage1(h, *plan["f1"], plan, l, 1, 0, 0, 2 * oc)
    mid_shape = (1, plan["kh"], n2 * plan["rh"], 2 * oc)
    return _filter_mid(ar.reshape(mid_shape), ai.reshape(mid_shape), inv, b0, plan)


def _hyena_conv(plan, kf, order, l, nb, u, u_row0, u_col0, gate, g_row0, g_col0, bias, c, out_dtype):
    n2 = plan["n2"]
    cb = c // LANES
    ar, ai = _fft_stage1(u, *plan["f1"], plan, l, nb, u_row0 // l, u_col0 * cb, c)
    mid_shape = (nb, plan["kh"], n2 * plan["rh"], c)
    dr, di = _conv_mid(ar.reshape(mid_shape), ai.reshape(mid_shape), kf[0], kf[1], order, plan)
    y = _fft_out(dr.reshape(ar.shape), di.reshape(ar.shape), *plan["g"], plan)
    return _gate(y, n2, u, u_row0, u_col0 * cb, gate, g_row0, g_col0 * cb, bias.reshape(1, c), out_dtype)


def _hyena_mixer(u, seqs, plans, c, fw, hy_bias):
    outs = []
    for (row0, nb, l) in seqs:
        plan = plans[l]
        kf = _hyena_filters(plan, l, c, *fw)
        z = _hyena_conv(plan, kf, 0, l, nb, u, row0, 0, u, row0, 1, hy_bias[0], c, F32)
        o = _hyena_conv(plan, kf, 1, l, nb, z, 0, 0, u, row0, 2, hy_bias[1], c, BF16)
        outs.append(o)
    return jnp.concatenate(outs, axis=0)


def _rope_tables(pos, hd):
    rot = hd // ROPE_FRACTION
    half = rot // 2
    inv = ROPE_THETA ** (-(jnp.arange(half, dtype=F32) * 2.0 / rot))
    ang = pos[:, None] * inv[None, :]
    cos, sin = jnp.cos(ang), jnp.sin(ang)
    m = pos.shape[0]
    one = jnp.ones((m, hd - rot), F32)
    zero = jnp.zeros((m, hd - rot), F32)
    zh = jnp.zeros((m, half), F32)
    c = jnp.concatenate([cos, cos, one], axis=1)
    s1 = jnp.concatenate([-sin, zh, zero], axis=1)
    s2 = jnp.concatenate([zh, sin, zero], axis=1)
    rep = LANES // hd
    return tuple(jnp.tile(t, (1, rep)) for t in (c, s1, s2)), half


def _trunk(x, bounds, seqs, p):
    m, dm = x.shape
    pos = jnp.concatenate([jnp.tile(jnp.arange(l, dtype=F32), nb) for (_, nb, l) in seqs])
    tabs_a, half_a = _rope_tables(pos, A_HEAD_DIM)
    tabs_c, half_c = _rope_tables(pos, C_HEAD_DIM)
    ident = (jnp.ones((m, LANES), F32), jnp.zeros((m, LANES), F32), jnp.zeros((m, LANES), F32))
    tabs_kv = tuple(jnp.concatenate([a, b], axis=1) for a, b in zip(tabs_a, ident))
    c_hy = dm - A_WIDTH
    plans = {l: _fft_plan(l) for l in sorted({l for (_, _, l) in seqs})}
    xf = x
    xb = x.astype(BF16)
    for i in range(DEPTH):
        j = i // 2
        if i % 2 == 0:
            w_in = p['mix_e_w_in'][j].astype(BF16)
            kv0 = A_WIDTH
            hy0 = A_WIDTH + 2 * A_KV_WIDTH
            q = _matmul_rope(xb, w_in[:, :kv0], tabs_a, half_a, tn=512)
            kv = _matmul_rope(xb, w_in[:, kv0:hy0], tabs_kv, half_a, tn=2 * A_KV_WIDTH)
            hy = _matmul(xb, w_in[:, hy0:], tn=512, out_dtype=F32)
            a_out = _even_attention(q, kv, p['a_sink'][j], bounds)
            u = _short_conv(hy, p['hy_conv_w'][j], p['hy_conv_b'][j], bounds)
            fw = (p['hy_w1'][j], p['hy_b1'][j], p['hy_f1'][j], p['hy_w2'][j], p['hy_b2'][j],
                  p['hy_f2'][j], p['hy_w3'][j], p['hy_b3'][j])
            h_out = _hyena_mixer(u, seqs, plans, c_hy, fw, p['hy_bias'][j])
            w_out = p['mix_e_w_out'][j].astype(BF16)
            xf, xb = _matmul_ln([a_out, h_out], [w_out[:A_WIDTH], w_out[A_WIDTH:]], xf,
                                p['ln1_g'][i], p['ln1_b'][i])
        else:
            w_in = p['mix_o_w_in'][j].astype(BF16)
            gw = C_HEADS * C_HEAD_DIM
            ng = len(C_DILATIONS)
            qkv = []
            for g, d in enumerate(C_DILATIONS):
                trio = []
                for part in range(3):
                    c0 = (part * ng + g) * gw
                    trio.append(_odd_proj(xb, w_in, c0, tabs_c, half_c, d, rope=part < 2))
                qkv.append(tuple(trio))
            o = _odd_attention(qkv, bounds, m)
            xf, xb = _matmul_ln([o], [p['mix_o_w_out'][j].astype(BF16)], xf, p['ln1_g'][i], p['ln1_b'][i])
        xf, xb = _ffn_ln(xb, xf, p['ffn_w_gate'][i].astype(BF16), p['ffn_w_up'][i].astype(BF16),
                         p['ffn_w_down'][i].astype(BF16), p['ln2_g'][i], p['ln2_b'][i])
    return xf


def kernel(x_prompt, x_sample, mix_e_w_in, a_sink, hy_conv_w, hy_conv_b, hy_w1, hy_b1, hy_f1, hy_w2, hy_b2,
           hy_f2, hy_w3, hy_b3, hy_bias, mix_e_w_out, mix_o_w_in, mix_o_w_out, ffn_w_gate, ffn_w_up,
           ffn_w_down, ln1_g, ln1_b, ln2_g, ln2_b):
    p = dict(mix_e_w_in=mix_e_w_in, a_sink=a_sink, hy_conv_w=hy_conv_w, hy_conv_b=hy_conv_b,
             hy_w1=hy_w1, hy_b1=hy_b1, hy_f1=hy_f1, hy_w2=hy_w2, hy_b2=hy_b2, hy_f2=hy_f2,
             hy_w3=hy_w3, hy_b3=hy_b3, hy_bias=hy_bias, mix_e_w_out=mix_e_w_out,
             mix_o_w_in=mix_o_w_in, mix_o_w_out=mix_o_w_out, ffn_w_gate=ffn_w_gate,
             ffn_w_up=ffn_w_up, ffn_w_down=ffn_w_down, ln1_g=ln1_g, ln1_b=ln1_b,
             ln2_g=ln2_g, ln2_b=ln2_b)
    dm = x_prompt.shape[-1]
    seqs, bounds, row = [], [0], 0
    for xs in (x_prompt, x_sample):
        nb, l = xs.shape[0], xs.shape[1]
        seqs.append((row, nb, l))
        for _ in range(nb):
            row += l
            bounds.append(row)
    x = jnp.concatenate([x_prompt.reshape(-1, dm), x_sample.reshape(-1, dm)], axis=0)
    y = _trunk(x, tuple(bounds), tuple(seqs), p)
    n_p = x_prompt.shape[0] * x_prompt.shape[1]
    return (y[:n_p].reshape(x_prompt.shape), y[n_p:].reshape(x_sample.shape))
```

```python
import functools
import math

import numpy as np
import jax
import jax.numpy as jnp
from jax import lax
from jax.experimental import pallas as pl
from jax.experimental.pallas import tpu as pltpu

F32 = jnp.float32
BF16 = jnp.bfloat16

DEPTH = 4
A_HEADS, A_KV_HEADS, A_HEAD_DIM, A_RADIUS = 16, 2, 64, 128
A_WIDTH = A_HEADS * A_HEAD_DIM
A_KV_WIDTH = A_KV_HEADS * A_HEAD_DIM
B_SHORT, B_EMB = 3, 33
B_BANDS = (B_EMB - 1) // 2
B_DECAY_TARGET, B_FAST_DECAY_PCT, B_SLOW_DECAY_PCT = 1e-2, 0.3, 1.5
C_HEADS, C_HEAD_DIM = 16, 128
C_DILATIONS = (1, 4, 16)
C_RADIUS = 64
ROPE_THETA, ROPE_FRACTION = 500000.0, 4
ALPHA = (2 * DEPTH) ** 0.25
LN_EPS = 1e-5

LANES = 128
VMEM_LIMIT = 56 * 1024 * 1024
FFT_N2 = 256

ODD_CHUNK = 2048
ATT_TQ = 256


def _cparams(sem):
    return pltpu.CompilerParams(dimension_semantics=sem, vmem_limit_bytes=VMEM_LIMIT)


def _seq_bounds(row, bounds):
    start = jnp.int32(bounds[0])
    end = jnp.int32(bounds[1])
    for b0, b1 in zip(bounds[1:-1], bounds[2:]):
        inside = row >= b0
        start = jnp.where(inside, jnp.int32(b0), start)
        end = jnp.where(inside, jnp.int32(b1), end)
    return start, end


def _rope(a, c, s1, s2, half):
    w = a.shape[-1]
    return a * c + pltpu.roll(a, w - half, 1) * s1 + pltpu.roll(a, half, 1) * s2


def _mm_kernel(x_ref, w_ref, o_ref):
    o_ref[...] = jnp.dot(x_ref[...], w_ref[...], preferred_element_type=F32).astype(o_ref.dtype)


def _matmul(x, w, tn, out_dtype, tm=1024):
    m, k = x.shape
    n = w.shape[1]
    return pl.pallas_call(
        _mm_kernel,
        grid=(m // tm, n // tn),
        in_specs=[pl.BlockSpec((tm, k), lambda i, j: (i, 0)),
                  pl.BlockSpec((k, tn), lambda i, j: (0, j))],
        out_specs=pl.BlockSpec((tm, tn), lambda i, j: (i, j)),
        out_shape=jax.ShapeDtypeStruct((m, n), out_dtype),
        compiler_params=_cparams(("parallel", "arbitrary")),
        name="matmul",
    )(x, w)


def _mm_rope_kernel(x_ref, w_ref, c_ref, s1_ref, s2_ref, o_ref, *, half):
    acc = jnp.dot(x_ref[...], w_ref[...], preferred_element_type=F32)
    tw = c_ref.shape[1]
    rc = 256
    for r0 in range(0, acc.shape[0], rc):
        rows = slice(r0, r0 + rc)
        for c in range(acc.shape[1] // tw):
            cols = slice(c * tw, (c + 1) * tw)
            o_ref[rows, cols] = _rope(acc[rows, cols], c_ref[rows, :], s1_ref[rows, :], s2_ref[rows, :],
                                      half).astype(o_ref.dtype)


def _matmul_rope(x, w, tabs, half, tn, tm=1024):
    m, k = x.shape
    n = w.shape[1]
    tw = tabs[0].shape[1]
    tab_spec = pl.BlockSpec((tm, tw), lambda i, j: (i, 0))
    return pl.pallas_call(
        functools.partial(_mm_rope_kernel, half=half),
        grid=(m // tm, n // tn),
        in_specs=[pl.BlockSpec((tm, k), lambda i, j: (i, 0)),
                  pl.BlockSpec((k, tn), lambda i, j: (0, j)),
                  tab_spec, tab_spec, tab_spec],
        out_specs=pl.BlockSpec((tm, tn), lambda i, j: (i, j)),
        out_shape=jax.ShapeDtypeStruct((m, n), BF16),
        compiler_params=_cparams(("parallel", "arbitrary")),
        name="matmul_rope",
    )(x, w, *tabs)


def _odd_proj_kernel(x_ref, w_ref, c_ref, s1_ref, s2_ref, o_ref, acc_ref, *, d, rope, half):
    hps = w_ref.shape[1] // LANES
    tm = x_ref.shape[0]
    t = tm // d
    pair = 2
    rc = 256
    for p in range(hps // pair):
        acc = jnp.dot(x_ref[...], w_ref[:, p * pair * LANES:(p + 1) * pair * LANES], preferred_element_type=F32)
        for h2 in range(pair):
            hh = p * pair + h2
            slot = (p % 2) * pair + h2
            acc_ref[slot, :, :] = acc[:, h2 * LANES:(h2 + 1) * LANES]
            for c0 in range(0, tm, rc):
                a = acc_ref[slot, c0:c0 + rc, :]
                if rope:
                    a = _rope(a, c_ref[c0:c0 + rc, :], s1_ref[c0:c0 + rc, :], s2_ref[c0:c0 + rc, :], half)
                if d == 1:
                    o_ref[hh, 0, c0:c0 + rc, :] = a.astype(BF16)
                elif rope:
                    acc_ref[slot, c0:c0 + rc, :] = a
            if d > 1:
                for r in range(d):
                    o_ref[hh, r, :, :] = acc_ref[slot, pl.ds(r, t, stride=d), :].astype(BF16)


def _odd_proj(x, w, col0, tabs, half, d, rope, hps=8):
    m, k = x.shape
    tm = ODD_CHUNK
    cb0 = col0 // (hps * LANES)
    tab_spec = pl.BlockSpec((tm, LANES), lambda i, j: (i, 0))
    return pl.pallas_call(
        functools.partial(_odd_proj_kernel, d=d, rope=rope, half=half),
        grid=(m // tm, C_HEADS // hps),
        in_specs=[pl.BlockSpec((tm, k), lambda i, j: (i, 0)),
                  pl.BlockSpec((k, hps * LANES), lambda i, j: (0, cb0 + j)),
                  tab_spec, tab_spec, tab_spec],
        out_specs=pl.BlockSpec((hps, d, tm // d, LANES), lambda i, j: (j, 0, i, 0)),
        out_shape=jax.ShapeDtypeStruct((C_HEADS, d, m // d, LANES), BF16),
        scratch_shapes=[pltpu.VMEM((4, tm, LANES), F32)],
        compiler_params=_cparams(("parallel", "arbitrary")),
        name="odd_proj",
    )(x, w, *tabs)


LN_ROWS = 128


def _layer_norm_store(x_ref, acc_ref, g_ref, b_ref, of_ref, ob_ref):
    for c in range(x_ref.shape[0] // LN_ROWS):
        rows = pl.ds(c * LN_ROWS, LN_ROWS)
        r = ALPHA * x_ref[rows, :] + acc_ref[rows, :]
        mu = jnp.mean(r, axis=-1, keepdims=True)
        xc = r - mu
        var = jnp.mean(xc * xc, axis=-1, keepdims=True)
        y = xc * lax.rsqrt(var + LN_EPS) * g_ref[...] + b_ref[...]
        of_ref[rows, :] = y
        ob_ref[rows, :] = y.astype(BF16)


def _mm_ln_kernel(*refs, n_in):
    ys = refs[:n_in]
    ws = refs[n_in:2 * n_in]
    x_ref, g_ref, b_ref, of_ref, ob_ref, acc_ref = refs[2 * n_in:]
    acc = jnp.dot(ys[0][...], ws[0][...], preferred_element_type=F32)
    for y_ref, w_ref in zip(ys[1:], ws[1:]):
        acc = acc + jnp.dot(y_ref[...], w_ref[...], preferred_element_type=F32)
    acc_ref[...] = acc
    _layer_norm_store(x_ref, acc_ref, g_ref, b_ref, of_ref, ob_ref)


def _matmul_ln(ys, ws, x, g, b, tm=256):
    m, dm = x.shape
    n_in = len(ys)
    in_specs = [pl.BlockSpec((tm, y.shape[1]), lambda i: (i, 0)) for y in ys]
    in_specs += [pl.BlockSpec(w.shape, lambda i: (0, 0)) for w in ws]
    in_specs += [pl.BlockSpec((tm, dm), lambda i: (i, 0)),
                 pl.BlockSpec((1, dm), lambda i: (0, 0)),
                 pl.BlockSpec((1, dm), lambda i: (0, 0))]
    return pl.pallas_call(
        functools.partial(_mm_ln_kernel, n_in=n_in),
        grid=(m // tm,),
        in_specs=in_specs,
        out_specs=[pl.BlockSpec((tm, dm), lambda i: (i, 0)), pl.BlockSpec((tm, dm), lambda i: (i, 0))],
        out_shape=[jax.ShapeDtypeStruct((m, dm), F32), jax.ShapeDtypeStruct((m, dm), BF16)],
        scratch_shapes=[pltpu.VMEM((tm, dm), F32)],
        compiler_params=_cparams(("parallel",)),
        name="matmul_ln",
    )(*ys, *ws, x, g.reshape(1, dm), b.reshape(1, dm))


def _ffn_kernel(xb_ref, xf_ref, wg_ref, wu_ref, wd_ref, g_ref, b_ref, of_ref, ob_ref, acc_ref):
    j = pl.program_id(1)

    @pl.when(j == 0)
    def _():
        acc_ref[...] = jnp.zeros_like(acc_ref)

    xb = xb_ref[...]
    gate = jnp.dot(xb, wg_ref[...], preferred_element_type=F32)
    up = jnp.dot(xb, wu_ref[...], preferred_element_type=F32)
    h = (gate * jax.nn.sigmoid(gate)) * up
    acc_ref[...] += jnp.dot(h.astype(BF16), wd_ref[...], preferred_element_type=F32)

    @pl.when(j == pl.num_programs(1) - 1)
    def _():
        _layer_norm_store(xf_ref, acc_ref, g_ref, b_ref, of_ref, ob_ref)


def _ffn_ln(xb, xf, wg, wu, wd, g, b, tm=512, tf=512):
    m, dm = xf.shape
    dff = wg.shape[1]
    row = lambda i, j: (i, 0)
    return pl.pallas_call(
        _ffn_kernel,
        grid=(m // tm, dff // tf),
        in_specs=[pl.BlockSpec((tm, dm), row), pl.BlockSpec((tm, dm), row),
                  pl.BlockSpec((dm, tf), lambda i, j: (0, j)),
                  pl.BlockSpec((dm, tf), lambda i, j: (0, j)),
                  pl.BlockSpec((tf, dm), lambda i, j: (j, 0)),
                  pl.BlockSpec((1, dm), lambda i, j: (0, 0)),
                  pl.BlockSpec((1, dm), lambda i, j: (0, 0))],
        out_specs=[pl.BlockSpec((tm, dm), row), pl.BlockSpec((tm, dm), row)],
        out_shape=[jax.ShapeDtypeStruct((m, dm), F32), jax.ShapeDtypeStruct((m, dm), BF16)],
        scratch_shapes=[pltpu.VMEM((tm, dm), F32)],
        compiler_params=_cparams(("parallel", "arbitrary")),
        name="ffn_ln",
    )(xb, xf, wg, wu, wd, g.reshape(1, dm), b.reshape(1, dm))


def _even_attn_kernel(sink_ref, q_ref, kp_ref, km_ref, kn_ref, o_ref, *, bounds):
    i = pl.program_id(0)
    tq = q_ref.shape[0]
    nk = tq + 2 * A_RADIUS
    row0 = i * tq
    start, end = _seq_bounds(row0, bounds)
    kv = jnp.concatenate([kp_ref[...], km_ref[...], kn_ref[...]], axis=0)
    rq = row0 + lax.broadcasted_iota(jnp.int32, (tq, nk), 0)
    rk = row0 - A_RADIUS + lax.broadcasted_iota(jnp.int32, (tq, nk), 1)
    mask = (jnp.abs(rk - rq) <= A_RADIUS) & (rk >= start) & (rk < end)
    group = A_HEADS // A_KV_HEADS
    scale = A_HEAD_DIM ** -0.5
    for j in range(A_KV_HEADS):
        k = kv[:, j * A_HEAD_DIM:(j + 1) * A_HEAD_DIM]
        v = kv[:, A_KV_WIDTH + j * A_HEAD_DIM:A_KV_WIDTH + (j + 1) * A_HEAD_DIM]
        for gq in range(group):
            h = j * group + gq
            qh = q_ref[:, h * A_HEAD_DIM:(h + 1) * A_HEAD_DIM]
            s = lax.dot_general(qh, k, (((1,), (1,)), ((), ())), preferred_element_type=F32) * scale
            s = jnp.where(mask, s, -jnp.inf)
            sk = sink_ref[h]
            m = jnp.maximum(jnp.max(s, axis=-1, keepdims=True), sk)
            p = jnp.exp(s - m)
            den = jnp.sum(p, axis=-1, keepdims=True) + jnp.exp(sk - m)
            o = jnp.dot(p.astype(BF16), v, preferred_element_type=F32) / den
            o_ref[:, h * A_HEAD_DIM:(h + 1) * A_HEAD_DIM] = o.astype(o_ref.dtype)


def _even_attention(q, kv, sink, bounds):
    m = q.shape[0]
    tq = ATT_TQ
    hb = A_RADIUS
    per = tq // hb
    last = m // hb - 1
    kvw = kv.shape[1]
    return pl.pallas_call(
        functools.partial(_even_attn_kernel, bounds=bounds),
        grid=(m // tq,),
        in_specs=[pl.BlockSpec(memory_space=pltpu.SMEM),
                  pl.BlockSpec((tq, A_WIDTH), lambda i: (i, 0)),
                  pl.BlockSpec((hb, kvw), lambda i: (jnp.maximum(i * per - 1, 0), 0)),
                  pl.BlockSpec((tq, kvw), lambda i: (i, 0)),
                  pl.BlockSpec((hb, kvw), lambda i: (jnp.minimum((i + 1) * per, last), 0))],
        out_specs=pl.BlockSpec((tq, A_WIDTH), lambda i: (i, 0)),
        out_shape=jax.ShapeDtypeStruct((m, A_WIDTH), BF16),
        compiler_params=_cparams(("parallel",)),
        name="even_attention",
    )(sink, q, kv, kv, kv)


def _odd_attn_kernel(*refs, bounds):
    ng = len(C_DILATIONS)
    o_ref, oacc, lacc = refs[7 * ng:]
    i = pl.program_id(0)
    chunk = ODD_CHUNK
    qb = 128
    nk = qb + 2 * C_RADIUS
    row0 = i * chunk
    start, end = _seq_bounds(row0, bounds)
    rr = lax.broadcasted_iota(jnp.int32, (qb, nk), 0)
    cc = lax.broadcasted_iota(jnp.int32, (qb, nk), 1)
    band = jnp.abs(cc - C_RADIUS - rr) <= C_RADIUS
    scale = C_HEAD_DIM ** -0.5
    for g, d in enumerate(C_DILATIONS):
        q_ref, kp_ref, km_ref, kn_ref, vp_ref, vm_ref, vn_ref = refs[7 * g:7 * g + 7]
        tg = chunk // d
        t_lo, t_hi, t_c0 = start // d, end // d, row0 // d
        for sb in range(tg // qb):
            lo, hi = qb * sb - C_RADIUS, qb * sb + qb + C_RADIUS
            tk = t_c0 + lo + cc
            mask = band & (tk >= t_lo) & (tk < t_hi)
            for r in range(d):
                def window(p_ref, m_ref, n_ref):
                    parts = []
                    if lo < 0:
                        parts.append(p_ref[0, r, :, :])
                    parts.append(m_ref[0, r, max(lo, 0):min(hi, tg), :])
                    if hi > tg:
                        parts.append(n_ref[0, r, :, :])
                    return parts[0] if len(parts) == 1 else jnp.concatenate(parts, axis=0)

                q = q_ref[0, r, qb * sb:qb * (sb + 1), :]
                k = window(kp_ref, km_ref, kn_ref)
                v = window(vp_ref, vm_ref, vn_ref)
                s = lax.dot_general(q, k, (((1,), (1,)), ((), ())), preferred_element_type=F32) * scale
                s = jnp.where(mask, s, -jnp.inf)
                m = jnp.max(s, axis=-1, keepdims=True)
                p = jnp.exp(s - m)
                den = jnp.sum(p, axis=-1, keepdims=True)
                o = jnp.dot(p.astype(BF16), v, preferred_element_type=F32) / den
                lse = jnp.broadcast_to(m + jnp.log(den), (qb, LANES))
                if d == 1:
                    rows = pl.ds(qb * sb, qb)
                else:
                    rows = pl.ds(r + d * qb * sb, qb, stride=d)
                oacc[g, rows, :] = o
                lacc[g, rows, :] = lse
    ls = [lacc[g] for g in range(ng)]
    mx = functools.reduce(jnp.maximum, ls)
    ws = [jnp.exp(l - mx) for l in ls]
    tot = functools.reduce(lambda a, b: a + b, ws)
    out = functools.reduce(lambda a, b: a + b, [(ws[g] / tot) * oacc[g] for g in range(ng)])
    o_ref[...] = out.astype(o_ref.dtype)


def _odd_attention(qkv, bounds, m):
    chunk = ODD_CHUNK
    hb = C_RADIUS
    operands, in_specs = [], []
    for (q, k, v), d in zip(qkv, C_DILATIONS):
        tg = chunk // d
        per = tg // hb
        last = m // d // hb - 1
        main = pl.BlockSpec((1, d, tg, LANES), lambda i, h: (h, 0, i, 0))
        prev = pl.BlockSpec((1, d, hb, LANES), lambda i, h, per=per: (h, 0, jnp.maximum(i * per - 1, 0), 0))
        nxt = pl.BlockSpec((1, d, hb, LANES), lambda i, h, per=per, last=last: (h, 0, jnp.minimum((i + 1) * per, last), 0))
        operands += [q, k, k, k, v, v, v]
        in_specs += [main, prev, main, nxt, prev, main, nxt]
    ng = len(C_DILATIONS)
    return pl.pallas_call(
        functools.partial(_odd_attn_kernel, bounds=bounds),
        grid=(m // chunk, C_HEADS),
        in_specs=in_specs,
        out_specs=pl.BlockSpec((chunk, LANES), lambda i, h: (i, h)),
        out_shape=jax.ShapeDtypeStruct((m, C_HEADS * C_HEAD_DIM), BF16),
        scratch_shapes=[pltpu.VMEM((ng, chunk, LANES), F32), pltpu.VMEM((ng, chunk, LANES), F32)],
        compiler_params=_cparams(("parallel", "arbitrary")),
        name="odd_attention",
    )(*operands)


def _short_conv_kernel(xp_ref, xm_ref, xn_ref, w_ref, b_ref, o_ref, *, bounds):
    i = pl.program_id(0)
    tr = xm_ref.shape[0]
    row0 = i * tr
    start, end = _seq_bounds(row0, bounds)
    x = xm_ref[...]
    hp = xp_ref.shape[0]
    before = jnp.where(row0 > start, xp_ref[hp - 1:hp, :], 0.0)
    after = jnp.where(row0 + tr < end, xn_ref[0:1, :], 0.0)
    ridx = lax.broadcasted_iota(jnp.int32, x.shape, 0)
    xl = jnp.where(ridx == 0, before, pltpu.roll(x, 1, 0))
    xr = jnp.where(ridx == tr - 1, after, pltpu.roll(x, tr - 1, 0))
    y = b_ref[...] + xl * w_ref[0:1, :]
    y = y + x * w_ref[1:2, :]
    y = y + xr * w_ref[2:3, :]
    o_ref[...] = y


def _short_conv(x, w, b, bounds, tr=512, tc=512):
    m, n = x.shape
    hb = 8
    per = tr // hb
    last = m // hb - 1
    return pl.pallas_call(
        functools.partial(_short_conv_kernel, bounds=bounds),
        grid=(m // tr, n // tc),
        in_specs=[pl.BlockSpec((hb, tc), lambda i, j: (jnp.maximum(i * per - 1, 0), j)),
                  pl.BlockSpec((tr, tc), lambda i, j: (i, j)),
                  pl.BlockSpec((hb, tc), lambda i, j: (jnp.minimum((i + 1) * per, last), j)),
                  pl.BlockSpec((B_SHORT, tc), lambda i, j: (0, j)),
                  pl.BlockSpec((1, tc), lambda i, j: (0, j))],
        out_specs=pl.BlockSpec((tr, tc), lambda i, j: (i, j)),
        out_shape=jax.ShapeDtypeStruct((m, n), F32),
        compiler_params=_cparams(("parallel", "arbitrary")),
        name="short_conv",
    )(x, x, x, w, b.reshape(1, n))


def _filter_mlp_kernel(z_ref, w1_ref, b1_ref, f1_ref, w2_ref, b2_ref, f2_ref, w3_ref, b3_ref, dl_ref,
                       h_ref, nrm_ref):
    i = pl.program_id(0)
    z = z_ref[...]
    h = jnp.sin(f1_ref[...] * (jnp.dot(z.astype(BF16), w1_ref[...], preferred_element_type=F32) + b1_ref[...]))
    h = jnp.sin(f2_ref[...] * (jnp.dot(h.astype(BF16), w2_ref[...], preferred_element_type=F32) + b2_ref[...]))
    h = jnp.dot(h.astype(BF16), w3_ref[...], preferred_element_type=F32) + b3_ref[...]
    decay = jnp.exp(-z[:, 0:1] * dl_ref[...])
    nrep = h.shape[1] // decay.shape[1]
    h = h * jnp.concatenate([decay] * nrep, axis=1)
    h_ref[...] = h

    @pl.when(i == 0)
    def _():
        nrm_ref[...] = jnp.zeros_like(nrm_ref)

    half = h.shape[1] // 2
    col = lax.broadcasted_iota(jnp.int32, h.shape, 1)
    row = lax.broadcasted_iota(jnp.int32, h.shape, 0) + i * h.shape[0]
    a = jnp.where((col >= half) & (row == 0), 0.0, jnp.abs(h))
    nrm_ref[...] += jnp.sum(a, axis=0, keepdims=True)


def _filter_mlp(z, w1, b1, f1, w2, b2, f2, w3, b3, deltas, tl=512):
    l, e = z.shape
    hid = w1.shape[1]
    n = w3.shape[1]
    c = deltas.shape[0]
    full = lambda shape: pl.BlockSpec(shape, lambda i: (0, 0))
    return pl.pallas_call(
        _filter_mlp_kernel,
        grid=(l // tl,),
        in_specs=[pl.BlockSpec((tl, e), lambda i: (i, 0)),
                  full((e, hid)), full((1, hid)), full((1, hid)),
                  full((hid, hid)), full((1, hid)), full((1, hid)),
                  full((hid, n)), full((1, n)), full((1, c))],
        out_specs=[pl.BlockSpec((tl, n), lambda i: (i, 0)), full((1, n))],
        out_shape=[jax.ShapeDtypeStruct((l, n), F32), jax.ShapeDtypeStruct((1, n), F32)],
        compiler_params=_cparams(("arbitrary",)),
        name="filter_mlp",
    )(z, w1, b1.reshape(1, hid), f1.reshape(1, hid), w2, b2.reshape(1, hid), f2.reshape(1, hid),
      w3, b3.reshape(1, n), deltas.reshape(1, c))


def _split(x):
    hi = x.astype(BF16)
    lo = (x - hi.astype(F32)).astype(BF16)
    return hi, lo


def _dot3(ch, cl, x):
    xh, xl = _split(x)
    r = jnp.dot(ch, xh, preferred_element_type=F32)
    r = r + jnp.dot(ch, xl, preferred_element_type=F32)
    return r + jnp.dot(cl, xh, preferred_element_type=F32)


def _dot1(ch, x):
    return jnp.dot(ch, x.astype(BF16), preferred_element_type=F32)


FFT_GROUP = 8


FFT_N2C = 4
FFT_RH = 24
FFT_SPB = 6


def _fft1_kernel(x_ref, fh_ref, fl_ref, ar_ref, ai_ref, *, kn1, n2):
    kh, rows, _ = ar_ref.shape
    n2q = n2 // FFT_N2C
    rh = rows // n2q
    rp = kh * rh
    c0 = pl.program_id(2) * n2q
    for g in range(n2q // FFT_GROUP):
        cols = [x_ref[pl.ds(c0 + g * FFT_GROUP + s, kn1, stride=n2), :] for s in range(FFT_GROUP)]
        out = _dot3(fh_ref[...], fl_ref[...], jnp.concatenate(cols, axis=1))
        for k in range(kh):
            for s in range(FFT_GROUP):
                r0 = (g * FFT_GROUP + s) * rh
                ar_ref[k, r0:r0 + rh, :] = out[k * rh:(k + 1) * rh, s * LANES:(s + 1) * LANES]
                ai_ref[k, r0:r0 + rh, :] = out[rp + k * rh:rp + (k + 1) * rh, s * LANES:(s + 1) * LANES]


def _fft_stage1(x, f1h, f1l, plan, l, nb, row_blk0, col_blk0, c):
    n2, kh, rh = plan["n2"], plan["kh"], plan["rh"]
    kn1 = f1h.shape[1]
    rows = n2 // FFT_N2C * rh
    out = jax.ShapeDtypeStruct((nb, kh, FFT_N2C, rows, c), F32)
    fspec = pl.BlockSpec(f1h.shape, lambda b, j, q: (0, 0))
    ospec = pl.BlockSpec((None, kh, None, rows, LANES), lambda b, j, q: (b, 0, q, 0, j))
    return pl.pallas_call(
        functools.partial(_fft1_kernel, kn1=kn1, n2=n2),
        grid=(nb, c // LANES, FFT_N2C),
        in_specs=[pl.BlockSpec((l, LANES), lambda b, j, q: (row_blk0 + b, col_blk0 + j)), fspec, fspec],
        out_specs=[ospec, ospec],
        out_shape=[out, out],
        compiler_params=_cparams(("parallel", "parallel", "arbitrary")),
        name="fft_stage1",
    )(x, f1h, f1l)


def _cpair(p, n):
    return p[:n, :LANES] - p[n:, LANES:], p[:n, LANES:] + p[n:, :LANES]


def _twiddle(xr, xi, tr, ti):
    return xr * tr - xi * ti, xr * ti + xi * tr


def _filter_mid_kernel(fr_ref, fi_ref, br_ref, bi_ref, tr_ref, ti_ref, w_ref, inv_ref, b0_ref,
                       kr_ref, ki_ref, *, nslab, rh, n2):
    k0 = pl.program_id(2) * FFT_SPB

    @pl.when(pl.program_id(1) * rh + k0 < nslab)
    def _():
        for t in range(FFT_SPB):
            rows = pl.ds(k0 + t, n2, stride=rh)
            tr, ti = tr_ref[t], ti_ref[t]
            x4 = jnp.concatenate(_twiddle(fr_ref[rows, :], fi_ref[rows, :], tr, ti)
                                 + _twiddle(br_ref[rows, :], bi_ref[rows, :], tr, ti), axis=1)
            p = _dot1(w_ref[...], x4)
            fr, fi = _cpair(p[:, :2 * LANES], n2)
            br, bi = _cpair(p[:, 2 * LANES:], n2)
            kr_ref[t] = (fr + (br - b0_ref[...])) * inv_ref[...]
            ki_ref[t] = (fi - bi) * inv_ref[...]

    @pl.when(pl.program_id(1) * rh + k0 >= nslab)
    def _():
        kr_ref[...] = jnp.zeros_like(kr_ref)
        ki_ref[...] = jnp.zeros_like(ki_ref)


def _filter_mid(ar, ai, inv, b0, plan):
    n2, kh, rh, rp = plan["n2"], plan["kh"], plan["rh"], plan["rp"]
    oc = ar.shape[-1] // 2
    nj = oc // LANES
    fwd = pl.BlockSpec((None, None, n2 * rh, LANES), lambda j, kk, k: (0, kk, 0, j))
    bwd = pl.BlockSpec((None, None, n2 * rh, LANES), lambda j, kk, k: (0, kk, 0, j + nj))
    steps = rh // FFT_SPB
    tspec = pl.BlockSpec((FFT_SPB, n2, LANES), lambda j, kk, k: (kk * steps + k, 0, 0))
    wspec = pl.BlockSpec((2 * n2, n2), lambda j, kk, k: (0, 0))
    vec = pl.BlockSpec((1, LANES), lambda j, kk, k: (0, j))
    ospec = pl.BlockSpec((FFT_SPB, n2, LANES), lambda j, kk, k: (kk * steps + k, 0, j))
    out = jax.ShapeDtypeStruct((rp, n2, oc), F32)
    return pl.pallas_call(
        functools.partial(_filter_mid_kernel, nslab=plan["r"], rh=rh, n2=n2),
        grid=(nj, kh, steps),
        in_specs=[fwd, fwd, bwd, bwd, tspec, tspec, wspec, vec, vec],
        out_specs=[ospec, ospec],
        out_shape=[out, out],
        compiler_params=_cparams(("parallel", "arbitrary", "arbitrary")),
        name="filter_mid",
    )(ar, ai, ar, ai, *plan["tw"], plan["f2"], inv, b0)


def _conv_mid_kernel(ar_ref, ai_ref, tr_ref, ti_ref, wf_ref, wi_ref, kr_ref, ki_ref, gh_ref, gl_ref,
                     y_ref, dr_ref, di_ref, *, nslab, rh, n2):
    k0 = pl.program_id(3) * FFT_SPB

    @pl.when((pl.program_id(2) == 0) & (pl.program_id(3) == 0))
    def _():
        y_ref[...] = jnp.zeros_like(y_ref)

    @pl.when(pl.program_id(2) * rh + k0 < nslab)
    def _():
        for t in range(FFT_SPB):
            rows = pl.ds(k0 + t, n2, stride=rh)
            tr, ti = tr_ref[t], ti_ref[t]
            x2 = jnp.concatenate(_twiddle(ar_ref[rows, :], ai_ref[rows, :], tr, ti), axis=1)
            xr, xi = _cpair(_dot1(wf_ref[...], x2), n2)
            kr, ki = kr_ref[t], ki_ref[t]
            y2 = jnp.concatenate([xr * kr - xi * ki, xr * ki + xi * kr], axis=1)
            cr, ci = _cpair(_dot1(wi_ref[...], y2), n2)
            dr, di = _twiddle(cr, ci, tr, -ti)
            dr_ref[rows, :] = dr
            di_ref[rows, :] = di

    @pl.when(pl.program_id(2) * rh + k0 >= nslab)
    def _():
        for t in range(FFT_SPB):
            rows = pl.ds(k0 + t, n2, stride=rh)
            dr_ref[rows, :] = jnp.zeros((n2, LANES), F32)
            di_ref[rows, :] = jnp.zeros((n2, LANES), F32)

    @pl.when(pl.program_id(3) == pl.num_programs(3) - 1)
    def _():
        n1c = y_ref.shape[0]
        n1r = y_ref.shape[1] // n2
        for g in range(n2 // FFT_GROUP):
            cols = []
            for s in range(FFT_GROUP):
                r0 = (g * FFT_GROUP + s) * rh
                cols.append(jnp.concatenate([dr_ref[r0:r0 + rh, :], di_ref[r0:r0 + rh, :]], axis=0))
            y = _dot3(gh_ref[...], gl_ref[...], jnp.concatenate(cols, axis=1))
            for c in range(n1c):
                for s in range(FFT_GROUP):
                    t0 = (g * FFT_GROUP + s) * n1r
                    y_ref[c, t0:t0 + n1r, :] += y[c * n1r:(c + 1) * n1r, s * LANES:(s + 1) * LANES]


def _conv_mid(ar, ai, kr, ki, order, plan):
    nb, kh, _, c = ar.shape
    n2, rh = plan["n2"], plan["rh"]
    gh, gl = plan["g"]
    nh = gh.shape[1]
    n1r = min(16, nh)
    n1c = nh // n1r
    koff = order * (c // LANES)
    blk = pl.BlockSpec((None, None, n2 * rh, LANES), lambda b, j, kk, k: (b, kk, 0, j))
    steps = rh // FFT_SPB
    tspec = pl.BlockSpec((FFT_SPB, n2, LANES), lambda b, j, kk, k: (kk * steps + k, 0, 0))
    wspec = pl.BlockSpec((2 * n2, n2), lambda b, j, kk, k: (0, 0))
    kspec = pl.BlockSpec((FFT_SPB, n2, LANES), lambda b, j, kk, k: (kk * steps + k, 0, j + koff))
    gspec = pl.BlockSpec((None, nh, 2 * rh), lambda b, j, kk, k: (kk, 0, 0))
    return pl.pallas_call(
        functools.partial(_conv_mid_kernel, nslab=plan["r"], rh=rh, n2=n2),
        grid=(nb, c // LANES, kh, steps),
        in_specs=[blk, blk, tspec, tspec, wspec, wspec, kspec, kspec, gspec, gspec],
        out_specs=pl.BlockSpec((None, n1c, n2 * n1r, LANES), lambda b, j, kk, k: (b, 0, 0, j)),
        out_shape=jax.ShapeDtypeStruct((nb, n1c, n2 * n1r, c), F32),
        scratch_shapes=[pltpu.VMEM((n2 * rh, LANES), F32), pltpu.VMEM((n2 * rh, LANES), F32)],
        compiler_params=_cparams(("parallel", "parallel", "arbitrary", "arbitrary")),
        name="conv_mid",
    )(ar, ai, *plan["tw"], plan["f2"], plan["f2i"], kr, ki, gh, gl)


def _gate_kernel(y_ref, u_ref, g_ref, bias_ref, o_ref, *, n2):
    n1r = y_ref.shape[0] // n2
    for a in range(n1r):
        rows = slice(a * n2, (a + 1) * n2)
        yt = y_ref[pl.ds(a, n2, stride=n1r), :]
        o_ref[rows, :] = (g_ref[rows, :] * (yt + u_ref[rows, :] * bias_ref[...])).astype(o_ref.dtype)


def _gate(y, n2, u, u_row0, u_col0, gate, g_row0, g_col0, bias, out_dtype):
    nb, n1c, yr, c = y.shape
    tr = yr
    return pl.pallas_call(
        functools.partial(_gate_kernel, n2=n2),
        grid=(nb, c // LANES, n1c),
        in_specs=[pl.BlockSpec((None, None, yr, LANES), lambda b, j, q: (b, q, 0, j)),
                  pl.BlockSpec((tr, LANES), lambda b, j, q: (u_row0 // tr + b * n1c + q, u_col0 + j)),
                  pl.BlockSpec((tr, LANES), lambda b, j, q: (g_row0 // tr + b * n1c + q, g_col0 + j)),
                  pl.BlockSpec((1, LANES), lambda b, j, q: (0, j))],
        out_specs=pl.BlockSpec((tr, LANES), lambda b, j, q: (b * n1c + q, j)),
        out_shape=jax.ShapeDtypeStruct((nb * n1c * tr, c), out_dtype),
        compiler_params=_cparams(("parallel", "parallel", "arbitrary")),
        name="hyena_gate",
    )(y, u, gate, bias)


def _np_split(a):
    a32 = np.asarray(a, np.float32)
    hi = a32.astype(BF16)
    lo = (a32 - hi.astype(np.float32)).astype(BF16)
    return hi, lo


def _fft_plan(l):
    n = 2 * l
    n2 = FFT_N2
    n1 = n // n2
    r = n1 // 2 + 1
    rh = FFT_RH
    kh = -(-r // rh)
    rp = kh * rh
    kn1 = n1 // 2
    k1 = np.arange(rp, dtype=np.float64)[:, None]
    live = (k1 < r).astype(np.float64)

    ang = 2 * np.pi * k1 * np.arange(kn1)[None, :] / n1
    f1 = np.concatenate([np.cos(ang) * live, -np.sin(ang) * live], axis=0)

    kk = np.arange(rp, dtype=np.float64)[None, :]
    wgt = np.where((kk == 0) | (kk == n1 // 2), 1.0, 2.0) * (kk < r) / n
    ango = 2 * np.pi * np.arange(n1 // 2)[:, None] * kk / n1
    gre = (np.cos(ango) * wgt).reshape(n1 // 2, kh, rh).transpose(1, 0, 2)
    gim = (-np.sin(ango) * wgt).reshape(n1 // 2, kh, rh).transpose(1, 0, 2)
    g = np.concatenate([gre, gim], axis=2)

    a2 = 2 * np.pi * np.outer(np.arange(n2), np.arange(n2)) / n2
    f2 = np.concatenate([np.cos(a2), -np.sin(a2)], axis=0)
    f2i = np.concatenate([np.cos(a2), np.sin(a2)], axis=0)

    idx = jnp.arange(rp, dtype=jnp.int32)[:, None] * jnp.arange(n2, dtype=jnp.int32)[None, :]
    ang = idx.astype(F32) * F32(2.0 * math.pi / n)
    tw = tuple(jnp.broadcast_to(t[:, :, None], (rp, n2, LANES)) for t in (jnp.cos(ang), -jnp.sin(ang)))
    return dict(n1=n1, n2=n2, r=r, rp=rp, kh=kh, rh=rh, f1=_np_split(f1), g=_np_split(g),
                f2=_np_split(f2)[0], f2i=_np_split(f2i)[0], tw=tw)


def _filter_features(l):
    t = jnp.linspace(0.0, 1.0, l, dtype=F32)[:, None]
    bands = jnp.linspace(1e-4, B_BANDS - 1, B_BANDS, dtype=F32)[None, :]
    w = 2.0 * math.pi * jnp.arange(l, dtype=F32)[:, None] / l
    return jnp.concatenate([t, jnp.cos(bands * w), -jnp.sin(bands * w)], axis=-1)


def _hyena_filters(plan, l, c, w1, b1, f1, w2, b2, f2, w3, b3):
    n1, n2 = plan["n1"], plan["n2"]
    z = _filter_features(l)
    e = z.shape[1]
    ep = -(-e // 16) * 16
    z = jnp.pad(z, ((0, 0), (0, ep - e)))
    w1p = jnp.pad(w1, ((0, ep - e), (0, 0))).astype(BF16)
    max_decay = math.log(B_DECAY_TARGET) / B_FAST_DECAY_PCT
    min_decay = math.log(B_DECAY_TARGET) / B_SLOW_DECAY_PCT
    deltas = jnp.abs(jnp.linspace(min_decay, max_decay, c, dtype=F32))
    h, nrm = _filter_mlp(z, w1p, b1, f1, w2.astype(BF16), b2, f2, w3.astype(BF16), b3, deltas,
                         tl=min(512, l))
    oc = h.shape[1] // 2
    inv = 1.0 / (nrm[:, :oc] + nrm[:, oc:])
    b0 = h[0:1, oc:]
    ar, ai = _fft_stage1(h, *plan["f1"], plan, l, 1, 0, 0, 2 * oc)
    mid_shape = (1, plan["kh"], n2 * plan["rh"], 2 * oc)
    return _filter_mid(ar.reshape(mid_shape), ai.reshape(mid_shape), inv, b0, plan)


def _hyena_conv(plan, kf, order, l, nb, u, u_row0, u_col0, gate, g_row0, g_col0, bias, c, out_dtype):
    n2 = plan["n2"]
    cb = c // LANES
    ar, ai = _fft_stage1(u, *plan["f1"], plan, l, nb, u_row0 // l, u_col0 * cb, c)
    mid_shape = (nb, plan["kh"], n2 * plan["rh"], c)
    y = _conv_mid(ar.reshape(mid_shape), ai.reshape(mid_shape), kf[0], kf[1], order, plan)
    return _gate(y, n2, u, u_row0, u_col0 * cb, gate, g_row0, g_col0 * cb, bias.reshape(1, c), out_dtype)


def _hyena_mixer(u, seqs, plans, c, fw, hy_bias):
    outs = []
    for (row0, nb, l) in seqs:
        plan = plans[l]
        kf = _hyena_filters(plan, l, c, *fw)
        z = _hyena_conv(plan, kf, 0, l, nb, u, row0, 0, u, row0, 1, hy_bias[0], c, F32)
        o = _hyena_conv(plan, kf, 1, l, nb, z, 0, 0, u, row0, 2, hy_bias[1], c, BF16)
        outs.append(o)
    return jnp.concatenate(outs, axis=0)


def _rope_tables(pos, hd):
    rot = hd // ROPE_FRACTION
    half = rot // 2
    inv = ROPE_THETA ** (-(jnp.arange(half, dtype=F32) * 2.0 / rot))
    ang = pos[:, None] * inv[None, :]
    cos, sin = jnp.cos(ang), jnp.sin(ang)
    m = pos.shape[0]
    one = jnp.ones((m, hd - rot), F32)
    zero = jnp.zeros((m, hd - rot), F32)
    zh = jnp.zeros((m, half), F32)
    c = jnp.concatenate([cos, cos, one], axis=1)
    s1 = jnp.concatenate([-sin, zh, zero], axis=1)
    s2 = jnp.concatenate([zh, sin, zero], axis=1)
    rep = LANES // hd
    return tuple(jnp.tile(t, (1, rep)) for t in (c, s1, s2)), half


def _trunk(x, bounds, seqs, p):
    m, dm = x.shape
    pos = jnp.concatenate([jnp.tile(jnp.arange(l, dtype=F32), nb) for (_, nb, l) in seqs])
    tabs_a, half_a = _rope_tables(pos, A_HEAD_DIM)
    tabs_c, half_c = _rope_tables(pos, C_HEAD_DIM)
    ident = (jnp.ones((m, LANES), F32), jnp.zeros((m, LANES), F32), jnp.zeros((m, LANES), F32))
    tabs_kv = tuple(jnp.concatenate([a, b], axis=1) for a, b in zip(tabs_a, ident))
    c_hy = dm - A_WIDTH
    plans = {l: _fft_plan(l) for l in sorted({l for (_, _, l) in seqs})}
    xf = x
    xb = x.astype(BF16)
    for i in range(DEPTH):
        j = i // 2
        if i % 2 == 0:
            w_in = p['mix_e_w_in'][j].astype(BF16)
            kv0 = A_WIDTH
            hy0 = A_WIDTH + 2 * A_KV_WIDTH
            q = _matmul_rope(xb, w_in[:, :kv0], tabs_a, half_a, tn=512)
            kv = _matmul_rope(xb, w_in[:, kv0:hy0], tabs_kv, half_a, tn=2 * A_KV_WIDTH)
            hy = _matmul(xb, w_in[:, hy0:], tn=512, out_dtype=F32)
            a_out = _even_attention(q, kv, p['a_sink'][j], bounds)
            u = _short_conv(hy, p['hy_conv_w'][j], p['hy_conv_b'][j], bounds)
            fw = (p['hy_w1'][j], p['hy_b1'][j], p['hy_f1'][j], p['hy_w2'][j], p['hy_b2'][j],
                  p['hy_f2'][j], p['hy_w3'][j], p['hy_b3'][j])
            h_out = _hyena_mixer(u, seqs, plans, c_hy, fw, p['hy_bias'][j])
            w_out = p['mix_e_w_out'][j].astype(BF16)
            xf, xb = _matmul_ln([a_out, h_out], [w_out[:A_WIDTH], w_out[A_WIDTH:]], xf,
                                p['ln1_g'][i], p['ln1_b'][i])
        else:
            w_in = p['mix_o_w_in'][j].astype(BF16)
            gw = C_HEADS * C_HEAD_DIM
            ng = len(C_DILATIONS)
            qkv = []
            for g, d in enumerate(C_DILATIONS):
                trio = []
                for part in range(3):
                    c0 = (part * ng + g) * gw
                    trio.append(_odd_proj(xb, w_in, c0, tabs_c, half_c, d, rope=part < 2))
                qkv.append(tuple(trio))
            o = _odd_attention(qkv, bounds, m)
            xf, xb = _matmul_ln([o], [p['mix_o_w_out'][j].astype(BF16)], xf, p['ln1_g'][i], p['ln1_b'][i])
        xf, xb = _ffn_ln(xb, xf, p['ffn_w_gate'][i].astype(BF16), p['ffn_w_up'][i].astype(BF16),
                         p['ffn_w_down'][i].astype(BF16), p['ln2_g'][i], p['ln2_b'][i])
    return xf


def kernel(x_prompt, x_sample, mix_e_w_in, a_sink, hy_conv_w, hy_conv_b, hy_w1, hy_b1, hy_f1, hy_w2, hy_b2,
           hy_f2, hy_w3, hy_b3, hy_bias, mix_e_w_out, mix_o_w_in, mix_o_w_out, ffn_w_gate, ffn_w_up,
           ffn_w_down, ln1_g, ln1_b, ln2_g, ln2_b):
    p = dict(mix_e_w_in=mix_e_w_in, a_sink=a_sink, hy_conv_w=hy_conv_w, hy_conv_b=hy_conv_b,
             hy_w1=hy_w1, hy_b1=hy_b1, hy_f1=hy_f1, hy_w2=hy_w2, hy_b2=hy_b2, hy_f2=hy_f2,
             hy_w3=hy_w3, hy_b3=hy_b3, hy_bias=hy_bias, mix_e_w_out=mix_e_w_out,
             mix_o_w_in=mix_o_w_in, mix_o_w_out=mix_o_w_out, ffn_w_gate=ffn_w_gate,
             ffn_w_up=ffn_w_up, ffn_w_down=ffn_w_down, ln1_g=ln1_g, ln1_b=ln1_b,
             ln2_g=ln2_g, ln2_b=ln2_b)
    dm = x_prompt.shape[-1]
    seqs, bounds, row = [], [0], 0
    for xs in (x_prompt, x_sample):
        nb, l = xs.shape[0], xs.shape[1]
        seqs.append((row, nb, l))
        for _ in range(nb):
            row += l
            bounds.append(row)
    x = jnp.concatenate([x_prompt.reshape(-1, dm), x_sample.reshape(-1, dm)], axis=0)
    y = _trunk(x, tuple(bounds), tuple(seqs), p)
    n_p = x_prompt.shape[0] * x_prompt.shape[1]
    return (y[:n_p].reshape(x_prompt.shape), y[n_p:].reshape(x_sample.shape))
```

```python
import functools
import math

import numpy as np
import jax
import jax.numpy as jnp
from jax import lax
from jax.experimental import pallas as pl
from jax.experimental.pallas import tpu as pltpu

F32 = jnp.float32
BF16 = jnp.bfloat16

DEPTH = 4
A_HEADS, A_KV_HEADS, A_HEAD_DIM, A_RADIUS = 16, 2, 64, 128
A_WIDTH = A_HEADS * A_HEAD_DIM
A_KV_WIDTH = A_KV_HEADS * A_HEAD_DIM
B_SHORT, B_EMB = 3, 33
B_BANDS = (B_EMB - 1) // 2
B_DECAY_TARGET, B_FAST_DECAY_PCT, B_SLOW_DECAY_PCT = 1e-2, 0.3, 1.5
C_HEADS, C_HEAD_DIM = 16, 128
C_DILATIONS = (1, 4, 16)
C_RADIUS = 64
ROPE_THETA, ROPE_FRACTION = 500000.0, 4
ALPHA = (2 * DEPTH) ** 0.25
LN_EPS = 1e-5

LANES = 128
VMEM_LIMIT = 56 * 1024 * 1024
FFT_N2 = 256

ODD_CHUNK = 2048
ATT_TQ = 256


def _cparams(sem):
    return pltpu.CompilerParams(dimension_semantics=sem, vmem_limit_bytes=VMEM_LIMIT)


def _seq_bounds(row, bounds):
    start = jnp.int32(bounds[0])
    end = jnp.int32(bounds[1])
    for b0, b1 in zip(bounds[1:-1], bounds[2:]):
        inside = row >= b0
        start = jnp.where(inside, jnp.int32(b0), start)
        end = jnp.where(inside, jnp.int32(b1), end)
    return start, end


def _rope(a, c, s1, s2, half):
    w = a.shape[-1]
    return a * c + pltpu.roll(a, w - half, 1) * s1 + pltpu.roll(a, half, 1) * s2


def _mm_conv_kernel(xp_ref, xm_ref, xn_ref, w_ref, cw_ref, cb_ref, o_ref, *, bounds):
    i = pl.program_id(0)
    tm, hb = xm_ref.shape[0], xp_ref.shape[0]
    rows = tm + 2 * hb
    row0 = i * tm
    start, end = _seq_bounds(row0, bounds)
    lhs = jnp.concatenate([xp_ref[...], xm_ref[...], xn_ref[...]], axis=0)
    acc = jnp.dot(lhs, w_ref[...], preferred_element_type=F32)
    h0 = acc[hb:hb + tm]
    hm = pltpu.roll(acc, 1, 0)[hb:hb + tm]
    hp = pltpu.roll(acc, rows - 1, 0)[hb:hb + tm]
    ridx = lax.broadcasted_iota(jnp.int32, h0.shape, 0)
    hm = jnp.where((ridx == 0) & (row0 <= start), 0.0, hm)
    hp = jnp.where((ridx == tm - 1) & (row0 + tm >= end), 0.0, hp)
    y = cb_ref[...] + hm * cw_ref[0:1, :]
    y = y + h0 * cw_ref[1:2, :]
    y = y + hp * cw_ref[2:3, :]
    o_ref[...] = y


def _matmul_conv(x, w, cw, cb, bounds, tn=512, tm=1024):
    m, k = x.shape
    n = w.shape[1]
    hb = 16
    per = tm // hb
    last = m // hb - 1
    return pl.pallas_call(
        functools.partial(_mm_conv_kernel, bounds=bounds),
        grid=(m // tm, n // tn),
        in_specs=[pl.BlockSpec((hb, k), lambda i, j: (jnp.maximum(i * per - 1, 0), 0)),
                  pl.BlockSpec((tm, k), lambda i, j: (i, 0)),
                  pl.BlockSpec((hb, k), lambda i, j: (jnp.minimum((i + 1) * per, last), 0)),
                  pl.BlockSpec((k, tn), lambda i, j: (0, j)),
                  pl.BlockSpec((B_SHORT, tn), lambda i, j: (0, j)),
                  pl.BlockSpec((1, tn), lambda i, j: (0, j))],
        out_specs=pl.BlockSpec((tm, tn), lambda i, j: (i, j)),
        out_shape=jax.ShapeDtypeStruct((m, n), F32),
        compiler_params=_cparams(("parallel", "arbitrary")),
        name="matmul_conv",
    )(x, x, x, w, cw, cb.reshape(1, n))


def _mm_rope_kernel(x_ref, w_ref, c_ref, s1_ref, s2_ref, o_ref, *, half):
    acc = jnp.dot(x_ref[...], w_ref[...], preferred_element_type=F32)
    tw = c_ref.shape[1]
    rc = 256
    for r0 in range(0, acc.shape[0], rc):
        rows = slice(r0, r0 + rc)
        for c in range(acc.shape[1] // tw):
            cols = slice(c * tw, (c + 1) * tw)
            o_ref[rows, cols] = _rope(acc[rows, cols], c_ref[rows, :], s1_ref[rows, :], s2_ref[rows, :],
                                      half).astype(o_ref.dtype)


def _matmul_rope(x, w, tabs, half, tn, tm=1024):
    m, k = x.shape
    n = w.shape[1]
    tw = tabs[0].shape[1]
    tab_spec = pl.BlockSpec((tm, tw), lambda i, j: (i, 0))
    return pl.pallas_call(
        functools.partial(_mm_rope_kernel, half=half),
        grid=(m // tm, n // tn),
        in_specs=[pl.BlockSpec((tm, k), lambda i, j: (i, 0)),
                  pl.BlockSpec((k, tn), lambda i, j: (0, j)),
                  tab_spec, tab_spec, tab_spec],
        out_specs=pl.BlockSpec((tm, tn), lambda i, j: (i, j)),
        out_shape=jax.ShapeDtypeStruct((m, n), BF16),
        compiler_params=_cparams(("parallel", "arbitrary")),
        name="matmul_rope",
    )(x, w, *tabs)


def _odd_proj_kernel(x_ref, w_ref, c_ref, s1_ref, s2_ref, o_ref, acc_ref, *, d, rope, half):
    hps = w_ref.shape[1] // LANES
    tm = x_ref.shape[0]
    t = tm // d
    pair = 2
    rc = 256
    for p in range(hps // pair):
        acc = jnp.dot(x_ref[...], w_ref[:, p * pair * LANES:(p + 1) * pair * LANES], preferred_element_type=F32)
        for h2 in range(pair):
            hh = p * pair + h2
            slot = (p % 2) * pair + h2
            acc_ref[slot, :, :] = acc[:, h2 * LANES:(h2 + 1) * LANES]
            for c0 in range(0, tm, rc):
                a = acc_ref[slot, c0:c0 + rc, :]
                if rope:
                    a = _rope(a, c_ref[c0:c0 + rc, :], s1_ref[c0:c0 + rc, :], s2_ref[c0:c0 + rc, :], half)
                if d == 1:
                    o_ref[hh, 0, c0:c0 + rc, :] = a.astype(BF16)
                elif rope:
                    acc_ref[slot, c0:c0 + rc, :] = a
            if d > 1:
                for r in range(d):
                    o_ref[hh, r, :, :] = acc_ref[slot, pl.ds(r, t, stride=d), :].astype(BF16)


def _odd_proj(x, w, col0, tabs, half, d, rope, hps=8):
    m, k = x.shape
    tm = ODD_CHUNK
    cb0 = col0 // (hps * LANES)
    tab_spec = pl.BlockSpec((tm, LANES), lambda i, j: (i, 0))
    return pl.pallas_call(
        functools.partial(_odd_proj_kernel, d=d, rope=rope, half=half),
        grid=(m // tm, C_HEADS // hps),
        in_specs=[pl.BlockSpec((tm, k), lambda i, j: (i, 0)),
                  pl.BlockSpec((k, hps * LANES), lambda i, j: (0, cb0 + j)),
                  tab_spec, tab_spec, tab_spec],
        out_specs=pl.BlockSpec((hps, d, tm // d, LANES), lambda i, j: (j, 0, i, 0)),
        out_shape=jax.ShapeDtypeStruct((C_HEADS, d, m // d, LANES), BF16),
        scratch_shapes=[pltpu.VMEM((4, tm, LANES), F32)],
        compiler_params=_cparams(("parallel", "arbitrary")),
        name="odd_proj",
    )(x, w, *tabs)


LN_ROWS = 128


def _layer_norm_store(x_ref, acc_ref, g_ref, b_ref, of_ref, ob_ref):
    for c in range(x_ref.shape[0] // LN_ROWS):
        rows = pl.ds(c * LN_ROWS, LN_ROWS)
        r = ALPHA * x_ref[rows, :] + acc_ref[rows, :]
        mu = jnp.mean(r, axis=-1, keepdims=True)
        xc = r - mu
        var = jnp.mean(xc * xc, axis=-1, keepdims=True)
        y = xc * lax.rsqrt(var + LN_EPS) * g_ref[...] + b_ref[...]
        of_ref[rows, :] = y
        ob_ref[rows, :] = y.astype(BF16)


def _mm_ln_kernel(*refs, n_in):
    ys = refs[:n_in]
    ws = refs[n_in:2 * n_in]
    x_ref, g_ref, b_ref, of_ref, ob_ref, acc_ref = refs[2 * n_in:]
    acc = jnp.dot(ys[0][...], ws[0][...], preferred_element_type=F32)
    for y_ref, w_ref in zip(ys[1:], ws[1:]):
        acc = acc + jnp.dot(y_ref[...], w_ref[...], preferred_element_type=F32)
    acc_ref[...] = acc
    _layer_norm_store(x_ref, acc_ref, g_ref, b_ref, of_ref, ob_ref)


def _matmul_ln(ys, ws, x, g, b, tm=256):
    m, dm = x.shape
    n_in = len(ys)
    in_specs = [pl.BlockSpec((tm, y.shape[1]), lambda i: (i, 0)) for y in ys]
    in_specs += [pl.BlockSpec(w.shape, lambda i: (0, 0)) for w in ws]
    in_specs += [pl.BlockSpec((tm, dm), lambda i: (i, 0)),
                 pl.BlockSpec((1, dm), lambda i: (0, 0)),
                 pl.BlockSpec((1, dm), lambda i: (0, 0))]
    return pl.pallas_call(
        functools.partial(_mm_ln_kernel, n_in=n_in),
        grid=(m // tm,),
        in_specs=in_specs,
        out_specs=[pl.BlockSpec((tm, dm), lambda i: (i, 0)), pl.BlockSpec((tm, dm), lambda i: (i, 0))],
        out_shape=[jax.ShapeDtypeStruct((m, dm), F32), jax.ShapeDtypeStruct((m, dm), BF16)],
        scratch_shapes=[pltpu.VMEM((tm, dm), F32)],
        compiler_params=_cparams(("parallel",)),
        name="matmul_ln",
    )(*ys, *ws, x, g.reshape(1, dm), b.reshape(1, dm))


def _ffn_kernel(xb_ref, xf_ref, wg_ref, wu_ref, wd_ref, g_ref, b_ref, of_ref, ob_ref, acc_ref):
    j = pl.program_id(1)

    @pl.when(j == 0)
    def _():
        acc_ref[...] = jnp.zeros_like(acc_ref)

    xb = xb_ref[...]
    gate = jnp.dot(xb, wg_ref[...], preferred_element_type=F32)
    up = jnp.dot(xb, wu_ref[...], preferred_element_type=F32)
    h = (gate * jax.nn.sigmoid(gate)) * up
    acc_ref[...] += jnp.dot(h.astype(BF16), wd_ref[...], preferred_element_type=F32)

    @pl.when(j == pl.num_programs(1) - 1)
    def _():
        _layer_norm_store(xf_ref, acc_ref, g_ref, b_ref, of_ref, ob_ref)


def _ffn_ln(xb, xf, wg, wu, wd, g, b, tm=512, tf=512):
    m, dm = xf.shape
    dff = wg.shape[1]
    row = lambda i, j: (i, 0)
    return pl.pallas_call(
        _ffn_kernel,
        grid=(m // tm, dff // tf),
        in_specs=[pl.BlockSpec((tm, dm), row), pl.BlockSpec((tm, dm), row),
                  pl.BlockSpec((dm, tf), lambda i, j: (0, j)),
                  pl.BlockSpec((dm, tf), lambda i, j: (0, j)),
                  pl.BlockSpec((tf, dm), lambda i, j: (j, 0)),
                  pl.BlockSpec((1, dm), lambda i, j: (0, 0)),
                  pl.BlockSpec((1, dm), lambda i, j: (0, 0))],
        out_specs=[pl.BlockSpec((tm, dm), row), pl.BlockSpec((tm, dm), row)],
        out_shape=[jax.ShapeDtypeStruct((m, dm), F32), jax.ShapeDtypeStruct((m, dm), BF16)],
        scratch_shapes=[pltpu.VMEM((tm, dm), F32)],
        compiler_params=_cparams(("parallel", "arbitrary")),
        name="ffn_ln",
    )(xb, xf, wg, wu, wd, g.reshape(1, dm), b.reshape(1, dm))


def _band_bias(nq, radius):
    r = np.arange(nq)[:, None]
    c = np.arange(nq + 2 * radius)[None, :]
    return np.where(np.abs(c - radius - r) <= radius, 0.0, -np.inf).astype(np.float32)


def _even_attn_kernel(sink_ref, band_ref, q_ref, kp_ref, km_ref, kn_ref, o_ref, bias_ref, *, bounds):
    i = pl.program_id(0)
    tq = q_ref.shape[0]
    sub = A_RADIUS
    nk = sub + 2 * A_RADIUS
    row0 = i * tq
    start, end = _seq_bounds(row0, bounds)
    group = A_HEADS // A_KV_HEADS
    scale = A_HEAD_DIM ** -0.5
    blocks = [kp_ref, km_ref, kn_ref]
    assert tq == 2 * sub
    for sb in range(tq // sub):
        kv = jnp.concatenate([r[...] for r in blocks[sb:sb + 2]], axis=0)
        rk = row0 + sb * sub - A_RADIUS + lax.broadcasted_iota(jnp.int32, (1, nk), 1)
        bias = band_ref[...] + jnp.where((rk >= start) & (rk < end), 0.0, -jnp.inf)
        bias_ref[0:sub, :] = bias
        bias_ref[sub:2 * sub, :] = bias
        first = lax.broadcasted_iota(jnp.int32, (2 * sub, 1), 0) < sub
        rows = slice(sb * sub, (sb + 1) * sub)
        for j in range(A_KV_HEADS):
            k = kv[:, j * A_HEAD_DIM:(j + 1) * A_HEAD_DIM]
            v = kv[:, A_KV_WIDTH + j * A_HEAD_DIM:A_KV_WIDTH + (j + 1) * A_HEAD_DIM]
            for gq in range(0, group, 2):
                heads = (j * group + gq, j * group + gq + 1)
                cols = [slice(h * A_HEAD_DIM, (h + 1) * A_HEAD_DIM) for h in heads]
                qh = jnp.concatenate([q_ref[rows, c] for c in cols], axis=0)
                s = lax.dot_general(qh, k, (((1,), (1,)), ((), ())), preferred_element_type=F32) * scale
                s = s + bias_ref[...]
                sk = jnp.where(first, sink_ref[heads[0]], sink_ref[heads[1]])
                m = jnp.maximum(jnp.max(s, axis=-1, keepdims=True), sk)
                p = jnp.exp(s - m)
                den = jnp.sum(p, axis=-1, keepdims=True) + jnp.exp(sk - m)
                o = jnp.dot(p.astype(BF16), v, preferred_element_type=F32) / den
                o_ref[rows, cols[0]] = o[0:sub].astype(o_ref.dtype)
                o_ref[rows, cols[1]] = o[sub:2 * sub].astype(o_ref.dtype)


def _even_attention(q, kv, sink, bounds):
    m = q.shape[0]
    tq = ATT_TQ
    hb = A_RADIUS
    per = tq // hb
    last = m // hb - 1
    kvw = kv.shape[1]
    band = _band_bias(A_RADIUS, A_RADIUS)
    return pl.pallas_call(
        functools.partial(_even_attn_kernel, bounds=bounds),
        grid=(m // tq,),
        in_specs=[pl.BlockSpec(memory_space=pltpu.SMEM),
                  pl.BlockSpec(band.shape, lambda i: (0, 0)),
                  pl.BlockSpec((tq, A_WIDTH), lambda i: (i, 0)),
                  pl.BlockSpec((hb, kvw), lambda i: (jnp.maximum(i * per - 1, 0), 0)),
                  pl.BlockSpec((tq, kvw), lambda i: (i, 0)),
                  pl.BlockSpec((hb, kvw), lambda i: (jnp.minimum((i + 1) * per, last), 0))],
        out_specs=pl.BlockSpec((tq, A_WIDTH), lambda i: (i, 0)),
        out_shape=jax.ShapeDtypeStruct((m, A_WIDTH), BF16),
        scratch_shapes=[pltpu.VMEM((2 * band.shape[0], band.shape[1]), F32)],
        compiler_params=_cparams(("parallel",)),
        name="even_attention",
    )(sink, band, q, kv, kv, kv)


def _odd_attn_kernel(*refs, bounds):
    ng = len(C_DILATIONS)
    band_ref = refs[7 * ng]
    o_ref, oacc, lacc = refs[7 * ng + 1:]
    i = pl.program_id(0)
    chunk = ODD_CHUNK
    qb = 128
    nk = qb + 2 * C_RADIUS
    row0 = i * chunk
    start, end = _seq_bounds(row0, bounds)
    cc = lax.broadcasted_iota(jnp.int32, (1, nk), 1)
    scale = C_HEAD_DIM ** -0.5
    for g, d in enumerate(C_DILATIONS):
        q_ref, kp_ref, km_ref, kn_ref, vp_ref, vm_ref, vn_ref = refs[7 * g:7 * g + 7]
        tg = chunk // d
        t_lo, t_hi, t_c0 = start // d, end // d, row0 // d
        for sb in range(tg // qb):
            lo, hi = qb * sb - C_RADIUS, qb * sb + qb + C_RADIUS
            tk = t_c0 + lo + cc
            col = jnp.where((tk >= t_lo) & (tk < t_hi), 0.0, -jnp.inf)
            for r in range(d):
                def window(p_ref, m_ref, n_ref):
                    parts = []
                    if lo < 0:
                        parts.append(p_ref[0, r, :, :])
                    parts.append(m_ref[0, r, max(lo, 0):min(hi, tg), :])
                    if hi > tg:
                        parts.append(n_ref[0, r, :, :])
                    return parts[0] if len(parts) == 1 else jnp.concatenate(parts, axis=0)

                q = q_ref[0, r, qb * sb:qb * (sb + 1), :]
                k = window(kp_ref, km_ref, kn_ref)
                v = window(vp_ref, vm_ref, vn_ref)
                s = lax.dot_general(q, k, (((1,), (1,)), ((), ())), preferred_element_type=F32) * scale
                s = s + band_ref[...] + col
                m = jnp.max(s, axis=-1, keepdims=True)
                p = jnp.exp(s - m)
                den = jnp.sum(p, axis=-1, keepdims=True)
                o = jnp.dot(p.astype(BF16), v, preferred_element_type=F32) / den
                lse = jnp.broadcast_to(m + jnp.log(den), (qb, LANES))
                if d == 1:
                    rows = pl.ds(qb * sb, qb)
                else:
                    rows = pl.ds(r + d * qb * sb, qb, stride=d)
                oacc[g, rows, :] = o
                lacc[g, rows, :] = lse
    ls = [lacc[g] for g in range(ng)]
    mx = functools.reduce(jnp.maximum, ls)
    ws = [jnp.exp(l - mx) for l in ls]
    tot = functools.reduce(lambda a, b: a + b, ws)
    out = functools.reduce(lambda a, b: a + b, [(ws[g] / tot) * oacc[g] for g in range(ng)])
    o_ref[...] = out.astype(o_ref.dtype)


def _odd_attention(qkv, bounds, m):
    chunk = ODD_CHUNK
    hb = C_RADIUS
    operands, in_specs = [], []
    for (q, k, v), d in zip(qkv, C_DILATIONS):
        tg = chunk // d
        per = tg // hb
        last = m // d // hb - 1
        main = pl.BlockSpec((1, d, tg, LANES), lambda i, h: (h, 0, i, 0))
        prev = pl.BlockSpec((1, d, hb, LANES), lambda i, h, per=per: (h, 0, jnp.maximum(i * per - 1, 0), 0))
        nxt = pl.BlockSpec((1, d, hb, LANES), lambda i, h, per=per, last=last: (h, 0, jnp.minimum((i + 1) * per, last), 0))
        operands += [q, k, k, k, v, v, v]
        in_specs += [main, prev, main, nxt, prev, main, nxt]
    band = _band_bias(128, C_RADIUS)
    operands.append(band)
    in_specs.append(pl.BlockSpec(band.shape, lambda i, h: (0, 0)))
    ng = len(C_DILATIONS)
    return pl.pallas_call(
        functools.partial(_odd_attn_kernel, bounds=bounds),
        grid=(m // chunk, C_HEADS),
        in_specs=in_specs,
        out_specs=pl.BlockSpec((chunk, LANES), lambda i, h: (i, h)),
        out_shape=jax.ShapeDtypeStruct((m, C_HEADS * C_HEAD_DIM), BF16),
        scratch_shapes=[pltpu.VMEM((ng, chunk, LANES), F32), pltpu.VMEM((ng, chunk, LANES), F32)],
        compiler_params=_cparams(("parallel", "arbitrary")),
        name="odd_attention",
    )(*operands)


def _filter_mlp_kernel(z_ref, w1_ref, b1_ref, f1_ref, w2_ref, b2_ref, f2_ref, w3_ref, b3_ref, dl_ref,
                       h_ref, nrm_ref):
    i = pl.program_id(0)
    z = z_ref[...]
    h = jnp.sin(f1_ref[...] * (jnp.dot(z.astype(BF16), w1_ref[...], preferred_element_type=F32) + b1_ref[...]))
    h = jnp.sin(f2_ref[...] * (jnp.dot(h.astype(BF16), w2_ref[...], preferred_element_type=F32) + b2_ref[...]))
    h = jnp.dot(h.astype(BF16), w3_ref[...], preferred_element_type=F32) + b3_ref[...]
    decay = jnp.exp(-z[:, 0:1] * dl_ref[...])
    nrep = h.shape[1] // decay.shape[1]
    h = h * jnp.concatenate([decay] * nrep, axis=1)
    h_ref[...] = h

    @pl.when(i == 0)
    def _():
        nrm_ref[...] = jnp.zeros_like(nrm_ref)

    half = h.shape[1] // 2
    col = lax.broadcasted_iota(jnp.int32, h.shape, 1)
    row = lax.broadcasted_iota(jnp.int32, h.shape, 0) + i * h.shape[0]
    a = jnp.where((col >= half) & (row == 0), 0.0, jnp.abs(h))
    nrm_ref[...] += jnp.sum(a, axis=0, keepdims=True)


def _filter_mlp(z, w1, b1, f1, w2, b2, f2, w3, b3, deltas, tl=512):
    l, e = z.shape
    hid = w1.shape[1]
    n = w3.shape[1]
    c = deltas.shape[0]
    full = lambda shape: pl.BlockSpec(shape, lambda i: (0, 0))
    return pl.pallas_call(
        _filter_mlp_kernel,
        grid=(l // tl,),
        in_specs=[pl.BlockSpec((tl, e), lambda i: (i, 0)),
                  full((e, hid)), full((1, hid)), full((1, hid)),
                  full((hid, hid)), full((1, hid)), full((1, hid)),
                  full((hid, n)), full((1, n)), full((1, c))],
        out_specs=[pl.BlockSpec((tl, n), lambda i: (i, 0)), full((1, n))],
        out_shape=[jax.ShapeDtypeStruct((l, n), F32), jax.ShapeDtypeStruct((1, n), F32)],
        compiler_params=_cparams(("arbitrary",)),
        name="filter_mlp",
    )(z, w1, b1.reshape(1, hid), f1.reshape(1, hid), w2, b2.reshape(1, hid), f2.reshape(1, hid),
      w3, b3.reshape(1, n), deltas.reshape(1, c))


def _split(x):
    hi = x.astype(BF16)
    lo = (x - hi.astype(F32)).astype(BF16)
    return hi, lo


def _dot3(ch, cl, x):
    xh, xl = _split(x)
    r = jnp.dot(ch, xh, preferred_element_type=F32)
    r = r + jnp.dot(ch, xl, preferred_element_type=F32)
    return r + jnp.dot(cl, xh, preferred_element_type=F32)


def _dot1(ch, x):
    return jnp.dot(ch, x.astype(BF16), preferred_element_type=F32)


FFT_GROUP = 8


FFT_N2C = 4
FFT_RH = 24
FFT_SPB = 6


def _fft1_kernel(x_ref, fh_ref, fl_ref, ar_ref, ai_ref, *, kn1, n2):
    kh, rows, _ = ar_ref.shape
    n2q = n2 // FFT_N2C
    rh = rows // n2q
    rp = kh * rh
    c0 = pl.program_id(2) * n2q
    for g in range(n2q // FFT_GROUP):
        cols = [x_ref[pl.ds(c0 + g * FFT_GROUP + s, kn1, stride=n2), :] for s in range(FFT_GROUP)]
        out = _dot3(fh_ref[...], fl_ref[...], jnp.concatenate(cols, axis=1))
        for k in range(kh):
            for s in range(FFT_GROUP):
                r0 = (g * FFT_GROUP + s) * rh
                ar_ref[k, r0:r0 + rh, :] = out[k * rh:(k + 1) * rh, s * LANES:(s + 1) * LANES]
                ai_ref[k, r0:r0 + rh, :] = out[rp + k * rh:rp + (k + 1) * rh, s * LANES:(s + 1) * LANES]


def _fft_stage1(x, f1h, f1l, plan, l, nb, row_blk0, col_blk0, c):
    n2, kh, rh = plan["n2"], plan["kh"], plan["rh"]
    kn1 = f1h.shape[1]
    rows = n2 // FFT_N2C * rh
    out = jax.ShapeDtypeStruct((nb, kh, FFT_N2C, rows, c), F32)
    fspec = pl.BlockSpec(f1h.shape, lambda b, j, q: (0, 0))
    ospec = pl.BlockSpec((None, kh, None, rows, LANES), lambda b, j, q: (b, 0, q, 0, j))
    return pl.pallas_call(
        functools.partial(_fft1_kernel, kn1=kn1, n2=n2),
        grid=(nb, c // LANES, FFT_N2C),
        in_specs=[pl.BlockSpec((l, LANES), lambda b, j, q: (row_blk0 + b, col_blk0 + j)), fspec, fspec],
        out_specs=[ospec, ospec],
        out_shape=[out, out],
        compiler_params=_cparams(("parallel", "parallel", "arbitrary")),
        name="fft_stage1",
    )(x, f1h, f1l)


def _cpair(p, n):
    return p[:n, :LANES] - p[n:, LANES:], p[:n, LANES:] + p[n:, :LANES]


def _twiddle(xr, xi, tr, ti):
    return xr * tr - xi * ti, xr * ti + xi * tr


def _filter_mid_kernel(fr_ref, fi_ref, br_ref, bi_ref, tr_ref, ti_ref, w_ref, inv_ref, b0_ref,
                       kr_ref, ki_ref, *, nslab, rh, n2):
    k0 = pl.program_id(2) * FFT_SPB

    @pl.when(pl.program_id(1) * rh + k0 < nslab)
    def _():
        for t in range(FFT_SPB):
            rows = pl.ds(k0 + t, n2, stride=rh)
            tr, ti = tr_ref[t], ti_ref[t]
            x4 = jnp.concatenate(_twiddle(fr_ref[rows, :], fi_ref[rows, :], tr, ti)
                                 + _twiddle(br_ref[rows, :], bi_ref[rows, :], tr, ti), axis=1)
            p = _dot1(w_ref[...], x4)
            fr, fi = _cpair(p[:, :2 * LANES], n2)
            br, bi = _cpair(p[:, 2 * LANES:], n2)
            kr_ref[t] = (fr + (br - b0_ref[...])) * inv_ref[...]
            ki_ref[t] = (fi - bi) * inv_ref[...]

    @pl.when(pl.program_id(1) * rh + k0 >= nslab)
    def _():
        kr_ref[...] = jnp.zeros_like(kr_ref)
        ki_ref[...] = jnp.zeros_like(ki_ref)


def _filter_mid(ar, ai, inv, b0, plan):
    n2, kh, rh, rp = plan["n2"], plan["kh"], plan["rh"], plan["rp"]
    oc = ar.shape[-1] // 2
    nj = oc // LANES
    fwd = pl.BlockSpec((None, None, n2 * rh, LANES), lambda j, kk, k: (0, kk, 0, j))
    bwd = pl.BlockSpec((None, None, n2 * rh, LANES), lambda j, kk, k: (0, kk, 0, j + nj))
    steps = rh // FFT_SPB
    tspec = pl.BlockSpec((FFT_SPB, n2, LANES), lambda j, kk, k: (kk * steps + k, 0, 0))
    wspec = pl.BlockSpec((2 * n2, n2), lambda j, kk, k: (0, 0))
    vec = pl.BlockSpec((1, LANES), lambda j, kk, k: (0, j))
    ospec = pl.BlockSpec((FFT_SPB, n2, LANES), lambda j, kk, k: (kk * steps + k, 0, j))
    out = jax.ShapeDtypeStruct((rp, n2, oc), F32)
    return pl.pallas_call(
        functools.partial(_filter_mid_kernel, nslab=plan["r"], rh=rh, n2=n2),
        grid=(nj, kh, steps),
        in_specs=[fwd, fwd, bwd, bwd, tspec, tspec, wspec, vec, vec],
        out_specs=[ospec, ospec],
        out_shape=[out, out],
        compiler_params=_cparams(("parallel", "arbitrary", "arbitrary")),
        name="filter_mid",
    )(ar, ai, ar, ai, *plan["tw"], plan["f2"], inv, b0)


def _conv_mid_kernel(ar_ref, ai_ref, tr_ref, ti_ref, wf_ref, wi_ref, kr_ref, ki_ref, gh_ref, gl_ref,
                     y_ref, dr_ref, di_ref, *, nslab, rh, n2):
    k0 = pl.program_id(3) * FFT_SPB

    @pl.when((pl.program_id(2) == 0) & (pl.program_id(3) == 0))
    def _():
        y_ref[...] = jnp.zeros_like(y_ref)

    @pl.when(pl.program_id(2) * rh + k0 < nslab)
    def _():
        for t in range(FFT_SPB):
            rows = pl.ds(k0 + t, n2, stride=rh)
            tr, ti = tr_ref[t], ti_ref[t]
            x2 = jnp.concatenate(_twiddle(ar_ref[rows, :], ai_ref[rows, :], tr, ti), axis=1)
            xr, xi = _cpair(_dot1(wf_ref[...], x2), n2)
            kr, ki = kr_ref[t], ki_ref[t]
            y2 = jnp.concatenate([xr * kr - xi * ki, xr * ki + xi * kr], axis=1)
            cr, ci = _cpair(_dot1(wi_ref[...], y2), n2)
            dr, di = _twiddle(cr, ci, tr, -ti)
            dr_ref[rows, :] = dr
            di_ref[rows, :] = di

    @pl.when(pl.program_id(2) * rh + k0 >= nslab)
    def _():
        for t in range(FFT_SPB):
            rows = pl.ds(k0 + t, n2, stride=rh)
            dr_ref[rows, :] = jnp.zeros((n2, LANES), F32)
            di_ref[rows, :] = jnp.zeros((n2, LANES), F32)

    @pl.when(pl.program_id(3) == pl.num_programs(3) - 1)
    def _():
        n1c = y_ref.shape[0]
        n1r = y_ref.shape[1] // n2
        for g in range(n2 // FFT_GROUP):
            cols = []
            for s in range(FFT_GROUP):
                r0 = (g * FFT_GROUP + s) * rh
                cols.append(jnp.concatenate([dr_ref[r0:r0 + rh, :], di_ref[r0:r0 + rh, :]], axis=0))
            y = _dot3(gh_ref[...], gl_ref[...], jnp.concatenate(cols, axis=1))
            for c in range(n1c):
                for s in range(FFT_GROUP):
                    t0 = (g * FFT_GROUP + s) * n1r
                    y_ref[c, t0:t0 + n1r, :] += y[c * n1r:(c + 1) * n1r, s * LANES:(s + 1) * LANES]


def _conv_mid(ar, ai, kr, ki, order, plan):
    nb, kh, _, c = ar.shape
    n2, rh = plan["n2"], plan["rh"]
    gh, gl = plan["g"]
    nh = gh.shape[1]
    n1r = min(16, nh)
    n1c = nh // n1r
    koff = order * (c // LANES)
    blk = pl.BlockSpec((None, None, n2 * rh, LANES), lambda b, j, kk, k: (b, kk, 0, j))
    steps = rh // FFT_SPB
    tspec = pl.BlockSpec((FFT_SPB, n2, LANES), lambda b, j, kk, k: (kk * steps + k, 0, 0))
    wspec = pl.BlockSpec((2 * n2, n2), lambda b, j, kk, k: (0, 0))
    kspec = pl.BlockSpec((FFT_SPB, n2, LANES), lambda b, j, kk, k: (kk * steps + k, 0, j + koff))
    gspec = pl.BlockSpec((None, nh, 2 * rh), lambda b, j, kk, k: (kk, 0, 0))
    return pl.pallas_call(
        functools.partial(_conv_mid_kernel, nslab=plan["r"], rh=rh, n2=n2),
        grid=(nb, c // LANES, kh, steps),
        in_specs=[blk, blk, tspec, tspec, wspec, wspec, kspec, kspec, gspec, gspec],
        out_specs=pl.BlockSpec((None, n1c, n2 * n1r, LANES), lambda b, j, kk, k: (b, 0, 0, j)),
        out_shape=jax.ShapeDtypeStruct((nb, n1c, n2 * n1r, c), F32),
        scratch_shapes=[pltpu.VMEM((n2 * rh, LANES), F32), pltpu.VMEM((n2 * rh, LANES), F32)],
        compiler_params=_cparams(("parallel", "parallel", "arbitrary", "arbitrary")),
        name="conv_mid",
    )(ar, ai, *plan["tw"], plan["f2"], plan["f2i"], kr, ki, gh, gl)


def _gate_kernel(y_ref, u_ref, g_ref, bias_ref, o_ref, *, n2):
    n1r = y_ref.shape[0] // n2
    for a in range(n1r):
        rows = slice(a * n2, (a + 1) * n2)
        yt = y_ref[pl.ds(a, n2, stride=n1r), :]
        o_ref[rows, :] = (g_ref[rows, :] * (yt + u_ref[rows, :] * bias_ref[...])).astype(o_ref.dtype)


def _gate(y, n2, u, u_row0, u_col0, gate, g_row0, g_col0, bias, out_dtype):
    nb, n1c, yr, c = y.shape
    tr = yr
    return pl.pallas_call(
        functools.partial(_gate_kernel, n2=n2),
        grid=(nb, c // LANES, n1c),
        in_specs=[pl.BlockSpec((None, None, yr, LANES), lambda b, j, q: (b, q, 0, j)),
                  pl.BlockSpec((tr, LANES), lambda b, j, q: (u_row0 // tr + b * n1c + q, u_col0 + j)),
                  pl.BlockSpec((tr, LANES), lambda b, j, q: (g_row0 // tr + b * n1c + q, g_col0 + j)),
                  pl.BlockSpec((1, LANES), lambda b, j, q: (0, j))],
        out_specs=pl.BlockSpec((tr, LANES), lambda b, j, q: (b * n1c + q, j)),
        out_shape=jax.ShapeDtypeStruct((nb * n1c * tr, c), out_dtype),
        compiler_params=_cparams(("parallel", "parallel", "arbitrary")),
        name="hyena_gate",
    )(y, u, gate, bias)


def _np_split(a):
    a32 = np.asarray(a, np.float32)
    hi = a32.astype(BF16)
    lo = (a32 - hi.astype(np.float32)).astype(BF16)
    return hi, lo


def _fft_plan(l):
    n = 2 * l
    n2 = FFT_N2
    n1 = n // n2
    r = n1 // 2 + 1
    rh = FFT_RH
    kh = -(-r // rh)
    rp = kh * rh
    kn1 = n1 // 2
    k1 = np.arange(rp, dtype=np.float64)[:, None]
    live = (k1 < r).astype(np.float64)

    ang = 2 * np.pi * k1 * np.arange(kn1)[None, :] / n1
    f1 = np.concatenate([np.cos(ang) * live, -np.sin(ang) * live], axis=0)

    kk = np.arange(rp, dtype=np.float64)[None, :]
    wgt = np.where((kk == 0) | (kk == n1 // 2), 1.0, 2.0) * (kk < r) / n
    ango = 2 * np.pi * np.arange(n1 // 2)[:, None] * kk / n1
    gre = (np.cos(ango) * wgt).reshape(n1 // 2, kh, rh).transpose(1, 0, 2)
    gim = (-np.sin(ango) * wgt).reshape(n1 // 2, kh, rh).transpose(1, 0, 2)
    g = np.concatenate([gre, gim], axis=2)

    a2 = 2 * np.pi * np.outer(np.arange(n2), np.arange(n2)) / n2
    f2 = np.concatenate([np.cos(a2), -np.sin(a2)], axis=0)
    f2i = np.concatenate([np.cos(a2), np.sin(a2)], axis=0)

    idx = jnp.arange(rp, dtype=jnp.int32)[:, None] * jnp.arange(n2, dtype=jnp.int32)[None, :]
    ang = idx.astype(F32) * F32(2.0 * math.pi / n)
    tw = tuple(jnp.broadcast_to(t[:, :, None], (rp, n2, LANES)) for t in (jnp.cos(ang), -jnp.sin(ang)))
    return dict(n1=n1, n2=n2, r=r, rp=rp, kh=kh, rh=rh, f1=_np_split(f1), g=_np_split(g),
                f2=_np_split(f2)[0], f2i=_np_split(f2i)[0], tw=tw)


def _filter_features(l):
    t = jnp.linspace(0.0, 1.0, l, dtype=F32)[:, None]
    bands = jnp.linspace(1e-4, B_BANDS - 1, B_BANDS, dtype=F32)[None, :]
    w = 2.0 * math.pi * jnp.arange(l, dtype=F32)[:, None] / l
    return jnp.concatenate([t, jnp.cos(bands * w), -jnp.sin(bands * w)], axis=-1)


def _hyena_filters(plan, l, c, w1, b1, f1, w2, b2, f2, w3, b3):
    n1, n2 = plan["n1"], plan["n2"]
    z = _filter_features(l)
    e = z.shape[1]
    ep = -(-e // 16) * 16
    z = jnp.pad(z, ((0, 0), (0, ep - e)))
    w1p = jnp.pad(w1, ((0, ep - e), (0, 0))).astype(BF16)
    max_decay = math.log(B_DECAY_TARGET) / B_FAST_DECAY_PCT
    min_decay = math.log(B_DECAY_TARGET) / B_SLOW_DECAY_PCT
    deltas = jnp.abs(jnp.linspace(min_decay, max_decay, c, dtype=F32))
    h, nrm = _filter_mlp(z, w1p, b1, f1, w2.astype(BF16), b2, f2, w3.astype(BF16), b3, deltas,
                         tl=min(512, l))
    oc = h.shape[1] // 2
    inv = 1.0 / (nrm[:, :oc] + nrm[:, oc:])
    b0 = h[0:1, oc:]
    ar, ai = _fft_stage1(h, *plan["f1"], plan, l, 1, 0, 0, 2 * oc)
    mid_shape = (1, plan["kh"], n2 * plan["rh"], 2 * oc)
    return _filter_mid(ar.reshape(mid_shape), ai.reshape(mid_shape), inv, b0, plan)


def _hyena_conv(plan, kf, order, l, nb, u, u_row0, u_col0, gate, g_row0, g_col0, bias, c, out_dtype):
    n2 = plan["n2"]
    cb = c // LANES
    ar, ai = _fft_stage1(u, *plan["f1"], plan, l, nb, u_row0 // l, u_col0 * cb, c)
    mid_shape = (nb, plan["kh"], n2 * plan["rh"], c)
    y = _conv_mid(ar.reshape(mid_shape), ai.reshape(mid_shape), kf[0], kf[1], order, plan)
    return _gate(y, n2, u, u_row0, u_col0 * cb, gate, g_row0, g_col0 * cb, bias.reshape(1, c), out_dtype)


def _hyena_mixer(u, seqs, plans, c, fw, hy_bias):
    outs = []
    for (row0, nb, l) in seqs:
        plan = plans[l]
        kf = _hyena_filters(plan, l, c, *fw)
        z = _hyena_conv(plan, kf, 0, l, nb, u, row0, 0, u, row0, 1, hy_bias[0], c, F32)
        o = _hyena_conv(plan, kf, 1, l, nb, z, 0, 0, u, row0, 2, hy_bias[1], c, BF16)
        outs.append(o)
    return jnp.concatenate(outs, axis=0)


def _rope_tables(pos, hd):
    rot = hd // ROPE_FRACTION
    half = rot // 2
    inv = ROPE_THETA ** (-(jnp.arange(half, dtype=F32) * 2.0 / rot))
    ang = pos[:, None] * inv[None, :]
    cos, sin = jnp.cos(ang), jnp.sin(ang)
    m = pos.shape[0]
    one = jnp.ones((m, hd - rot), F32)
    zero = jnp.zeros((m, hd - rot), F32)
    zh = jnp.zeros((m, half), F32)
    c = jnp.concatenate([cos, cos, one], axis=1)
    s1 = jnp.concatenate([-sin, zh, zero], axis=1)
    s2 = jnp.concatenate([zh, sin, zero], axis=1)
    rep = LANES // hd
    return tuple(jnp.tile(t, (1, rep)) for t in (c, s1, s2)), half


def _trunk(x, bounds, seqs, p):
    m, dm = x.shape
    pos = jnp.concatenate([jnp.tile(jnp.arange(l, dtype=F32), nb) for (_, nb, l) in seqs])
    tabs_a, half_a = _rope_tables(pos, A_HEAD_DIM)
    tabs_c, half_c = _rope_tables(pos, C_HEAD_DIM)
    ident = (jnp.ones((m, LANES), F32), jnp.zeros((m, LANES), F32), jnp.zeros((m, LANES), F32))
    tabs_kv = tuple(jnp.concatenate([a, b], axis=1) for a, b in zip(tabs_a, ident))
    c_hy = dm - A_WIDTH
    plans = {l: _fft_plan(l) for l in sorted({l for (_, _, l) in seqs})}
    xf = x
    xb = x.astype(BF16)
    for i in range(DEPTH):
        j = i // 2
        if i % 2 == 0:
            w_in = p['mix_e_w_in'][j].astype(BF16)
            kv0 = A_WIDTH
            hy0 = A_WIDTH + 2 * A_KV_WIDTH
            q = _matmul_rope(xb, w_in[:, :kv0], tabs_a, half_a, tn=512)
            kv = _matmul_rope(xb, w_in[:, kv0:hy0], tabs_kv, half_a, tn=2 * A_KV_WIDTH)
            u = _matmul_conv(xb, w_in[:, hy0:], p['hy_conv_w'][j], p['hy_conv_b'][j], bounds)
            a_out = _even_attention(q, kv, p['a_sink'][j], bounds)
            fw = (p['hy_w1'][j], p['hy_b1'][j], p['hy_f1'][j], p['hy_w2'][j], p['hy_b2'][j],
                  p['hy_f2'][j], p['hy_w3'][j], p['hy_b3'][j])
            h_out = _hyena_mixer(u, seqs, plans, c_hy, fw, p['hy_bias'][j])
            w_out = p['mix_e_w_out'][j].astype(BF16)
            xf, xb = _matmul_ln([a_out, h_out], [w_out[:A_WIDTH], w_out[A_WIDTH:]], xf,
                                p['ln1_g'][i], p['ln1_b'][i])
        else:
            w_in = p['mix_o_w_in'][j].astype(BF16)
            gw = C_HEADS * C_HEAD_DIM
            ng = len(C_DILATIONS)
            qkv = []
            for g, d in enumerate(C_DILATIONS):
                trio = []
                for part in range(3):
                    c0 = (part * ng + g) * gw
                    trio.append(_odd_proj(xb, w_in, c0, tabs_c, half_c, d, rope=part < 2))
                qkv.append(tuple(trio))
            o = _odd_attention(qkv, bounds, m)
            xf, xb = _matmul_ln([o], [p['mix_o_w_out'][j].astype(BF16)], xf, p['ln1_g'][i], p['ln1_b'][i])
        xf, xb = _ffn_ln(xb, xf, p['ffn_w_gate'][i].astype(BF16), p['ffn_w_up'][i].astype(BF16),
                         p['ffn_w_down'][i].astype(BF16), p['ln2_g'][i], p['ln2_b'][i])
    return xf


def kernel(x_prompt, x_sample, mix_e_w_in, a_sink, hy_conv_w, hy_conv_b, hy_w1, hy_b1, hy_f1, hy_w2, hy_b2,
           hy_f2, hy_w3, hy_b3, hy_bias, mix_e_w_out, mix_o_w_in, mix_o_w_out, ffn_w_gate, ffn_w_up,
           ffn_w_down, ln1_g, ln1_b, ln2_g, ln2_b):
    p = dict(mix_e_w_in=mix_e_w_in, a_sink=a_sink, hy_conv_w=hy_conv_w, hy_conv_b=hy_conv_b,
             hy_w1=hy_w1, hy_b1=hy_b1, hy_f1=hy_f1, hy_w2=hy_w2, hy_b2=hy_b2, hy_f2=hy_f2,
             hy_w3=hy_w3, hy_b3=hy_b3, hy_bias=hy_bias, mix_e_w_out=mix_e_w_out,
             mix_o_w_in=mix_o_w_in, mix_o_w_out=mix_o_w_out, ffn_w_gate=ffn_w_gate,
             ffn_w_up=ffn_w_up, ffn_w_down=ffn_w_down, ln1_g=ln1_g, ln1_b=ln1_b,
             ln2_g=ln2_g, ln2_b=ln2_b)
    dm = x_prompt.shape[-1]
    seqs, bounds, row = [], [0], 0
    for xs in (x_prompt, x_sample):
        nb, l = xs.shape[0], xs.shape[1]
        seqs.append((row, nb, l))
        for _ in range(nb):
            row += l
            bounds.append(row)
    x = jnp.concatenate([x_prompt.reshape(-1, dm), x_sample.reshape(-1, dm)], axis=0)
    y = _trunk(x, tuple(bounds), tuple(seqs), p)
    n_p = x_prompt.shape[0] * x_prompt.shape[1]
    return (y[:n_p].reshape(x_prompt.shape), y[n_p:].reshape(x_sample.shape))
```

```python
import functools
import math

import numpy as np
import jax
import jax.numpy as jnp
from jax import lax
from jax.experimental import pallas as pl
from jax.experimental.pallas import tpu as pltpu

F32 = jnp.float32
BF16 = jnp.bfloat16

DEPTH = 4
A_HEADS, A_KV_HEADS, A_HEAD_DIM, A_RADIUS = 16, 2, 64, 128
A_WIDTH = A_HEADS * A_HEAD_DIM
A_KV_WIDTH = A_KV_HEADS * A_HEAD_DIM
B_SHORT, B_EMB = 3, 33
B_BANDS = (B_EMB - 1) // 2
B_DECAY_TARGET, B_FAST_DECAY_PCT, B_SLOW_DECAY_PCT = 1e-2, 0.3, 1.5
C_HEADS, C_HEAD_DIM = 16, 128
C_DILATIONS = (1, 4, 16)
C_RADIUS = 64
ROPE_THETA, ROPE_FRACTION = 500000.0, 4
ALPHA = (2 * DEPTH) ** 0.25
LN_EPS = 1e-5

LANES = 128
VMEM_LIMIT = 56 * 1024 * 1024
FFT_N2 = 256

ODD_CHUNK = 2048
ATT_TQ = 256


def _cparams(sem):
    return pltpu.CompilerParams(dimension_semantics=sem, vmem_limit_bytes=VMEM_LIMIT)


def _seq_bounds(row, bounds):
    start = jnp.int32(bounds[0])
    end = jnp.int32(bounds[1])
    for b0, b1 in zip(bounds[1:-1], bounds[2:]):
        inside = row >= b0
        start = jnp.where(inside, jnp.int32(b0), start)
        end = jnp.where(inside, jnp.int32(b1), end)
    return start, end


def _rope(a, c, s1, s2, half):
    w = a.shape[-1]
    return a * c + pltpu.roll(a, w - half, 1) * s1 + pltpu.roll(a, half, 1) * s2


def _mm_conv_kernel(xp_ref, xm_ref, xn_ref, w_ref, cw_ref, cb_ref, o_ref, *, bounds):
    i = pl.program_id(0)
    tm, hb = xm_ref.shape[0], xp_ref.shape[0]
    rows = tm + 2 * hb
    row0 = i * tm
    start, end = _seq_bounds(row0, bounds)
    lhs = jnp.concatenate([xp_ref[...], xm_ref[...], xn_ref[...]], axis=0)
    acc = jnp.dot(lhs, w_ref[...], preferred_element_type=F32)
    h0 = acc[hb:hb + tm]
    hm = pltpu.roll(acc, 1, 0)[hb:hb + tm]
    hp = pltpu.roll(acc, rows - 1, 0)[hb:hb + tm]
    ridx = lax.broadcasted_iota(jnp.int32, h0.shape, 0)
    hm = jnp.where((ridx == 0) & (row0 <= start), 0.0, hm)
    hp = jnp.where((ridx == tm - 1) & (row0 + tm >= end), 0.0, hp)
    y = cb_ref[...] + hm * cw_ref[0:1, :]
    y = y + h0 * cw_ref[1:2, :]
    y = y + hp * cw_ref[2:3, :]
    o_ref[...] = y


def _matmul_conv(x, w, cw, cb, bounds, tn=512, tm=1024):
    m, k = x.shape
    n = w.shape[1]
    hb = 16
    per = tm // hb
    last = m // hb - 1
    return pl.pallas_call(
        functools.partial(_mm_conv_kernel, bounds=bounds),
        grid=(m // tm, n // tn),
        in_specs=[pl.BlockSpec((hb, k), lambda i, j: (jnp.maximum(i * per - 1, 0), 0)),
                  pl.BlockSpec((tm, k), lambda i, j: (i, 0)),
                  pl.BlockSpec((hb, k), lambda i, j: (jnp.minimum((i + 1) * per, last), 0)),
                  pl.BlockSpec((k, tn), lambda i, j: (0, j)),
                  pl.BlockSpec((B_SHORT, tn), lambda i, j: (0, j)),
                  pl.BlockSpec((1, tn), lambda i, j: (0, j))],
        out_specs=pl.BlockSpec((tm, tn), lambda i, j: (i, j)),
        out_shape=jax.ShapeDtypeStruct((m, n), F32),
        compiler_params=_cparams(("parallel", "arbitrary")),
        name="matmul_conv",
    )(x, x, x, w, cw, cb.reshape(1, n))


def _mm_rope_kernel(x_ref, w_ref, c_ref, s1_ref, s2_ref, o_ref, *, half):
    acc = jnp.dot(x_ref[...], w_ref[...], preferred_element_type=F32)
    tw = c_ref.shape[1]
    rc = 256
    for r0 in range(0, acc.shape[0], rc):
        rows = slice(r0, r0 + rc)
        for c in range(acc.shape[1] // tw):
            cols = slice(c * tw, (c + 1) * tw)
            o_ref[rows, cols] = _rope(acc[rows, cols], c_ref[rows, :], s1_ref[rows, :], s2_ref[rows, :],
                                      half).astype(o_ref.dtype)


def _matmul_rope(x, w, tabs, half, tn, tm=1024):
    m, k = x.shape
    n = w.shape[1]
    tw = tabs[0].shape[1]
    tab_spec = pl.BlockSpec((tm, tw), lambda i, j: (i, 0))
    return pl.pallas_call(
        functools.partial(_mm_rope_kernel, half=half),
        grid=(m // tm, n // tn),
        in_specs=[pl.BlockSpec((tm, k), lambda i, j: (i, 0)),
                  pl.BlockSpec((k, tn), lambda i, j: (0, j)),
                  tab_spec, tab_spec, tab_spec],
        out_specs=pl.BlockSpec((tm, tn), lambda i, j: (i, j)),
        out_shape=jax.ShapeDtypeStruct((m, n), BF16),
        compiler_params=_cparams(("parallel", "arbitrary")),
        name="matmul_rope",
    )(x, w, *tabs)


def _odd_proj_kernel(x_ref, w_ref, c_ref, s1_ref, s2_ref, o_ref, acc_ref, tmp_ref, *, d, rope, half):
    hps = w_ref.shape[1] // LANES
    tm = x_ref.shape[0]
    t = tm // d
    pair = 2
    rc = 256
    for p in range(hps // pair):
        acc = jnp.dot(x_ref[...], w_ref[:, p * pair * LANES:(p + 1) * pair * LANES], preferred_element_type=F32)
        for h2 in range(pair):
            hh = p * pair + h2
            slot = (p % 2) * pair + h2
            acc_ref[slot, :, :] = acc[:, h2 * LANES:(h2 + 1) * LANES]
            for c0 in range(0, tm, rc):
                a = acc_ref[slot, c0:c0 + rc, :]
                if rope:
                    a = _rope(a, c_ref[c0:c0 + rc, :], s1_ref[c0:c0 + rc, :], s2_ref[c0:c0 + rc, :], half)
                if d == 1:
                    o_ref[hh, 0, c0:c0 + rc, :] = a.astype(BF16)
                elif rope:
                    acc_ref[slot, c0:c0 + rc, :] = a
            if d == 16:
                ts = hh % 2
                for ra in range(4):
                    tmp_ref[ts, ra * (tm // 4):(ra + 1) * (tm // 4), :] = acc_ref[slot, pl.ds(ra, tm // 4, stride=4), :]
                for ra in range(4):
                    for rb in range(4):
                        o_ref[hh, ra + 4 * rb, :, :] = tmp_ref[ts, pl.ds(ra * (tm // 4) + rb, t, stride=4), :].astype(BF16)
            elif d > 1:
                for r in range(d):
                    o_ref[hh, r, :, :] = acc_ref[slot, pl.ds(r, t, stride=d), :].astype(BF16)


def _odd_proj(x, w, col0, tabs, half, d, rope, hps=8):
    m, k = x.shape
    tm = ODD_CHUNK
    cb0 = col0 // (hps * LANES)
    tab_spec = pl.BlockSpec((tm, LANES), lambda i, j: (i, 0))
    return pl.pallas_call(
        functools.partial(_odd_proj_kernel, d=d, rope=rope, half=half),
        grid=(m // tm, C_HEADS // hps),
        in_specs=[pl.BlockSpec((tm, k), lambda i, j: (i, 0)),
                  pl.BlockSpec((k, hps * LANES), lambda i, j: (0, cb0 + j)),
                  tab_spec, tab_spec, tab_spec],
        out_specs=pl.BlockSpec((hps, d, tm // d, LANES), lambda i, j: (j, 0, i, 0)),
        out_shape=jax.ShapeDtypeStruct((C_HEADS, d, m // d, LANES), BF16),
        scratch_shapes=[pltpu.VMEM((4, tm, LANES), F32), pltpu.VMEM((2, tm, LANES), F32)],
        compiler_params=_cparams(("parallel", "arbitrary")),
        name="odd_proj",
    )(x, w, *tabs)


LN_ROWS = 128


def _layer_norm_store(x_ref, acc_ref, g_ref, b_ref, of_ref, ob_ref, row0=0, nrows=None):
    nrows = x_ref.shape[0] if nrows is None else nrows
    for c in range(nrows // LN_ROWS):
        rows = pl.ds(row0 + c * LN_ROWS, LN_ROWS)
        r = ALPHA * x_ref[rows, :] + acc_ref[rows, :]
        mu = jnp.mean(r, axis=-1, keepdims=True)
        xc = r - mu
        var = jnp.mean(xc * xc, axis=-1, keepdims=True)
        y = xc * lax.rsqrt(var + LN_EPS) * g_ref[...] + b_ref[...]
        of_ref[rows, :] = y
        ob_ref[rows, :] = y.astype(BF16)


def _mm_ln_kernel(*refs, n_in):
    ys = refs[:n_in]
    ws = refs[n_in:2 * n_in]
    x_ref, g_ref, b_ref, of_ref, ob_ref, acc_ref = refs[2 * n_in:]
    half = x_ref.shape[0] // 2
    for r0 in (0, half):
        rows = slice(r0, r0 + half)
        acc = jnp.dot(ys[0][rows, :], ws[0][...], preferred_element_type=F32)
        for y_ref, w_ref in zip(ys[1:], ws[1:]):
            acc = acc + jnp.dot(y_ref[rows, :], w_ref[...], preferred_element_type=F32)
        acc_ref[rows, :] = acc
        _layer_norm_store(x_ref, acc_ref, g_ref, b_ref, of_ref, ob_ref, r0, half)


def _matmul_ln(ys, ws, x, g, b, tm=512):
    m, dm = x.shape
    n_in = len(ys)
    in_specs = [pl.BlockSpec((tm, y.shape[1]), lambda i: (i, 0)) for y in ys]
    in_specs += [pl.BlockSpec(w.shape, lambda i: (0, 0)) for w in ws]
    in_specs += [pl.BlockSpec((tm, dm), lambda i: (i, 0)),
                 pl.BlockSpec((1, dm), lambda i: (0, 0)),
                 pl.BlockSpec((1, dm), lambda i: (0, 0))]
    return pl.pallas_call(
        functools.partial(_mm_ln_kernel, n_in=n_in),
        grid=(m // tm,),
        in_specs=in_specs,
        out_specs=[pl.BlockSpec((tm, dm), lambda i: (i, 0)), pl.BlockSpec((tm, dm), lambda i: (i, 0))],
        out_shape=[jax.ShapeDtypeStruct((m, dm), F32), jax.ShapeDtypeStruct((m, dm), BF16)],
        scratch_shapes=[pltpu.VMEM((tm, dm), F32)],
        compiler_params=_cparams(("parallel",)),
        name="matmul_ln",
    )(*ys, *ws, x, g.reshape(1, dm), b.reshape(1, dm))


def _ffn_kernel(xb_ref, xf_ref, wg_ref, wu_ref, wd_ref, g_ref, b_ref, of_ref, ob_ref, acc_ref):
    j = pl.program_id(1)

    @pl.when(j == 0)
    def _():
        acc_ref[...] = jnp.zeros_like(acc_ref)

    xb = xb_ref[...]
    gate = jnp.dot(xb, wg_ref[...], preferred_element_type=F32)
    up = jnp.dot(xb, wu_ref[...], preferred_element_type=F32)
    h = (gate * jax.nn.sigmoid(gate)) * up
    acc_ref[...] += jnp.dot(h.astype(BF16), wd_ref[...], preferred_element_type=F32)

    @pl.when(j == pl.num_programs(1) - 1)
    def _():
        _layer_norm_store(xf_ref, acc_ref, g_ref, b_ref, of_ref, ob_ref)


def _ffn_ln(xb, xf, wg, wu, wd, g, b, tm=512, tf=512):
    m, dm = xf.shape
    dff = wg.shape[1]
    row = lambda i, j: (i, 0)
    return pl.pallas_call(
        _ffn_kernel,
        grid=(m // tm, dff // tf),
        in_specs=[pl.BlockSpec((tm, dm), row), pl.BlockSpec((tm, dm), row),
                  pl.BlockSpec((dm, tf), lambda i, j: (0, j)),
                  pl.BlockSpec((dm, tf), lambda i, j: (0, j)),
                  pl.BlockSpec((tf, dm), lambda i, j: (j, 0)),
                  pl.BlockSpec((1, dm), lambda i, j: (0, 0)),
                  pl.BlockSpec((1, dm), lambda i, j: (0, 0))],
        out_specs=[pl.BlockSpec((tm, dm), row), pl.BlockSpec((tm, dm), row)],
        out_shape=[jax.ShapeDtypeStruct((m, dm), F32), jax.ShapeDtypeStruct((m, dm), BF16)],
        scratch_shapes=[pltpu.VMEM((tm, dm), F32)],
        compiler_params=_cparams(("parallel", "arbitrary")),
        name="ffn_ln",
    )(xb, xf, wg, wu, wd, g.reshape(1, dm), b.reshape(1, dm))


def _band_bias(nq, radius):
    r = np.arange(nq)[:, None]
    c = np.arange(nq + 2 * radius)[None, :]
    return np.where(np.abs(c - radius - r) <= radius, 0.0, -np.inf).astype(np.float32)


def _even_attn_kernel(sink_ref, band_ref, q_ref, kp_ref, km_ref, kn_ref, o_ref, bias_ref, *, bounds):
    i = pl.program_id(0)
    tq = q_ref.shape[0]
    sub = A_RADIUS
    nk = sub + 2 * A_RADIUS
    row0 = i * tq
    start, end = _seq_bounds(row0, bounds)
    group = A_HEADS // A_KV_HEADS
    scale = A_HEAD_DIM ** -0.5
    blocks = [kp_ref, km_ref, kn_ref]
    assert tq == 2 * sub
    for sb in range(tq // sub):
        kv = jnp.concatenate([r[...] for r in blocks[sb:sb + 2]], axis=0)
        rk = row0 + sb * sub - A_RADIUS + lax.broadcasted_iota(jnp.int32, (1, nk), 1)
        bias = band_ref[...] + jnp.where((rk >= start) & (rk < end), 0.0, -jnp.inf)
        bias_ref[0:sub, :] = bias
        bias_ref[sub:2 * sub, :] = bias
        first = lax.broadcasted_iota(jnp.int32, (2 * sub, 1), 0) < sub
        rows = slice(sb * sub, (sb + 1) * sub)
        for j in range(A_KV_HEADS):
            k = kv[:, j * A_HEAD_DIM:(j + 1) * A_HEAD_DIM]
            v = kv[:, A_KV_WIDTH + j * A_HEAD_DIM:A_KV_WIDTH + (j + 1) * A_HEAD_DIM]
            for gq in range(0, group, 2):
                heads = (j * group + gq, j * group + gq + 1)
                cols = [slice(h * A_HEAD_DIM, (h + 1) * A_HEAD_DIM) for h in heads]
                qh = jnp.concatenate([q_ref[rows, c] for c in cols], axis=0)
                s = lax.dot_general(qh, k, (((1,), (1,)), ((), ())), preferred_element_type=F32) * scale
                s = s + bias_ref[...]
                sk = jnp.where(first, sink_ref[heads[0]], sink_ref[heads[1]])
                m = jnp.maximum(jnp.max(s, axis=-1, keepdims=True), sk)
                p = jnp.exp(s - m)
                den = jnp.sum(p, axis=-1, keepdims=True) + jnp.exp(sk - m)
                o = jnp.dot(p.astype(BF16), v, preferred_element_type=F32) / den
                o_ref[rows, cols[0]] = o[0:sub].astype(o_ref.dtype)
                o_ref[rows, cols[1]] = o[sub:2 * sub].astype(o_ref.dtype)


def _even_attention(q, kv, sink, bounds):
    m = q.shape[0]
    tq = ATT_TQ
    hb = A_RADIUS
    per = tq // hb
    last = m // hb - 1
    kvw = kv.shape[1]
    band = _band_bias(A_RADIUS, A_RADIUS)
    return pl.pallas_call(
        functools.partial(_even_attn_kernel, bounds=bounds),
        grid=(m // tq,),
        in_specs=[pl.BlockSpec(memory_space=pltpu.SMEM),
                  pl.BlockSpec(band.shape, lambda i: (0, 0)),
                  pl.BlockSpec((tq, A_WIDTH), lambda i: (i, 0)),
                  pl.BlockSpec((hb, kvw), lambda i: (jnp.maximum(i * per - 1, 0), 0)),
                  pl.BlockSpec((tq, kvw), lambda i: (i, 0)),
                  pl.BlockSpec((hb, kvw), lambda i: (jnp.minimum((i + 1) * per, last), 0))],
        out_specs=pl.BlockSpec((tq, A_WIDTH), lambda i: (i, 0)),
        out_shape=jax.ShapeDtypeStruct((m, A_WIDTH), BF16),
        scratch_shapes=[pltpu.VMEM((2 * band.shape[0], band.shape[1]), F32)],
        compiler_params=_cparams(("parallel",)),
        name="even_attention",
    )(sink, band, q, kv, kv, kv)


def _odd_attn_kernel(*refs, bounds):
    ng = len(C_DILATIONS)
    band_ref = refs[7 * ng]
    o_ref, oacc, lacc = refs[7 * ng + 1:]
    i = pl.program_id(0)
    chunk = ODD_CHUNK
    qb = 128
    nk = qb + 2 * C_RADIUS
    row0 = i * chunk
    start, end = _seq_bounds(row0, bounds)
    cc = lax.broadcasted_iota(jnp.int32, (1, nk), 1)
    scale = C_HEAD_DIM ** -0.5
    for g, d in enumerate(C_DILATIONS):
        q_ref, kp_ref, km_ref, kn_ref, vp_ref, vm_ref, vn_ref = refs[7 * g:7 * g + 7]
        tg = chunk // d
        t_lo, t_hi, t_c0 = start // d, end // d, row0 // d
        for sb in range(tg // qb):
            lo, hi = qb * sb - C_RADIUS, qb * sb + qb + C_RADIUS
            tk = t_c0 + lo + cc
            col = jnp.where((tk >= t_lo) & (tk < t_hi), 0.0, -jnp.inf)
            for r in range(d):
                def window(p_ref, m_ref, n_ref):
                    parts = []
                    if lo < 0:
                        parts.append(p_ref[0, r, :, :])
                    parts.append(m_ref[0, r, max(lo, 0):min(hi, tg), :])
                    if hi > tg:
                        parts.append(n_ref[0, r, :, :])
                    return parts[0] if len(parts) == 1 else jnp.concatenate(parts, axis=0)

                q = q_ref[0, r, qb * sb:qb * (sb + 1), :]
                k = window(kp_ref, km_ref, kn_ref)
                v = window(vp_ref, vm_ref, vn_ref)
                s = lax.dot_general(q, k, (((1,), (1,)), ((), ())), preferred_element_type=F32) * scale
                s = s + band_ref[...] + col
                m = jnp.max(s, axis=-1, keepdims=True)
                p = jnp.exp(s - m)
                den = jnp.sum(p, axis=-1, keepdims=True)
                o = jnp.dot(p.astype(BF16), v, preferred_element_type=F32) / den
                lse = jnp.broadcast_to(m + jnp.log(den), (qb, LANES))
                if d == 1:
                    rows = pl.ds(qb * sb, qb)
                else:
                    rows = pl.ds(r + d * qb * sb, qb, stride=d)
                oacc[g, rows, :] = o
                lacc[g, rows, :] = lse
    ls = [lacc[g] for g in range(ng)]
    mx = functools.reduce(jnp.maximum, ls)
    ws = [jnp.exp(l - mx) for l in ls]
    tot = functools.reduce(lambda a, b: a + b, ws)
    out = functools.reduce(lambda a, b: a + b, [(ws[g] / tot) * oacc[g] for g in range(ng)])
    o_ref[...] = out.astype(o_ref.dtype)


def _odd_attention(qkv, bounds, m):
    chunk = ODD_CHUNK
    hb = C_RADIUS
    operands, in_specs = [], []
    for (q, k, v), d in zip(qkv, C_DILATIONS):
        tg = chunk // d
        per = tg // hb
        last = m // d // hb - 1
        main = pl.BlockSpec((1, d, tg, LANES), lambda i, h: (h, 0, i, 0))
        prev = pl.BlockSpec((1, d, hb, LANES), lambda i, h, per=per: (h, 0, jnp.maximum(i * per - 1, 0), 0))
        nxt = pl.BlockSpec((1, d, hb, LANES), lambda i, h, per=per, last=last: (h, 0, jnp.minimum((i + 1) * per, last), 0))
        operands += [q, k, k, k, v, v, v]
        in_specs += [main, prev, main, nxt, prev, main, nxt]
    band = _band_bias(128, C_RADIUS)
    operands.append(band)
    in_specs.append(pl.BlockSpec(band.shape, lambda i, h: (0, 0)))
    ng = len(C_DILATIONS)
    return pl.pallas_call(
        functools.partial(_odd_attn_kernel, bounds=bounds),
        grid=(m // chunk, C_HEADS),
        in_specs=in_specs,
        out_specs=pl.BlockSpec((chunk, LANES), lambda i, h: (i, h)),
        out_shape=jax.ShapeDtypeStruct((m, C_HEADS * C_HEAD_DIM), BF16),
        scratch_shapes=[pltpu.VMEM((ng, chunk, LANES), F32), pltpu.VMEM((ng, chunk, LANES), F32)],
        compiler_params=_cparams(("parallel", "arbitrary")),
        name="odd_attention",
    )(*operands)


def _filter_mlp_kernel(z_ref, w1_ref, b1_ref, f1_ref, w2_ref, b2_ref, f2_ref, w3_ref, b3_ref, dl_ref,
                       h_ref, nrm_ref):
    i = pl.program_id(0)
    z = z_ref[...]
    h = jnp.sin(f1_ref[...] * (jnp.dot(z.astype(BF16), w1_ref[...], preferred_element_type=F32) + b1_ref[...]))
    h = jnp.sin(f2_ref[...] * (jnp.dot(h.astype(BF16), w2_ref[...], preferred_element_type=F32) + b2_ref[...]))
    h = jnp.dot(h.astype(BF16), w3_ref[...], preferred_element_type=F32) + b3_ref[...]
    decay = jnp.exp(-z[:, 0:1] * dl_ref[...])
    nrep = h.shape[1] // decay.shape[1]
    h = h * jnp.concatenate([decay] * nrep, axis=1)
    h_ref[...] = h

    @pl.when(i == 0)
    def _():
        nrm_ref[...] = jnp.zeros_like(nrm_ref)

    half = h.shape[1] // 2
    col = lax.broadcasted_iota(jnp.int32, h.shape, 1)
    row = lax.broadcasted_iota(jnp.int32, h.shape, 0) + i * h.shape[0]
    a = jnp.where((col >= half) & (row == 0), 0.0, jnp.abs(h))
    nrm_ref[...] += jnp.sum(a, axis=0, keepdims=True)


def _filter_mlp(z, w1, b1, f1, w2, b2, f2, w3, b3, deltas, tl=512):
    l, e = z.shape
    hid = w1.shape[1]
    n = w3.shape[1]
    c = deltas.shape[0]
    full = lambda shape: pl.BlockSpec(shape, lambda i: (0, 0))
    return pl.pallas_call(
        _filter_mlp_kernel,
        grid=(l // tl,),
        in_specs=[pl.BlockSpec((tl, e), lambda i: (i, 0)),
                  full((e, hid)), full((1, hid)), full((1, hid)),
                  full((hid, hid)), full((1, hid)), full((1, hid)),
                  full((hid, n)), full((1, n)), full((1, c))],
        out_specs=[pl.BlockSpec((tl, n), lambda i: (i, 0)), full((1, n))],
        out_shape=[jax.ShapeDtypeStruct((l, n), F32), jax.ShapeDtypeStruct((1, n), F32)],
        compiler_params=_cparams(("arbitrary",)),
        name="filter_mlp",
    )(z, w1, b1.reshape(1, hid), f1.reshape(1, hid), w2, b2.reshape(1, hid), f2.reshape(1, hid),
      w3, b3.reshape(1, n), deltas.reshape(1, c))


def _split(x):
    hi = x.astype(BF16)
    lo = (x - hi.astype(F32)).astype(BF16)
    return hi, lo


def _dot3(ch, cl, x):
    xh, xl = _split(x)
    r = jnp.dot(ch, xh, preferred_element_type=F32)
    r = r + jnp.dot(ch, xl, preferred_element_type=F32)
    return r + jnp.dot(cl, xh, preferred_element_type=F32)


def _dot1(ch, x):
    return jnp.dot(ch, x.astype(BF16), preferred_element_type=F32)


FFT_GROUP = 8


FFT_N2C = 4
FFT_RH = 24
FFT_SPB = 6


def _fft1_kernel(x_ref, fh_ref, fl_ref, ar_ref, ai_ref, *, kn1, n2):
    kh, rows, _ = ar_ref.shape
    n2q = n2 // FFT_N2C
    rh = rows // n2q
    rp = kh * rh
    c0 = pl.program_id(2) * n2q
    for g in range(n2q // FFT_GROUP):
        cols = [x_ref[pl.ds(c0 + g * FFT_GROUP + s, kn1, stride=n2), :] for s in range(FFT_GROUP)]
        out = _dot3(fh_ref[...], fl_ref[...], jnp.concatenate(cols, axis=1))
        for k in range(kh):
            for s in range(FFT_GROUP):
                r0 = (g * FFT_GROUP + s) * rh
                ar_ref[k, r0:r0 + rh, :] = out[k * rh:(k + 1) * rh, s * LANES:(s + 1) * LANES]
                ai_ref[k, r0:r0 + rh, :] = out[rp + k * rh:rp + (k + 1) * rh, s * LANES:(s + 1) * LANES]


def _fft_stage1(x, f1h, f1l, plan, l, nb, row_blk0, col_blk0, c):
    n2, kh, rh = plan["n2"], plan["kh"], plan["rh"]
    kn1 = f1h.shape[1]
    rows = n2 // FFT_N2C * rh
    out = jax.ShapeDtypeStruct((nb, kh, FFT_N2C, rows, c), F32)
    fspec = pl.BlockSpec(f1h.shape, lambda b, j, q: (0, 0))
    ospec = pl.BlockSpec((None, kh, None, rows, LANES), lambda b, j, q: (b, 0, q, 0, j))
    return pl.pallas_call(
        functools.partial(_fft1_kernel, kn1=kn1, n2=n2),
        grid=(nb, c // LANES, FFT_N2C),
        in_specs=[pl.BlockSpec((l, LANES), lambda b, j, q: (row_blk0 + b, col_blk0 + j)), fspec, fspec],
        out_specs=[ospec, ospec],
        out_shape=[out, out],
        compiler_params=_cparams(("parallel", "parallel", "arbitrary")),
        name="fft_stage1",
    )(x, f1h, f1l)


def _cpair(p, n):
    return p[:n, :LANES] - p[n:, LANES:], p[:n, LANES:] + p[n:, :LANES]


def _twiddle(xr, xi, tr, ti):
    return xr * tr - xi * ti, xr * ti + xi * tr


def _filter_mid_kernel(fr_ref, fi_ref, br_ref, bi_ref, tr_ref, ti_ref, w_ref, inv_ref, b0_ref,
                       kr_ref, ki_ref, *, nslab, rh, n2):
    k0 = pl.program_id(2) * FFT_SPB

    @pl.when(pl.program_id(1) * rh + k0 < nslab)
    def _():
        for t in range(FFT_SPB):
            rows = pl.ds(k0 + t, n2, stride=rh)
            tr, ti = tr_ref[t], ti_ref[t]
            x4 = jnp.concatenate(_twiddle(fr_ref[rows, :], fi_ref[rows, :], tr, ti)
                                 + _twiddle(br_ref[rows, :], bi_ref[rows, :], tr, ti), axis=1)
            p = _dot1(w_ref[...], x4)
            fr, fi = _cpair(p[:, :2 * LANES], n2)
            br, bi = _cpair(p[:, 2 * LANES:], n2)
            kr_ref[t] = (fr + (br - b0_ref[...])) * inv_ref[...]
            ki_ref[t] = (fi - bi) * inv_ref[...]

    @pl.when(pl.program_id(1) * rh + k0 >= nslab)
    def _():
        kr_ref[...] = jnp.zeros_like(kr_ref)
        ki_ref[...] = jnp.zeros_like(ki_ref)


def _filter_mid(ar, ai, inv, b0, plan):
    n2, kh, rh, rp = plan["n2"], plan["kh"], plan["rh"], plan["rp"]
    oc = ar.shape[-1] // 2
    nj = oc // LANES
    fwd = pl.BlockSpec((None, None, n2 * rh, LANES), lambda j, kk, k: (0, kk, 0, j))
    bwd = pl.BlockSpec((None, None, n2 * rh, LANES), lambda j, kk, k: (0, kk, 0, j + nj))
    steps = rh // FFT_SPB
    tspec = pl.BlockSpec((FFT_SPB, n2, LANES), lambda j, kk, k: (kk * steps + k, 0, 0))
    wspec = pl.BlockSpec((2 * n2, n2), lambda j, kk, k: (0, 0))
    vec = pl.BlockSpec((1, LANES), lambda j, kk, k: (0, j))
    ospec = pl.BlockSpec((FFT_SPB, n2, LANES), lambda j, kk, k: (kk * steps + k, 0, j))
    out = jax.ShapeDtypeStruct((rp, n2, oc), F32)
    return pl.pallas_call(
        functools.partial(_filter_mid_kernel, nslab=plan["r"], rh=rh, n2=n2),
        grid=(nj, kh, steps),
        in_specs=[fwd, fwd, bwd, bwd, tspec, tspec, wspec, vec, vec],
        out_specs=[ospec, ospec],
        out_shape=[out, out],
        compiler_params=_cparams(("parallel", "arbitrary", "arbitrary")),
        name="filter_mid",
    )(ar, ai, ar, ai, *plan["tw"], plan["f2"], inv, b0)


def _conv_mid_kernel(ar_ref, ai_ref, tr_ref, ti_ref, wf_ref, wi_ref, kr_ref, ki_ref, gh_ref, gl_ref,
                     y_ref, dr_ref, di_ref, *, nslab, rh, n2):
    k0 = pl.program_id(3) * FFT_SPB

    @pl.when((pl.program_id(2) == 0) & (pl.program_id(3) == 0))
    def _():
        y_ref[...] = jnp.zeros_like(y_ref)

    @pl.when(pl.program_id(2) * rh + k0 < nslab)
    def _():
        for t in range(FFT_SPB):
            rows = pl.ds(k0 + t, n2, stride=rh)
            tr, ti = tr_ref[t], ti_ref[t]
            x2 = jnp.concatenate(_twiddle(ar_ref[rows, :], ai_ref[rows, :], tr, ti), axis=1)
            xr, xi = _cpair(_dot1(wf_ref[...], x2), n2)
            kr, ki = kr_ref[t], ki_ref[t]
            y2 = jnp.concatenate([xr * kr - xi * ki, xr * ki + xi * kr], axis=1)
            cr, ci = _cpair(_dot1(wi_ref[...], y2), n2)
            dr, di = _twiddle(cr, ci, tr, -ti)
            dr_ref[rows, :] = dr
            di_ref[rows, :] = di

    @pl.when(pl.program_id(2) * rh + k0 >= nslab)
    def _():
        for t in range(FFT_SPB):
            rows = pl.ds(k0 + t, n2, stride=rh)
            dr_ref[rows, :] = jnp.zeros((n2, LANES), F32)
            di_ref[rows, :] = jnp.zeros((n2, LANES), F32)

    @pl.when(pl.program_id(3) == pl.num_programs(3) - 1)
    def _():
        n1c = y_ref.shape[0]
        n1r = y_ref.shape[1] // n2
        for g in range(n2 // FFT_GROUP):
            cols = []
            for s in range(FFT_GROUP):
                r0 = (g * FFT_GROUP + s) * rh
                cols.append(jnp.concatenate([dr_ref[r0:r0 + rh, :], di_ref[r0:r0 + rh, :]], axis=0))
            y = _dot3(gh_ref[...], gl_ref[...], jnp.concatenate(cols, axis=1))
            for c in range(n1c):
                for s in range(FFT_GROUP):
                    t0 = (g * FFT_GROUP + s) * n1r
                    y_ref[c, t0:t0 + n1r, :] += y[c * n1r:(c + 1) * n1r, s * LANES:(s + 1) * LANES]


def _conv_mid(ar, ai, kr, ki, order, plan):
    nb, kh, _, c = ar.shape
    n2, rh = plan["n2"], plan["rh"]
    gh, gl = plan["g"]
    nh = gh.shape[1]
    n1r = min(16, nh)
    n1c = nh // n1r
    koff = order * (c // LANES)
    blk = pl.BlockSpec((None, None, n2 * rh, LANES), lambda b, j, kk, k: (b, kk, 0, j))
    steps = rh // FFT_SPB
    tspec = pl.BlockSpec((FFT_SPB, n2, LANES), lambda b, j, kk, k: (kk * steps + k, 0, 0))
    wspec = pl.BlockSpec((2 * n2, n2), lambda b, j, kk, k: (0, 0))
    kspec = pl.BlockSpec((FFT_SPB, n2, LANES), lambda b, j, kk, k: (kk * steps + k, 0, j + koff))
    gspec = pl.BlockSpec((None, nh, 2 * rh), lambda b, j, kk, k: (kk, 0, 0))
    return pl.pallas_call(
        functools.partial(_conv_mid_kernel, nslab=plan["r"], rh=rh, n2=n2),
        grid=(nb, c // LANES, kh, steps),
        in_specs=[blk, blk, tspec, tspec, wspec, wspec, kspec, kspec, gspec, gspec],
        out_specs=pl.BlockSpec((None, n1c, n2 * n1r, LANES), lambda b, j, kk, k: (b, 0, 0, j)),
        out_shape=jax.ShapeDtypeStruct((nb, n1c, n2 * n1r, c), F32),
        scratch_shapes=[pltpu.VMEM((n2 * rh, LANES), F32), pltpu.VMEM((n2 * rh, LANES), F32)],
        compiler_params=_cparams(("parallel", "parallel", "arbitrary", "arbitrary")),
        name="conv_mid",
    )(ar, ai, *plan["tw"], plan["f2"], plan["f2i"], kr, ki, gh, gl)


def _gate_kernel(y_ref, u_ref, g_ref, bias_ref, o_ref, *, n2):
    n1r = y_ref.shape[0] // n2
    for a in range(n1r):
        rows = slice(a * n2, (a + 1) * n2)
        yt = y_ref[pl.ds(a, n2, stride=n1r), :]
        o_ref[rows, :] = (g_ref[rows, :] * (yt + u_ref[rows, :] * bias_ref[...])).astype(o_ref.dtype)


def _gate(y, n2, u, u_row0, u_col0, gate, g_row0, g_col0, bias, out_dtype):
    nb, n1c, yr, c = y.shape
    tr = yr
    return pl.pallas_call(
        functools.partial(_gate_kernel, n2=n2),
        grid=(nb, c // LANES, n1c),
        in_specs=[pl.BlockSpec((None, None, yr, LANES), lambda b, j, q: (b, q, 0, j)),
                  pl.BlockSpec((tr, LANES), lambda b, j, q: (u_row0 // tr + b * n1c + q, u_col0 + j)),
                  pl.BlockSpec((tr, LANES), lambda b, j, q: (g_row0 // tr + b * n1c + q, g_col0 + j)),
                  pl.BlockSpec((1, LANES), lambda b, j, q: (0, j))],
        out_specs=pl.BlockSpec((tr, LANES), lambda b, j, q: (b * n1c + q, j)),
        out_shape=jax.ShapeDtypeStruct((nb * n1c * tr, c), out_dtype),
        compiler_params=_cparams(("parallel", "parallel", "arbitrary")),
        name="hyena_gate",
    )(y, u, gate, bias)


def _np_split(a):
    a32 = np.asarray(a, np.float32)
    hi = a32.astype(BF16)
    lo = (a32 - hi.astype(np.float32)).astype(BF16)
    return hi, lo


def _fft_plan(l):
    n = 2 * l
    n2 = FFT_N2
    n1 = n // n2
    r = n1 // 2 + 1
    rh = FFT_RH
    kh = -(-r // rh)
    rp = kh * rh
    kn1 = n1 // 2
    k1 = np.arange(rp, dtype=np.float64)[:, None]
    live = (k1 < r).astype(np.float64)

    ang = 2 * np.pi * k1 * np.arange(kn1)[None, :] / n1
    f1 = np.concatenate([np.cos(ang) * live, -np.sin(ang) * live], axis=0)

    kk = np.arange(rp, dtype=np.float64)[None, :]
    wgt = np.where((kk == 0) | (kk == n1 // 2), 1.0, 2.0) * (kk < r) / n
    ango = 2 * np.pi * np.arange(n1 // 2)[:, None] * kk / n1
    gre = (np.cos(ango) * wgt).reshape(n1 // 2, kh, rh).transpose(1, 0, 2)
    gim = (-np.sin(ango) * wgt).reshape(n1 // 2, kh, rh).transpose(1, 0, 2)
    g = np.concatenate([gre, gim], axis=2)

    a2 = 2 * np.pi * np.outer(np.arange(n2), np.arange(n2)) / n2
    f2 = np.concatenate([np.cos(a2), -np.sin(a2)], axis=0)
    f2i = np.concatenate([np.cos(a2), np.sin(a2)], axis=0)

    idx = jnp.arange(rp, dtype=jnp.int32)[:, None] * jnp.arange(n2, dtype=jnp.int32)[None, :]
    ang = idx.astype(F32) * F32(2.0 * math.pi / n)
    tw = tuple(jnp.broadcast_to(t[:, :, None], (rp, n2, LANES)) for t in (jnp.cos(ang), -jnp.sin(ang)))
    return dict(n1=n1, n2=n2, r=r, rp=rp, kh=kh, rh=rh, f1=_np_split(f1), g=_np_split(g),
                f2=_np_split(f2)[0], f2i=_np_split(f2i)[0], tw=tw)


def _filter_features(l):
    t = jnp.linspace(0.0, 1.0, l, dtype=F32)[:, None]
    bands = jnp.linspace(1e-4, B_BANDS - 1, B_BANDS, dtype=F32)[None, :]
    w = 2.0 * math.pi * jnp.arange(l, dtype=F32)[:, None] / l
    return jnp.concatenate([t, jnp.cos(bands * w), -jnp.sin(bands * w)], axis=-1)


def _hyena_filters(plan, l, c, w1, b1, f1, w2, b2, f2, w3, b3):
    n1, n2 = plan["n1"], plan["n2"]
    z = _filter_features(l)
    e = z.shape[1]
    ep = -(-e // 16) * 16
    z = jnp.pad(z, ((0, 0), (0, ep - e)))
    w1p = jnp.pad(w1, ((0, ep - e), (0, 0))).astype(BF16)
    max_decay = math.log(B_DECAY_TARGET) / B_FAST_DECAY_PCT
    min_decay = math.log(B_DECAY_TARGET) / B_SLOW_DECAY_PCT
    deltas = jnp.abs(jnp.linspace(min_decay, max_decay, c, dtype=F32))
    h, nrm = _filter_mlp(z, w1p, b1, f1, w2.astype(BF16), b2, f2, w3.astype(BF16), b3, deltas,
                         tl=min(512, l))
    oc = h.shape[1] // 2
    inv = 1.0 / (nrm[:, :oc] + nrm[:, oc:])
    b0 = h[0:1, oc:]
    ar, ai = _fft_stage1(h, *plan["f1"], plan, l, 1, 0, 0, 2 * oc)
    mid_shape = (1, plan["kh"], n2 * plan["rh"], 2 * oc)
    return _filter_mid(ar.reshape(mid_shape), ai.reshape(mid_shape), inv, b0, plan)


def _hyena_conv(plan, kf, order, l, nb, u, u_row0, u_col0, gate, g_row0, g_col0, bias, c, out_dtype):
    n2 = plan["n2"]
    cb = c // LANES
    ar, ai = _fft_stage1(u, *plan["f1"], plan, l, nb, u_row0 // l, u_col0 * cb, c)
    mid_shape = (nb, plan["kh"], n2 * plan["rh"], c)
    y = _conv_mid(ar.reshape(mid_shape), ai.reshape(mid_shape), kf[0], kf[1], order, plan)
    return _gate(y, n2, u, u_row0, u_col0 * cb, gate, g_row0, g_col0 * cb, bias.reshape(1, c), out_dtype)


def _hyena_mixer(u, seqs, plans, c, fw, hy_bias):
    outs = []
    for (row0, nb, l) in seqs:
        plan = plans[l]
        kf = _hyena_filters(plan, l, c, *fw)
        z = _hyena_conv(plan, kf, 0, l, nb, u, row0, 0, u, row0, 1, hy_bias[0], c, F32)
        o = _hyena_conv(plan, kf, 1, l, nb, z, 0, 0, u, row0, 2, hy_bias[1], c, BF16)
        outs.append(o)
    return jnp.concatenate(outs, axis=0)


def _rope_tables(pos, hd):
    rot = hd // ROPE_FRACTION
    half = rot // 2
    inv = ROPE_THETA ** (-(jnp.arange(half, dtype=F32) * 2.0 / rot))
    ang = pos[:, None] * inv[None, :]
    cos, sin = jnp.cos(ang), jnp.sin(ang)
    m = pos.shape[0]
    one = jnp.ones((m, hd - rot), F32)
    zero = jnp.zeros((m, hd - rot), F32)
    zh = jnp.zeros((m, half), F32)
    c = jnp.concatenate([cos, cos, one], axis=1)
    s1 = jnp.concatenate([-sin, zh, zero], axis=1)
    s2 = jnp.concatenate([zh, sin, zero], axis=1)
    rep = LANES // hd
    return tuple(jnp.tile(t, (1, rep)) for t in (c, s1, s2)), half


def _trunk(x, bounds, seqs, p):
    m, dm = x.shape
    pos = jnp.concatenate([jnp.tile(jnp.arange(l, dtype=F32), nb) for (_, nb, l) in seqs])
    tabs_a, half_a = _rope_tables(pos, A_HEAD_DIM)
    tabs_c, half_c = _rope_tables(pos, C_HEAD_DIM)
    ident = (jnp.ones((m, LANES), F32), jnp.zeros((m, LANES), F32), jnp.zeros((m, LANES), F32))
    tabs_kv = tuple(jnp.concatenate([a, b], axis=1) for a, b in zip(tabs_a, ident))
    c_hy = dm - A_WIDTH
    plans = {l: _fft_plan(l) for l in sorted({l for (_, _, l) in seqs})}
    xf = x
    xb = x.astype(BF16)
    for i in range(DEPTH):
        j = i // 2
        if i % 2 == 0:
            w_in = p['mix_e_w_in'][j].astype(BF16)
            kv0 = A_WIDTH
            hy0 = A_WIDTH + 2 * A_KV_WIDTH
            q = _matmul_rope(xb, w_in[:, :kv0], tabs_a, half_a, tn=512)
            kv = _matmul_rope(xb, w_in[:, kv0:hy0], tabs_kv, half_a, tn=2 * A_KV_WIDTH)
            u = _matmul_conv(xb, w_in[:, hy0:], p['hy_conv_w'][j], p['hy_conv_b'][j], bounds)
            a_out = _even_attention(q, kv, p['a_sink'][j], bounds)
            fw = (p['hy_w1'][j], p['hy_b1'][j], p['hy_f1'][j], p['hy_w2'][j], p['hy_b2'][j],
                  p['hy_f2'][j], p['hy_w3'][j], p['hy_b3'][j])
            h_out = _hyena_mixer(u, seqs, plans, c_hy, fw, p['hy_bias'][j])
            w_out = p['mix_e_w_out'][j].astype(BF16)
            xf, xb = _matmul_ln([a_out, h_out], [w_out[:A_WIDTH], w_out[A_WIDTH:]], xf,
                                p['ln1_g'][i], p['ln1_b'][i])
        else:
            w_in = p['mix_o_w_in'][j].astype(BF16)
            gw = C_HEADS * C_HEAD_DIM
            ng = len(C_DILATIONS)
            qkv = []
            for g, d in enumerate(C_DILATIONS):
                trio = []
                for part in range(3):
                    c0 = (part * ng + g) * gw
                    trio.append(_odd_proj(xb, w_in, c0, tabs_c, half_c, d, rope=part < 2))
                qkv.append(tuple(trio))
            o = _odd_attention(qkv, bounds, m)
            xf, xb = _matmul_ln([o], [p['mix_o_w_out'][j].astype(BF16)], xf, p['ln1_g'][i], p['ln1_b'][i])
        xf, xb = _ffn_ln(xb, xf, p['ffn_w_gate'][i].astype(BF16), p['ffn_w_up'][i].astype(BF16),
                         p['ffn_w_down'][i].astype(BF16), p['ln2_g'][i], p['ln2_b'][i])
    return xf


def kernel(x_prompt, x_sample, mix_e_w_in, a_sink, hy_conv_w, hy_conv_b, hy_w1, hy_b1, hy_f1, hy_w2, hy_b2,
           hy_f2, hy_w3, hy_b3, hy_bias, mix_e_w_out, mix_o_w_in, mix_o_w_out, ffn_w_gate, ffn_w_up,
           ffn_w_down, ln1_g, ln1_b, ln2_g, ln2_b):
    p = dict(mix_e_w_in=mix_e_w_in, a_sink=a_sink, hy_conv_w=hy_conv_w, hy_conv_b=hy_conv_b,
             hy_w1=hy_w1, hy_b1=hy_b1, hy_f1=hy_f1, hy_w2=hy_w2, hy_b2=hy_b2, hy_f2=hy_f2,
             hy_w3=hy_w3, hy_b3=hy_b3, hy_bias=hy_bias, mix_e_w_out=mix_e_w_out,
             mix_o_w_in=mix_o_w_in, mix_o_w_out=mix_o_w_out, ffn_w_gate=ffn_w_gate,
             ffn_w_up=ffn_w_up, ffn_w_down=ffn_w_down, ln1_g=ln1_g, ln1_b=ln1_b,
             ln2_g=ln2_g, ln2_b=ln2_b)
    dm = x_prompt.shape[-1]
    seqs, bounds, row = [], [0], 0
    for xs in (x_prompt, x_sample):
        nb, l = xs.shape[0], xs.shape[1]
        seqs.append((row, nb, l))
        for _ in range(nb):
            row += l
            bounds.append(row)
    x = jnp.concatenate([x_prompt.reshape(-1, dm), x_sample.reshape(-1, dm)], axis=0)
    y = _trunk(x, tuple(bounds), tuple(seqs), p)
    n_p = x_prompt.shape[0] * x_prompt.shape[1]
    return (y[:n_p].reshape(x_prompt.shape), y[n_p:].reshape(x_sample.shape))
```

```python
import functools
import math

import numpy as np
import jax
import jax.numpy as jnp
from jax import lax
from jax.experimental import pallas as pl
from jax.experimental.pallas import tpu as pltpu

F32 = jnp.float32
BF16 = jnp.bfloat16

DEPTH = 4
A_HEADS, A_KV_HEADS, A_HEAD_DIM, A_RADIUS = 16, 2, 64, 128
A_WIDTH = A_HEADS * A_HEAD_DIM
A_KV_WIDTH = A_KV_HEADS * A_HEAD_DIM
B_SHORT, B_EMB = 3, 33
B_BANDS = (B_EMB - 1) // 2
B_DECAY_TARGET, B_FAST_DECAY_PCT, B_SLOW_DECAY_PCT = 1e-2, 0.3, 1.5
C_HEADS, C_HEAD_DIM = 16, 128
C_DILATIONS = (1, 4, 16)
C_RADIUS = 64
ROPE_THETA, ROPE_FRACTION = 500000.0, 4
ALPHA = (2 * DEPTH) ** 0.25
LN_EPS = 1e-5

LANES = 128
VMEM_LIMIT = 56 * 1024 * 1024
FFT_N2 = 256

ODD_CHUNK = 2048
ATT_TQ = 256


def _cparams(sem):
    return pltpu.CompilerParams(dimension_semantics=sem, vmem_limit_bytes=VMEM_LIMIT)


def _seq_bounds(row, bounds):
    start = jnp.int32(bounds[0])
    end = jnp.int32(bounds[1])
    for b0, b1 in zip(bounds[1:-1], bounds[2:]):
        inside = row >= b0
        start = jnp.where(inside, jnp.int32(b0), start)
        end = jnp.where(inside, jnp.int32(b1), end)
    return start, end


def _rope(a, c, s1, s2, half):
    w = a.shape[-1]
    return a * c + pltpu.roll(a, w - half, 1) * s1 + pltpu.roll(a, half, 1) * s2


def _mm_conv_kernel(xp_ref, xm_ref, xn_ref, w_ref, cw_ref, cb_ref, o_ref, *, bounds):
    i = pl.program_id(0)
    tm, hb = xm_ref.shape[0], xp_ref.shape[0]
    rows = tm + 2 * hb
    row0 = i * tm
    start, end = _seq_bounds(row0, bounds)
    lhs = jnp.concatenate([xp_ref[...], xm_ref[...], xn_ref[...]], axis=0)
    acc = jnp.dot(lhs, w_ref[...], preferred_element_type=F32)
    h0 = acc[hb:hb + tm]
    hm = pltpu.roll(acc, 1, 0)[hb:hb + tm]
    hp = pltpu.roll(acc, rows - 1, 0)[hb:hb + tm]
    ridx = lax.broadcasted_iota(jnp.int32, h0.shape, 0)
    hm = jnp.where((ridx == 0) & (row0 <= start), 0.0, hm)
    hp = jnp.where((ridx == tm - 1) & (row0 + tm >= end), 0.0, hp)
    y = cb_ref[...] + hm * cw_ref[0:1, :]
    y = y + h0 * cw_ref[1:2, :]
    y = y + hp * cw_ref[2:3, :]
    o_ref[...] = y


def _matmul_conv(x, w, cw, cb, bounds, tn=512, tm=1024):
    m, k = x.shape
    n = w.shape[1]
    hb = 16
    per = tm // hb
    last = m // hb - 1
    return pl.pallas_call(
        functools.partial(_mm_conv_kernel, bounds=bounds),
        grid=(m // tm, n // tn),
        in_specs=[pl.BlockSpec((hb, k), lambda i, j: (jnp.maximum(i * per - 1, 0), 0)),
                  pl.BlockSpec((tm, k), lambda i, j: (i, 0)),
                  pl.BlockSpec((hb, k), lambda i, j: (jnp.minimum((i + 1) * per, last), 0)),
                  pl.BlockSpec((k, tn), lambda i, j: (0, j)),
                  pl.BlockSpec((B_SHORT, tn), lambda i, j: (0, j)),
                  pl.BlockSpec((1, tn), lambda i, j: (0, j))],
        out_specs=pl.BlockSpec((tm, tn), lambda i, j: (i, j)),
        out_shape=jax.ShapeDtypeStruct((m, n), F32),
        compiler_params=_cparams(("parallel", "arbitrary")),
        name="matmul_conv",
    )(x, x, x, w, cw, cb.reshape(1, n))


def _mm_rope_kernel(x_ref, w_ref, c_ref, s1_ref, s2_ref, o_ref, *, half):
    acc = jnp.dot(x_ref[...], w_ref[...], preferred_element_type=F32)
    tw = c_ref.shape[1]
    rc = 256
    for r0 in range(0, acc.shape[0], rc):
        rows = slice(r0, r0 + rc)
        for c in range(acc.shape[1] // tw):
            cols = slice(c * tw, (c + 1) * tw)
            o_ref[rows, cols] = _rope(acc[rows, cols], c_ref[rows, :], s1_ref[rows, :], s2_ref[rows, :],
                                      half).astype(o_ref.dtype)


def _matmul_rope(x, w, tabs, half, tn, tm=1024):
    m, k = x.shape
    n = w.shape[1]
    tw = tabs[0].shape[1]
    tab_spec = pl.BlockSpec((tm, tw), lambda i, j: (i, 0))
    return pl.pallas_call(
        functools.partial(_mm_rope_kernel, half=half),
        grid=(m // tm, n // tn),
        in_specs=[pl.BlockSpec((tm, k), lambda i, j: (i, 0)),
                  pl.BlockSpec((k, tn), lambda i, j: (0, j)),
                  tab_spec, tab_spec, tab_spec],
        out_specs=pl.BlockSpec((tm, tn), lambda i, j: (i, j)),
        out_shape=jax.ShapeDtypeStruct((m, n), BF16),
        compiler_params=_cparams(("parallel", "arbitrary")),
        name="matmul_rope",
    )(x, w, *tabs)


def _odd_proj_kernel(x_ref, w_ref, c_ref, s1_ref, s2_ref, o_ref, acc_ref, tmp_ref, *, d, rope, half):
    hps = w_ref.shape[1] // LANES
    tm = x_ref.shape[0]
    t = tm // d
    pair = 2
    rc = 256
    for p in range(hps // pair):
        acc = jnp.dot(x_ref[...], w_ref[:, p * pair * LANES:(p + 1) * pair * LANES], preferred_element_type=F32)
        for h2 in range(pair):
            hh = p * pair + h2
            slot = (p % 2) * pair + h2
            acc_ref[slot, :, :] = acc[:, h2 * LANES:(h2 + 1) * LANES]
            for c0 in range(0, tm, rc):
                a = acc_ref[slot, c0:c0 + rc, :]
                if rope:
                    a = _rope(a, c_ref[c0:c0 + rc, :], s1_ref[c0:c0 + rc, :], s2_ref[c0:c0 + rc, :], half)
                if d == 1:
                    o_ref[hh, 0, c0:c0 + rc, :] = a.astype(BF16)
                elif rope:
                    acc_ref[slot, c0:c0 + rc, :] = a
            if d == 16:
                ts = hh % 2
                for ra in range(4):
                    tmp_ref[ts, ra * (tm // 4):(ra + 1) * (tm // 4), :] = acc_ref[slot, pl.ds(ra, tm // 4, stride=4), :]
                for ra in range(4):
                    for rb in range(4):
                        o_ref[hh, ra + 4 * rb, :, :] = tmp_ref[ts, pl.ds(ra * (tm // 4) + rb, t, stride=4), :].astype(BF16)
            elif d > 1:
                for r in range(d):
                    o_ref[hh, r, :, :] = acc_ref[slot, pl.ds(r, t, stride=d), :].astype(BF16)


def _odd_proj(x, w, col0, tabs, half, d, rope, hps=8):
    m, k = x.shape
    tm = ODD_CHUNK
    cb0 = col0 // (hps * LANES)
    tab_spec = pl.BlockSpec((tm, LANES), lambda i, j: (i, 0))
    return pl.pallas_call(
        functools.partial(_odd_proj_kernel, d=d, rope=rope, half=half),
        grid=(m // tm, C_HEADS // hps),
        in_specs=[pl.BlockSpec((tm, k), lambda i, j: (i, 0)),
                  pl.BlockSpec((k, hps * LANES), lambda i, j: (0, cb0 + j)),
                  tab_spec, tab_spec, tab_spec],
        out_specs=pl.BlockSpec((hps, d, tm // d, LANES), lambda i, j: (j, 0, i, 0)),
        out_shape=jax.ShapeDtypeStruct((C_HEADS, d, m // d, LANES), BF16),
        scratch_shapes=[pltpu.VMEM((4, tm, LANES), F32), pltpu.VMEM((2, tm, LANES), F32)],
        compiler_params=_cparams(("parallel", "arbitrary")),
        name="odd_proj",
    )(x, w, *tabs)


LN_ROWS = 128


def _layer_norm_store(x_ref, acc_ref, g_ref, b_ref, of_ref, ob_ref, row0=0, nrows=None):
    nrows = x_ref.shape[0] if nrows is None else nrows
    for c in range(nrows // LN_ROWS):
        rows = pl.ds(row0 + c * LN_ROWS, LN_ROWS)
        r = ALPHA * x_ref[rows, :] + acc_ref[rows, :]
        mu = jnp.mean(r, axis=-1, keepdims=True)
        xc = r - mu
        var = jnp.mean(xc * xc, axis=-1, keepdims=True)
        y = xc * lax.rsqrt(var + LN_EPS) * g_ref[...] + b_ref[...]
        of_ref[rows, :] = y
        ob_ref[rows, :] = y.astype(BF16)


def _mm_ln_kernel(*refs, n_in):
    ys = refs[:n_in]
    ws = refs[n_in:2 * n_in]
    x_ref, g_ref, b_ref, of_ref, ob_ref, acc_ref = refs[2 * n_in:]
    half = x_ref.shape[0] // 2
    for r0 in (0, half):
        rows = slice(r0, r0 + half)
        acc = jnp.dot(ys[0][rows, :], ws[0][...], preferred_element_type=F32)
        for y_ref, w_ref in zip(ys[1:], ws[1:]):
            acc = acc + jnp.dot(y_ref[rows, :], w_ref[...], preferred_element_type=F32)
        acc_ref[rows, :] = acc
        _layer_norm_store(x_ref, acc_ref, g_ref, b_ref, of_ref, ob_ref, r0, half)


def _matmul_ln(ys, ws, x, g, b, tm=512):
    m, dm = x.shape
    n_in = len(ys)
    in_specs = [pl.BlockSpec((tm, y.shape[1]), lambda i: (i, 0)) for y in ys]
    in_specs += [pl.BlockSpec(w.shape, lambda i: (0, 0)) for w in ws]
    in_specs += [pl.BlockSpec((tm, dm), lambda i: (i, 0)),
                 pl.BlockSpec((1, dm), lambda i: (0, 0)),
                 pl.BlockSpec((1, dm), lambda i: (0, 0))]
    return pl.pallas_call(
        functools.partial(_mm_ln_kernel, n_in=n_in),
        grid=(m // tm,),
        in_specs=in_specs,
        out_specs=[pl.BlockSpec((tm, dm), lambda i: (i, 0)), pl.BlockSpec((tm, dm), lambda i: (i, 0))],
        out_shape=[jax.ShapeDtypeStruct((m, dm), F32), jax.ShapeDtypeStruct((m, dm), BF16)],
        scratch_shapes=[pltpu.VMEM((tm, dm), F32)],
        compiler_params=_cparams(("parallel",)),
        name="matmul_ln",
    )(*ys, *ws, x, g.reshape(1, dm), b.reshape(1, dm))


def _ffn_kernel(xb_ref, xf_ref, wg_ref, wu_ref, wd_ref, g_ref, b_ref, of_ref, ob_ref, acc_ref):
    j = pl.program_id(1)

    @pl.when(j == 0)
    def _():
        acc_ref[...] = jnp.zeros_like(acc_ref)

    xb = xb_ref[...]
    gate = jnp.dot(xb, wg_ref[...], preferred_element_type=F32)
    up = jnp.dot(xb, wu_ref[...], preferred_element_type=F32)
    h = (gate * jax.nn.sigmoid(gate)) * up
    acc_ref[...] += jnp.dot(h.astype(BF16), wd_ref[...], preferred_element_type=F32)

    @pl.when(j == pl.num_programs(1) - 1)
    def _():
        _layer_norm_store(xf_ref, acc_ref, g_ref, b_ref, of_ref, ob_ref)


def _ffn_ln(xb, xf, wg, wu, wd, g, b, tm=512, tf=512):
    m, dm = xf.shape
    dff = wg.shape[1]
    row = lambda i, j: (i, 0)
    return pl.pallas_call(
        _ffn_kernel,
        grid=(m // tm, dff // tf),
        in_specs=[pl.BlockSpec((tm, dm), row), pl.BlockSpec((tm, dm), row),
                  pl.BlockSpec((dm, tf), lambda i, j: (0, j)),
                  pl.BlockSpec((dm, tf), lambda i, j: (0, j)),
                  pl.BlockSpec((tf, dm), lambda i, j: (j, 0)),
                  pl.BlockSpec((1, dm), lambda i, j: (0, 0)),
                  pl.BlockSpec((1, dm), lambda i, j: (0, 0))],
        out_specs=[pl.BlockSpec((tm, dm), row), pl.BlockSpec((tm, dm), row)],
        out_shape=[jax.ShapeDtypeStruct((m, dm), F32), jax.ShapeDtypeStruct((m, dm), BF16)],
        scratch_shapes=[pltpu.VMEM((tm, dm), F32)],
        compiler_params=_cparams(("parallel", "arbitrary")),
        name="ffn_ln",
    )(xb, xf, wg, wu, wd, g.reshape(1, dm), b.reshape(1, dm))


def _band_bias(nq, radius):
    r = np.arange(nq)[:, None]
    c = np.arange(nq + 2 * radius)[None, :]
    return np.where(np.abs(c - radius - r) <= radius, 0.0, -np.inf).astype(np.float32)


def _even_attn_kernel(sink_ref, band_ref, q_ref, kp_ref, km_ref, kn_ref, o_ref, bias_ref, *, bounds):
    i = pl.program_id(0)
    tq = q_ref.shape[0]
    sub = A_RADIUS
    nk = sub + 2 * A_RADIUS
    row0 = i * tq
    start, end = _seq_bounds(row0, bounds)
    group = A_HEADS // A_KV_HEADS
    scale = A_HEAD_DIM ** -0.5
    blocks = [kp_ref, km_ref, kn_ref]
    assert tq == 2 * sub
    for sb in range(tq // sub):
        kv = jnp.concatenate([r[...] for r in blocks[sb:sb + 2]], axis=0)
        rk = row0 + sb * sub - A_RADIUS + lax.broadcasted_iota(jnp.int32, (1, nk), 1)
        bias = band_ref[...] + jnp.where((rk >= start) & (rk < end), 0.0, -jnp.inf)
        bias_ref[0:sub, :] = bias
        bias_ref[sub:2 * sub, :] = bias
        first = lax.broadcasted_iota(jnp.int32, (2 * sub, 1), 0) < sub
        rows = slice(sb * sub, (sb + 1) * sub)
        for j in range(A_KV_HEADS):
            k = kv[:, j * A_HEAD_DIM:(j + 1) * A_HEAD_DIM]
            v = kv[:, A_KV_WIDTH + j * A_HEAD_DIM:A_KV_WIDTH + (j + 1) * A_HEAD_DIM]
            for gq in range(0, group, 2):
                heads = (j * group + gq, j * group + gq + 1)
                cols = [slice(h * A_HEAD_DIM, (h + 1) * A_HEAD_DIM) for h in heads]
                qh = jnp.concatenate([q_ref[rows, c] for c in cols], axis=0)
                s = lax.dot_general(qh, k, (((1,), (1,)), ((), ())), preferred_element_type=F32) * scale
                s = s + bias_ref[...]
                sk = jnp.where(first, sink_ref[heads[0]], sink_ref[heads[1]])
                m = jnp.maximum(jnp.max(s, axis=-1, keepdims=True), sk)
                p = jnp.exp(s - m)
                den = jnp.sum(p, axis=-1, keepdims=True) + jnp.exp(sk - m)
                o = jnp.dot(p.astype(BF16), v, preferred_element_type=F32) / den
                o_ref[rows, cols[0]] = o[0:sub].astype(o_ref.dtype)
                o_ref[rows, cols[1]] = o[sub:2 * sub].astype(o_ref.dtype)


def _even_attention(q, kv, sink, bounds):
    m = q.shape[0]
    tq = ATT_TQ
    hb = A_RADIUS
    per = tq // hb
    last = m // hb - 1
    kvw = kv.shape[1]
    band = _band_bias(A_RADIUS, A_RADIUS)
    return pl.pallas_call(
        functools.partial(_even_attn_kernel, bounds=bounds),
        grid=(m // tq,),
        in_specs=[pl.BlockSpec(memory_space=pltpu.SMEM),
                  pl.BlockSpec(band.shape, lambda i: (0, 0)),
                  pl.BlockSpec((tq, A_WIDTH), lambda i: (i, 0)),
                  pl.BlockSpec((hb, kvw), lambda i: (jnp.maximum(i * per - 1, 0), 0)),
                  pl.BlockSpec((tq, kvw), lambda i: (i, 0)),
                  pl.BlockSpec((hb, kvw), lambda i: (jnp.minimum((i + 1) * per, last), 0))],
        out_specs=pl.BlockSpec((tq, A_WIDTH), lambda i: (i, 0)),
        out_shape=jax.ShapeDtypeStruct((m, A_WIDTH), BF16),
        scratch_shapes=[pltpu.VMEM((2 * band.shape[0], band.shape[1]), F32)],
        compiler_params=_cparams(("parallel",)),
        name="even_attention",
    )(sink, band, q, kv, kv, kv)


def _odd_attn_kernel(*refs, bounds):
    ng = len(C_DILATIONS)
    band_ref = refs[7 * ng]
    o_ref, oacc, lacc = refs[7 * ng + 1:]
    i = pl.program_id(0)
    chunk = ODD_CHUNK
    qb = 128
    nk = qb + 2 * C_RADIUS
    row0 = i * chunk
    start, end = _seq_bounds(row0, bounds)
    cc = lax.broadcasted_iota(jnp.int32, (1, nk), 1)
    scale = C_HEAD_DIM ** -0.5
    for g, d in enumerate(C_DILATIONS):
        q_ref, kp_ref, km_ref, kn_ref, vp_ref, vm_ref, vn_ref = refs[7 * g:7 * g + 7]
        tg = chunk // d
        t_lo, t_hi, t_c0 = start // d, end // d, row0 // d
        for sb in range(tg // qb):
            lo, hi = qb * sb - C_RADIUS, qb * sb + qb + C_RADIUS
            tk = t_c0 + lo + cc
            col = jnp.where((tk >= t_lo) & (tk < t_hi), 0.0, -jnp.inf)
            for r in range(d):
                def window(p_ref, m_ref, n_ref):
                    parts = []
                    if lo < 0:
                        parts.append(p_ref[0, r, :, :])
                    parts.append(m_ref[0, r, max(lo, 0):min(hi, tg), :])
                    if hi > tg:
                        parts.append(n_ref[0, r, :, :])
                    return parts[0] if len(parts) == 1 else jnp.concatenate(parts, axis=0)

                q = q_ref[0, r, qb * sb:qb * (sb + 1), :]
                k = window(kp_ref, km_ref, kn_ref)
                v = window(vp_ref, vm_ref, vn_ref)
                s = lax.dot_general(q, k, (((1,), (1,)), ((), ())), preferred_element_type=F32) * scale
                s = s + band_ref[...] + col
                m = jnp.max(s, axis=-1, keepdims=True)
                p = jnp.exp(s - m)
                den = jnp.sum(p, axis=-1, keepdims=True)
                o = jnp.dot(p.astype(BF16), v, preferred_element_type=F32) / den
                lse = jnp.broadcast_to(m + jnp.log(den), (qb, LANES))
                if d == 1:
                    rows = pl.ds(qb * sb, qb)
                else:
                    rows = pl.ds(r + d * qb * sb, qb, stride=d)
                oacc[g, rows, :] = o
                lacc[g, rows, :] = lse
    ls = [lacc[g] for g in range(ng)]
    mx = functools.reduce(jnp.maximum, ls)
    ws = [jnp.exp(l - mx) for l in ls]
    tot = functools.reduce(lambda a, b: a + b, ws)
    out = functools.reduce(lambda a, b: a + b, [(ws[g] / tot) * oacc[g] for g in range(ng)])
    o_ref[...] = out.astype(o_ref.dtype)


def _odd_attention(qkv, bounds, m):
    chunk = ODD_CHUNK
    hb = C_RADIUS
    operands, in_specs = [], []
    for (q, k, v), d in zip(qkv, C_DILATIONS):
        tg = chunk // d
        per = tg // hb
        last = m // d // hb - 1
        main = pl.BlockSpec((1, d, tg, LANES), lambda i, h: (h, 0, i, 0))
        prev = pl.BlockSpec((1, d, hb, LANES), lambda i, h, per=per: (h, 0, jnp.maximum(i * per - 1, 0), 0))
        nxt = pl.BlockSpec((1, d, hb, LANES), lambda i, h, per=per, last=last: (h, 0, jnp.minimum((i + 1) * per, last), 0))
        operands += [q, k, k, k, v, v, v]
        in_specs += [main, prev, main, nxt, prev, main, nxt]
    band = _band_bias(128, C_RADIUS)
    operands.append(band)
    in_specs.append(pl.BlockSpec(band.shape, lambda i, h: (0, 0)))
    ng = len(C_DILATIONS)
    return pl.pallas_call(
        functools.partial(_odd_attn_kernel, bounds=bounds),
        grid=(m // chunk, C_HEADS),
        in_specs=in_specs,
        out_specs=pl.BlockSpec((chunk, LANES), lambda i, h: (i, h)),
        out_shape=jax.ShapeDtypeStruct((m, C_HEADS * C_HEAD_DIM), BF16),
        scratch_shapes=[pltpu.VMEM((ng, chunk, LANES), F32), pltpu.VMEM((ng, chunk, LANES), F32)],
        compiler_params=_cparams(("parallel", "arbitrary")),
        name="odd_attention",
    )(*operands)


def _filter_mlp_kernel(z_ref, w1_ref, b1_ref, f1_ref, w2_ref, b2_ref, f2_ref, w3_ref, b3_ref, dl_ref,
                       h_ref, nrm_ref):
    i = pl.program_id(0)
    z = z_ref[...]
    h = jnp.sin(f1_ref[...] * (jnp.dot(z.astype(BF16), w1_ref[...], preferred_element_type=F32) + b1_ref[...]))
    h = jnp.sin(f2_ref[...] * (jnp.dot(h.astype(BF16), w2_ref[...], preferred_element_type=F32) + b2_ref[...]))
    h = jnp.dot(h.astype(BF16), w3_ref[...], preferred_element_type=F32) + b3_ref[...]
    decay = jnp.exp(-z[:, 0:1] * dl_ref[...])
    nrep = h.shape[1] // decay.shape[1]
    h = h * jnp.concatenate([decay] * nrep, axis=1)
    h_ref[...] = h

    @pl.when(i == 0)
    def _():
        nrm_ref[...] = jnp.zeros_like(nrm_ref)

    half = h.shape[1] // 2
    col = lax.broadcasted_iota(jnp.int32, h.shape, 1)
    row = lax.broadcasted_iota(jnp.int32, h.shape, 0) + i * h.shape[0]
    a = jnp.where((col >= half) & (row == 0), 0.0, jnp.abs(h))
    nrm_ref[...] += jnp.sum(a, axis=0, keepdims=True)


def _filter_mlp(z, w1, b1, f1, w2, b2, f2, w3, b3, deltas, tl=512):
    l, e = z.shape
    hid = w1.shape[1]
    n = w3.shape[1]
    c = deltas.shape[0]
    full = lambda shape: pl.BlockSpec(shape, lambda i: (0, 0))
    return pl.pallas_call(
        _filter_mlp_kernel,
        grid=(l // tl,),
        in_specs=[pl.BlockSpec((tl, e), lambda i: (i, 0)),
                  full((e, hid)), full((1, hid)), full((1, hid)),
                  full((hid, hid)), full((1, hid)), full((1, hid)),
                  full((hid, n)), full((1, n)), full((1, c))],
        out_specs=[pl.BlockSpec((tl, n), lambda i: (i, 0)), full((1, n))],
        out_shape=[jax.ShapeDtypeStruct((l, n), F32), jax.ShapeDtypeStruct((1, n), F32)],
        compiler_params=_cparams(("arbitrary",)),
        name="filter_mlp",
    )(z, w1, b1.reshape(1, hid), f1.reshape(1, hid), w2, b2.reshape(1, hid), f2.reshape(1, hid),
      w3, b3.reshape(1, n), deltas.reshape(1, c))


def _split(x):
    hi = x.astype(BF16)
    lo = (x - hi.astype(F32)).astype(BF16)
    return hi, lo


def _dot3(ch, cl, x):
    xh, xl = _split(x)
    r = jnp.dot(ch, xh, preferred_element_type=F32)
    r = r + jnp.dot(ch, xl, preferred_element_type=F32)
    return r + jnp.dot(cl, xh, preferred_element_type=F32)


def _dot1(ch, x):
    return jnp.dot(ch, x.astype(BF16), preferred_element_type=F32)


FFT_GROUP = 8


FFT_N2C = 4
FFT_RH = 24
FFT_SPB = 6


def _fft1_kernel(x_ref, fh_ref, fl_ref, ar_ref, ai_ref, *, kn1, n2):
    kh, rows, _ = ar_ref.shape
    n2q = n2 // FFT_N2C
    rh = rows // n2q
    rp = kh * rh
    c0 = pl.program_id(2) * n2q
    for g in range(n2q // FFT_GROUP):
        cols = [x_ref[pl.ds(c0 + g * FFT_GROUP + s, kn1, stride=n2), :] for s in range(FFT_GROUP)]
        out = _dot3(fh_ref[...], fl_ref[...], jnp.concatenate(cols, axis=1))
        for k in range(kh):
            for s in range(FFT_GROUP):
                r0 = (g * FFT_GROUP + s) * rh
                ar_ref[k, r0:r0 + rh, :] = out[k * rh:(k + 1) * rh, s * LANES:(s + 1) * LANES]
                ai_ref[k, r0:r0 + rh, :] = out[rp + k * rh:rp + (k + 1) * rh, s * LANES:(s + 1) * LANES]


def _fft_stage1(x, f1h, f1l, plan, l, nb, row_blk0, col_blk0, c):
    n2, kh, rh = plan["n2"], plan["kh"], plan["rh"]
    kn1 = f1h.shape[1]
    rows = n2 // FFT_N2C * rh
    out = jax.ShapeDtypeStruct((nb, kh, FFT_N2C, rows, c), F32)
    fspec = pl.BlockSpec(f1h.shape, lambda b, j, q: (0, 0))
    ospec = pl.BlockSpec((None, kh, None, rows, LANES), lambda b, j, q: (b, 0, q, 0, j))
    return pl.pallas_call(
        functools.partial(_fft1_kernel, kn1=kn1, n2=n2),
        grid=(nb, c // LANES, FFT_N2C),
        in_specs=[pl.BlockSpec((l, LANES), lambda b, j, q: (row_blk0 + b, col_blk0 + j)), fspec, fspec],
        out_specs=[ospec, ospec],
        out_shape=[out, out],
        compiler_params=_cparams(("parallel", "parallel", "arbitrary")),
        name="fft_stage1",
    )(x, f1h, f1l)


def _cpair(p, n):
    return p[:n, :LANES] - p[n:, LANES:], p[:n, LANES:] + p[n:, :LANES]


def _twiddle(xr, xi, tr, ti):
    return xr * tr - xi * ti, xr * ti + xi * tr


def _filter_mid_kernel(fr_ref, fi_ref, br_ref, bi_ref, tr_ref, ti_ref, w_ref, inv_ref, b0_ref,
                       kr_ref, ki_ref, *, nslab, rh, n2):
    k0 = pl.program_id(2) * FFT_SPB

    @pl.when(pl.program_id(1) * rh + k0 < nslab)
    def _():
        for t in range(FFT_SPB):
            rows = pl.ds(k0 + t, n2, stride=rh)
            tr, ti = tr_ref[t], ti_ref[t]
            x4 = jnp.concatenate(_twiddle(fr_ref[rows, :], fi_ref[rows, :], tr, ti)
                                 + _twiddle(br_ref[rows, :], bi_ref[rows, :], tr, ti), axis=1)
            p = _dot1(w_ref[...], x4)
            fr, fi = _cpair(p[:, :2 * LANES], n2)
            br, bi = _cpair(p[:, 2 * LANES:], n2)
            kr_ref[t] = (fr + (br - b0_ref[...])) * inv_ref[...]
            ki_ref[t] = (fi - bi) * inv_ref[...]

    @pl.when(pl.program_id(1) * rh + k0 >= nslab)
    def _():
        kr_ref[...] = jnp.zeros_like(kr_ref)
        ki_ref[...] = jnp.zeros_like(ki_ref)


def _filter_mid(ar, ai, inv, b0, plan):
    n2, kh, rh, rp = plan["n2"], plan["kh"], plan["rh"], plan["rp"]
    oc = ar.shape[-1] // 2
    nj = oc // LANES
    fwd = pl.BlockSpec((None, None, n2 * rh, LANES), lambda j, kk, k: (0, kk, 0, j))
    bwd = pl.BlockSpec((None, None, n2 * rh, LANES), lambda j, kk, k: (0, kk, 0, j + nj))
    steps = rh // FFT_SPB
    tspec = pl.BlockSpec((FFT_SPB, n2, LANES), lambda j, kk, k: (kk * steps + k, 0, 0))
    wspec = pl.BlockSpec((2 * n2, n2), lambda j, kk, k: (0, 0))
    vec = pl.BlockSpec((1, LANES), lambda j, kk, k: (0, j))
    ospec = pl.BlockSpec((FFT_SPB, n2, LANES), lambda j, kk, k: (kk * steps + k, 0, j))
    out = jax.ShapeDtypeStruct((rp, n2, oc), F32)
    return pl.pallas_call(
        functools.partial(_filter_mid_kernel, nslab=plan["r"], rh=rh, n2=n2),
        grid=(nj, kh, steps),
        in_specs=[fwd, fwd, bwd, bwd, tspec, tspec, wspec, vec, vec],
        out_specs=[ospec, ospec],
        out_shape=[out, out],
        compiler_params=_cparams(("parallel", "arbitrary", "arbitrary")),
        name="filter_mid",
    )(ar, ai, ar, ai, *plan["tw"], plan["f2"], inv, b0)


def _conv_mid_kernel(x_ref, fh_ref, fl_ref, tr_ref, ti_ref, wf_ref, wi_ref, kr_ref, ki_ref, gh_ref, gl_ref,
                     y_ref, ar_ref, ai_ref, dr_ref, di_ref, *, nslab, rh, n2):
    k0 = pl.program_id(3) * FFT_SPB

    @pl.when((pl.program_id(2) == 0) & (pl.program_id(3) == 0))
    def _():
        y_ref[...] = jnp.zeros_like(y_ref)

    @pl.when(pl.program_id(3) == 0)
    def _():
        kn1 = fh_ref.shape[1]
        for g in range(n2 // FFT_GROUP):
            cols = [x_ref[pl.ds(g * FFT_GROUP + s, kn1, stride=n2), :] for s in range(FFT_GROUP)]
            out = _dot3(fh_ref[...], fl_ref[...], jnp.concatenate(cols, axis=1))
            for s in range(FFT_GROUP):
                r0 = (g * FFT_GROUP + s) * rh
                ar_ref[r0:r0 + rh, :] = out[:rh, s * LANES:(s + 1) * LANES]
                ai_ref[r0:r0 + rh, :] = out[rh:, s * LANES:(s + 1) * LANES]

    @pl.when(pl.program_id(2) * rh + k0 < nslab)
    def _():
        for t in range(FFT_SPB):
            rows = pl.ds(k0 + t, n2, stride=rh)
            tr, ti = tr_ref[t], ti_ref[t]
            x2 = jnp.concatenate(_twiddle(ar_ref[rows, :], ai_ref[rows, :], tr, ti), axis=1)
            xr, xi = _cpair(_dot1(wf_ref[...], x2), n2)
            kr, ki = kr_ref[t], ki_ref[t]
            y2 = jnp.concatenate([xr * kr - xi * ki, xr * ki + xi * kr], axis=1)
            cr, ci = _cpair(_dot1(wi_ref[...], y2), n2)
            dr, di = _twiddle(cr, ci, tr, -ti)
            dr_ref[rows, :] = dr
            di_ref[rows, :] = di

    @pl.when(pl.program_id(2) * rh + k0 >= nslab)
    def _():
        for t in range(FFT_SPB):
            rows = pl.ds(k0 + t, n2, stride=rh)
            dr_ref[rows, :] = jnp.zeros((n2, LANES), F32)
            di_ref[rows, :] = jnp.zeros((n2, LANES), F32)

    @pl.when(pl.program_id(3) == pl.num_programs(3) - 1)
    def _():
        n1c = y_ref.shape[0]
        n1r = y_ref.shape[1] // n2
        for g in range(n2 // FFT_GROUP):
            cols = []
            for s in range(FFT_GROUP):
                r0 = (g * FFT_GROUP + s) * rh
                cols.append(jnp.concatenate([dr_ref[r0:r0 + rh, :], di_ref[r0:r0 + rh, :]], axis=0))
            y = _dot3(gh_ref[...], gl_ref[...], jnp.concatenate(cols, axis=1))
            for c in range(n1c):
                for s in range(FFT_GROUP):
                    t0 = (g * FFT_GROUP + s) * n1r
                    y_ref[c, t0:t0 + n1r, :] += y[c * n1r:(c + 1) * n1r, s * LANES:(s + 1) * LANES]


def _conv_mid(x, row_blk0, col_blk0, l, nb, c, kr, ki, order, plan):
    n2, rh, kh = plan["n2"], plan["rh"], plan["kh"]
    fh, fl = plan["f1k"]
    gh, gl = plan["g"]
    nh = gh.shape[1]
    n1r = min(16, nh)
    n1c = nh // n1r
    koff = order * (c // LANES)
    xspec = pl.BlockSpec((l, LANES), lambda b, j, kk, k: (row_blk0 + b, col_blk0 + j))
    fspec = pl.BlockSpec((None,) + fh.shape[1:], lambda b, j, kk, k: (kk, 0, 0))
    steps = rh // FFT_SPB
    tspec = pl.BlockSpec((FFT_SPB, n2, LANES), lambda b, j, kk, k: (kk * steps + k, 0, 0))
    wspec = pl.BlockSpec((2 * n2, n2), lambda b, j, kk, k: (0, 0))
    kspec = pl.BlockSpec((FFT_SPB, n2, LANES), lambda b, j, kk, k: (kk * steps + k, 0, j + koff))
    gspec = pl.BlockSpec((None, nh, 2 * rh), lambda b, j, kk, k: (kk, 0, 0))
    return pl.pallas_call(
        functools.partial(_conv_mid_kernel, nslab=plan["r"], rh=rh, n2=n2),
        grid=(nb, c // LANES, kh, steps),
        in_specs=[xspec, fspec, fspec, tspec, tspec, wspec, wspec, kspec, kspec, gspec, gspec],
        out_specs=pl.BlockSpec((None, n1c, n2 * n1r, LANES), lambda b, j, kk, k: (b, 0, 0, j)),
        out_shape=jax.ShapeDtypeStruct((nb, n1c, n2 * n1r, c), F32),
        scratch_shapes=[pltpu.VMEM((n2 * rh, LANES), F32) for _ in range(4)],
        compiler_params=_cparams(("parallel", "parallel", "arbitrary", "arbitrary")),
        name="conv_mid",
    )(x, fh, fl, *plan["tw"], plan["f2"], plan["f2i"], kr, ki, gh, gl)


def _gate_kernel(y_ref, u_ref, g_ref, bias_ref, o_ref, *, n2):
    n1r = y_ref.shape[0] // n2
    for a in range(n1r):
        rows = slice(a * n2, (a + 1) * n2)
        yt = y_ref[pl.ds(a, n2, stride=n1r), :]
        o_ref[rows, :] = (g_ref[rows, :] * (yt + u_ref[rows, :] * bias_ref[...])).astype(o_ref.dtype)


def _gate(y, n2, u, u_row0, u_col0, gate, g_row0, g_col0, bias, out_dtype):
    nb, n1c, yr, c = y.shape
    tr = yr
    return pl.pallas_call(
        functools.partial(_gate_kernel, n2=n2),
        grid=(nb, c // LANES, n1c),
        in_specs=[pl.BlockSpec((None, None, yr, LANES), lambda b, j, q: (b, q, 0, j)),
                  pl.BlockSpec((tr, LANES), lambda b, j, q: (u_row0 // tr + b * n1c + q, u_col0 + j)),
                  pl.BlockSpec((tr, LANES), lambda b, j, q: (g_row0 // tr + b * n1c + q, g_col0 + j)),
                  pl.BlockSpec((1, LANES), lambda b, j, q: (0, j))],
        out_specs=pl.BlockSpec((tr, LANES), lambda b, j, q: (b * n1c + q, j)),
        out_shape=jax.ShapeDtypeStruct((nb * n1c * tr, c), out_dtype),
        compiler_params=_cparams(("parallel", "parallel", "arbitrary")),
        name="hyena_gate",
    )(y, u, gate, bias)


def _np_split(a):
    a32 = np.asarray(a, np.float32)
    hi = a32.astype(BF16)
    lo = (a32 - hi.astype(np.float32)).astype(BF16)
    return hi, lo


def _fft_plan(l):
    n = 2 * l
    n2 = FFT_N2
    n1 = n // n2
    r = n1 // 2 + 1
    rh = FFT_RH
    kh = -(-r // rh)
    rp = kh * rh
    kn1 = n1 // 2
    k1 = np.arange(rp, dtype=np.float64)[:, None]
    live = (k1 < r).astype(np.float64)

    ang = 2 * np.pi * k1 * np.arange(kn1)[None, :] / n1
    f1 = np.concatenate([np.cos(ang) * live, -np.sin(ang) * live], axis=0)
    f1k = np.concatenate([(np.cos(ang) * live).reshape(kh, rh, kn1),
                          (-np.sin(ang) * live).reshape(kh, rh, kn1)], axis=1)

    kk = np.arange(rp, dtype=np.float64)[None, :]
    wgt = np.where((kk == 0) | (kk == n1 // 2), 1.0, 2.0) * (kk < r) / n
    ango = 2 * np.pi * np.arange(n1 // 2)[:, None] * kk / n1
    gre = (np.cos(ango) * wgt).reshape(n1 // 2, kh, rh).transpose(1, 0, 2)
    gim = (-np.sin(ango) * wgt).reshape(n1 // 2, kh, rh).transpose(1, 0, 2)
    g = np.concatenate([gre, gim], axis=2)

    a2 = 2 * np.pi * np.outer(np.arange(n2), np.arange(n2)) / n2
    f2 = np.concatenate([np.cos(a2), -np.sin(a2)], axis=0)
    f2i = np.concatenate([np.cos(a2), np.sin(a2)], axis=0)

    idx = jnp.arange(rp, dtype=jnp.int32)[:, None] * jnp.arange(n2, dtype=jnp.int32)[None, :]
    ang = idx.astype(F32) * F32(2.0 * math.pi / n)
    tw = tuple(jnp.broadcast_to(t[:, :, None], (rp, n2, LANES)) for t in (jnp.cos(ang), -jnp.sin(ang)))
    return dict(n1=n1, n2=n2, r=r, rp=rp, kh=kh, rh=rh, f1=_np_split(f1), f1k=_np_split(f1k), g=_np_split(g),
                f2=_np_split(f2)[0], f2i=_np_split(f2i)[0], tw=tw)


def _filter_features(l):
    t = jnp.linspace(0.0, 1.0, l, dtype=F32)[:, None]
    bands = jnp.linspace(1e-4, B_BANDS - 1, B_BANDS, dtype=F32)[None, :]
    w = 2.0 * math.pi * jnp.arange(l, dtype=F32)[:, None] / l
    return jnp.concatenate([t, jnp.cos(bands * w), -jnp.sin(bands * w)], axis=-1)


def _hyena_filters(plan, l, c, w1, b1, f1, w2, b2, f2, w3, b3):
    n1, n2 = plan["n1"], plan["n2"]
    z = _filter_features(l)
    e = z.shape[1]
    ep = -(-e // 16) * 16
    z = jnp.pad(z, ((0, 0), (0, ep - e)))
    w1p = jnp.pad(w1, ((0, ep - e), (0, 0))).astype(BF16)
    max_decay = math.log(B_DECAY_TARGET) / B_FAST_DECAY_PCT
    min_decay = math.log(B_DECAY_TARGET) / B_SLOW_DECAY_PCT
    deltas = jnp.abs(jnp.linspace(min_decay, max_decay, c, dtype=F32))
    h, nrm = _filter_mlp(z, w1p, b1, f1, w2.astype(BF16), b2, f2, w3.astype(BF16), b3, deltas,
                         tl=min(512, l))
    oc = h.shape[1] // 2
    inv = 1.0 / (nrm[:, :oc] + nrm[:, oc:])
    b0 = h[0:1, oc:]
    ar, ai = _fft_stage1(h, *plan["f1"], plan, l, 1, 0, 0, 2 * oc)
    mid_shape = (1, plan["kh"], n2 * plan["rh"], 2 * oc)
    return _filter_mid(ar.reshape(mid_shape), ai.reshape(mid_shape), inv, b0, plan)


def _hyena_conv(plan, kf, order, l, nb, u, u_row0, u_col0, gate, g_row0, g_col0, bias, c, out_dtype):
    n2 = plan["n2"]
    cb = c // LANES
    y = _conv_mid(u, u_row0 // l, u_col0 * cb, l, nb, c, kf[0], kf[1], order, plan)
    return _gate(y, n2, u, u_row0, u_col0 * cb, gate, g_row0, g_col0 * cb, bias.reshape(1, c), out_dtype)


def _hyena_mixer(u, seqs, plans, c, fw, hy_bias):
    outs = []
    for (row0, nb, l) in seqs:
        plan = plans[l]
        kf = _hyena_filters(plan, l, c, *fw)
        z = _hyena_conv(plan, kf, 0, l, nb, u, row0, 0, u, row0, 1, hy_bias[0], c, F32)
        o = _hyena_conv(plan, kf, 1, l, nb, z, 0, 0, u, row0, 2, hy_bias[1], c, BF16)
        outs.append(o)
    return jnp.concatenate(outs, axis=0)


def _rope_tables(pos, hd):
    rot = hd // ROPE_FRACTION
    half = rot // 2
    inv = ROPE_THETA ** (-(jnp.arange(half, dtype=F32) * 2.0 / rot))
    ang = pos[:, None] * inv[None, :]
    cos, sin = jnp.cos(ang), jnp.sin(ang)
    m = pos.shape[0]
    one = jnp.ones((m, hd - rot), F32)
    zero = jnp.zeros((m, hd - rot), F32)
    zh = jnp.zeros((m, half), F32)
    c = jnp.concatenate([cos, cos, one], axis=1)
    s1 = jnp.concatenate([-sin, zh, zero], axis=1)
    s2 = jnp.concatenate([zh, sin, zero], axis=1)
    rep = LANES // hd
    return tuple(jnp.tile(t, (1, rep)) for t in (c, s1, s2)), half


def _trunk(x, bounds, seqs, p):
    m, dm = x.shape
    pos = jnp.concatenate([jnp.tile(jnp.arange(l, dtype=F32), nb) for (_, nb, l) in seqs])
    tabs_a, half_a = _rope_tables(pos, A_HEAD_DIM)
    tabs_c, half_c = _rope_tables(pos, C_HEAD_DIM)
    ident = (jnp.ones((m, LANES), F32), jnp.zeros((m, LANES), F32), jnp.zeros((m, LANES), F32))
    tabs_kv = tuple(jnp.concatenate([a, b], axis=1) for a, b in zip(tabs_a, ident))
    c_hy = dm - A_WIDTH
    plans = {l: _fft_plan(l) for l in sorted({l for (_, _, l) in seqs})}
    xf = x
    xb = x.astype(BF16)
    for i in range(DEPTH):
        j = i // 2
        if i % 2 == 0:
            w_in = p['mix_e_w_in'][j].astype(BF16)
            kv0 = A_WIDTH
            hy0 = A_WIDTH + 2 * A_KV_WIDTH
            q = _matmul_rope(xb, w_in[:, :kv0], tabs_a, half_a, tn=512)
            kv = _matmul_rope(xb, w_in[:, kv0:hy0], tabs_kv, half_a, tn=2 * A_KV_WIDTH)
            u = _matmul_conv(xb, w_in[:, hy0:], p['hy_conv_w'][j], p['hy_conv_b'][j], bounds)
            a_out = _even_attention(q, kv, p['a_sink'][j], bounds)
            fw = (p['hy_w1'][j], p['hy_b1'][j], p['hy_f1'][j], p['hy_w2'][j], p['hy_b2'][j],
                  p['hy_f2'][j], p['hy_w3'][j], p['hy_b3'][j])
            h_out = _hyena_mixer(u, seqs, plans, c_hy, fw, p['hy_bias'][j])
            w_out = p['mix_e_w_out'][j].astype(BF16)
            xf, xb = _matmul_ln([a_out, h_out], [w_out[:A_WIDTH], w_out[A_WIDTH:]], xf,
                                p['ln1_g'][i], p['ln1_b'][i])
        else:
            w_in = p['mix_o_w_in'][j].astype(BF16)
            gw = C_HEADS * C_HEAD_DIM
            ng = len(C_DILATIONS)
            qkv = []
            for g, d in enumerate(C_DILATIONS):
                trio = []
                for part in range(3):
                    c0 = (part * ng + g) * gw
                    trio.append(_odd_proj(xb, w_in, c0, tabs_c, half_c, d, rope=part < 2))
                qkv.append(tuple(trio))
            o = _odd_attention(qkv, bounds, m)
            xf, xb = _matmul_ln([o], [p['mix_o_w_out'][j].astype(BF16)], xf, p['ln1_g'][i], p['ln1_b'][i])
        xf, xb = _ffn_ln(xb, xf, p['ffn_w_gate'][i].astype(BF16), p['ffn_w_up'][i].astype(BF16),
                         p['ffn_w_down'][i].astype(BF16), p['ln2_g'][i], p['ln2_b'][i])
    return xf


def kernel(x_prompt, x_sample, mix_e_w_in, a_sink, hy_conv_w, hy_conv_b, hy_w1, hy_b1, hy_f1, hy_w2, hy_b2,
           hy_f2, hy_w3, hy_b3, hy_bias, mix_e_w_out, mix_o_w_in, mix_o_w_out, ffn_w_gate, ffn_w_up,
           ffn_w_down, ln1_g, ln1_b, ln2_g, ln2_b):
    p = dict(mix_e_w_in=mix_e_w_in, a_sink=a_sink, hy_conv_w=hy_conv_w, hy_conv_b=hy_conv_b,
             hy_w1=hy_w1, hy_b1=hy_b1, hy_f1=hy_f1, hy_w2=hy_w2, hy_b2=hy_b2, hy_f2=hy_f2,
             hy_w3=hy_w3, hy_b3=hy_b3, hy_bias=hy_bias, mix_e_w_out=mix_e_w_out,
             mix_o_w_in=mix_o_w_in, mix_o_w_out=mix_o_w_out, ffn_w_gate=ffn_w_gate,
             ffn_w_up=ffn_w_up, ffn_w_down=ffn_w_down, ln1_g=ln1_g, ln1_b=ln1_b,
             ln2_g=ln2_g, ln2_b=ln2_b)
    dm = x_prompt.shape[-1]
    seqs, bounds, row = [], [0], 0
    for xs in (x_prompt, x_sample):
        nb, l = xs.shape[0], xs.shape[1]
        seqs.append((row, nb, l))
        for _ in range(nb):
            row += l
            bounds.append(row)
    x = jnp.concatenate([x_prompt.reshape(-1, dm), x_sample.reshape(-1, dm)], axis=0)
    y = _trunk(x, tuple(bounds), tuple(seqs), p)
    n_p = x_prompt.shape[0] * x_prompt.shape[1]
    return (y[:n_p].reshape(x_prompt.shape), y[n_p:].reshape(x_sample.shape))
```

```python
import functools
import math

import numpy as np
import jax
import jax.numpy as jnp
from jax import lax
from jax.experimental import pallas as pl
from jax.experimental.pallas import tpu as pltpu

F32 = jnp.float32
BF16 = jnp.bfloat16

DEPTH = 4
A_HEADS, A_KV_HEADS, A_HEAD_DIM, A_RADIUS = 16, 2, 64, 128
A_WIDTH = A_HEADS * A_HEAD_DIM
A_KV_WIDTH = A_KV_HEADS * A_HEAD_DIM
B_SHORT, B_EMB = 3, 33
B_BANDS = (B_EMB - 1) // 2
B_DECAY_TARGET, B_FAST_DECAY_PCT, B_SLOW_DECAY_PCT = 1e-2, 0.3, 1.5
C_HEADS, C_HEAD_DIM = 16, 128
C_DILATIONS = (1, 4, 16)
C_RADIUS = 64
ROPE_THETA, ROPE_FRACTION = 500000.0, 4
ALPHA = (2 * DEPTH) ** 0.25
LN_EPS = 1e-5

LANES = 128
VMEM_LIMIT = 56 * 1024 * 1024
FFT_N2 = 256

ODD_CHUNK = 2048
ATT_TQ = 256


def _cparams(sem):
    return pltpu.CompilerParams(dimension_semantics=sem, vmem_limit_bytes=VMEM_LIMIT)


def _seq_bounds(row, bounds):
    start = jnp.int32(bounds[0])
    end = jnp.int32(bounds[1])
    for b0, b1 in zip(bounds[1:-1], bounds[2:]):
        inside = row >= b0
        start = jnp.where(inside, jnp.int32(b0), start)
        end = jnp.where(inside, jnp.int32(b1), end)
    return start, end


def _rope(a, c, s1, s2, half):
    w = a.shape[-1]
    return a * c + pltpu.roll(a, w - half, 1) * s1 + pltpu.roll(a, half, 1) * s2


def _mm_conv_kernel(xp_ref, xm_ref, xn_ref, w_ref, cw_ref, cb_ref, o_ref, *, bounds):
    i = pl.program_id(0)
    tm, hb = xm_ref.shape[0], xp_ref.shape[0]
    rows = tm + 2 * hb
    row0 = i * tm
    start, end = _seq_bounds(row0, bounds)
    lhs = jnp.concatenate([xp_ref[...], xm_ref[...], xn_ref[...]], axis=0)
    acc = jnp.dot(lhs, w_ref[...], preferred_element_type=F32)
    h0 = acc[hb:hb + tm]
    hm = pltpu.roll(acc, 1, 0)[hb:hb + tm]
    hp = pltpu.roll(acc, rows - 1, 0)[hb:hb + tm]
    ridx = lax.broadcasted_iota(jnp.int32, h0.shape, 0)
    hm = jnp.where((ridx == 0) & (row0 <= start), 0.0, hm)
    hp = jnp.where((ridx == tm - 1) & (row0 + tm >= end), 0.0, hp)
    y = cb_ref[...] + hm * cw_ref[0:1, :]
    y = y + h0 * cw_ref[1:2, :]
    y = y + hp * cw_ref[2:3, :]
    o_ref[...] = y


def _matmul_conv(x, w, cw, cb, bounds, tn=512, tm=1024):
    m, k = x.shape
    n = w.shape[1]
    hb = 16
    per = tm // hb
    last = m // hb - 1
    return pl.pallas_call(
        functools.partial(_mm_conv_kernel, bounds=bounds),
        grid=(m // tm, n // tn),
        in_specs=[pl.BlockSpec((hb, k), lambda i, j: (jnp.maximum(i * per - 1, 0), 0)),
                  pl.BlockSpec((tm, k), lambda i, j: (i, 0)),
                  pl.BlockSpec((hb, k), lambda i, j: (jnp.minimum((i + 1) * per, last), 0)),
                  pl.BlockSpec((k, tn), lambda i, j: (0, j)),
                  pl.BlockSpec((B_SHORT, tn), lambda i, j: (0, j)),
                  pl.BlockSpec((1, tn), lambda i, j: (0, j))],
        out_specs=pl.BlockSpec((tm, tn), lambda i, j: (i, j)),
        out_shape=jax.ShapeDtypeStruct((m, n), F32),
        compiler_params=_cparams(("parallel", "arbitrary")),
        name="matmul_conv",
    )(x, x, x, w, cw, cb.reshape(1, n))


def _mm_rope_kernel(x_ref, w_ref, c_ref, s1_ref, s2_ref, o_ref, *, half):
    acc = jnp.dot(x_ref[...], w_ref[...], preferred_element_type=F32)
    tw = c_ref.shape[1]
    rc = 256
    for r0 in range(0, acc.shape[0], rc):
        rows = slice(r0, r0 + rc)
        for c in range(acc.shape[1] // tw):
            cols = slice(c * tw, (c + 1) * tw)
            o_ref[rows, cols] = _rope(acc[rows, cols], c_ref[rows, :], s1_ref[rows, :], s2_ref[rows, :],
                                      half).astype(o_ref.dtype)


def _matmul_rope(x, w, tabs, half, tn, tm=1024):
    m, k = x.shape
    n = w.shape[1]
    tw = tabs[0].shape[1]
    tab_spec = pl.BlockSpec((tm, tw), lambda i, j: (i, 0))
    return pl.pallas_call(
        functools.partial(_mm_rope_kernel, half=half),
        grid=(m // tm, n // tn),
        in_specs=[pl.BlockSpec((tm, k), lambda i, j: (i, 0)),
                  pl.BlockSpec((k, tn), lambda i, j: (0, j)),
                  tab_spec, tab_spec, tab_spec],
        out_specs=pl.BlockSpec((tm, tn), lambda i, j: (i, j)),
        out_shape=jax.ShapeDtypeStruct((m, n), BF16),
        compiler_params=_cparams(("parallel", "arbitrary")),
        name="matmul_rope",
    )(x, w, *tabs)


def _odd_proj_kernel(x_ref, w_ref, c_ref, s1_ref, s2_ref, o_ref, acc_ref, tmp_ref, *, d, rope, half):
    hps = w_ref.shape[1] // LANES
    tm = x_ref.shape[0]
    t = tm // d
    pair = 2
    rc = 256
    for p in range(hps // pair):
        acc = jnp.dot(x_ref[...], w_ref[:, p * pair * LANES:(p + 1) * pair * LANES], preferred_element_type=F32)
        for h2 in range(pair):
            hh = p * pair + h2
            slot = (p % 2) * pair + h2
            acc_ref[slot, :, :] = acc[:, h2 * LANES:(h2 + 1) * LANES]
            for c0 in range(0, tm, rc):
                a = acc_ref[slot, c0:c0 + rc, :]
                if rope:
                    a = _rope(a, c_ref[c0:c0 + rc, :], s1_ref[c0:c0 + rc, :], s2_ref[c0:c0 + rc, :], half)
                if d == 1:
                    o_ref[hh, 0, c0:c0 + rc, :] = a.astype(BF16)
                elif rope:
                    acc_ref[slot, c0:c0 + rc, :] = a
            if d == 16:
                ts = hh % 2
                for ra in range(4):
                    tmp_ref[ts, ra * (tm // 4):(ra + 1) * (tm // 4), :] = acc_ref[slot, pl.ds(ra, tm // 4, stride=4), :]
                for ra in range(4):
                    for rb in range(4):
                        o_ref[hh, ra + 4 * rb, :, :] = tmp_ref[ts, pl.ds(ra * (tm // 4) + rb, t, stride=4), :].astype(BF16)
            elif d > 1:
                for r in range(d):
                    o_ref[hh, r, :, :] = acc_ref[slot, pl.ds(r, t, stride=d), :].astype(BF16)


def _odd_proj(x, w, col0, tabs, half, d, rope, hps=8):
    m, k = x.shape
    tm = ODD_CHUNK
    cb0 = col0 // (hps * LANES)
    tab_spec = pl.BlockSpec((tm, LANES), lambda i, j: (i, 0))
    return pl.pallas_call(
        functools.partial(_odd_proj_kernel, d=d, rope=rope, half=half),
        grid=(m // tm, C_HEADS // hps),
        in_specs=[pl.BlockSpec((tm, k), lambda i, j: (i, 0)),
                  pl.BlockSpec((k, hps * LANES), lambda i, j: (0, cb0 + j)),
                  tab_spec, tab_spec, tab_spec],
        out_specs=pl.BlockSpec((hps, d, tm // d, LANES), lambda i, j: (j, 0, i, 0)),
        out_shape=jax.ShapeDtypeStruct((C_HEADS, d, m // d, LANES), BF16),
        scratch_shapes=[pltpu.VMEM((4, tm, LANES), F32), pltpu.VMEM((2, tm, LANES), F32)],
        compiler_params=_cparams(("parallel", "arbitrary")),
        name="odd_proj",
    )(x, w, *tabs)


LN_ROWS = 128


def _layer_norm_store(x_ref, acc_ref, g_ref, b_ref, of_ref, ob_ref, row0=0, nrows=None):
    nrows = x_ref.shape[0] if nrows is None else nrows
    for c in range(nrows // LN_ROWS):
        rows = pl.ds(row0 + c * LN_ROWS, LN_ROWS)
        r = ALPHA * x_ref[rows, :] + acc_ref[rows, :]
        mu = jnp.mean(r, axis=-1, keepdims=True)
        xc = r - mu
        var = jnp.mean(xc * xc, axis=-1, keepdims=True)
        y = xc * lax.rsqrt(var + LN_EPS) * g_ref[...] + b_ref[...]
        of_ref[rows, :] = y
        ob_ref[rows, :] = y.astype(BF16)


def _mm_ln_kernel(*refs, n_in):
    ys = refs[:n_in]
    ws = refs[n_in:2 * n_in]
    x_ref, g_ref, b_ref, of_ref, ob_ref, acc_ref = refs[2 * n_in:]
    half = x_ref.shape[0] // 2
    for r0 in (0, half):
        rows = slice(r0, r0 + half)
        acc = jnp.dot(ys[0][rows, :], ws[0][...], preferred_element_type=F32)
        for y_ref, w_ref in zip(ys[1:], ws[1:]):
            acc = acc + jnp.dot(y_ref[rows, :], w_ref[...], preferred_element_type=F32)
        acc_ref[rows, :] = acc
        _layer_norm_store(x_ref, acc_ref, g_ref, b_ref, of_ref, ob_ref, r0, half)


def _matmul_ln(ys, ws, x, g, b, tm=512):
    m, dm = x.shape
    n_in = len(ys)
    in_specs = [pl.BlockSpec((tm, y.shape[1]), lambda i: (i, 0)) for y in ys]
    in_specs += [pl.BlockSpec(w.shape, lambda i: (0, 0)) for w in ws]
    in_specs += [pl.BlockSpec((tm, dm), lambda i: (i, 0)),
                 pl.BlockSpec((1, dm), lambda i: (0, 0)),
                 pl.BlockSpec((1, dm), lambda i: (0, 0))]
    return pl.pallas_call(
        functools.partial(_mm_ln_kernel, n_in=n_in),
        grid=(m // tm,),
        in_specs=in_specs,
        out_specs=[pl.BlockSpec((tm, dm), lambda i: (i, 0)), pl.BlockSpec((tm, dm), lambda i: (i, 0))],
        out_shape=[jax.ShapeDtypeStruct((m, dm), F32), jax.ShapeDtypeStruct((m, dm), BF16)],
        scratch_shapes=[pltpu.VMEM((tm, dm), F32)],
        compiler_params=_cparams(("parallel",)),
        name="matmul_ln",
    )(*ys, *ws, x, g.reshape(1, dm), b.reshape(1, dm))


def _ffn_kernel(xb_ref, xf_ref, wg_ref, wu_ref, wd_ref, g_ref, b_ref, of_ref, ob_ref, acc_ref):
    j = pl.program_id(1)

    @pl.when(j == 0)
    def _():
        acc_ref[...] = jnp.zeros_like(acc_ref)

    xb = xb_ref[...]
    gate = jnp.dot(xb, wg_ref[...], preferred_element_type=F32)
    up = jnp.dot(xb, wu_ref[...], preferred_element_type=F32)
    h = (gate * jax.nn.sigmoid(gate)) * up
    acc_ref[...] += jnp.dot(h.astype(BF16), wd_ref[...], preferred_element_type=F32)

    @pl.when(j == pl.num_programs(1) - 1)
    def _():
        _layer_norm_store(xf_ref, acc_ref, g_ref, b_ref, of_ref, ob_ref)


def _ffn_ln(xb, xf, wg, wu, wd, g, b, tm=512, tf=512):
    m, dm = xf.shape
    dff = wg.shape[1]
    row = lambda i, j: (i, 0)
    return pl.pallas_call(
        _ffn_kernel,
        grid=(m // tm, dff // tf),
        in_specs=[pl.BlockSpec((tm, dm), row), pl.BlockSpec((tm, dm), row),
                  pl.BlockSpec((dm, tf), lambda i, j: (0, j)),
                  pl.BlockSpec((dm, tf), lambda i, j: (0, j)),
                  pl.BlockSpec((tf, dm), lambda i, j: (j, 0)),
                  pl.BlockSpec((1, dm), lambda i, j: (0, 0)),
                  pl.BlockSpec((1, dm), lambda i, j: (0, 0))],
        out_specs=[pl.BlockSpec((tm, dm), row), pl.BlockSpec((tm, dm), row)],
        out_shape=[jax.ShapeDtypeStruct((m, dm), F32), jax.ShapeDtypeStruct((m, dm), BF16)],
        scratch_shapes=[pltpu.VMEM((tm, dm), F32)],
        compiler_params=_cparams(("parallel", "arbitrary")),
        name="ffn_ln",
    )(xb, xf, wg, wu, wd, g.reshape(1, dm), b.reshape(1, dm))


def _band_bias(nq, radius):
    r = np.arange(nq)[:, None]
    c = np.arange(nq + 2 * radius)[None, :]
    return np.where(np.abs(c - radius - r) <= radius, 0.0, -np.inf).astype(np.float32)


def _even_attn_kernel(sink_ref, band_ref, q_ref, kp_ref, km_ref, kn_ref, o_ref, bias_ref, *, bounds):
    i = pl.program_id(0)
    tq = q_ref.shape[0]
    sub = A_RADIUS
    nk = sub + 2 * A_RADIUS
    row0 = i * tq
    start, end = _seq_bounds(row0, bounds)
    group = A_HEADS // A_KV_HEADS
    scale = A_HEAD_DIM ** -0.5
    blocks = [kp_ref, km_ref, kn_ref]
    assert tq == 2 * sub
    for sb in range(tq // sub):
        kv = jnp.concatenate([r[...] for r in blocks[sb:sb + 2]], axis=0)
        rk = row0 + sb * sub - A_RADIUS + lax.broadcasted_iota(jnp.int32, (1, nk), 1)
        bias = band_ref[...] + jnp.where((rk >= start) & (rk < end), 0.0, -jnp.inf)
        bias_ref[0:sub, :] = bias
        bias_ref[sub:2 * sub, :] = bias
        first = lax.broadcasted_iota(jnp.int32, (2 * sub, 1), 0) < sub
        rows = slice(sb * sub, (sb + 1) * sub)
        for j in range(A_KV_HEADS):
            k = kv[:, j * A_HEAD_DIM:(j + 1) * A_HEAD_DIM]
            v = kv[:, A_KV_WIDTH + j * A_HEAD_DIM:A_KV_WIDTH + (j + 1) * A_HEAD_DIM]
            for gq in range(0, group, 2):
                heads = (j * group + gq, j * group + gq + 1)
                cols = [slice(h * A_HEAD_DIM, (h + 1) * A_HEAD_DIM) for h in heads]
                qh = jnp.concatenate([q_ref[rows, c] for c in cols], axis=0)
                s = lax.dot_general(qh, k, (((1,), (1,)), ((), ())), preferred_element_type=F32) * scale
                s = s + bias_ref[...]
                sk = jnp.where(first, sink_ref[heads[0]], sink_ref[heads[1]])
                m = jnp.maximum(jnp.max(s, axis=-1, keepdims=True), sk)
                p = jnp.exp(s - m)
                den = jnp.sum(p, axis=-1, keepdims=True) + jnp.exp(sk - m)
                o = jnp.dot(p.astype(BF16), v, preferred_element_type=F32) / den
                o_ref[rows, cols[0]] = o[0:sub].astype(o_ref.dtype)
                o_ref[rows, cols[1]] = o[sub:2 * sub].astype(o_ref.dtype)


def _even_attention(q, kv, sink, bounds):
    m = q.shape[0]
    tq = ATT_TQ
    hb = A_RADIUS
    per = tq // hb
    last = m // hb - 1
    kvw = kv.shape[1]
    band = _band_bias(A_RADIUS, A_RADIUS)
    return pl.pallas_call(
        functools.partial(_even_attn_kernel, bounds=bounds),
        grid=(m // tq,),
        in_specs=[pl.BlockSpec(memory_space=pltpu.SMEM),
                  pl.BlockSpec(band.shape, lambda i: (0, 0)),
                  pl.BlockSpec((tq, A_WIDTH), lambda i: (i, 0)),
                  pl.BlockSpec((hb, kvw), lambda i: (jnp.maximum(i * per - 1, 0), 0)),
                  pl.BlockSpec((tq, kvw), lambda i: (i, 0)),
                  pl.BlockSpec((hb, kvw), lambda i: (jnp.minimum((i + 1) * per, last), 0))],
        out_specs=pl.BlockSpec((tq, A_WIDTH), lambda i: (i, 0)),
        out_shape=jax.ShapeDtypeStruct((m, A_WIDTH), BF16),
        scratch_shapes=[pltpu.VMEM((2 * band.shape[0], band.shape[1]), F32)],
        compiler_params=_cparams(("parallel",)),
        name="even_attention",
    )(sink, band, q, kv, kv, kv)


def _odd_attn_kernel(*refs, bounds):
    ng = len(C_DILATIONS)
    band_ref = refs[7 * ng]
    o_ref, oacc, lacc = refs[7 * ng + 1:]
    i = pl.program_id(0)
    chunk = ODD_CHUNK
    qb = 128
    nk = qb + 2 * C_RADIUS
    row0 = i * chunk
    start, end = _seq_bounds(row0, bounds)
    cc = lax.broadcasted_iota(jnp.int32, (1, nk), 1)
    scale = C_HEAD_DIM ** -0.5
    for g, d in enumerate(C_DILATIONS):
        q_ref, kp_ref, km_ref, kn_ref, vp_ref, vm_ref, vn_ref = refs[7 * g:7 * g + 7]
        tg = chunk // d
        t_lo, t_hi, t_c0 = start // d, end // d, row0 // d
        for sb in range(tg // qb):
            lo, hi = qb * sb - C_RADIUS, qb * sb + qb + C_RADIUS
            tk = t_c0 + lo + cc
            col = jnp.where((tk >= t_lo) & (tk < t_hi), 0.0, -jnp.inf)
            for r in range(d):
                def window(p_ref, m_ref, n_ref):
                    parts = []
                    if lo < 0:
                        parts.append(p_ref[0, r, :, :])
                    parts.append(m_ref[0, r, max(lo, 0):min(hi, tg), :])
                    if hi > tg:
                        parts.append(n_ref[0, r, :, :])
                    return parts[0] if len(parts) == 1 else jnp.concatenate(parts, axis=0)

                q = q_ref[0, r, qb * sb:qb * (sb + 1), :]
                k = window(kp_ref, km_ref, kn_ref)
                v = window(vp_ref, vm_ref, vn_ref)
                s = lax.dot_general(q, k, (((1,), (1,)), ((), ())), preferred_element_type=F32) * scale
                s = s + band_ref[...] + col
                m = jnp.max(s, axis=-1, keepdims=True)
                p = jnp.exp(s - m)
                den = jnp.sum(p, axis=-1, keepdims=True)
                o = jnp.dot(p.astype(BF16), v, preferred_element_type=F32) / den
                lse = jnp.broadcast_to(m + jnp.log(den), (qb, LANES))
                if d == 1:
                    rows = pl.ds(qb * sb, qb)
                else:
                    rows = pl.ds(r + d * qb * sb, qb, stride=d)
                oacc[g, rows, :] = o
                lacc[g, rows, :] = lse
    ls = [lacc[g] for g in range(ng)]
    mx = functools.reduce(jnp.maximum, ls)
    ws = [jnp.exp(l - mx) for l in ls]
    tot = functools.reduce(lambda a, b: a + b, ws)
    out = functools.reduce(lambda a, b: a + b, [(ws[g] / tot) * oacc[g] for g in range(ng)])
    o_ref[...] = out.astype(o_ref.dtype)


def _odd_attention(qkv, bounds, m):
    chunk = ODD_CHUNK
    hb = C_RADIUS
    operands, in_specs = [], []
    for (q, k, v), d in zip(qkv, C_DILATIONS):
        tg = chunk // d
        per = tg // hb
        last = m // d // hb - 1
        main = pl.BlockSpec((1, d, tg, LANES), lambda i, h: (h, 0, i, 0))
        prev = pl.BlockSpec((1, d, hb, LANES), lambda i, h, per=per: (h, 0, jnp.maximum(i * per - 1, 0), 0))
        nxt = pl.BlockSpec((1, d, hb, LANES), lambda i, h, per=per, last=last: (h, 0, jnp.minimum((i + 1) * per, last), 0))
        operands += [q, k, k, k, v, v, v]
        in_specs += [main, prev, main, nxt, prev, main, nxt]
    band = _band_bias(128, C_RADIUS)
    operands.append(band)
    in_specs.append(pl.BlockSpec(band.shape, lambda i, h: (0, 0)))
    ng = len(C_DILATIONS)
    return pl.pallas_call(
        functools.partial(_odd_attn_kernel, bounds=bounds),
        grid=(m // chunk, C_HEADS),
        in_specs=in_specs,
        out_specs=pl.BlockSpec((chunk, LANES), lambda i, h: (i, h)),
        out_shape=jax.ShapeDtypeStruct((m, C_HEADS * C_HEAD_DIM), BF16),
        scratch_shapes=[pltpu.VMEM((ng, chunk, LANES), F32), pltpu.VMEM((ng, chunk, LANES), F32)],
        compiler_params=_cparams(("parallel", "arbitrary")),
        name="odd_attention",
    )(*operands)


def _filter_mlp_kernel(z_ref, w1_ref, b1_ref, f1_ref, w2_ref, b2_ref, f2_ref, w3_ref, b3_ref, dl_ref,
                       h_ref, nrm_ref):
    i = pl.program_id(0)
    z = z_ref[...]
    h = jnp.sin(f1_ref[...] * (jnp.dot(z.astype(BF16), w1_ref[...], preferred_element_type=F32) + b1_ref[...]))
    h = jnp.sin(f2_ref[...] * (jnp.dot(h.astype(BF16), w2_ref[...], preferred_element_type=F32) + b2_ref[...]))
    h = jnp.dot(h.astype(BF16), w3_ref[...], preferred_element_type=F32) + b3_ref[...]
    decay = jnp.exp(-z[:, 0:1] * dl_ref[...])
    nrep = h.shape[1] // decay.shape[1]
    h = h * jnp.concatenate([decay] * nrep, axis=1)
    h_ref[...] = h

    @pl.when(i == 0)
    def _():
        nrm_ref[...] = jnp.zeros_like(nrm_ref)

    half = h.shape[1] // 2
    col = lax.broadcasted_iota(jnp.int32, h.shape, 1)
    row = lax.broadcasted_iota(jnp.int32, h.shape, 0) + i * h.shape[0]
    a = jnp.where((col >= half) & (row == 0), 0.0, jnp.abs(h))
    nrm_ref[...] += jnp.sum(a, axis=0, keepdims=True)


def _filter_mlp(z, w1, b1, f1, w2, b2, f2, w3, b3, deltas, tl=512):
    l, e = z.shape
    hid = w1.shape[1]
    n = w3.shape[1]
    c = deltas.shape[0]
    full = lambda shape: pl.BlockSpec(shape, lambda i: (0, 0))
    return pl.pallas_call(
        _filter_mlp_kernel,
        grid=(l // tl,),
        in_specs=[pl.BlockSpec((tl, e), lambda i: (i, 0)),
                  full((e, hid)), full((1, hid)), full((1, hid)),
                  full((hid, hid)), full((1, hid)), full((1, hid)),
                  full((hid, n)), full((1, n)), full((1, c))],
        out_specs=[pl.BlockSpec((tl, n), lambda i: (i, 0)), full((1, n))],
        out_shape=[jax.ShapeDtypeStruct((l, n), F32), jax.ShapeDtypeStruct((1, n), F32)],
        compiler_params=_cparams(("arbitrary",)),
        name="filter_mlp",
    )(z, w1, b1.reshape(1, hid), f1.reshape(1, hid), w2, b2.reshape(1, hid), f2.reshape(1, hid),
      w3, b3.reshape(1, n), deltas.reshape(1, c))


def _split(x):
    hi = x.astype(BF16)
    lo = (x - hi.astype(F32)).astype(BF16)
    return hi, lo


def _dot3(ch, cl, x):
    xh, xl = _split(x)
    r = jnp.dot(ch, xh, preferred_element_type=F32)
    r = r + jnp.dot(ch, xl, preferred_element_type=F32)
    return r + jnp.dot(cl, xh, preferred_element_type=F32)


def _dot1(ch, x):
    return jnp.dot(ch, x.astype(BF16), preferred_element_type=F32)


FFT_GROUP = 8


FFT_RH = 24
FFT_SPB = 6


def _outer_dft_chunk(x_ref, fh_ref, fl_ref, ar_ref, ai_ref, rh, n2, gather):
    kn1 = fh_ref.shape[1]
    for g in range(n2 // FFT_GROUP):
        cols = []
        for s in range(FFT_GROUP):
            i2 = g * FFT_GROUP + s
            cols.append(x_ref[pl.ds(i2, kn1, stride=n2), :] if gather else x_ref[i2 * kn1:(i2 + 1) * kn1, :])
        out = _dot3(fh_ref[...], fl_ref[...], jnp.concatenate(cols, axis=1))
        for s in range(FFT_GROUP):
            r0 = (g * FFT_GROUP + s) * rh
            ar_ref[r0:r0 + rh, :] = out[:rh, s * LANES:(s + 1) * LANES]
            ai_ref[r0:r0 + rh, :] = out[rh:, s * LANES:(s + 1) * LANES]


def _cpair(p, n):
    return p[:n, :LANES] - p[n:, LANES:], p[:n, LANES:] + p[n:, :LANES]


def _twiddle(xr, xi, tr, ti):
    return xr * tr - xi * ti, xr * ti + xi * tr


def _filter_mid_kernel(hf_ref, hb_ref, fh_ref, fl_ref, tr_ref, ti_ref, w_ref, inv_ref, b0_ref,
                       kr_ref, ki_ref, fr_ref, fi_ref, br_ref, bi_ref, *, nslab, rh, n2):
    k0 = pl.program_id(2) * FFT_SPB

    @pl.when(pl.program_id(2) == 0)
    def _():
        _outer_dft_chunk(hf_ref, fh_ref, fl_ref, fr_ref, fi_ref, rh, n2, gather=False)
        _outer_dft_chunk(hb_ref, fh_ref, fl_ref, br_ref, bi_ref, rh, n2, gather=False)

    @pl.when(pl.program_id(1) * rh + k0 < nslab)
    def _():
        for t in range(FFT_SPB):
            rows = pl.ds(k0 + t, n2, stride=rh)
            tr, ti = tr_ref[t], ti_ref[t]
            x4 = jnp.concatenate(_twiddle(fr_ref[rows, :], fi_ref[rows, :], tr, ti)
                                 + _twiddle(br_ref[rows, :], bi_ref[rows, :], tr, ti), axis=1)
            p = _dot1(w_ref[...], x4)
            fr, fi = _cpair(p[:, :2 * LANES], n2)
            br, bi = _cpair(p[:, 2 * LANES:], n2)
            kr_ref[t] = (fr + (br - b0_ref[...])) * inv_ref[...]
            ki_ref[t] = (fi - bi) * inv_ref[...]

    @pl.when(pl.program_id(1) * rh + k0 >= nslab)
    def _():
        kr_ref[...] = jnp.zeros_like(kr_ref)
        ki_ref[...] = jnp.zeros_like(ki_ref)


def _filter_mid(h, inv, b0, plan):
    n2, kh, rh, rp = plan["n2"], plan["kh"], plan["rh"], plan["rp"]
    l = h.shape[0]
    oc = h.shape[1] // 2
    nj = oc // LANES
    fh, fl = plan["f1k"]
    fwd = pl.BlockSpec((l, LANES), lambda j, kk, k: (0, j))
    bwd = pl.BlockSpec((l, LANES), lambda j, kk, k: (0, j + nj))
    fspec = pl.BlockSpec((None,) + fh.shape[1:], lambda j, kk, k: (kk, 0, 0))
    steps = rh // FFT_SPB
    tspec = pl.BlockSpec((FFT_SPB, n2, LANES), lambda j, kk, k: (kk * steps + k, 0, 0))
    wspec = pl.BlockSpec((2 * n2, n2), lambda j, kk, k: (0, 0))
    vec = pl.BlockSpec((1, LANES), lambda j, kk, k: (0, j))
    ospec = pl.BlockSpec((FFT_SPB, n2, LANES), lambda j, kk, k: (kk * steps + k, 0, j))
    out = jax.ShapeDtypeStruct((rp, n2, oc), F32)
    return pl.pallas_call(
        functools.partial(_filter_mid_kernel, nslab=plan["r"], rh=rh, n2=n2),
        grid=(nj, kh, steps),
        in_specs=[fwd, bwd, fspec, fspec, tspec, tspec, wspec, vec, vec],
        out_specs=[ospec, ospec],
        out_shape=[out, out],
        scratch_shapes=[pltpu.VMEM((n2 * rh, LANES), F32) for _ in range(4)],
        compiler_params=_cparams(("parallel", "arbitrary", "arbitrary")),
        name="filter_mid",
    )(h, h, fh, fl, *plan["tw"], plan["f2"], inv, b0)


def _conv_mid_kernel(x_ref, fh_ref, fl_ref, tr_ref, ti_ref, wf_ref, wi_ref, kr_ref, ki_ref, gh_ref, gl_ref,
                     y_ref, ar_ref, ai_ref, dr_ref, di_ref, *, nslab, rh, n2):
    k0 = pl.program_id(3) * FFT_SPB

    @pl.when((pl.program_id(2) == 0) & (pl.program_id(3) == 0))
    def _():
        y_ref[...] = jnp.zeros_like(y_ref)

    @pl.when(pl.program_id(3) == 0)
    def _():
        _outer_dft_chunk(x_ref, fh_ref, fl_ref, ar_ref, ai_ref, rh, n2, gather=True)

    @pl.when(pl.program_id(2) * rh + k0 < nslab)
    def _():
        for t in range(FFT_SPB):
            rows = pl.ds(k0 + t, n2, stride=rh)
            tr, ti = tr_ref[t], ti_ref[t]
            x2 = jnp.concatenate(_twiddle(ar_ref[rows, :], ai_ref[rows, :], tr, ti), axis=1)
            xr, xi = _cpair(_dot1(wf_ref[...], x2), n2)
            kr, ki = kr_ref[t], ki_ref[t]
            y2 = jnp.concatenate([xr * kr - xi * ki, xr * ki + xi * kr], axis=1)
            cr, ci = _cpair(_dot1(wi_ref[...], y2), n2)
            dr, di = _twiddle(cr, ci, tr, -ti)
            dr_ref[rows, :] = dr
            di_ref[rows, :] = di

    @pl.when(pl.program_id(2) * rh + k0 >= nslab)
    def _():
        for t in range(FFT_SPB):
            rows = pl.ds(k0 + t, n2, stride=rh)
            dr_ref[rows, :] = jnp.zeros((n2, LANES), F32)
            di_ref[rows, :] = jnp.zeros((n2, LANES), F32)

    @pl.when(pl.program_id(3) == pl.num_programs(3) - 1)
    def _():
        n1c = y_ref.shape[0]
        n1r = y_ref.shape[1] // n2
        for g in range(n2 // FFT_GROUP):
            cols = []
            for s in range(FFT_GROUP):
                r0 = (g * FFT_GROUP + s) * rh
                cols.append(jnp.concatenate([dr_ref[r0:r0 + rh, :], di_ref[r0:r0 + rh, :]], axis=0))
            y = _dot3(gh_ref[...], gl_ref[...], jnp.concatenate(cols, axis=1))
            for c in range(n1c):
                for s in range(FFT_GROUP):
                    t0 = (g * FFT_GROUP + s) * n1r
                    y_ref[c, t0:t0 + n1r, :] += y[c * n1r:(c + 1) * n1r, s * LANES:(s + 1) * LANES]


def _conv_mid(x, row_blk0, col_blk0, l, nb, c, kr, ki, order, plan):
    n2, rh, kh = plan["n2"], plan["rh"], plan["kh"]
    fh, fl = plan["f1k"]
    gh, gl = plan["g"]
    nh = gh.shape[1]
    n1r = min(16, nh)
    n1c = nh // n1r
    koff = order * (c // LANES)
    xspec = pl.BlockSpec((l, LANES), lambda b, j, kk, k: (row_blk0 + b, col_blk0 + j))
    fspec = pl.BlockSpec((None,) + fh.shape[1:], lambda b, j, kk, k: (kk, 0, 0))
    steps = rh // FFT_SPB
    tspec = pl.BlockSpec((FFT_SPB, n2, LANES), lambda b, j, kk, k: (kk * steps + k, 0, 0))
    wspec = pl.BlockSpec((2 * n2, n2), lambda b, j, kk, k: (0, 0))
    kspec = pl.BlockSpec((FFT_SPB, n2, LANES), lambda b, j, kk, k: (kk * steps + k, 0, j + koff))
    gspec = pl.BlockSpec((None, nh, 2 * rh), lambda b, j, kk, k: (kk, 0, 0))
    return pl.pallas_call(
        functools.partial(_conv_mid_kernel, nslab=plan["r"], rh=rh, n2=n2),
        grid=(nb, c // LANES, kh, steps),
        in_specs=[xspec, fspec, fspec, tspec, tspec, wspec, wspec, kspec, kspec, gspec, gspec],
        out_specs=pl.BlockSpec((None, n1c, n2 * n1r, LANES), lambda b, j, kk, k: (b, 0, 0, j)),
        out_shape=jax.ShapeDtypeStruct((nb, n1c, n2 * n1r, c), F32),
        scratch_shapes=[pltpu.VMEM((n2 * rh, LANES), F32) for _ in range(4)],
        compiler_params=_cparams(("parallel", "parallel", "arbitrary", "arbitrary")),
        name="conv_mid",
    )(x, fh, fl, *plan["tw"], plan["f2"], plan["f2i"], kr, ki, gh, gl)


def _gate_kernel(y_ref, u_ref, g_ref, bias_ref, o_ref, *, n2):
    n1r = y_ref.shape[0] // n2
    for a in range(n1r):
        rows = slice(a * n2, (a + 1) * n2)
        yt = y_ref[pl.ds(a, n2, stride=n1r), :]
        o_ref[rows, :] = (g_ref[rows, :] * (yt + u_ref[rows, :] * bias_ref[...])).astype(o_ref.dtype)


def _gate(y, n2, u, u_row0, u_col0, gate, g_row0, g_col0, bias, out_dtype):
    nb, n1c, yr, c = y.shape
    tr = yr
    return pl.pallas_call(
        functools.partial(_gate_kernel, n2=n2),
        grid=(nb, c // LANES, n1c),
        in_specs=[pl.BlockSpec((None, None, yr, LANES), lambda b, j, q: (b, q, 0, j)),
                  pl.BlockSpec((tr, LANES), lambda b, j, q: (u_row0 // tr + b * n1c + q, u_col0 + j)),
                  pl.BlockSpec((tr, LANES), lambda b, j, q: (g_row0 // tr + b * n1c + q, g_col0 + j)),
                  pl.BlockSpec((1, LANES), lambda b, j, q: (0, j))],
        out_specs=pl.BlockSpec((tr, LANES), lambda b, j, q: (b * n1c + q, j)),
        out_shape=jax.ShapeDtypeStruct((nb * n1c * tr, c), out_dtype),
        compiler_params=_cparams(("parallel", "parallel", "arbitrary")),
        name="hyena_gate",
    )(y, u, gate, bias)


def _np_split(a):
    a32 = np.asarray(a, np.float32)
    hi = a32.astype(BF16)
    lo = (a32 - hi.astype(np.float32)).astype(BF16)
    return hi, lo


def _fft_plan(l):
    n = 2 * l
    n2 = FFT_N2
    n1 = n // n2
    r = n1 // 2 + 1
    rh = FFT_RH
    kh = -(-r // rh)
    rp = kh * rh
    kn1 = n1 // 2
    k1 = np.arange(rp, dtype=np.float64)[:, None]
    live = (k1 < r).astype(np.float64)

    ang = 2 * np.pi * k1 * np.arange(kn1)[None, :] / n1
    f1k = np.concatenate([(np.cos(ang) * live).reshape(kh, rh, kn1),
                          (-np.sin(ang) * live).reshape(kh, rh, kn1)], axis=1)

    kk = np.arange(rp, dtype=np.float64)[None, :]
    wgt = np.where((kk == 0) | (kk == n1 // 2), 1.0, 2.0) * (kk < r) / n
    ango = 2 * np.pi * np.arange(n1 // 2)[:, None] * kk / n1
    gre = (np.cos(ango) * wgt).reshape(n1 // 2, kh, rh).transpose(1, 0, 2)
    gim = (-np.sin(ango) * wgt).reshape(n1 // 2, kh, rh).transpose(1, 0, 2)
    g = np.concatenate([gre, gim], axis=2)

    a2 = 2 * np.pi * np.outer(np.arange(n2), np.arange(n2)) / n2
    f2 = np.concatenate([np.cos(a2), -np.sin(a2)], axis=0)
    f2i = np.concatenate([np.cos(a2), np.sin(a2)], axis=0)

    idx = jnp.arange(rp, dtype=jnp.int32)[:, None] * jnp.arange(n2, dtype=jnp.int32)[None, :]
    ang = idx.astype(F32) * F32(2.0 * math.pi / n)
    tw = tuple(jnp.broadcast_to(t[:, :, None], (rp, n2, LANES)) for t in (jnp.cos(ang), -jnp.sin(ang)))
    return dict(n1=n1, n2=n2, r=r, rp=rp, kh=kh, rh=rh, f1k=_np_split(f1k), g=_np_split(g),
                f2=_np_split(f2)[0], f2i=_np_split(f2i)[0], tw=tw)


def _filter_features(l):
    t = jnp.linspace(0.0, 1.0, l, dtype=F32)[:, None]
    bands = jnp.linspace(1e-4, B_BANDS - 1, B_BANDS, dtype=F32)[None, :]
    w = 2.0 * math.pi * jnp.arange(l, dtype=F32)[:, None] / l
    return jnp.concatenate([t, jnp.cos(bands * w), -jnp.sin(bands * w)], axis=-1)


def _hyena_filters(plan, l, c, w1, b1, f1, w2, b2, f2, w3, b3):
    n1, n2 = plan["n1"], plan["n2"]
    z = _filter_features(l)
    z = z.reshape(n1 // 2, n2, z.shape[1]).transpose(1, 0, 2).reshape(z.shape)
    e = z.shape[1]
    ep = -(-e // 16) * 16
    z = jnp.pad(z, ((0, 0), (0, ep - e)))
    w1p = jnp.pad(w1, ((0, ep - e), (0, 0))).astype(BF16)
    max_decay = math.log(B_DECAY_TARGET) / B_FAST_DECAY_PCT
    min_decay = math.log(B_DECAY_TARGET) / B_SLOW_DECAY_PCT
    deltas = jnp.abs(jnp.linspace(min_decay, max_decay, c, dtype=F32))
    h, nrm = _filter_mlp(z, w1p, b1, f1, w2.astype(BF16), b2, f2, w3.astype(BF16), b3, deltas,
                         tl=min(512, l))
    oc = h.shape[1] // 2
    inv = 1.0 / (nrm[:, :oc] + nrm[:, oc:])
    b0 = h[0:1, oc:]
    return _filter_mid(h, inv, b0, plan)


def _hyena_conv(plan, kf, order, l, nb, u, u_row0, u_col0, gate, g_row0, g_col0, bias, c, out_dtype):
    n2 = plan["n2"]
    cb = c // LANES
    y = _conv_mid(u, u_row0 // l, u_col0 * cb, l, nb, c, kf[0], kf[1], order, plan)
    return _gate(y, n2, u, u_row0, u_col0 * cb, gate, g_row0, g_col0 * cb, bias.reshape(1, c), out_dtype)


def _hyena_mixer(u, seqs, plans, c, fw, hy_bias):
    outs = []
    for (row0, nb, l) in seqs:
        plan = plans[l]
        kf = _hyena_filters(plan, l, c, *fw)
        z = _hyena_conv(plan, kf, 0, l, nb, u, row0, 0, u, row0, 1, hy_bias[0], c, F32)
        o = _hyena_conv(plan, kf, 1, l, nb, z, 0, 0, u, row0, 2, hy_bias[1], c, BF16)
        outs.append(o)
    return jnp.concatenate(outs, axis=0)


def _rope_tables(pos, hd):
    rot = hd // ROPE_FRACTION
    half = rot // 2
    inv = ROPE_THETA ** (-(jnp.arange(half, dtype=F32) * 2.0 / rot))
    ang = pos[:, None] * inv[None, :]
    cos, sin = jnp.cos(ang), jnp.sin(ang)
    m = pos.shape[0]
    one = jnp.ones((m, hd - rot), F32)
    zero = jnp.zeros((m, hd - rot), F32)
    zh = jnp.zeros((m, half), F32)
    c = jnp.concatenate([cos, cos, one], axis=1)
    s1 = jnp.concatenate([-sin, zh, zero], axis=1)
    s2 = jnp.concatenate([zh, sin, zero], axis=1)
    rep = LANES // hd
    return tuple(jnp.tile(t, (1, rep)) for t in (c, s1, s2)), half


def _trunk(x, bounds, seqs, p):
    m, dm = x.shape
    pos = jnp.concatenate([jnp.tile(jnp.arange(l, dtype=F32), nb) for (_, nb, l) in seqs])
    tabs_a, half_a = _rope_tables(pos, A_HEAD_DIM)
    tabs_c, half_c = _rope_tables(pos, C_HEAD_DIM)
    ident = (jnp.ones((m, LANES), F32), jnp.zeros((m, LANES), F32), jnp.zeros((m, LANES), F32))
    tabs_kv = tuple(jnp.concatenate([a, b], axis=1) for a, b in zip(tabs_a, ident))
    c_hy = dm - A_WIDTH
    plans = {l: _fft_plan(l) for l in sorted({l for (_, _, l) in seqs})}
    xf = x
    xb = x.astype(BF16)
    for i in range(DEPTH):
        j = i // 2
        if i % 2 == 0:
            w_in = p['mix_e_w_in'][j].astype(BF16)
            kv0 = A_WIDTH
            hy0 = A_WIDTH + 2 * A_KV_WIDTH
            q = _matmul_rope(xb, w_in[:, :kv0], tabs_a, half_a, tn=512)
            kv = _matmul_rope(xb, w_in[:, kv0:hy0], tabs_kv, half_a, tn=2 * A_KV_WIDTH)
            u = _matmul_conv(xb, w_in[:, hy0:], p['hy_conv_w'][j], p['hy_conv_b'][j], bounds)
            a_out = _even_attention(q, kv, p['a_sink'][j], bounds)
            fw = (p['hy_w1'][j], p['hy_b1'][j], p['hy_f1'][j], p['hy_w2'][j], p['hy_b2'][j],
                  p['hy_f2'][j], p['hy_w3'][j], p['hy_b3'][j])
            h_out = _hyena_mixer(u, seqs, plans, c_hy, fw, p['hy_bias'][j])
            w_out = p['mix_e_w_out'][j].astype(BF16)
            xf, xb = _matmul_ln([a_out, h_out], [w_out[:A_WIDTH], w_out[A_WIDTH:]], xf,
                                p['ln1_g'][i], p['ln1_b'][i])
        else:
            w_in = p['mix_o_w_in'][j].astype(BF16)
            gw = C_HEADS * C_HEAD_DIM
            ng = len(C_DILATIONS)
            qkv = []
            for g, d in enumerate(C_DILATIONS):
                trio = []
                for part in range(3):
                    c0 = (part * ng + g) * gw
                    trio.append(_odd_proj(xb, w_in, c0, tabs_c, half_c, d, rope=part < 2))
                qkv.append(tuple(trio))
            o = _odd_attention(qkv, bounds, m)
            xf, xb = _matmul_ln([o], [p['mix_o_w_out'][j].astype(BF16)], xf, p['ln1_g'][i], p['ln1_b'][i])
        xf, xb = _ffn_ln(xb, xf, p['ffn_w_gate'][i].astype(BF16), p['ffn_w_up'][i].astype(BF16),
                         p['ffn_w_down'][i].astype(BF16), p['ln2_g'][i], p['ln2_b'][i])
    return xf


def kernel(x_prompt, x_sample, mix_e_w_in, a_sink, hy_conv_w, hy_conv_b, hy_w1, hy_b1, hy_f1, hy_w2, hy_b2,
           hy_f2, hy_w3, hy_b3, hy_bias, mix_e_w_out, mix_o_w_in, mix_o_w_out, ffn_w_gate, ffn_w_up,
           ffn_w_down, ln1_g, ln1_b, ln2_g, ln2_b):
    p = dict(mix_e_w_in=mix_e_w_in, a_sink=a_sink, hy_conv_w=hy_conv_w, hy_conv_b=hy_conv_b,
             hy_w1=hy_w1, hy_b1=hy_b1, hy_f1=hy_f1, hy_w2=hy_w2, hy_b2=hy_b2, hy_f2=hy_f2,
             hy_w3=hy_w3, hy_b3=hy_b3, hy_bias=hy_bias, mix_e_w_out=mix_e_w_out,
             mix_o_w_in=mix_o_w_in, mix_o_w_out=mix_o_w_out, ffn_w_gate=ffn_w_gate,
             ffn_w_up=ffn_w_up, ffn_w_down=ffn_w_down, ln1_g=ln1_g, ln1_b=ln1_b,
             ln2_g=ln2_g, ln2_b=ln2_b)
    dm = x_prompt.shape[-1]
    seqs, bounds, row = [], [0], 0
    for xs in (x_prompt, x_sample):
        nb, l = xs.shape[0], xs.shape[1]
        seqs.append((row, nb, l))
        for _ in range(nb):
            row += l
            bounds.append(row)
    x = jnp.concatenate([x_prompt.reshape(-1, dm), x_sample.reshape(-1, dm)], axis=0)
    y = _trunk(x, tuple(bounds), tuple(seqs), p)
    n_p = x_prompt.shape[0] * x_prompt.shape[1]
    return (y[:n_p].reshape(x_prompt.shape), y[n_p:].reshape(x_sample.shape))
```

```python
import functools
import math

import numpy as np
import jax
import jax.numpy as jnp
from jax import lax
from jax.experimental import pallas as pl
from jax.experimental.pallas import tpu as pltpu

F32 = jnp.float32
BF16 = jnp.bfloat16

DEPTH = 4
A_HEADS, A_KV_HEADS, A_HEAD_DIM, A_RADIUS = 16, 2, 64, 128
A_WIDTH = A_HEADS * A_HEAD_DIM
A_KV_WIDTH = A_KV_HEADS * A_HEAD_DIM
B_SHORT, B_EMB = 3, 33
B_BANDS = (B_EMB - 1) // 2
B_DECAY_TARGET, B_FAST_DECAY_PCT, B_SLOW_DECAY_PCT = 1e-2, 0.3, 1.5
C_HEADS, C_HEAD_DIM = 16, 128
C_DILATIONS = (1, 4, 16)
C_RADIUS = 64
ROPE_THETA, ROPE_FRACTION = 500000.0, 4
ALPHA = (2 * DEPTH) ** 0.25
LN_EPS = 1e-5

LANES = 128
VMEM_LIMIT = 56 * 1024 * 1024
FFT_N2 = 256

ODD_CHUNK = 2048
ATT_TQ = 256


def _cparams(sem):
    return pltpu.CompilerParams(dimension_semantics=sem, vmem_limit_bytes=VMEM_LIMIT)


def _seq_bounds(row, bounds):
    start = jnp.int32(bounds[0])
    end = jnp.int32(bounds[1])
    for b0, b1 in zip(bounds[1:-1], bounds[2:]):
        inside = row >= b0
        start = jnp.where(inside, jnp.int32(b0), start)
        end = jnp.where(inside, jnp.int32(b1), end)
    return start, end


def _rope(a, c, s1, s2, half):
    w = a.shape[-1]
    return a * c + pltpu.roll(a, w - half, 1) * s1 + pltpu.roll(a, half, 1) * s2


def _mm_conv_kernel(xp_ref, xm_ref, xn_ref, w_ref, cw_ref, cb_ref, o_ref, *, bounds):
    i = pl.program_id(0)
    tm, hb = xm_ref.shape[0], xp_ref.shape[0]
    rows = tm + 2 * hb
    row0 = i * tm
    start, end = _seq_bounds(row0, bounds)
    lhs = jnp.concatenate([xp_ref[...], xm_ref[...], xn_ref[...]], axis=0)
    acc = jnp.dot(lhs, w_ref[...], preferred_element_type=F32)
    h0 = acc[hb:hb + tm]
    hm = pltpu.roll(acc, 1, 0)[hb:hb + tm]
    hp = pltpu.roll(acc, rows - 1, 0)[hb:hb + tm]
    ridx = lax.broadcasted_iota(jnp.int32, h0.shape, 0)
    hm = jnp.where((ridx == 0) & (row0 <= start), 0.0, hm)
    hp = jnp.where((ridx == tm - 1) & (row0 + tm >= end), 0.0, hp)
    y = cb_ref[...] + hm * cw_ref[0:1, :]
    y = y + h0 * cw_ref[1:2, :]
    y = y + hp * cw_ref[2:3, :]
    o_ref[...] = y


def _matmul_conv(x, w, cw, cb, bounds, tn=512, tm=1024):
    m, k = x.shape
    n = w.shape[1]
    hb = 16
    per = tm // hb
    last = m // hb - 1
    return pl.pallas_call(
        functools.partial(_mm_conv_kernel, bounds=bounds),
        grid=(m // tm, n // tn),
        in_specs=[pl.BlockSpec((hb, k), lambda i, j: (jnp.maximum(i * per - 1, 0), 0)),
                  pl.BlockSpec((tm, k), lambda i, j: (i, 0)),
                  pl.BlockSpec((hb, k), lambda i, j: (jnp.minimum((i + 1) * per, last), 0)),
                  pl.BlockSpec((k, tn), lambda i, j: (0, j)),
                  pl.BlockSpec((B_SHORT, tn), lambda i, j: (0, j)),
                  pl.BlockSpec((1, tn), lambda i, j: (0, j))],
        out_specs=pl.BlockSpec((tm, tn), lambda i, j: (i, j)),
        out_shape=jax.ShapeDtypeStruct((m, n), F32),
        compiler_params=_cparams(("parallel", "arbitrary")),
        name="matmul_conv",
    )(x, x, x, w, cw, cb.reshape(1, n))


def _mm_rope_kernel(x_ref, w_ref, c_ref, s1_ref, s2_ref, o_ref, *, half):
    acc = jnp.dot(x_ref[...], w_ref[...], preferred_element_type=F32)
    tw = c_ref.shape[1]
    rc = 256
    for r0 in range(0, acc.shape[0], rc):
        rows = slice(r0, r0 + rc)
        for c in range(acc.shape[1] // tw):
            cols = slice(c * tw, (c + 1) * tw)
            o_ref[rows, cols] = _rope(acc[rows, cols], c_ref[rows, :], s1_ref[rows, :], s2_ref[rows, :],
                                      half).astype(o_ref.dtype)


def _matmul_rope(x, w, tabs, half, tn, tm=1024):
    m, k = x.shape
    n = w.shape[1]
    tw = tabs[0].shape[1]
    tab_spec = pl.BlockSpec((tm, tw), lambda i, j: (i, 0))
    return pl.pallas_call(
        functools.partial(_mm_rope_kernel, half=half),
        grid=(m // tm, n // tn),
        in_specs=[pl.BlockSpec((tm, k), lambda i, j: (i, 0)),
                  pl.BlockSpec((k, tn), lambda i, j: (0, j)),
                  tab_spec, tab_spec, tab_spec],
        out_specs=pl.BlockSpec((tm, tn), lambda i, j: (i, j)),
        out_shape=jax.ShapeDtypeStruct((m, n), BF16),
        compiler_params=_cparams(("parallel", "arbitrary")),
        name="matmul_rope",
    )(x, w, *tabs)


def _odd_proj_kernel(x_ref, w_ref, c_ref, s1_ref, s2_ref, o_ref, acc_ref, tmp_ref, *, d, rope, half):
    hps = w_ref.shape[1] // LANES
    tm = x_ref.shape[0]
    t = tm // d
    pair = 2
    rc = 256
    for p in range(hps // pair):
        acc = jnp.dot(x_ref[...], w_ref[:, p * pair * LANES:(p + 1) * pair * LANES], preferred_element_type=F32)
        for h2 in range(pair):
            hh = p * pair + h2
            slot = (p % 2) * pair + h2
            acc_ref[slot, :, :] = acc[:, h2 * LANES:(h2 + 1) * LANES]
            for c0 in range(0, tm, rc):
                a = acc_ref[slot, c0:c0 + rc, :]
                if rope:
                    a = _rope(a, c_ref[c0:c0 + rc, :], s1_ref[c0:c0 + rc, :], s2_ref[c0:c0 + rc, :], half)
                if d == 1:
                    o_ref[hh, 0, c0:c0 + rc, :] = a.astype(BF16)
                elif rope:
                    acc_ref[slot, c0:c0 + rc, :] = a
            if d == 16:
                ts = hh % 2
                for ra in range(4):
                    tmp_ref[ts, ra * (tm // 4):(ra + 1) * (tm // 4), :] = acc_ref[slot, pl.ds(ra, tm // 4, stride=4), :]
                for ra in range(4):
                    for rb in range(4):
                        o_ref[hh, ra + 4 * rb, :, :] = tmp_ref[ts, pl.ds(ra * (tm // 4) + rb, t, stride=4), :].astype(BF16)
            elif d > 1:
                for r in range(d):
                    o_ref[hh, r, :, :] = acc_ref[slot, pl.ds(r, t, stride=d), :].astype(BF16)


def _odd_proj(x, w, col0, tabs, half, d, rope, hps=8):
    m, k = x.shape
    tm = ODD_CHUNK
    cb0 = col0 // (hps * LANES)
    tab_spec = pl.BlockSpec((tm, LANES), lambda i, j: (i, 0))
    return pl.pallas_call(
        functools.partial(_odd_proj_kernel, d=d, rope=rope, half=half),
        grid=(m // tm, C_HEADS // hps),
        in_specs=[pl.BlockSpec((tm, k), lambda i, j: (i, 0)),
                  pl.BlockSpec((k, hps * LANES), lambda i, j: (0, cb0 + j)),
                  tab_spec, tab_spec, tab_spec],
        out_specs=pl.BlockSpec((hps, d, tm // d, LANES), lambda i, j: (j, 0, i, 0)),
        out_shape=jax.ShapeDtypeStruct((C_HEADS, d, m // d, LANES), BF16),
        scratch_shapes=[pltpu.VMEM((4, tm, LANES), F32), pltpu.VMEM((2, tm, LANES), F32)],
        compiler_params=_cparams(("parallel", "arbitrary")),
        name="odd_proj",
    )(x, w, *tabs)


LN_ROWS = 128


def _layer_norm_store(x_ref, acc_ref, g_ref, b_ref, of_ref, ob_ref, row0=0, nrows=None):
    nrows = x_ref.shape[0] if nrows is None else nrows
    for c in range(nrows // LN_ROWS):
        rows = pl.ds(row0 + c * LN_ROWS, LN_ROWS)
        r = ALPHA * x_ref[rows, :] + acc_ref[rows, :]
        mu = jnp.mean(r, axis=-1, keepdims=True)
        xc = r - mu
        var = jnp.mean(xc * xc, axis=-1, keepdims=True)
        y = xc * lax.rsqrt(var + LN_EPS) * g_ref[...] + b_ref[...]
        of_ref[rows, :] = y
        ob_ref[rows, :] = y.astype(BF16)


def _mm_ln_kernel(*refs, n_in):
    ys = refs[:n_in]
    ws = refs[n_in:2 * n_in]
    x_ref, g_ref, b_ref, of_ref, ob_ref, acc_ref = refs[2 * n_in:]
    half = x_ref.shape[0] // 2
    for r0 in (0, half):
        rows = slice(r0, r0 + half)
        acc = jnp.dot(ys[0][rows, :], ws[0][...], preferred_element_type=F32)
        for y_ref, w_ref in zip(ys[1:], ws[1:]):
            acc = acc + jnp.dot(y_ref[rows, :], w_ref[...], preferred_element_type=F32)
        acc_ref[rows, :] = acc
        _layer_norm_store(x_ref, acc_ref, g_ref, b_ref, of_ref, ob_ref, r0, half)


def _matmul_ln(ys, ws, x, g, b, tm=512):
    m, dm = x.shape
    n_in = len(ys)
    in_specs = [pl.BlockSpec((tm, y.shape[1]), lambda i: (i, 0)) for y in ys]
    in_specs += [pl.BlockSpec(w.shape, lambda i: (0, 0)) for w in ws]
    in_specs += [pl.BlockSpec((tm, dm), lambda i: (i, 0)),
                 pl.BlockSpec((1, dm), lambda i: (0, 0)),
                 pl.BlockSpec((1, dm), lambda i: (0, 0))]
    return pl.pallas_call(
        functools.partial(_mm_ln_kernel, n_in=n_in),
        grid=(m // tm,),
        in_specs=in_specs,
        out_specs=[pl.BlockSpec((tm, dm), lambda i: (i, 0)), pl.BlockSpec((tm, dm), lambda i: (i, 0))],
        out_shape=[jax.ShapeDtypeStruct((m, dm), F32), jax.ShapeDtypeStruct((m, dm), BF16)],
        scratch_shapes=[pltpu.VMEM((tm, dm), F32)],
        compiler_params=_cparams(("parallel",)),
        name="matmul_ln",
    )(*ys, *ws, x, g.reshape(1, dm), b.reshape(1, dm))


def _ffn_kernel(xb_ref, xf_ref, wg_ref, wu_ref, wd_ref, g_ref, b_ref, of_ref, ob_ref, acc_ref):
    j = pl.program_id(1)

    @pl.when(j == 0)
    def _():
        acc_ref[...] = jnp.zeros_like(acc_ref)

    xb = xb_ref[...]
    gate = jnp.dot(xb, wg_ref[...], preferred_element_type=F32)
    up = jnp.dot(xb, wu_ref[...], preferred_element_type=F32)
    h = (gate * jax.nn.sigmoid(gate)) * up
    acc_ref[...] += jnp.dot(h.astype(BF16), wd_ref[...], preferred_element_type=F32)

    @pl.when(j == pl.num_programs(1) - 1)
    def _():
        _layer_norm_store(xf_ref, acc_ref, g_ref, b_ref, of_ref, ob_ref)


def _ffn_ln(xb, xf, wg, wu, wd, g, b, tm=512, tf=512):
    m, dm = xf.shape
    dff = wg.shape[1]
    row = lambda i, j: (i, 0)
    return pl.pallas_call(
        _ffn_kernel,
        grid=(m // tm, dff // tf),
        in_specs=[pl.BlockSpec((tm, dm), row), pl.BlockSpec((tm, dm), row),
                  pl.BlockSpec((dm, tf), lambda i, j: (0, j)),
                  pl.BlockSpec((dm, tf), lambda i, j: (0, j)),
                  pl.BlockSpec((tf, dm), lambda i, j: (j, 0)),
                  pl.BlockSpec((1, dm), lambda i, j: (0, 0)),
                  pl.BlockSpec((1, dm), lambda i, j: (0, 0))],
        out_specs=[pl.BlockSpec((tm, dm), row), pl.BlockSpec((tm, dm), row)],
        out_shape=[jax.ShapeDtypeStruct((m, dm), F32), jax.ShapeDtypeStruct((m, dm), BF16)],
        scratch_shapes=[pltpu.VMEM((tm, dm), F32)],
        compiler_params=_cparams(("parallel", "arbitrary")),
        name="ffn_ln",
    )(xb, xf, wg, wu, wd, g.reshape(1, dm), b.reshape(1, dm))


def _band_bias(nq, radius):
    r = np.arange(nq)[:, None]
    c = np.arange(nq + 2 * radius)[None, :]
    return np.where(np.abs(c - radius - r) <= radius, 0.0, -np.inf).astype(np.float32)


def _even_attn_kernel(sink_ref, band_ref, q_ref, kp_ref, km_ref, kn_ref, o_ref, bias_ref, *, bounds):
    i = pl.program_id(0)
    tq = q_ref.shape[0]
    sub = A_RADIUS
    nk = sub + 2 * A_RADIUS
    row0 = i * tq
    start, end = _seq_bounds(row0, bounds)
    group = A_HEADS // A_KV_HEADS
    scale = A_HEAD_DIM ** -0.5
    blocks = [kp_ref, km_ref, kn_ref]
    assert tq == 2 * sub
    for sb in range(tq // sub):
        kv = jnp.concatenate([r[...] for r in blocks[sb:sb + 2]], axis=0)
        rk = row0 + sb * sub - A_RADIUS + lax.broadcasted_iota(jnp.int32, (1, nk), 1)
        bias = band_ref[...] + jnp.where((rk >= start) & (rk < end), 0.0, -jnp.inf)
        bias_ref[0:sub, :] = bias
        bias_ref[sub:2 * sub, :] = bias
        first = lax.broadcasted_iota(jnp.int32, (2 * sub, 1), 0) < sub
        rows = slice(sb * sub, (sb + 1) * sub)
        for j in range(A_KV_HEADS):
            k = kv[:, j * A_HEAD_DIM:(j + 1) * A_HEAD_DIM]
            v = kv[:, A_KV_WIDTH + j * A_HEAD_DIM:A_KV_WIDTH + (j + 1) * A_HEAD_DIM]
            for gq in range(0, group, 2):
                heads = (j * group + gq, j * group + gq + 1)
                cols = [slice(h * A_HEAD_DIM, (h + 1) * A_HEAD_DIM) for h in heads]
                qh = jnp.concatenate([q_ref[rows, c] for c in cols], axis=0)
                s = lax.dot_general(qh, k, (((1,), (1,)), ((), ())), preferred_element_type=F32) * scale
                s = s + bias_ref[...]
                sk = jnp.where(first, sink_ref[heads[0]], sink_ref[heads[1]])
                m = jnp.maximum(jnp.max(s, axis=-1, keepdims=True), sk)
                p = jnp.exp(s - m)
                den = jnp.sum(p, axis=-1, keepdims=True) + jnp.exp(sk - m)
                o = jnp.dot(p.astype(BF16), v, preferred_element_type=F32) / den
                o_ref[rows, cols[0]] = o[0:sub].astype(o_ref.dtype)
                o_ref[rows, cols[1]] = o[sub:2 * sub].astype(o_ref.dtype)


def _even_attention(q, kv, sink, bounds):
    m = q.shape[0]
    tq = ATT_TQ
    hb = A_RADIUS
    per = tq // hb
    last = m // hb - 1
    kvw = kv.shape[1]
    band = _band_bias(A_RADIUS, A_RADIUS)
    return pl.pallas_call(
        functools.partial(_even_attn_kernel, bounds=bounds),
        grid=(m // tq,),
        in_specs=[pl.BlockSpec(memory_space=pltpu.SMEM),
                  pl.BlockSpec(band.shape, lambda i: (0, 0)),
                  pl.BlockSpec((tq, A_WIDTH), lambda i: (i, 0)),
                  pl.BlockSpec((hb, kvw), lambda i: (jnp.maximum(i * per - 1, 0), 0)),
                  pl.BlockSpec((tq, kvw), lambda i: (i, 0)),
                  pl.BlockSpec((hb, kvw), lambda i: (jnp.minimum((i + 1) * per, last), 0))],
        out_specs=pl.BlockSpec((tq, A_WIDTH), lambda i: (i, 0)),
        out_shape=jax.ShapeDtypeStruct((m, A_WIDTH), BF16),
        scratch_shapes=[pltpu.VMEM((2 * band.shape[0], band.shape[1]), F32)],
        compiler_params=_cparams(("parallel",)),
        name="even_attention",
    )(sink, band, q, kv, kv, kv)


def _odd_attn_kernel(*refs, bounds):
    ng = len(C_DILATIONS)
    band_ref = refs[7 * ng]
    o_ref, oacc, lacc = refs[7 * ng + 1:]
    i = pl.program_id(0)
    chunk = ODD_CHUNK
    qb = 128
    nk = qb + 2 * C_RADIUS
    row0 = i * chunk
    start, end = _seq_bounds(row0, bounds)
    cc = lax.broadcasted_iota(jnp.int32, (1, nk), 1)
    scale = C_HEAD_DIM ** -0.5
    for g, d in enumerate(C_DILATIONS):
        q_ref, kp_ref, km_ref, kn_ref, vp_ref, vm_ref, vn_ref = refs[7 * g:7 * g + 7]
        tg = chunk // d
        t_lo, t_hi, t_c0 = start // d, end // d, row0 // d
        for sb in range(tg // qb):
            lo, hi = qb * sb - C_RADIUS, qb * sb + qb + C_RADIUS
            tk = t_c0 + lo + cc
            col = jnp.where((tk >= t_lo) & (tk < t_hi), 0.0, -jnp.inf)
            for r in range(d):
                def window(p_ref, m_ref, n_ref):
                    parts = []
                    if lo < 0:
                        parts.append(p_ref[0, r, :, :])
                    parts.append(m_ref[0, r, max(lo, 0):min(hi, tg), :])
                    if hi > tg:
                        parts.append(n_ref[0, r, :, :])
                    return parts[0] if len(parts) == 1 else jnp.concatenate(parts, axis=0)

                q = q_ref[0, r, qb * sb:qb * (sb + 1), :]
                k = window(kp_ref, km_ref, kn_ref)
                v = window(vp_ref, vm_ref, vn_ref)
                s = lax.dot_general(q, k, (((1,), (1,)), ((), ())), preferred_element_type=F32) * scale
                s = s + band_ref[...] + col
                m = jnp.max(s, axis=-1, keepdims=True)
                p = jnp.exp(s - m)
                den = jnp.sum(p, axis=-1, keepdims=True)
                o = jnp.dot(p.astype(BF16), v, preferred_element_type=F32) / den
                lse = jnp.broadcast_to(m + jnp.log(den), (qb, LANES))
                if d == 1:
                    rows = pl.ds(qb * sb, qb)
                else:
                    rows = pl.ds(r + d * qb * sb, qb, stride=d)
                oacc[g, rows, :] = o
                lacc[g, rows, :] = lse
    ls = [lacc[g] for g in range(ng)]
    mx = functools.reduce(jnp.maximum, ls)
    ws = [jnp.exp(l - mx) for l in ls]
    tot = functools.reduce(lambda a, b: a + b, ws)
    out = functools.reduce(lambda a, b: a + b, [(ws[g] / tot) * oacc[g] for g in range(ng)])
    o_ref[...] = out.astype(o_ref.dtype)


def _odd_attention(qkv, bounds, m):
    chunk = ODD_CHUNK
    hb = C_RADIUS
    operands, in_specs = [], []
    for (q, k, v), d in zip(qkv, C_DILATIONS):
        tg = chunk // d
        per = tg // hb
        last = m // d // hb - 1
        main = pl.BlockSpec((1, d, tg, LANES), lambda i, h: (h, 0, i, 0))
        prev = pl.BlockSpec((1, d, hb, LANES), lambda i, h, per=per: (h, 0, jnp.maximum(i * per - 1, 0), 0))
        nxt = pl.BlockSpec((1, d, hb, LANES), lambda i, h, per=per, last=last: (h, 0, jnp.minimum((i + 1) * per, last), 0))
        operands += [q, k, k, k, v, v, v]
        in_specs += [main, prev, main, nxt, prev, main, nxt]
    band = _band_bias(128, C_RADIUS)
    operands.append(band)
    in_specs.append(pl.BlockSpec(band.shape, lambda i, h: (0, 0)))
    ng = len(C_DILATIONS)
    return pl.pallas_call(
        functools.partial(_odd_attn_kernel, bounds=bounds),
        grid=(m // chunk, C_HEADS),
        in_specs=in_specs,
        out_specs=pl.BlockSpec((chunk, LANES), lambda i, h: (i, h)),
        out_shape=jax.ShapeDtypeStruct((m, C_HEADS * C_HEAD_DIM), BF16),
        scratch_shapes=[pltpu.VMEM((ng, chunk, LANES), F32), pltpu.VMEM((ng, chunk, LANES), F32)],
        compiler_params=_cparams(("parallel", "arbitrary")),
        name="odd_attention",
    )(*operands)


def _filter_mlp_kernel(z_ref, w1_ref, b1_ref, f1_ref, w2_ref, b2_ref, f2_ref, w3_ref, b3_ref, dl_ref,
                       h_ref, nrm_ref):
    i = pl.program_id(0)
    z = z_ref[...]
    h = jnp.sin(f1_ref[...] * (jnp.dot(z.astype(BF16), w1_ref[...], preferred_element_type=F32) + b1_ref[...]))
    h = jnp.sin(f2_ref[...] * (jnp.dot(h.astype(BF16), w2_ref[...], preferred_element_type=F32) + b2_ref[...]))
    h = jnp.dot(h.astype(BF16), w3_ref[...], preferred_element_type=F32) + b3_ref[...]
    decay = jnp.exp(-z[:, 0:1] * dl_ref[...])
    nrep = h.shape[1] // decay.shape[1]
    h = h * jnp.concatenate([decay] * nrep, axis=1)
    h_ref[...] = h

    @pl.when(i == 0)
    def _():
        nrm_ref[...] = jnp.zeros_like(nrm_ref)

    half = h.shape[1] // 2
    col = lax.broadcasted_iota(jnp.int32, h.shape, 1)
    row = lax.broadcasted_iota(jnp.int32, h.shape, 0) + i * h.shape[0]
    a = jnp.where((col >= half) & (row == 0), 0.0, jnp.abs(h))
    nrm_ref[...] += jnp.sum(a, axis=0, keepdims=True)


def _filter_mlp(z, w1, b1, f1, w2, b2, f2, w3, b3, deltas, tl=512):
    l, e = z.shape
    hid = w1.shape[1]
    n = w3.shape[1]
    c = deltas.shape[0]
    full = lambda shape: pl.BlockSpec(shape, lambda i: (0, 0))
    return pl.pallas_call(
        _filter_mlp_kernel,
        grid=(l // tl,),
        in_specs=[pl.BlockSpec((tl, e), lambda i: (i, 0)),
                  full((e, hid)), full((1, hid)), full((1, hid)),
                  full((hid, hid)), full((1, hid)), full((1, hid)),
                  full((hid, n)), full((1, n)), full((1, c))],
        out_specs=[pl.BlockSpec((tl, n), lambda i: (i, 0)), full((1, n))],
        out_shape=[jax.ShapeDtypeStruct((l, n), F32), jax.ShapeDtypeStruct((1, n), F32)],
        compiler_params=_cparams(("arbitrary",)),
        name="filter_mlp",
    )(z, w1, b1.reshape(1, hid), f1.reshape(1, hid), w2, b2.reshape(1, hid), f2.reshape(1, hid),
      w3, b3.reshape(1, n), deltas.reshape(1, c))


def _dot1(ch, x):
    return jnp.dot(ch, x.astype(BF16), preferred_element_type=F32)


FFT_GROUP = 8


FFT_RH = 24
FFT_SPB = 6


def _outer_dft_chunk(x_ref, f_ref, ar_ref, ai_ref, rh, n2, gather):
    kn1 = f_ref.shape[1]
    for g in range(n2 // FFT_GROUP):
        cols = []
        for s in range(FFT_GROUP):
            i2 = g * FFT_GROUP + s
            cols.append(x_ref[pl.ds(i2, kn1, stride=n2), :] if gather else x_ref[i2 * kn1:(i2 + 1) * kn1, :])
        out = _dot1(f_ref[...], jnp.concatenate(cols, axis=1))
        for s in range(FFT_GROUP):
            r0 = (g * FFT_GROUP + s) * rh
            ar_ref[r0:r0 + rh, :] = out[:rh, s * LANES:(s + 1) * LANES]
            ai_ref[r0:r0 + rh, :] = out[rh:, s * LANES:(s + 1) * LANES]


def _cpair(p, n):
    return p[:n, :LANES] - p[n:, LANES:], p[:n, LANES:] + p[n:, :LANES]


def _twiddle(xr, xi, tr, ti):
    return xr * tr - xi * ti, xr * ti + xi * tr


def _filter_mid_kernel(hf_ref, hb_ref, f_ref, tr_ref, ti_ref, w_ref, inv_ref, b0_ref,
                       kr_ref, ki_ref, fr_ref, fi_ref, br_ref, bi_ref, *, nslab, rh, n2):
    k0 = pl.program_id(2) * FFT_SPB

    @pl.when(pl.program_id(2) == 0)
    def _():
        _outer_dft_chunk(hf_ref, f_ref, fr_ref, fi_ref, rh, n2, gather=False)
        _outer_dft_chunk(hb_ref, f_ref, br_ref, bi_ref, rh, n2, gather=False)

    @pl.when(pl.program_id(1) * rh + k0 < nslab)
    def _():
        for t in range(FFT_SPB):
            rows = pl.ds(k0 + t, n2, stride=rh)
            tr, ti = tr_ref[t], ti_ref[t]
            x4 = jnp.concatenate(_twiddle(fr_ref[rows, :], fi_ref[rows, :], tr, ti)
                                 + _twiddle(br_ref[rows, :], bi_ref[rows, :], tr, ti), axis=1)
            p = _dot1(w_ref[...], x4)
            fr, fi = _cpair(p[:, :2 * LANES], n2)
            br, bi = _cpair(p[:, 2 * LANES:], n2)
            kr_ref[t] = (fr + (br - b0_ref[...])) * inv_ref[...]
            ki_ref[t] = (fi - bi) * inv_ref[...]

    @pl.when(pl.program_id(1) * rh + k0 >= nslab)
    def _():
        kr_ref[...] = jnp.zeros_like(kr_ref)
        ki_ref[...] = jnp.zeros_like(ki_ref)


def _filter_mid(h, inv, b0, plan):
    n2, kh, rh, rp = plan["n2"], plan["kh"], plan["rh"], plan["rp"]
    l = h.shape[0]
    oc = h.shape[1] // 2
    nj = oc // LANES
    f1k = plan["f1k"]
    fwd = pl.BlockSpec((l, LANES), lambda j, kk, k: (0, j))
    bwd = pl.BlockSpec((l, LANES), lambda j, kk, k: (0, j + nj))
    fspec = pl.BlockSpec((None,) + f1k.shape[1:], lambda j, kk, k: (kk, 0, 0))
    steps = rh // FFT_SPB
    tspec = pl.BlockSpec((FFT_SPB, n2, LANES), lambda j, kk, k: (kk * steps + k, 0, 0))
    wspec = pl.BlockSpec((2 * n2, n2), lambda j, kk, k: (0, 0))
    vec = pl.BlockSpec((1, LANES), lambda j, kk, k: (0, j))
    ospec = pl.BlockSpec((FFT_SPB, n2, LANES), lambda j, kk, k: (kk * steps + k, 0, j))
    out = jax.ShapeDtypeStruct((rp, n2, oc), F32)
    return pl.pallas_call(
        functools.partial(_filter_mid_kernel, nslab=plan["r"], rh=rh, n2=n2),
        grid=(nj, kh, steps),
        in_specs=[fwd, bwd, fspec, tspec, tspec, wspec, vec, vec],
        out_specs=[ospec, ospec],
        out_shape=[out, out],
        scratch_shapes=[pltpu.VMEM((n2 * rh, LANES), F32) for _ in range(4)],
        compiler_params=_cparams(("parallel", "arbitrary", "arbitrary")),
        name="filter_mid",
    )(h, h, f1k, *plan["tw"], plan["f2"], inv, b0)


def _conv_mid_kernel(x_ref, f_ref, tr_ref, ti_ref, wf_ref, wi_ref, kr_ref, ki_ref, g_ref,
                     y_ref, ar_ref, ai_ref, dr_ref, di_ref, *, nslab, rh, n2):
    k0 = pl.program_id(3) * FFT_SPB

    @pl.when((pl.program_id(2) == 0) & (pl.program_id(3) == 0))
    def _():
        y_ref[...] = jnp.zeros_like(y_ref)

    @pl.when(pl.program_id(3) == 0)
    def _():
        _outer_dft_chunk(x_ref, f_ref, ar_ref, ai_ref, rh, n2, gather=True)

    @pl.when(pl.program_id(2) * rh + k0 < nslab)
    def _():
        for t in range(FFT_SPB):
            rows = pl.ds(k0 + t, n2, stride=rh)
            tr, ti = tr_ref[t], ti_ref[t]
            x2 = jnp.concatenate(_twiddle(ar_ref[rows, :], ai_ref[rows, :], tr, ti), axis=1)
            xr, xi = _cpair(_dot1(wf_ref[...], x2), n2)
            kr, ki = kr_ref[t], ki_ref[t]
            y2 = jnp.concatenate([xr * kr - xi * ki, xr * ki + xi * kr], axis=1)
            cr, ci = _cpair(_dot1(wi_ref[...], y2), n2)
            dr, di = _twiddle(cr, ci, tr, -ti)
            dr_ref[rows, :] = dr
            di_ref[rows, :] = di

    @pl.when(pl.program_id(2) * rh + k0 >= nslab)
    def _():
        for t in range(FFT_SPB):
            rows = pl.ds(k0 + t, n2, stride=rh)
            dr_ref[rows, :] = jnp.zeros((n2, LANES), F32)
            di_ref[rows, :] = jnp.zeros((n2, LANES), F32)

    @pl.when(pl.program_id(3) == pl.num_programs(3) - 1)
    def _():
        n1c = y_ref.shape[0]
        n1r = y_ref.shape[1] // n2
        for g in range(n2 // FFT_GROUP):
            cols = []
            for s in range(FFT_GROUP):
                r0 = (g * FFT_GROUP + s) * rh
                cols.append(jnp.concatenate([dr_ref[r0:r0 + rh, :], di_ref[r0:r0 + rh, :]], axis=0))
            y = _dot1(g_ref[...], jnp.concatenate(cols, axis=1))
            for c in range(n1c):
                for s in range(FFT_GROUP):
                    t0 = (g * FFT_GROUP + s) * n1r
                    y_ref[c, t0:t0 + n1r, :] += y[c * n1r:(c + 1) * n1r, s * LANES:(s + 1) * LANES]


def _conv_mid(x, row_blk0, col_blk0, l, nb, c, kr, ki, order, plan):
    n2, rh, kh = plan["n2"], plan["rh"], plan["kh"]
    f1k, g = plan["f1k"], plan["g"]
    nh = g.shape[1]
    n1r = min(16, nh)
    n1c = nh // n1r
    koff = order * (c // LANES)
    xspec = pl.BlockSpec((l, LANES), lambda b, j, kk, k: (row_blk0 + b, col_blk0 + j))
    fspec = pl.BlockSpec((None,) + f1k.shape[1:], lambda b, j, kk, k: (kk, 0, 0))
    steps = rh // FFT_SPB
    tspec = pl.BlockSpec((FFT_SPB, n2, LANES), lambda b, j, kk, k: (kk * steps + k, 0, 0))
    wspec = pl.BlockSpec((2 * n2, n2), lambda b, j, kk, k: (0, 0))
    kspec = pl.BlockSpec((FFT_SPB, n2, LANES), lambda b, j, kk, k: (kk * steps + k, 0, j + koff))
    gspec = pl.BlockSpec((None, nh, 2 * rh), lambda b, j, kk, k: (kk, 0, 0))
    return pl.pallas_call(
        functools.partial(_conv_mid_kernel, nslab=plan["r"], rh=rh, n2=n2),
        grid=(nb, c // LANES, kh, steps),
        in_specs=[xspec, fspec, tspec, tspec, wspec, wspec, kspec, kspec, gspec],
        out_specs=pl.BlockSpec((None, n1c, n2 * n1r, LANES), lambda b, j, kk, k: (b, 0, 0, j)),
        out_shape=jax.ShapeDtypeStruct((nb, n1c, n2 * n1r, c), F32),
        scratch_shapes=[pltpu.VMEM((n2 * rh, LANES), F32) for _ in range(4)],
        compiler_params=_cparams(("parallel", "parallel", "arbitrary", "arbitrary")),
        name="conv_mid",
    )(x, f1k, *plan["tw"], plan["f2"], plan["f2i"], kr, ki, g)


def _gate_kernel(y_ref, u_ref, g_ref, bias_ref, o_ref, *, n2):
    n1r = y_ref.shape[0] // n2
    for a in range(n1r):
        rows = slice(a * n2, (a + 1) * n2)
        yt = y_ref[pl.ds(a, n2, stride=n1r), :]
        o_ref[rows, :] = (g_ref[rows, :] * (yt + u_ref[rows, :] * bias_ref[...])).astype(o_ref.dtype)


def _gate(y, n2, u, u_row0, u_col0, gate, g_row0, g_col0, bias, out_dtype):
    nb, n1c, yr, c = y.shape
    tr = yr
    return pl.pallas_call(
        functools.partial(_gate_kernel, n2=n2),
        grid=(nb, c // LANES, n1c),
        in_specs=[pl.BlockSpec((None, None, yr, LANES), lambda b, j, q: (b, q, 0, j)),
                  pl.BlockSpec((tr, LANES), lambda b, j, q: (u_row0 // tr + b * n1c + q, u_col0 + j)),
                  pl.BlockSpec((tr, LANES), lambda b, j, q: (g_row0 // tr + b * n1c + q, g_col0 + j)),
                  pl.BlockSpec((1, LANES), lambda b, j, q: (0, j))],
        out_specs=pl.BlockSpec((tr, LANES), lambda b, j, q: (b * n1c + q, j)),
        out_shape=jax.ShapeDtypeStruct((nb * n1c * tr, c), out_dtype),
        compiler_params=_cparams(("parallel", "parallel", "arbitrary")),
        name="hyena_gate",
    )(y, u, gate, bias)


def _np_bf16(a):
    return np.asarray(a, np.float32).astype(BF16)


def _fft_plan(l):
    n = 2 * l
    n2 = FFT_N2
    n1 = n // n2
    r = n1 // 2 + 1
    rh = FFT_RH
    kh = -(-r // rh)
    rp = kh * rh
    kn1 = n1 // 2
    k1 = np.arange(rp, dtype=np.float64)[:, None]
    live = (k1 < r).astype(np.float64)

    ang = 2 * np.pi * k1 * np.arange(kn1)[None, :] / n1
    f1k = np.concatenate([(np.cos(ang) * live).reshape(kh, rh, kn1),
                          (-np.sin(ang) * live).reshape(kh, rh, kn1)], axis=1)

    kk = np.arange(rp, dtype=np.float64)[None, :]
    wgt = np.where((kk == 0) | (kk == n1 // 2), 1.0, 2.0) * (kk < r) / n
    ango = 2 * np.pi * np.arange(n1 // 2)[:, None] * kk / n1
    gre = (np.cos(ango) * wgt).reshape(n1 // 2, kh, rh).transpose(1, 0, 2)
    gim = (-np.sin(ango) * wgt).reshape(n1 // 2, kh, rh).transpose(1, 0, 2)
    g = np.concatenate([gre, gim], axis=2)

    a2 = 2 * np.pi * np.outer(np.arange(n2), np.arange(n2)) / n2
    f2 = np.concatenate([np.cos(a2), -np.sin(a2)], axis=0)
    f2i = np.concatenate([np.cos(a2), np.sin(a2)], axis=0)

    idx = jnp.arange(rp, dtype=jnp.int32)[:, None] * jnp.arange(n2, dtype=jnp.int32)[None, :]
    ang = idx.astype(F32) * F32(2.0 * math.pi / n)
    tw = tuple(jnp.broadcast_to(t[:, :, None], (rp, n2, LANES)) for t in (jnp.cos(ang), -jnp.sin(ang)))
    return dict(n1=n1, n2=n2, r=r, rp=rp, kh=kh, rh=rh, f1k=_np_bf16(f1k), g=_np_bf16(g),
                f2=_np_bf16(f2), f2i=_np_bf16(f2i), tw=tw)


def _filter_features(l):
    t = jnp.linspace(0.0, 1.0, l, dtype=F32)[:, None]
    bands = jnp.linspace(1e-4, B_BANDS - 1, B_BANDS, dtype=F32)[None, :]
    w = 2.0 * math.pi * jnp.arange(l, dtype=F32)[:, None] / l
    return jnp.concatenate([t, jnp.cos(bands * w), -jnp.sin(bands * w)], axis=-1)


def _hyena_filters(plan, l, c, w1, b1, f1, w2, b2, f2, w3, b3):
    n1, n2 = plan["n1"], plan["n2"]
    z = _filter_features(l)
    z = z.reshape(n1 // 2, n2, z.shape[1]).transpose(1, 0, 2).reshape(z.shape)
    e = z.shape[1]
    ep = -(-e // 16) * 16
    z = jnp.pad(z, ((0, 0), (0, ep - e)))
    w1p = jnp.pad(w1, ((0, ep - e), (0, 0))).astype(BF16)
    max_decay = math.log(B_DECAY_TARGET) / B_FAST_DECAY_PCT
    min_decay = math.log(B_DECAY_TARGET) / B_SLOW_DECAY_PCT
    deltas = jnp.abs(jnp.linspace(min_decay, max_decay, c, dtype=F32))
    h, nrm = _filter_mlp(z, w1p, b1, f1, w2.astype(BF16), b2, f2, w3.astype(BF16), b3, deltas,
                         tl=min(512, l))
    oc = h.shape[1] // 2
    inv = 1.0 / (nrm[:, :oc] + nrm[:, oc:])
    b0 = h[0:1, oc:]
    return _filter_mid(h, inv, b0, plan)


def _hyena_conv(plan, kf, order, l, nb, u, u_row0, u_col0, gate, g_row0, g_col0, bias, c, out_dtype):
    n2 = plan["n2"]
    cb = c // LANES
    y = _conv_mid(u, u_row0 // l, u_col0 * cb, l, nb, c, kf[0], kf[1], order, plan)
    return _gate(y, n2, u, u_row0, u_col0 * cb, gate, g_row0, g_col0 * cb, bias.reshape(1, c), out_dtype)


def _hyena_mixer(u, seqs, plans, c, fw, hy_bias):
    outs = []
    for (row0, nb, l) in seqs:
        plan = plans[l]
        kf = _hyena_filters(plan, l, c, *fw)
        z = _hyena_conv(plan, kf, 0, l, nb, u, row0, 0, u, row0, 1, hy_bias[0], c, F32)
        o = _hyena_conv(plan, kf, 1, l, nb, z, 0, 0, u, row0, 2, hy_bias[1], c, BF16)
        outs.append(o)
    return jnp.concatenate(outs, axis=0)


def _rope_tables(pos, hd):
    rot = hd // ROPE_FRACTION
    half = rot // 2
    inv = ROPE_THETA ** (-(jnp.arange(half, dtype=F32) * 2.0 / rot))
    ang = pos[:, None] * inv[None, :]
    cos, sin = jnp.cos(ang), jnp.sin(ang)
    m = pos.shape[0]
    one = jnp.ones((m, hd - rot), F32)
    zero = jnp.zeros((m, hd - rot), F32)
    zh = jnp.zeros((m, half), F32)
    c = jnp.concatenate([cos, cos, one], axis=1)
    s1 = jnp.concatenate([-sin, zh, zero], axis=1)
    s2 = jnp.concatenate([zh, sin, zero], axis=1)
    rep = LANES // hd
    return tuple(jnp.tile(t, (1, rep)) for t in (c, s1, s2)), half


def _trunk(x, bounds, seqs, p):
    m, dm = x.shape
    pos = jnp.concatenate([jnp.tile(jnp.arange(l, dtype=F32), nb) for (_, nb, l) in seqs])
    tabs_a, half_a = _rope_tables(pos, A_HEAD_DIM)
    tabs_c, half_c = _rope_tables(pos, C_HEAD_DIM)
    ident = (jnp.ones((m, LANES), F32), jnp.zeros((m, LANES), F32), jnp.zeros((m, LANES), F32))
    tabs_kv = tuple(jnp.concatenate([a, b], axis=1) for a, b in zip(tabs_a, ident))
    c_hy = dm - A_WIDTH
    plans = {l: _fft_plan(l) for l in sorted({l for (_, _, l) in seqs})}
    xf = x
    xb = x.astype(BF16)
    for i in range(DEPTH):
        j = i // 2
        if i % 2 == 0:
            w_in = p['mix_e_w_in'][j].astype(BF16)
            kv0 = A_WIDTH
            hy0 = A_WIDTH + 2 * A_KV_WIDTH
            q = _matmul_rope(xb, w_in[:, :kv0], tabs_a, half_a, tn=512)
            kv = _matmul_rope(xb, w_in[:, kv0:hy0], tabs_kv, half_a, tn=2 * A_KV_WIDTH)
            u = _matmul_conv(xb, w_in[:, hy0:], p['hy_conv_w'][j], p['hy_conv_b'][j], bounds)
            a_out = _even_attention(q, kv, p['a_sink'][j], bounds)
            fw = (p['hy_w1'][j], p['hy_b1'][j], p['hy_f1'][j], p['hy_w2'][j], p['hy_b2'][j],
                  p['hy_f2'][j], p['hy_w3'][j], p['hy_b3'][j])
            h_out = _hyena_mixer(u, seqs, plans, c_hy, fw, p['hy_bias'][j])
            w_out = p['mix_e_w_out'][j].astype(BF16)
            xf, xb = _matmul_ln([a_out, h_out], [w_out[:A_WIDTH], w_out[A_WIDTH:]], xf,
                                p['ln1_g'][i], p['ln1_b'][i])
        else:
            w_in = p['mix_o_w_in'][j].astype(BF16)
            gw = C_HEADS * C_HEAD_DIM
            ng = len(C_DILATIONS)
            qkv = []
            for g, d in enumerate(C_DILATIONS):
                trio = []
                for part in range(3):
                    c0 = (part * ng + g) * gw
                    trio.append(_odd_proj(xb, w_in, c0, tabs_c, half_c, d, rope=part < 2))
                qkv.append(tuple(trio))
            o = _odd_attention(qkv, bounds, m)
            xf, xb = _matmul_ln([o], [p['mix_o_w_out'][j].astype(BF16)], xf, p['ln1_g'][i], p['ln1_b'][i])
        xf, xb = _ffn_ln(xb, xf, p['ffn_w_gate'][i].astype(BF16), p['ffn_w_up'][i].astype(BF16),
                         p['ffn_w_down'][i].astype(BF16), p['ln2_g'][i], p['ln2_b'][i])
    return xf


def kernel(x_prompt, x_sample, mix_e_w_in, a_sink, hy_conv_w, hy_conv_b, hy_w1, hy_b1, hy_f1, hy_w2, hy_b2,
           hy_f2, hy_w3, hy_b3, hy_bias, mix_e_w_out, mix_o_w_in, mix_o_w_out, ffn_w_gate, ffn_w_up,
           ffn_w_down, ln1_g, ln1_b, ln2_g, ln2_b):
    p = dict(mix_e_w_in=mix_e_w_in, a_sink=a_sink, hy_conv_w=hy_conv_w, hy_conv_b=hy_conv_b,
             hy_w1=hy_w1, hy_b1=hy_b1, hy_f1=hy_f1, hy_w2=hy_w2, hy_b2=hy_b2, hy_f2=hy_f2,
             hy_w3=hy_w3, hy_b3=hy_b3, hy_bias=hy_bias, mix_e_w_out=mix_e_w_out,
             mix_o_w_in=mix_o_w_in, mix_o_w_out=mix_o_w_out, ffn_w_gate=ffn_w_gate,
             ffn_w_up=ffn_w_up, ffn_w_down=ffn_w_down, ln1_g=ln1_g, ln1_b=ln1_b,
             ln2_g=ln2_g, ln2_b=ln2_b)
    dm = x_prompt.shape[-1]
    seqs, bounds, row = [], [0], 0
    for xs in (x_prompt, x_sample):
        nb, l = xs.shape[0], xs.shape[1]
        seqs.append((row, nb, l))
        for _ in range(nb):
            row += l
            bounds.append(row)
    x = jnp.concatenate([x_prompt.reshape(-1, dm), x_sample.reshape(-1, dm)], axis=0)
    y = _trunk(x, tuple(bounds), tuple(seqs), p)
    n_p = x_prompt.shape[0] * x_prompt.shape[1]
    return (y[:n_p].reshape(x_prompt.shape), y[n_p:].reshape(x_sample.shape))
```

```python
import functools
import math

import numpy as np
import jax
import jax.numpy as jnp
from jax import lax
from jax.experimental import pallas as pl
from jax.experimental.pallas import tpu as pltpu

F32 = jnp.float32
BF16 = jnp.bfloat16

DEPTH = 4
A_HEADS, A_KV_HEADS, A_HEAD_DIM, A_RADIUS = 16, 2, 64, 128
A_WIDTH = A_HEADS * A_HEAD_DIM
A_KV_WIDTH = A_KV_HEADS * A_HEAD_DIM
B_SHORT, B_EMB = 3, 33
B_BANDS = (B_EMB - 1) // 2
B_DECAY_TARGET, B_FAST_DECAY_PCT, B_SLOW_DECAY_PCT = 1e-2, 0.3, 1.5
C_HEADS, C_HEAD_DIM = 16, 128
C_DILATIONS = (1, 4, 16)
C_RADIUS = 64
ROPE_THETA, ROPE_FRACTION = 500000.0, 4
ALPHA = (2 * DEPTH) ** 0.25
LN_EPS = 1e-5

LANES = 128
VMEM_LIMIT = 56 * 1024 * 1024
FFT_N2 = 256

ODD_CHUNK = 2048
ATT_TQ = 256


def _cparams(sem):
    return pltpu.CompilerParams(dimension_semantics=sem, vmem_limit_bytes=VMEM_LIMIT)


def _seq_bounds(row, bounds):
    start = jnp.int32(bounds[0])
    end = jnp.int32(bounds[1])
    for b0, b1 in zip(bounds[1:-1], bounds[2:]):
        inside = row >= b0
        start = jnp.where(inside, jnp.int32(b0), start)
        end = jnp.where(inside, jnp.int32(b1), end)
    return start, end


def _rope(a, c, s1, s2, half):
    w = a.shape[-1]
    return a * c + pltpu.roll(a, w - half, 1) * s1 + pltpu.roll(a, half, 1) * s2


def _mm_conv_kernel(xp_ref, xm_ref, xn_ref, w_ref, cw_ref, cb_ref, o_ref, *, bounds):
    i = pl.program_id(0)
    tm, hb = xm_ref.shape[0], xp_ref.shape[0]
    rows = tm + 2 * hb
    row0 = i * tm
    start, end = _seq_bounds(row0, bounds)
    lhs = jnp.concatenate([xp_ref[...], xm_ref[...], xn_ref[...]], axis=0)
    acc = jnp.dot(lhs, w_ref[...], preferred_element_type=F32)
    h0 = acc[hb:hb + tm]
    hm = pltpu.roll(acc, 1, 0)[hb:hb + tm]
    hp = pltpu.roll(acc, rows - 1, 0)[hb:hb + tm]
    ridx = lax.broadcasted_iota(jnp.int32, h0.shape, 0)
    hm = jnp.where((ridx == 0) & (row0 <= start), 0.0, hm)
    hp = jnp.where((ridx == tm - 1) & (row0 + tm >= end), 0.0, hp)
    y = cb_ref[...] + hm * cw_ref[0:1, :]
    y = y + h0 * cw_ref[1:2, :]
    y = y + hp * cw_ref[2:3, :]
    o_ref[...] = y


def _matmul_conv(x, w, cw, cb, bounds, tn=512, tm=1024):
    m, k = x.shape
    n = w.shape[1]
    hb = 16
    per = tm // hb
    last = m // hb - 1
    return pl.pallas_call(
        functools.partial(_mm_conv_kernel, bounds=bounds),
        grid=(m // tm, n // tn),
        in_specs=[pl.BlockSpec((hb, k), lambda i, j: (jnp.maximum(i * per - 1, 0), 0)),
                  pl.BlockSpec((tm, k), lambda i, j: (i, 0)),
                  pl.BlockSpec((hb, k), lambda i, j: (jnp.minimum((i + 1) * per, last), 0)),
                  pl.BlockSpec((k, tn), lambda i, j: (0, j)),
                  pl.BlockSpec((B_SHORT, tn), lambda i, j: (0, j)),
                  pl.BlockSpec((1, tn), lambda i, j: (0, j))],
        out_specs=pl.BlockSpec((tm, tn), lambda i, j: (i, j)),
        out_shape=jax.ShapeDtypeStruct((m, n), F32),
        compiler_params=_cparams(("parallel", "arbitrary")),
        name="matmul_conv",
    )(x, x, x, w, cw, cb.reshape(1, n))


def _mm_rope_kernel(x_ref, w_ref, c_ref, s1_ref, s2_ref, o_ref, *, half):
    acc = jnp.dot(x_ref[...], w_ref[...], preferred_element_type=F32)
    tw = c_ref.shape[1]
    rc = 256
    for r0 in range(0, acc.shape[0], rc):
        rows = slice(r0, r0 + rc)
        for c in range(acc.shape[1] // tw):
            cols = slice(c * tw, (c + 1) * tw)
            o_ref[rows, cols] = _rope(acc[rows, cols], c_ref[rows, :], s1_ref[rows, :], s2_ref[rows, :],
                                      half).astype(o_ref.dtype)


def _matmul_rope(x, w, tabs, half, tn, tm=1024):
    m, k = x.shape
    n = w.shape[1]
    tw = tabs[0].shape[1]
    tab_spec = pl.BlockSpec((tm, tw), lambda i, j: (i, 0))
    return pl.pallas_call(
        functools.partial(_mm_rope_kernel, half=half),
        grid=(m // tm, n // tn),
        in_specs=[pl.BlockSpec((tm, k), lambda i, j: (i, 0)),
                  pl.BlockSpec((k, tn), lambda i, j: (0, j)),
                  tab_spec, tab_spec, tab_spec],
        out_specs=pl.BlockSpec((tm, tn), lambda i, j: (i, j)),
        out_shape=jax.ShapeDtypeStruct((m, n), BF16),
        compiler_params=_cparams(("parallel", "arbitrary")),
        name="matmul_rope",
    )(x, w, *tabs)


def _odd_proj_kernel(x_ref, w_ref, c_ref, s1_ref, s2_ref, o_ref, acc_ref, tmp_ref, *, d, rope, half):
    hps = w_ref.shape[1] // LANES
    tm = x_ref.shape[0]
    t = tm // d
    pair = 2
    rc = 256
    for p in range(hps // pair):
        acc = jnp.dot(x_ref[...], w_ref[:, p * pair * LANES:(p + 1) * pair * LANES], preferred_element_type=F32)
        for h2 in range(pair):
            hh = p * pair + h2
            slot = (p % 2) * pair + h2
            acc_ref[slot, :, :] = acc[:, h2 * LANES:(h2 + 1) * LANES]
            for c0 in range(0, tm, rc):
                a = acc_ref[slot, c0:c0 + rc, :]
                if rope:
                    a = _rope(a, c_ref[c0:c0 + rc, :], s1_ref[c0:c0 + rc, :], s2_ref[c0:c0 + rc, :], half)
                if d == 1:
                    o_ref[hh, 0, c0:c0 + rc, :] = a.astype(BF16)
                elif rope:
                    acc_ref[slot, c0:c0 + rc, :] = a
            if d == 16:
                ts = hh % 2
                for ra in range(4):
                    tmp_ref[ts, ra * (tm // 4):(ra + 1) * (tm // 4), :] = acc_ref[slot, pl.ds(ra, tm // 4, stride=4), :]
                for ra in range(4):
                    for rb in range(4):
                        o_ref[hh, ra + 4 * rb, :, :] = tmp_ref[ts, pl.ds(ra * (tm // 4) + rb, t, stride=4), :].astype(BF16)
            elif d > 1:
                for r in range(d):
                    o_ref[hh, r, :, :] = acc_ref[slot, pl.ds(r, t, stride=d), :].astype(BF16)


def _odd_proj(x, w, col0, tabs, half, d, rope, hps=8):
    m, k = x.shape
    tm = ODD_CHUNK
    cb0 = col0 // (hps * LANES)
    tab_spec = pl.BlockSpec((tm, LANES), lambda i, j: (i, 0))
    return pl.pallas_call(
        functools.partial(_odd_proj_kernel, d=d, rope=rope, half=half),
        grid=(m // tm, C_HEADS // hps),
        in_specs=[pl.BlockSpec((tm, k), lambda i, j: (i, 0)),
                  pl.BlockSpec((k, hps * LANES), lambda i, j: (0, cb0 + j)),
                  tab_spec, tab_spec, tab_spec],
        out_specs=pl.BlockSpec((hps, d, tm // d, LANES), lambda i, j: (j, 0, i, 0)),
        out_shape=jax.ShapeDtypeStruct((C_HEADS, d, m // d, LANES), BF16),
        scratch_shapes=[pltpu.VMEM((4, tm, LANES), F32), pltpu.VMEM((2, tm, LANES), F32)],
        compiler_params=_cparams(("parallel", "arbitrary")),
        name="odd_proj",
    )(x, w, *tabs)


LN_ROWS = 128


def _layer_norm_store(x_ref, acc_ref, g_ref, b_ref, of_ref, ob_ref, row0=0, nrows=None):
    nrows = x_ref.shape[0] if nrows is None else nrows
    for c in range(nrows // LN_ROWS):
        rows = pl.ds(row0 + c * LN_ROWS, LN_ROWS)
        r = ALPHA * x_ref[rows, :] + acc_ref[rows, :]
        mu = jnp.mean(r, axis=-1, keepdims=True)
        xc = r - mu
        var = jnp.mean(xc * xc, axis=-1, keepdims=True)
        y = xc * lax.rsqrt(var + LN_EPS) * g_ref[...] + b_ref[...]
        of_ref[rows, :] = y
        ob_ref[rows, :] = y.astype(BF16)


def _mm_ln_kernel(*refs, n_in):
    ys = refs[:n_in]
    ws = refs[n_in:2 * n_in]
    x_ref, g_ref, b_ref, of_ref, ob_ref, acc_ref = refs[2 * n_in:]
    half = x_ref.shape[0] // 2
    for r0 in (0, half):
        rows = slice(r0, r0 + half)
        acc = jnp.dot(ys[0][rows, :], ws[0][...], preferred_element_type=F32)
        for y_ref, w_ref in zip(ys[1:], ws[1:]):
            acc = acc + jnp.dot(y_ref[rows, :], w_ref[...], preferred_element_type=F32)
        acc_ref[rows, :] = acc
        _layer_norm_store(x_ref, acc_ref, g_ref, b_ref, of_ref, ob_ref, r0, half)


def _matmul_ln(ys, ws, x, g, b, tm=512):
    m, dm = x.shape
    n_in = len(ys)
    in_specs = [pl.BlockSpec((tm, y.shape[1]), lambda i: (i, 0)) for y in ys]
    in_specs += [pl.BlockSpec(w.shape, lambda i: (0, 0)) for w in ws]
    in_specs += [pl.BlockSpec((tm, dm), lambda i: (i, 0)),
                 pl.BlockSpec((1, dm), lambda i: (0, 0)),
                 pl.BlockSpec((1, dm), lambda i: (0, 0))]
    return pl.pallas_call(
        functools.partial(_mm_ln_kernel, n_in=n_in),
        grid=(m // tm,),
        in_specs=in_specs,
        out_specs=[pl.BlockSpec((tm, dm), lambda i: (i, 0)), pl.BlockSpec((tm, dm), lambda i: (i, 0))],
        out_shape=[jax.ShapeDtypeStruct((m, dm), F32), jax.ShapeDtypeStruct((m, dm), BF16)],
        scratch_shapes=[pltpu.VMEM((tm, dm), F32)],
        compiler_params=_cparams(("parallel",)),
        name="matmul_ln",
    )(*ys, *ws, x, g.reshape(1, dm), b.reshape(1, dm))


def _ffn_kernel(xb_ref, xf_ref, wg_ref, wu_ref, wd_ref, g_ref, b_ref, of_ref, ob_ref, acc_ref):
    j = pl.program_id(1)

    @pl.when(j == 0)
    def _():
        acc_ref[...] = jnp.zeros_like(acc_ref)

    xb = xb_ref[...]
    gate = jnp.dot(xb, wg_ref[...], preferred_element_type=F32)
    up = jnp.dot(xb, wu_ref[...], preferred_element_type=F32)
    h = (gate * jax.nn.sigmoid(gate)) * up
    acc_ref[...] += jnp.dot(h.astype(BF16), wd_ref[...], preferred_element_type=F32)

    @pl.when(j == pl.num_programs(1) - 1)
    def _():
        _layer_norm_store(xf_ref, acc_ref, g_ref, b_ref, of_ref, ob_ref)


def _ffn_ln(xb, xf, wg, wu, wd, g, b, tm=512, tf=512):
    m, dm = xf.shape
    dff = wg.shape[1]
    row = lambda i, j: (i, 0)
    return pl.pallas_call(
        _ffn_kernel,
        grid=(m // tm, dff // tf),
        in_specs=[pl.BlockSpec((tm, dm), row), pl.BlockSpec((tm, dm), row),
                  pl.BlockSpec((dm, tf), lambda i, j: (0, j)),
                  pl.BlockSpec((dm, tf), lambda i, j: (0, j)),
                  pl.BlockSpec((tf, dm), lambda i, j: (j, 0)),
                  pl.BlockSpec((1, dm), lambda i, j: (0, 0)),
                  pl.BlockSpec((1, dm), lambda i, j: (0, 0))],
        out_specs=[pl.BlockSpec((tm, dm), row), pl.BlockSpec((tm, dm), row)],
        out_shape=[jax.ShapeDtypeStruct((m, dm), F32), jax.ShapeDtypeStruct((m, dm), BF16)],
        scratch_shapes=[pltpu.VMEM((tm, dm), F32)],
        compiler_params=_cparams(("parallel", "arbitrary")),
        name="ffn_ln",
    )(xb, xf, wg, wu, wd, g.reshape(1, dm), b.reshape(1, dm))


def _band_bias(nq, radius):
    r = np.arange(nq)[:, None]
    c = np.arange(nq + 2 * radius)[None, :]
    return np.where(np.abs(c - radius - r) <= radius, 0.0, -np.inf).astype(np.float32)


def _even_attn_kernel(sink_ref, band_ref, q_ref, kp_ref, km_ref, kn_ref, o_ref, bias_ref, *, bounds):
    i = pl.program_id(0)
    tq = q_ref.shape[0]
    sub = A_RADIUS
    nk = sub + 2 * A_RADIUS
    row0 = i * tq
    start, end = _seq_bounds(row0, bounds)
    group = A_HEADS // A_KV_HEADS
    scale = A_HEAD_DIM ** -0.5
    blocks = [kp_ref, km_ref, kn_ref]
    assert tq == 2 * sub
    for sb in range(tq // sub):
        kv = jnp.concatenate([r[...] for r in blocks[sb:sb + 2]], axis=0)
        rk = row0 + sb * sub - A_RADIUS + lax.broadcasted_iota(jnp.int32, (1, nk), 1)
        bias = band_ref[...] + jnp.where((rk >= start) & (rk < end), 0.0, -jnp.inf)
        bias_ref[0:sub, :] = bias
        bias_ref[sub:2 * sub, :] = bias
        first = lax.broadcasted_iota(jnp.int32, (2 * sub, 1), 0) < sub
        rows = slice(sb * sub, (sb + 1) * sub)
        for j in range(A_KV_HEADS):
            k = kv[:, j * A_HEAD_DIM:(j + 1) * A_HEAD_DIM]
            v = kv[:, A_KV_WIDTH + j * A_HEAD_DIM:A_KV_WIDTH + (j + 1) * A_HEAD_DIM]
            for gq in range(0, group, 2):
                heads = (j * group + gq, j * group + gq + 1)
                cols = [slice(h * A_HEAD_DIM, (h + 1) * A_HEAD_DIM) for h in heads]
                qh = jnp.concatenate([q_ref[rows, c] for c in cols], axis=0)
                s = lax.dot_general(qh, k, (((1,), (1,)), ((), ())), preferred_element_type=F32) * scale
                s = s + bias_ref[...]
                sk = jnp.where(first, sink_ref[heads[0]], sink_ref[heads[1]])
                m = jnp.maximum(jnp.max(s, axis=-1, keepdims=True), sk)
                p = jnp.exp(s - m)
                den = jnp.sum(p, axis=-1, keepdims=True) + jnp.exp(sk - m)
                o = jnp.dot(p.astype(BF16), v, preferred_element_type=F32) / den
                o_ref[rows, cols[0]] = o[0:sub].astype(o_ref.dtype)
                o_ref[rows, cols[1]] = o[sub:2 * sub].astype(o_ref.dtype)


def _even_attention(q, kv, sink, bounds):
    m = q.shape[0]
    tq = ATT_TQ
    hb = A_RADIUS
    per = tq // hb
    last = m // hb - 1
    kvw = kv.shape[1]
    band = _band_bias(A_RADIUS, A_RADIUS)
    return pl.pallas_call(
        functools.partial(_even_attn_kernel, bounds=bounds),
        grid=(m // tq,),
        in_specs=[pl.BlockSpec(memory_space=pltpu.SMEM),
                  pl.BlockSpec(band.shape, lambda i: (0, 0)),
                  pl.BlockSpec((tq, A_WIDTH), lambda i: (i, 0)),
                  pl.BlockSpec((hb, kvw), lambda i: (jnp.maximum(i * per - 1, 0), 0)),
                  pl.BlockSpec((tq, kvw), lambda i: (i, 0)),
                  pl.BlockSpec((hb, kvw), lambda i: (jnp.minimum((i + 1) * per, last), 0))],
        out_specs=pl.BlockSpec((tq, A_WIDTH), lambda i: (i, 0)),
        out_shape=jax.ShapeDtypeStruct((m, A_WIDTH), BF16),
        scratch_shapes=[pltpu.VMEM((2 * band.shape[0], band.shape[1]), F32)],
        compiler_params=_cparams(("parallel",)),
        name="even_attention",
    )(sink, band, q, kv, kv, kv)


def _odd_attn_kernel(*refs, bounds):
    ng = len(C_DILATIONS)
    band_ref = refs[7 * ng]
    o_ref, oacc, lacc = refs[7 * ng + 1:]
    i = pl.program_id(0)
    chunk = ODD_CHUNK
    qb = 128
    nk = qb + 2 * C_RADIUS
    row0 = i * chunk
    start, end = _seq_bounds(row0, bounds)
    cc = lax.broadcasted_iota(jnp.int32, (1, nk), 1)
    scale = C_HEAD_DIM ** -0.5
    for g, d in enumerate(C_DILATIONS):
        q_ref, kp_ref, km_ref, kn_ref, vp_ref, vm_ref, vn_ref = refs[7 * g:7 * g + 7]
        tg = chunk // d
        t_lo, t_hi, t_c0 = start // d, end // d, row0 // d
        for sb in range(tg // qb):
            lo, hi = qb * sb - C_RADIUS, qb * sb + qb + C_RADIUS
            tk = t_c0 + lo + cc
            col = jnp.where((tk >= t_lo) & (tk < t_hi), 0.0, -jnp.inf)
            for r in range(d):
                def window(p_ref, m_ref, n_ref):
                    parts = []
                    if lo < 0:
                        parts.append(p_ref[0, r, :, :])
                    parts.append(m_ref[0, r, max(lo, 0):min(hi, tg), :])
                    if hi > tg:
                        parts.append(n_ref[0, r, :, :])
                    return parts[0] if len(parts) == 1 else jnp.concatenate(parts, axis=0)

                q = q_ref[0, r, qb * sb:qb * (sb + 1), :]
                k = window(kp_ref, km_ref, kn_ref)
                v = window(vp_ref, vm_ref, vn_ref)
                s = lax.dot_general(q, k, (((1,), (1,)), ((), ())), preferred_element_type=F32) * scale
                s = s + band_ref[...] + col
                m = jnp.max(s, axis=-1, keepdims=True)
                p = jnp.exp(s - m)
                den = jnp.sum(p, axis=-1, keepdims=True)
                o = jnp.dot(p.astype(BF16), v, preferred_element_type=F32) / den
                lse = jnp.broadcast_to(m + jnp.log(den), (qb, LANES))
                if d == 1:
                    rows = pl.ds(qb * sb, qb)
                else:
                    rows = pl.ds(r + d * qb * sb, qb, stride=d)
                oacc[g, rows, :] = o
                lacc[g, rows, :] = lse
    ls = [lacc[g] for g in range(ng)]
    mx = functools.reduce(jnp.maximum, ls)
    ws = [jnp.exp(l - mx) for l in ls]
    tot = functools.reduce(lambda a, b: a + b, ws)
    out = functools.reduce(lambda a, b: a + b, [(ws[g] / tot) * oacc[g] for g in range(ng)])
    o_ref[...] = out.astype(o_ref.dtype)


def _odd_attention(qkv, bounds, m):
    chunk = ODD_CHUNK
    hb = C_RADIUS
    operands, in_specs = [], []
    for (q, k, v), d in zip(qkv, C_DILATIONS):
        tg = chunk // d
        per = tg // hb
        last = m // d // hb - 1
        main = pl.BlockSpec((1, d, tg, LANES), lambda i, h: (h, 0, i, 0))
        prev = pl.BlockSpec((1, d, hb, LANES), lambda i, h, per=per: (h, 0, jnp.maximum(i * per - 1, 0), 0))
        nxt = pl.BlockSpec((1, d, hb, LANES), lambda i, h, per=per, last=last: (h, 0, jnp.minimum((i + 1) * per, last), 0))
        operands += [q, k, k, k, v, v, v]
        in_specs += [main, prev, main, nxt, prev, main, nxt]
    band = _band_bias(128, C_RADIUS)
    operands.append(band)
    in_specs.append(pl.BlockSpec(band.shape, lambda i, h: (0, 0)))
    ng = len(C_DILATIONS)
    return pl.pallas_call(
        functools.partial(_odd_attn_kernel, bounds=bounds),
        grid=(m // chunk, C_HEADS),
        in_specs=in_specs,
        out_specs=pl.BlockSpec((chunk, LANES), lambda i, h: (i, h)),
        out_shape=jax.ShapeDtypeStruct((m, C_HEADS * C_HEAD_DIM), BF16),
        scratch_shapes=[pltpu.VMEM((ng, chunk, LANES), F32), pltpu.VMEM((ng, chunk, LANES), F32)],
        compiler_params=_cparams(("parallel", "arbitrary")),
        name="odd_attention",
    )(*operands)


def _filter_mlp_kernel(z_ref, w1_ref, b1_ref, f1_ref, w2_ref, b2_ref, f2_ref, w3_ref, b3_ref, dl_ref,
                       h_ref, nrm_ref):
    i = pl.program_id(0)
    z = z_ref[...]
    h = jnp.sin(f1_ref[...] * (jnp.dot(z.astype(BF16), w1_ref[...], preferred_element_type=F32) + b1_ref[...]))
    h = jnp.sin(f2_ref[...] * (jnp.dot(h.astype(BF16), w2_ref[...], preferred_element_type=F32) + b2_ref[...]))
    h = jnp.dot(h.astype(BF16), w3_ref[...], preferred_element_type=F32) + b3_ref[...]
    decay = jnp.exp(-z[:, 0:1] * dl_ref[...])
    nrep = h.shape[1] // decay.shape[1]
    h = h * jnp.concatenate([decay] * nrep, axis=1)
    h_ref[...] = h

    @pl.when(i == 0)
    def _():
        nrm_ref[...] = jnp.zeros_like(nrm_ref)

    half = h.shape[1] // 2
    col = lax.broadcasted_iota(jnp.int32, h.shape, 1)
    row = lax.broadcasted_iota(jnp.int32, h.shape, 0) + i * h.shape[0]
    a = jnp.where((col >= half) & (row == 0), 0.0, jnp.abs(h))
    nrm_ref[...] += jnp.sum(a, axis=0, keepdims=True)


def _filter_mlp(z, w1, b1, f1, w2, b2, f2, w3, b3, deltas, tl=512):
    l, e = z.shape
    hid = w1.shape[1]
    n = w3.shape[1]
    c = deltas.shape[0]
    full = lambda shape: pl.BlockSpec(shape, lambda i: (0, 0))
    return pl.pallas_call(
        _filter_mlp_kernel,
        grid=(l // tl,),
        in_specs=[pl.BlockSpec((tl, e), lambda i: (i, 0)),
                  full((e, hid)), full((1, hid)), full((1, hid)),
                  full((hid, hid)), full((1, hid)), full((1, hid)),
                  full((hid, n)), full((1, n)), full((1, c))],
        out_specs=[pl.BlockSpec((tl, n), lambda i: (i, 0)), full((1, n))],
        out_shape=[jax.ShapeDtypeStruct((l, n), F32), jax.ShapeDtypeStruct((1, n), F32)],
        compiler_params=_cparams(("arbitrary",)),
        name="filter_mlp",
    )(z, w1, b1.reshape(1, hid), f1.reshape(1, hid), w2, b2.reshape(1, hid), f2.reshape(1, hid),
      w3, b3.reshape(1, n), deltas.reshape(1, c))


def _dot1(ch, x):
    return jnp.dot(ch, x.astype(BF16), preferred_element_type=F32)


FFT_GROUP = 8


FFT_RH = 24
FFT_SPB = 6


def _outer_dft_chunk(x_ref, f_ref, ar_ref, ai_ref, rh, n2, gather):
    kn1 = f_ref.shape[1]
    for g in range(n2 // FFT_GROUP):
        cols = []
        for s in range(FFT_GROUP):
            i2 = g * FFT_GROUP + s
            cols.append(x_ref[pl.ds(i2, kn1, stride=n2), :] if gather else x_ref[i2 * kn1:(i2 + 1) * kn1, :])
        out = _dot1(f_ref[...], jnp.concatenate(cols, axis=1))
        for s in range(FFT_GROUP):
            r0 = (g * FFT_GROUP + s) * rh
            ar_ref[r0:r0 + rh, :] = out[:rh, s * LANES:(s + 1) * LANES]
            ai_ref[r0:r0 + rh, :] = out[rh:, s * LANES:(s + 1) * LANES]


def _cpair(p, n):
    return p[:n, :LANES] - p[n:, LANES:], p[:n, LANES:] + p[n:, :LANES]


def _twiddle(xr, xi, tr, ti):
    return xr * tr - xi * ti, xr * ti + xi * tr


def _filter_mid_kernel(hf_ref, hb_ref, f_ref, tr_ref, ti_ref, w_ref, inv_ref, b0_ref,
                       kr_ref, ki_ref, fr_ref, fi_ref, br_ref, bi_ref, *, nslab, rh, n2):
    k0 = pl.program_id(2) * FFT_SPB

    @pl.when(pl.program_id(2) == 0)
    def _():
        _outer_dft_chunk(hf_ref, f_ref, fr_ref, fi_ref, rh, n2, gather=False)
        _outer_dft_chunk(hb_ref, f_ref, br_ref, bi_ref, rh, n2, gather=False)

    @pl.when(pl.program_id(1) * rh + k0 < nslab)
    def _():
        for t in range(FFT_SPB):
            rows = pl.ds(k0 + t, n2, stride=rh)
            tr, ti = tr_ref[t], ti_ref[t]
            x4 = jnp.concatenate(_twiddle(fr_ref[rows, :], fi_ref[rows, :], tr, ti)
                                 + _twiddle(br_ref[rows, :], bi_ref[rows, :], tr, ti), axis=1)
            p = _dot1(w_ref[...], x4)
            fr, fi = _cpair(p[:, :2 * LANES], n2)
            br, bi = _cpair(p[:, 2 * LANES:], n2)
            kr_ref[t] = (fr + (br - b0_ref[...])) * inv_ref[...]
            ki_ref[t] = (fi - bi) * inv_ref[...]

    @pl.when(pl.program_id(1) * rh + k0 >= nslab)
    def _():
        kr_ref[...] = jnp.zeros_like(kr_ref)
        ki_ref[...] = jnp.zeros_like(ki_ref)


def _filter_mid(h, inv, b0, plan):
    n2, kh, rh, rp = plan["n2"], plan["kh"], plan["rh"], plan["rp"]
    l = h.shape[0]
    oc = h.shape[1] // 2
    nj = oc // LANES
    f1k = plan["f1k"]
    fwd = pl.BlockSpec((l, LANES), lambda j, kk, k: (0, j))
    bwd = pl.BlockSpec((l, LANES), lambda j, kk, k: (0, j + nj))
    fspec = pl.BlockSpec((None,) + f1k.shape[1:], lambda j, kk, k: (kk, 0, 0))
    steps = rh // FFT_SPB
    tspec = pl.BlockSpec((FFT_SPB, n2, LANES), lambda j, kk, k: (kk * steps + k, 0, 0))
    wspec = pl.BlockSpec((2 * n2, n2), lambda j, kk, k: (0, 0))
    vec = pl.BlockSpec((1, LANES), lambda j, kk, k: (0, j))
    ospec = pl.BlockSpec((FFT_SPB, n2, LANES), lambda j, kk, k: (kk * steps + k, 0, j))
    out = jax.ShapeDtypeStruct((rp, n2, oc), F32)
    return pl.pallas_call(
        functools.partial(_filter_mid_kernel, nslab=plan["r"], rh=rh, n2=n2),
        grid=(nj, kh, steps),
        in_specs=[fwd, bwd, fspec, tspec, tspec, wspec, vec, vec],
        out_specs=[ospec, ospec],
        out_shape=[out, out],
        scratch_shapes=[pltpu.VMEM((n2 * rh, LANES), F32) for _ in range(4)],
        compiler_params=_cparams(("parallel", "arbitrary", "arbitrary")),
        name="filter_mid",
    )(h, h, f1k, *plan["tw"], plan["f2"], inv, b0)


def _conv_mid_kernel(x_ref, f_ref, tr_ref, ti_ref, wf_ref, wi_ref, kr_ref, ki_ref, g_ref,
                     y_ref, ar_ref, ai_ref, dr_ref, di_ref, *, nslab, rh, n2):
    k0 = pl.program_id(3) * FFT_SPB

    @pl.when((pl.program_id(2) == 0) & (pl.program_id(3) == 0))
    def _():
        y_ref[...] = jnp.zeros_like(y_ref)

    @pl.when(pl.program_id(3) == 0)
    def _():
        _outer_dft_chunk(x_ref, f_ref, ar_ref, ai_ref, rh, n2, gather=True)

    @pl.when(pl.program_id(2) * rh + k0 < nslab)
    def _():
        for t in range(FFT_SPB):
            rows = pl.ds(k0 + t, n2, stride=rh)
            tr, ti = tr_ref[t], ti_ref[t]
            x2 = jnp.concatenate(_twiddle(ar_ref[rows, :], ai_ref[rows, :], tr, ti), axis=1)
            xr, xi = _cpair(_dot1(wf_ref[...], x2), n2)
            kr, ki = kr_ref[t], ki_ref[t]
            y2 = jnp.concatenate([xr * kr - xi * ki, xr * ki + xi * kr], axis=1)
            cr, ci = _cpair(_dot1(wi_ref[...], y2), n2)
            dr, di = _twiddle(cr, ci, tr, -ti)
            dr_ref[rows, :] = dr
            di_ref[rows, :] = di

    @pl.when(pl.program_id(2) * rh + k0 >= nslab)
    def _():
        for t in range(FFT_SPB):
            rows = pl.ds(k0 + t, n2, stride=rh)
            dr_ref[rows, :] = jnp.zeros((n2, LANES), F32)
            di_ref[rows, :] = jnp.zeros((n2, LANES), F32)

    @pl.when(pl.program_id(3) == pl.num_programs(3) - 1)
    def _():
        n1c = y_ref.shape[0]
        n1r = y_ref.shape[1] // n2
        for g in range(n2 // FFT_GROUP):
            cols = []
            for s in range(FFT_GROUP):
                r0 = (g * FFT_GROUP + s) * rh
                cols.append(jnp.concatenate([dr_ref[r0:r0 + rh, :], di_ref[r0:r0 + rh, :]], axis=0))
            y = _dot1(g_ref[...], jnp.concatenate(cols, axis=1))
            for c in range(n1c):
                for s in range(FFT_GROUP):
                    t0 = (g * FFT_GROUP + s) * n1r
                    y_ref[c, t0:t0 + n1r, :] += y[c * n1r:(c + 1) * n1r, s * LANES:(s + 1) * LANES]


def _conv_mid(x, row_blk0, col_blk0, l, nb, c, kr, ki, order, plan):
    n2, rh, kh = plan["n2"], plan["rh"], plan["kh"]
    f1k, g = plan["f1k"], plan["g"]
    nh = g.shape[1]
    n1r = min(8, nh)
    n1c = nh // n1r
    koff = order * (c // LANES)
    xspec = pl.BlockSpec((l, LANES), lambda b, j, kk, k: (row_blk0 + b, col_blk0 + j))
    fspec = pl.BlockSpec((None,) + f1k.shape[1:], lambda b, j, kk, k: (kk, 0, 0))
    steps = rh // FFT_SPB
    tspec = pl.BlockSpec((FFT_SPB, n2, LANES), lambda b, j, kk, k: (kk * steps + k, 0, 0))
    wspec = pl.BlockSpec((2 * n2, n2), lambda b, j, kk, k: (0, 0))
    kspec = pl.BlockSpec((FFT_SPB, n2, LANES), lambda b, j, kk, k: (kk * steps + k, 0, j + koff))
    gspec = pl.BlockSpec((None, nh, 2 * rh), lambda b, j, kk, k: (kk, 0, 0))
    return pl.pallas_call(
        functools.partial(_conv_mid_kernel, nslab=plan["r"], rh=rh, n2=n2),
        grid=(nb, c // LANES, kh, steps),
        in_specs=[xspec, fspec, tspec, tspec, wspec, wspec, kspec, kspec, gspec],
        out_specs=pl.BlockSpec((None, n1c, n2 * n1r, LANES), lambda b, j, kk, k: (b, 0, 0, j)),
        out_shape=jax.ShapeDtypeStruct((nb, n1c, n2 * n1r, c), F32),
        scratch_shapes=[pltpu.VMEM((n2 * rh, LANES), F32) for _ in range(4)],
        compiler_params=_cparams(("parallel", "parallel", "arbitrary", "arbitrary")),
        name="conv_mid",
    )(x, f1k, *plan["tw"], plan["f2"], plan["f2i"], kr, ki, g)


def _gate_kernel(y_ref, u_ref, g_ref, bias_ref, o_ref, *, n2):
    n1r = y_ref.shape[0] // n2
    for a in range(n1r):
        rows = slice(a * n2, (a + 1) * n2)
        yt = y_ref[pl.ds(a, n2, stride=n1r), :]
        o_ref[rows, :] = (g_ref[rows, :] * (yt + u_ref[rows, :] * bias_ref[...])).astype(o_ref.dtype)


def _gate(y, n2, u, u_row0, u_col0, gate, g_row0, g_col0, bias, out_dtype):
    nb, n1c, yr, c = y.shape
    tr = yr
    return pl.pallas_call(
        functools.partial(_gate_kernel, n2=n2),
        grid=(nb, c // LANES, n1c),
        in_specs=[pl.BlockSpec((None, None, yr, LANES), lambda b, j, q: (b, q, 0, j)),
                  pl.BlockSpec((tr, LANES), lambda b, j, q: (u_row0 // tr + b * n1c + q, u_col0 + j)),
                  pl.BlockSpec((tr, LANES), lambda b, j, q: (g_row0 // tr + b * n1c + q, g_col0 + j)),
                  pl.BlockSpec((1, LANES), lambda b, j, q: (0, j))],
        out_specs=pl.BlockSpec((tr, LANES), lambda b, j, q: (b * n1c + q, j)),
        out_shape=jax.ShapeDtypeStruct((nb * n1c * tr, c), out_dtype),
        compiler_params=_cparams(("parallel", "parallel", "arbitrary")),
        name="hyena_gate",
    )(y, u, gate, bias)


def _np_bf16(a):
    return np.asarray(a, np.float32).astype(BF16)


def _fft_plan(l):
    n = 2 * l
    n2 = FFT_N2
    n1 = n // n2
    r = n1 // 2 + 1
    rh = FFT_RH
    kh = -(-r // rh)
    rp = kh * rh
    kn1 = n1 // 2
    k1 = np.arange(rp, dtype=np.float64)[:, None]
    live = (k1 < r).astype(np.float64)

    ang = 2 * np.pi * k1 * np.arange(kn1)[None, :] / n1
    f1k = np.concatenate([(np.cos(ang) * live).reshape(kh, rh, kn1),
                          (-np.sin(ang) * live).reshape(kh, rh, kn1)], axis=1)

    kk = np.arange(rp, dtype=np.float64)[None, :]
    wgt = np.where((kk == 0) | (kk == n1 // 2), 1.0, 2.0) * (kk < r) / n
    ango = 2 * np.pi * np.arange(n1 // 2)[:, None] * kk / n1
    gre = (np.cos(ango) * wgt).reshape(n1 // 2, kh, rh).transpose(1, 0, 2)
    gim = (-np.sin(ango) * wgt).reshape(n1 // 2, kh, rh).transpose(1, 0, 2)
    g = np.concatenate([gre, gim], axis=2)

    a2 = 2 * np.pi * np.outer(np.arange(n2), np.arange(n2)) / n2
    f2 = np.concatenate([np.cos(a2), -np.sin(a2)], axis=0)
    f2i = np.concatenate([np.cos(a2), np.sin(a2)], axis=0)

    idx = jnp.arange(rp, dtype=jnp.int32)[:, None] * jnp.arange(n2, dtype=jnp.int32)[None, :]
    ang = idx.astype(F32) * F32(2.0 * math.pi / n)
    tw = tuple(jnp.broadcast_to(t[:, :, None], (rp, n2, LANES)) for t in (jnp.cos(ang), -jnp.sin(ang)))
    return dict(n1=n1, n2=n2, r=r, rp=rp, kh=kh, rh=rh, f1k=_np_bf16(f1k), g=_np_bf16(g),
                f2=_np_bf16(f2), f2i=_np_bf16(f2i), tw=tw)


def _filter_features(l):
    t = jnp.linspace(0.0, 1.0, l, dtype=F32)[:, None]
    bands = jnp.linspace(1e-4, B_BANDS - 1, B_BANDS, dtype=F32)[None, :]
    w = 2.0 * math.pi * jnp.arange(l, dtype=F32)[:, None] / l
    return jnp.concatenate([t, jnp.cos(bands * w), -jnp.sin(bands * w)], axis=-1)


def _hyena_filters(plan, l, c, w1, b1, f1, w2, b2, f2, w3, b3):
    n1, n2 = plan["n1"], plan["n2"]
    z = _filter_features(l)
    z = z.reshape(n1 // 2, n2, z.shape[1]).transpose(1, 0, 2).reshape(z.shape)
    e = z.shape[1]
    ep = -(-e // 16) * 16
    z = jnp.pad(z, ((0, 0), (0, ep - e)))
    w1p = jnp.pad(w1, ((0, ep - e), (0, 0))).astype(BF16)
    max_decay = math.log(B_DECAY_TARGET) / B_FAST_DECAY_PCT
    min_decay = math.log(B_DECAY_TARGET) / B_SLOW_DECAY_PCT
    deltas = jnp.abs(jnp.linspace(min_decay, max_decay, c, dtype=F32))
    h, nrm = _filter_mlp(z, w1p, b1, f1, w2.astype(BF16), b2, f2, w3.astype(BF16), b3, deltas,
                         tl=min(512, l))
    oc = h.shape[1] // 2
    inv = 1.0 / (nrm[:, :oc] + nrm[:, oc:])
    b0 = h[0:1, oc:]
    return _filter_mid(h, inv, b0, plan)


def _hyena_conv(plan, kf, order, l, nb, u, u_row0, u_col0, gate, g_row0, g_col0, bias, c, out_dtype):
    n2 = plan["n2"]
    cb = c // LANES
    y = _conv_mid(u, u_row0 // l, u_col0 * cb, l, nb, c, kf[0], kf[1], order, plan)
    return _gate(y, n2, u, u_row0, u_col0 * cb, gate, g_row0, g_col0 * cb, bias.reshape(1, c), out_dtype)


def _hyena_mixer(u, seqs, plans, c, fw, hy_bias):
    outs = []
    for (row0, nb, l) in seqs:
        plan = plans[l]
        kf = _hyena_filters(plan, l, c, *fw)
        z = _hyena_conv(plan, kf, 0, l, nb, u, row0, 0, u, row0, 1, hy_bias[0], c, F32)
        o = _hyena_conv(plan, kf, 1, l, nb, z, 0, 0, u, row0, 2, hy_bias[1], c, BF16)
        outs.append(o)
    return jnp.concatenate(outs, axis=0)


def _rope_tables(pos, hd):
    rot = hd // ROPE_FRACTION
    half = rot // 2
    inv = ROPE_THETA ** (-(jnp.arange(half, dtype=F32) * 2.0 / rot))
    ang = pos[:, None] * inv[None, :]
    cos, sin = jnp.cos(ang), jnp.sin(ang)
    m = pos.shape[0]
    one = jnp.ones((m, hd - rot), F32)
    zero = jnp.zeros((m, hd - rot), F32)
    zh = jnp.zeros((m, half), F32)
    c = jnp.concatenate([cos, cos, one], axis=1)
    s1 = jnp.concatenate([-sin, zh, zero], axis=1)
    s2 = jnp.concatenate([zh, sin, zero], axis=1)
    rep = LANES // hd
    return tuple(jnp.tile(t, (1, rep)) for t in (c, s1, s2)), half


def _trunk(x, bounds, seqs, p):
    m, dm = x.shape
    pos = jnp.concatenate([jnp.tile(jnp.arange(l, dtype=F32), nb) for (_, nb, l) in seqs])
    tabs_a, half_a = _rope_tables(pos, A_HEAD_DIM)
    tabs_c, half_c = _rope_tables(pos, C_HEAD_DIM)
    ident = (jnp.ones((m, LANES), F32), jnp.zeros((m, LANES), F32), jnp.zeros((m, LANES), F32))
    tabs_kv = tuple(jnp.concatenate([a, b], axis=1) for a, b in zip(tabs_a, ident))
    c_hy = dm - A_WIDTH
    plans = {l: _fft_plan(l) for l in sorted({l for (_, _, l) in seqs})}
    xf = x
    xb = x.astype(BF16)
    for i in range(DEPTH):
        j = i // 2
        if i % 2 == 0:
            w_in = p['mix_e_w_in'][j].astype(BF16)
            kv0 = A_WIDTH
            hy0 = A_WIDTH + 2 * A_KV_WIDTH
            q = _matmul_rope(xb, w_in[:, :kv0], tabs_a, half_a, tn=512)
            kv = _matmul_rope(xb, w_in[:, kv0:hy0], tabs_kv, half_a, tn=2 * A_KV_WIDTH)
            u = _matmul_conv(xb, w_in[:, hy0:], p['hy_conv_w'][j], p['hy_conv_b'][j], bounds)
            a_out = _even_attention(q, kv, p['a_sink'][j], bounds)
            fw = (p['hy_w1'][j], p['hy_b1'][j], p['hy_f1'][j], p['hy_w2'][j], p['hy_b2'][j],
                  p['hy_f2'][j], p['hy_w3'][j], p['hy_b3'][j])
            h_out = _hyena_mixer(u, seqs, plans, c_hy, fw, p['hy_bias'][j])
            w_out = p['mix_e_w_out'][j].astype(BF16)
            xf, xb = _matmul_ln([a_out, h_out], [w_out[:A_WIDTH], w_out[A_WIDTH:]], xf,
                                p['ln1_g'][i], p['ln1_b'][i])
        else:
            w_in = p['mix_o_w_in'][j].astype(BF16)
            gw = C_HEADS * C_HEAD_DIM
            ng = len(C_DILATIONS)
            qkv = []
            for g, d in enumerate(C_DILATIONS):
                trio = []
                for part in range(3):
                    c0 = (part * ng + g) * gw
                    trio.append(_odd_proj(xb, w_in, c0, tabs_c, half_c, d, rope=part < 2))
                qkv.append(tuple(trio))
            o = _odd_attention(qkv, bounds, m)
            xf, xb = _matmul_ln([o], [p['mix_o_w_out'][j].astype(BF16)], xf, p['ln1_g'][i], p['ln1_b'][i])
        xf, xb = _ffn_ln(xb, xf, p['ffn_w_gate'][i].astype(BF16), p['ffn_w_up'][i].astype(BF16),
                         p['ffn_w_down'][i].astype(BF16), p['ln2_g'][i], p['ln2_b'][i])
    return xf


def kernel(x_prompt, x_sample, mix_e_w_in, a_sink, hy_conv_w, hy_conv_b, hy_w1, hy_b1, hy_f1, hy_w2, hy_b2,
           hy_f2, hy_w3, hy_b3, hy_bias, mix_e_w_out, mix_o_w_in, mix_o_w_out, ffn_w_gate, ffn_w_up,
           ffn_w_down, ln1_g, ln1_b, ln2_g, ln2_b):
    p = dict(mix_e_w_in=mix_e_w_in, a_sink=a_sink, hy_conv_w=hy_conv_w, hy_conv_b=hy_conv_b,
             hy_w1=hy_w1, hy_b1=hy_b1, hy_f1=hy_f1, hy_w2=hy_w2, hy_b2=hy_b2, hy_f2=hy_f2,
             hy_w3=hy_w3, hy_b3=hy_b3, hy_bias=hy_bias, mix_e_w_out=mix_e_w_out,
             mix_o_w_in=mix_o_w_in, mix_o_w_out=mix_o_w_out, ffn_w_gate=ffn_w_gate,
             ffn_w_up=ffn_w_up, ffn_w_down=ffn_w_down, ln1_g=ln1_g, ln1_b=ln1_b,
             ln2_g=ln2_g, ln2_b=ln2_b)
    dm = x_prompt.shape[-1]
    seqs, bounds, row = [], [0], 0
    for xs in (x_prompt, x_sample):
        nb, l = xs.shape[0], xs.shape[1]
        seqs.append((row, nb, l))
        for _ in range(nb):
            row += l
            bounds.append(row)
    x = jnp.concatenate([x_prompt.reshape(-1, dm), x_sample.reshape(-1, dm)], axis=0)
    y = _trunk(x, tuple(bounds), tuple(seqs), p)
    n_p = x_prompt.shape[0] * x_prompt.shape[1]
    return (y[:n_p].reshape(x_prompt.shape), y[n_p:].reshape(x_sample.shape))
```

```python
import functools
import math

import numpy as np
import jax
import jax.numpy as jnp
from jax import lax
from jax.experimental import pallas as pl
from jax.experimental.pallas import tpu as pltpu

F32 = jnp.float32
BF16 = jnp.bfloat16

DEPTH = 4
A_HEADS, A_KV_HEADS, A_HEAD_DIM, A_RADIUS = 16, 2, 64, 128
A_WIDTH = A_HEADS * A_HEAD_DIM
A_KV_WIDTH = A_KV_HEADS * A_HEAD_DIM
B_SHORT, B_EMB = 3, 33
B_BANDS = (B_EMB - 1) // 2
B_DECAY_TARGET, B_FAST_DECAY_PCT, B_SLOW_DECAY_PCT = 1e-2, 0.3, 1.5
C_HEADS, C_HEAD_DIM = 16, 128
C_DILATIONS = (1, 4, 16)
C_RADIUS = 64
ROPE_THETA, ROPE_FRACTION = 500000.0, 4
ALPHA = (2 * DEPTH) ** 0.25
LN_EPS = 1e-5

LANES = 128
VMEM_LIMIT = 56 * 1024 * 1024
FFT_N2 = 256

ODD_CHUNK = 2048
ATT_TQ = 256


def _cparams(sem):
    return pltpu.CompilerParams(dimension_semantics=sem, vmem_limit_bytes=VMEM_LIMIT)


def _seq_bounds(row, bounds):
    start = jnp.int32(bounds[0])
    end = jnp.int32(bounds[1])
    for b0, b1 in zip(bounds[1:-1], bounds[2:]):
        inside = row >= b0
        start = jnp.where(inside, jnp.int32(b0), start)
        end = jnp.where(inside, jnp.int32(b1), end)
    return start, end


def _rope(a, c, s1, s2, half):
    w = a.shape[-1]
    return a * c + pltpu.roll(a, w - half, 1) * s1 + pltpu.roll(a, half, 1) * s2


def _mm_conv_kernel(xp_ref, xm_ref, xn_ref, w_ref, cw_ref, cb_ref, o_ref, *, bounds):
    i = pl.program_id(0)
    tm, hb = xm_ref.shape[0], xp_ref.shape[0]
    rows = tm + 2 * hb
    row0 = i * tm
    start, end = _seq_bounds(row0, bounds)
    lhs = jnp.concatenate([xp_ref[...], xm_ref[...], xn_ref[...]], axis=0)
    acc = jnp.dot(lhs, w_ref[...], preferred_element_type=F32)
    h0 = acc[hb:hb + tm]
    hm = pltpu.roll(acc, 1, 0)[hb:hb + tm]
    hp = pltpu.roll(acc, rows - 1, 0)[hb:hb + tm]
    ridx = lax.broadcasted_iota(jnp.int32, h0.shape, 0)
    hm = jnp.where((ridx == 0) & (row0 <= start), 0.0, hm)
    hp = jnp.where((ridx == tm - 1) & (row0 + tm >= end), 0.0, hp)
    y = cb_ref[...] + hm * cw_ref[0:1, :]
    y = y + h0 * cw_ref[1:2, :]
    y = y + hp * cw_ref[2:3, :]
    o_ref[...] = y


def _matmul_conv(x, w, cw, cb, bounds, tn=512, tm=1024):
    m, k = x.shape
    n = w.shape[1]
    hb = 16
    per = tm // hb
    last = m // hb - 1
    return pl.pallas_call(
        functools.partial(_mm_conv_kernel, bounds=bounds),
        grid=(m // tm, n // tn),
        in_specs=[pl.BlockSpec((hb, k), lambda i, j: (jnp.maximum(i * per - 1, 0), 0)),
                  pl.BlockSpec((tm, k), lambda i, j: (i, 0)),
                  pl.BlockSpec((hb, k), lambda i, j: (jnp.minimum((i + 1) * per, last), 0)),
                  pl.BlockSpec((k, tn), lambda i, j: (0, j)),
                  pl.BlockSpec((B_SHORT, tn), lambda i, j: (0, j)),
                  pl.BlockSpec((1, tn), lambda i, j: (0, j))],
        out_specs=pl.BlockSpec((tm, tn), lambda i, j: (i, j)),
        out_shape=jax.ShapeDtypeStruct((m, n), F32),
        compiler_params=_cparams(("parallel", "arbitrary")),
        name="matmul_conv",
    )(x, x, x, w, cw, cb.reshape(1, n))


def _mm_rope_kernel(x_ref, w_ref, c_ref, s1_ref, s2_ref, o_ref, *, half):
    acc = jnp.dot(x_ref[...], w_ref[...], preferred_element_type=F32)
    tw = c_ref.shape[1]
    rc = 256
    for r0 in range(0, acc.shape[0], rc):
        rows = slice(r0, r0 + rc)
        for c in range(acc.shape[1] // tw):
            cols = slice(c * tw, (c + 1) * tw)
            o_ref[rows, cols] = _rope(acc[rows, cols], c_ref[rows, :], s1_ref[rows, :], s2_ref[rows, :],
                                      half).astype(o_ref.dtype)


def _matmul_rope(x, w, tabs, half, tn, tm=1024):
    m, k = x.shape
    n = w.shape[1]
    tw = tabs[0].shape[1]
    tab_spec = pl.BlockSpec((tm, tw), lambda i, j: (i, 0))
    return pl.pallas_call(
        functools.partial(_mm_rope_kernel, half=half),
        grid=(m // tm, n // tn),
        in_specs=[pl.BlockSpec((tm, k), lambda i, j: (i, 0)),
                  pl.BlockSpec((k, tn), lambda i, j: (0, j)),
                  tab_spec, tab_spec, tab_spec],
        out_specs=pl.BlockSpec((tm, tn), lambda i, j: (i, j)),
        out_shape=jax.ShapeDtypeStruct((m, n), BF16),
        compiler_params=_cparams(("parallel", "arbitrary")),
        name="matmul_rope",
    )(x, w, *tabs)


def _odd_proj_kernel(x_ref, w_ref, c_ref, s1_ref, s2_ref, o_ref, acc_ref, tmp_ref, *, d, rope, half):
    hps = w_ref.shape[1] // LANES
    tm = x_ref.shape[0]
    t = tm // d
    pair = 2
    rc = 256
    for p in range(hps // pair):
        acc = jnp.dot(x_ref[...], w_ref[:, p * pair * LANES:(p + 1) * pair * LANES], preferred_element_type=F32)
        for h2 in range(pair):
            hh = p * pair + h2
            slot = (p % 2) * pair + h2
            acc_ref[slot, :, :] = acc[:, h2 * LANES:(h2 + 1) * LANES]
            for c0 in range(0, tm, rc):
                a = acc_ref[slot, c0:c0 + rc, :]
                if rope:
                    a = _rope(a, c_ref[c0:c0 + rc, :], s1_ref[c0:c0 + rc, :], s2_ref[c0:c0 + rc, :], half)
                if d == 1:
                    o_ref[hh, 0, c0:c0 + rc, :] = a.astype(BF16)
                elif rope:
                    acc_ref[slot, c0:c0 + rc, :] = a
            if d == 16:
                ts = hh % 2
                for ra in range(4):
                    tmp_ref[ts, ra * (tm // 4):(ra + 1) * (tm // 4), :] = acc_ref[slot, pl.ds(ra, tm // 4, stride=4), :]
                for ra in range(4):
                    for rb in range(4):
                        o_ref[hh, ra + 4 * rb, :, :] = tmp_ref[ts, pl.ds(ra * (tm // 4) + rb, t, stride=4), :].astype(BF16)
            elif d > 1:
                for r in range(d):
                    o_ref[hh, r, :, :] = acc_ref[slot, pl.ds(r, t, stride=d), :].astype(BF16)


def _odd_proj(x, w, col0, tabs, half, d, rope, hps=8):
    m, k = x.shape
    tm = ODD_CHUNK
    cb0 = col0 // (hps * LANES)
    tab_spec = pl.BlockSpec((tm, LANES), lambda i, j: (i, 0))
    return pl.pallas_call(
        functools.partial(_odd_proj_kernel, d=d, rope=rope, half=half),
        grid=(m // tm, C_HEADS // hps),
        in_specs=[pl.BlockSpec((tm, k), lambda i, j: (i, 0)),
                  pl.BlockSpec((k, hps * LANES), lambda i, j: (0, cb0 + j)),
                  tab_spec, tab_spec, tab_spec],
        out_specs=pl.BlockSpec((hps, d, tm // d, LANES), lambda i, j: (j, 0, i, 0)),
        out_shape=jax.ShapeDtypeStruct((C_HEADS, d, m // d, LANES), BF16),
        scratch_shapes=[pltpu.VMEM((4, tm, LANES), F32), pltpu.VMEM((2, tm, LANES), F32)],
        compiler_params=_cparams(("parallel", "arbitrary")),
        name="odd_proj",
    )(x, w, *tabs)


LN_ROWS = 128


def _layer_norm_store(x_ref, acc_ref, g_ref, b_ref, of_ref, ob_ref, row0=0, nrows=None):
    nrows = x_ref.shape[0] if nrows is None else nrows
    for c in range(nrows // LN_ROWS):
        rows = pl.ds(row0 + c * LN_ROWS, LN_ROWS)
        r = ALPHA * x_ref[rows, :] + acc_ref[rows, :]
        mu = jnp.mean(r, axis=-1, keepdims=True)
        xc = r - mu
        var = jnp.mean(xc * xc, axis=-1, keepdims=True)
        y = xc * lax.rsqrt(var + LN_EPS) * g_ref[...] + b_ref[...]
        of_ref[rows, :] = y
        ob_ref[rows, :] = y.astype(BF16)


def _mm_ln_kernel(*refs, n_in):
    ys = refs[:n_in]
    ws = refs[n_in:2 * n_in]
    x_ref, g_ref, b_ref, of_ref, ob_ref, acc_ref = refs[2 * n_in:]
    half = x_ref.shape[0] // 2
    for r0 in (0, half):
        rows = slice(r0, r0 + half)
        acc = jnp.dot(ys[0][rows, :], ws[0][...], preferred_element_type=F32)
        for y_ref, w_ref in zip(ys[1:], ws[1:]):
            acc = acc + jnp.dot(y_ref[rows, :], w_ref[...], preferred_element_type=F32)
        acc_ref[rows, :] = acc
        _layer_norm_store(x_ref, acc_ref, g_ref, b_ref, of_ref, ob_ref, r0, half)


def _matmul_ln(ys, ws, x, g, b, tm=512):
    m, dm = x.shape
    n_in = len(ys)
    in_specs = [pl.BlockSpec((tm, y.shape[1]), lambda i: (i, 0)) for y in ys]
    in_specs += [pl.BlockSpec(w.shape, lambda i: (0, 0)) for w in ws]
    in_specs += [pl.BlockSpec((tm, dm), lambda i: (i, 0)),
                 pl.BlockSpec((1, dm), lambda i: (0, 0)),
                 pl.BlockSpec((1, dm), lambda i: (0, 0))]
    return pl.pallas_call(
        functools.partial(_mm_ln_kernel, n_in=n_in),
        grid=(m // tm,),
        in_specs=in_specs,
        out_specs=[pl.BlockSpec((tm, dm), lambda i: (i, 0)), pl.BlockSpec((tm, dm), lambda i: (i, 0))],
        out_shape=[jax.ShapeDtypeStruct((m, dm), F32), jax.ShapeDtypeStruct((m, dm), BF16)],
        scratch_shapes=[pltpu.VMEM((tm, dm), F32)],
        compiler_params=_cparams(("parallel",)),
        name="matmul_ln",
    )(*ys, *ws, x, g.reshape(1, dm), b.reshape(1, dm))


def _ffn_kernel(xb_ref, xf_ref, wg_ref, wu_ref, wd_ref, g_ref, b_ref, of_ref, ob_ref, acc_ref):
    j = pl.program_id(1)

    @pl.when(j == 0)
    def _():
        acc_ref[...] = jnp.zeros_like(acc_ref)

    xb = xb_ref[...]
    gate = jnp.dot(xb, wg_ref[...], preferred_element_type=F32)
    up = jnp.dot(xb, wu_ref[...], preferred_element_type=F32)
    h = (gate * jax.nn.sigmoid(gate)) * up
    acc_ref[...] += jnp.dot(h.astype(BF16), wd_ref[...], preferred_element_type=F32)

    @pl.when(j == pl.num_programs(1) - 1)
    def _():
        _layer_norm_store(xf_ref, acc_ref, g_ref, b_ref, of_ref, ob_ref)


def _ffn_ln(xb, xf, wg, wu, wd, g, b, tm=512, tf=512):
    m, dm = xf.shape
    dff = wg.shape[1]
    row = lambda i, j: (i, 0)
    return pl.pallas_call(
        _ffn_kernel,
        grid=(m // tm, dff // tf),
        in_specs=[pl.BlockSpec((tm, dm), row), pl.BlockSpec((tm, dm), row),
                  pl.BlockSpec((dm, tf), lambda i, j: (0, j)),
                  pl.BlockSpec((dm, tf), lambda i, j: (0, j)),
                  pl.BlockSpec((tf, dm), lambda i, j: (j, 0)),
                  pl.BlockSpec((1, dm), lambda i, j: (0, 0)),
                  pl.BlockSpec((1, dm), lambda i, j: (0, 0))],
        out_specs=[pl.BlockSpec((tm, dm), row), pl.BlockSpec((tm, dm), row)],
        out_shape=[jax.ShapeDtypeStruct((m, dm), F32), jax.ShapeDtypeStruct((m, dm), BF16)],
        scratch_shapes=[pltpu.VMEM((tm, dm), F32)],
        compiler_params=_cparams(("parallel", "arbitrary")),
        name="ffn_ln",
    )(xb, xf, wg, wu, wd, g.reshape(1, dm), b.reshape(1, dm))


def _band_bias(nq, radius):
    r = np.arange(nq)[:, None]
    c = np.arange(nq + 2 * radius)[None, :]
    return np.where(np.abs(c - radius - r) <= radius, 0.0, -np.inf).astype(np.float32)


def _even_attn_kernel(sink_ref, band_ref, q_ref, kp_ref, km_ref, kn_ref, o_ref, bias_ref, *, bounds):
    i = pl.program_id(0)
    tq = q_ref.shape[0]
    sub = A_RADIUS
    nk = sub + 2 * A_RADIUS
    row0 = i * tq
    start, end = _seq_bounds(row0, bounds)
    group = A_HEADS // A_KV_HEADS
    scale = A_HEAD_DIM ** -0.5
    blocks = [kp_ref, km_ref, kn_ref]
    assert tq == 2 * sub
    for sb in range(tq // sub):
        kv = jnp.concatenate([r[...] for r in blocks[sb:sb + 2]], axis=0)
        rk = row0 + sb * sub - A_RADIUS + lax.broadcasted_iota(jnp.int32, (1, nk), 1)
        bias = band_ref[...] + jnp.where((rk >= start) & (rk < end), 0.0, -jnp.inf)
        bias_ref[0:sub, :] = bias
        bias_ref[sub:2 * sub, :] = bias
        first = lax.broadcasted_iota(jnp.int32, (2 * sub, 1), 0) < sub
        rows = slice(sb * sub, (sb + 1) * sub)
        for j in range(A_KV_HEADS):
            k = kv[:, j * A_HEAD_DIM:(j + 1) * A_HEAD_DIM]
            v = kv[:, A_KV_WIDTH + j * A_HEAD_DIM:A_KV_WIDTH + (j + 1) * A_HEAD_DIM]
            for gq in range(0, group, 2):
                heads = (j * group + gq, j * group + gq + 1)
                cols = [slice(h * A_HEAD_DIM, (h + 1) * A_HEAD_DIM) for h in heads]
                qh = jnp.concatenate([q_ref[rows, c] for c in cols], axis=0)
                s = lax.dot_general(qh, k, (((1,), (1,)), ((), ())), preferred_element_type=F32) * scale
                s = s + bias_ref[...]
                sk = jnp.where(first, sink_ref[heads[0]], sink_ref[heads[1]])
                m = jnp.maximum(jnp.max(s, axis=-1, keepdims=True), sk)
                p = jnp.exp(s - m)
                den = jnp.sum(p, axis=-1, keepdims=True) + jnp.exp(sk - m)
                o = jnp.dot(p.astype(BF16), v, preferred_element_type=F32) / den
                o_ref[rows, cols[0]] = o[0:sub].astype(o_ref.dtype)
                o_ref[rows, cols[1]] = o[sub:2 * sub].astype(o_ref.dtype)


def _even_attention(q, kv, sink, bounds):
    m = q.shape[0]
    tq = ATT_TQ
    hb = A_RADIUS
    per = tq // hb
    last = m // hb - 1
    kvw = kv.shape[1]
    band = _band_bias(A_RADIUS, A_RADIUS)
    return pl.pallas_call(
        functools.partial(_even_attn_kernel, bounds=bounds),
        grid=(m // tq,),
        in_specs=[pl.BlockSpec(memory_space=pltpu.SMEM),
                  pl.BlockSpec(band.shape, lambda i: (0, 0)),
                  pl.BlockSpec((tq, A_WIDTH), lambda i: (i, 0)),
                  pl.BlockSpec((hb, kvw), lambda i: (jnp.maximum(i * per - 1, 0), 0)),
                  pl.BlockSpec((tq, kvw), lambda i: (i, 0)),
                  pl.BlockSpec((hb, kvw), lambda i: (jnp.minimum((i + 1) * per, last), 0))],
        out_specs=pl.BlockSpec((tq, A_WIDTH), lambda i: (i, 0)),
        out_shape=jax.ShapeDtypeStruct((m, A_WIDTH), BF16),
        scratch_shapes=[pltpu.VMEM((2 * band.shape[0], band.shape[1]), F32)],
        compiler_params=_cparams(("parallel",)),
        name="even_attention",
    )(sink, band, q, kv, kv, kv)


def _odd_attn_kernel(*refs, bounds):
    ng = len(C_DILATIONS)
    band_ref = refs[7 * ng]
    o_ref, oacc, lacc = refs[7 * ng + 1:]
    i = pl.program_id(0)
    chunk = ODD_CHUNK
    qb = 128
    nk = qb + 2 * C_RADIUS
    row0 = i * chunk
    start, end = _seq_bounds(row0, bounds)
    cc = lax.broadcasted_iota(jnp.int32, (1, nk), 1)
    scale = C_HEAD_DIM ** -0.5
    for g, d in enumerate(C_DILATIONS):
        q_ref, kp_ref, km_ref, kn_ref, vp_ref, vm_ref, vn_ref = refs[7 * g:7 * g + 7]
        tg = chunk // d
        t_lo, t_hi, t_c0 = start // d, end // d, row0 // d
        for sb in range(tg // qb):
            lo, hi = qb * sb - C_RADIUS, qb * sb + qb + C_RADIUS
            tk = t_c0 + lo + cc
            col = jnp.where((tk >= t_lo) & (tk < t_hi), 0.0, -jnp.inf)
            for r in range(d):
                def window(p_ref, m_ref, n_ref):
                    parts = []
                    if lo < 0:
                        parts.append(p_ref[0, r, :, :])
                    parts.append(m_ref[0, r, max(lo, 0):min(hi, tg), :])
                    if hi > tg:
                        parts.append(n_ref[0, r, :, :])
                    return parts[0] if len(parts) == 1 else jnp.concatenate(parts, axis=0)

                q = q_ref[0, r, qb * sb:qb * (sb + 1), :]
                k = window(kp_ref, km_ref, kn_ref)
                v = window(vp_ref, vm_ref, vn_ref)
                s = lax.dot_general(q, k, (((1,), (1,)), ((), ())), preferred_element_type=F32) * scale
                s = s + band_ref[...] + col
                m = jnp.max(s, axis=-1, keepdims=True)
                p = jnp.exp(s - m)
                den = jnp.sum(p, axis=-1, keepdims=True)
                o = jnp.dot(p.astype(BF16), v, preferred_element_type=F32) / den
                lse = jnp.broadcast_to(m + jnp.log(den), (qb, LANES))
                if d == 1:
                    rows = pl.ds(qb * sb, qb)
                else:
                    rows = pl.ds(r + d * qb * sb, qb, stride=d)
                oacc[g, rows, :] = o
                lacc[g, rows, :] = lse
    ls = [lacc[g] for g in range(ng)]
    mx = functools.reduce(jnp.maximum, ls)
    ws = [jnp.exp(l - mx) for l in ls]
    tot = functools.reduce(lambda a, b: a + b, ws)
    out = functools.reduce(lambda a, b: a + b, [(ws[g] / tot) * oacc[g] for g in range(ng)])
    o_ref[...] = out.astype(o_ref.dtype)


def _odd_attention(qkv, bounds, m):
    chunk = ODD_CHUNK
    hb = C_RADIUS
    operands, in_specs = [], []
    for (q, k, v), d in zip(qkv, C_DILATIONS):
        tg = chunk // d
        per = tg // hb
        last = m // d // hb - 1
        main = pl.BlockSpec((1, d, tg, LANES), lambda i, h: (h, 0, i, 0))
        prev = pl.BlockSpec((1, d, hb, LANES), lambda i, h, per=per: (h, 0, jnp.maximum(i * per - 1, 0), 0))
        nxt = pl.BlockSpec((1, d, hb, LANES), lambda i, h, per=per, last=last: (h, 0, jnp.minimum((i + 1) * per, last), 0))
        operands += [q, k, k, k, v, v, v]
        in_specs += [main, prev, main, nxt, prev, main, nxt]
    band = _band_bias(128, C_RADIUS)
    operands.append(band)
    in_specs.append(pl.BlockSpec(band.shape, lambda i, h: (0, 0)))
    ng = len(C_DILATIONS)
    return pl.pallas_call(
        functools.partial(_odd_attn_kernel, bounds=bounds),
        grid=(m // chunk, C_HEADS),
        in_specs=in_specs,
        out_specs=pl.BlockSpec((chunk, LANES), lambda i, h: (i, h)),
        out_shape=jax.ShapeDtypeStruct((m, C_HEADS * C_HEAD_DIM), BF16),
        scratch_shapes=[pltpu.VMEM((ng, chunk, LANES), F32), pltpu.VMEM((ng, chunk, LANES), F32)],
        compiler_params=_cparams(("parallel", "arbitrary")),
        name="odd_attention",
    )(*operands)


def _filter_mlp_kernel(z_ref, w1_ref, b1_ref, f1_ref, w2_ref, b2_ref, f2_ref, w3_ref, b3_ref, dl_ref,
                       h_ref, nrm_ref):
    i = pl.program_id(0)
    z = z_ref[...]
    h = jnp.sin(f1_ref[...] * (jnp.dot(z.astype(BF16), w1_ref[...], preferred_element_type=F32) + b1_ref[...]))
    h = jnp.sin(f2_ref[...] * (jnp.dot(h.astype(BF16), w2_ref[...], preferred_element_type=F32) + b2_ref[...]))
    h = jnp.dot(h.astype(BF16), w3_ref[...], preferred_element_type=F32) + b3_ref[...]
    decay = jnp.exp(-z[:, 0:1] * dl_ref[...])
    nrep = h.shape[1] // decay.shape[1]
    h = h * jnp.concatenate([decay] * nrep, axis=1)
    h_ref[...] = h

    @pl.when(i == 0)
    def _():
        nrm_ref[...] = jnp.zeros_like(nrm_ref)

    half = h.shape[1] // 2
    col = lax.broadcasted_iota(jnp.int32, h.shape, 1)
    row = lax.broadcasted_iota(jnp.int32, h.shape, 0) + i * h.shape[0]
    a = jnp.where((col >= half) & (row == 0), 0.0, jnp.abs(h))
    nrm_ref[...] += jnp.sum(a, axis=0, keepdims=True)


def _filter_mlp(z, w1, b1, f1, w2, b2, f2, w3, b3, deltas, tl=512):
    l, e = z.shape
    hid = w1.shape[1]
    n = w3.shape[1]
    c = deltas.shape[0]
    full = lambda shape: pl.BlockSpec(shape, lambda i: (0, 0))
    return pl.pallas_call(
        _filter_mlp_kernel,
        grid=(l // tl,),
        in_specs=[pl.BlockSpec((tl, e), lambda i: (i, 0)),
                  full((e, hid)), full((1, hid)), full((1, hid)),
                  full((hid, hid)), full((1, hid)), full((1, hid)),
                  full((hid, n)), full((1, n)), full((1, c))],
        out_specs=[pl.BlockSpec((tl, n), lambda i: (i, 0)), full((1, n))],
        out_shape=[jax.ShapeDtypeStruct((l, n), F32), jax.ShapeDtypeStruct((1, n), F32)],
        compiler_params=_cparams(("arbitrary",)),
        name="filter_mlp",
    )(z, w1, b1.reshape(1, hid), f1.reshape(1, hid), w2, b2.reshape(1, hid), f2.reshape(1, hid),
      w3, b3.reshape(1, n), deltas.reshape(1, c))


def _dot1(ch, x):
    return jnp.dot(ch, x.astype(BF16), preferred_element_type=F32)


FFT_GROUP = 8


FFT_RH = 24
FFT_SPB = 6


def _outer_dft_chunk(x_ref, f_ref, ar_ref, ai_ref, rh, n2, gather):
    kn1 = f_ref.shape[1]
    for g in range(n2 // FFT_GROUP):
        cols = []
        for s in range(FFT_GROUP):
            i2 = g * FFT_GROUP + s
            cols.append(x_ref[pl.ds(i2, kn1, stride=n2), :] if gather else x_ref[i2 * kn1:(i2 + 1) * kn1, :])
        out = _dot1(f_ref[...], jnp.concatenate(cols, axis=1))
        for s in range(FFT_GROUP):
            r0 = (g * FFT_GROUP + s) * rh
            ar_ref[r0:r0 + rh, :] = out[:rh, s * LANES:(s + 1) * LANES]
            ai_ref[r0:r0 + rh, :] = out[rh:, s * LANES:(s + 1) * LANES]


def _cpair(p, n):
    return p[:n, :LANES] - p[n:, LANES:], p[:n, LANES:] + p[n:, :LANES]


def _twiddle(xr, xi, tr, ti):
    return xr * tr - xi * ti, xr * ti + xi * tr


def _filter_mid_kernel(hf_ref, hb_ref, f_ref, tr_ref, ti_ref, w_ref, inv_ref, b0_ref,
                       kr_ref, ki_ref, fr_ref, fi_ref, br_ref, bi_ref, *, nslab, rh, n2):
    k0 = pl.program_id(2) * FFT_SPB

    @pl.when(pl.program_id(2) == 0)
    def _():
        _outer_dft_chunk(hf_ref, f_ref, fr_ref, fi_ref, rh, n2, gather=False)
        _outer_dft_chunk(hb_ref, f_ref, br_ref, bi_ref, rh, n2, gather=False)

    @pl.when(pl.program_id(1) * rh + k0 < nslab)
    def _():
        for t in range(FFT_SPB):
            rows = pl.ds(k0 + t, n2, stride=rh)
            tr, ti = tr_ref[t], ti_ref[t]
            x4 = jnp.concatenate(_twiddle(fr_ref[rows, :], fi_ref[rows, :], tr, ti)
                                 + _twiddle(br_ref[rows, :], bi_ref[rows, :], tr, ti), axis=1)
            p = _dot1(w_ref[...], x4)
            fr, fi = _cpair(p[:, :2 * LANES], n2)
            br, bi = _cpair(p[:, 2 * LANES:], n2)
            kr_ref[t] = (fr + (br - b0_ref[...])) * inv_ref[...]
            ki_ref[t] = (fi - bi) * inv_ref[...]

    @pl.when(pl.program_id(1) * rh + k0 >= nslab)
    def _():
        kr_ref[...] = jnp.zeros_like(kr_ref)
        ki_ref[...] = jnp.zeros_like(ki_ref)


def _filter_mid(h, inv, b0, plan):
    n2, kh, rh, rp = plan["n2"], plan["kh"], plan["rh"], plan["rp"]
    l = h.shape[0]
    oc = h.shape[1] // 2
    nj = oc // LANES
    f1k = plan["f1k"]
    fwd = pl.BlockSpec((l, LANES), lambda j, kk, k: (0, j))
    bwd = pl.BlockSpec((l, LANES), lambda j, kk, k: (0, j + nj))
    fspec = pl.BlockSpec((None,) + f1k.shape[1:], lambda j, kk, k: (kk, 0, 0))
    steps = rh // FFT_SPB
    tspec = pl.BlockSpec((FFT_SPB, n2, LANES), lambda j, kk, k: (kk * steps + k, 0, 0))
    wspec = pl.BlockSpec((2 * n2, n2), lambda j, kk, k: (0, 0))
    vec = pl.BlockSpec((1, LANES), lambda j, kk, k: (0, j))
    ospec = pl.BlockSpec((FFT_SPB, n2, LANES), lambda j, kk, k: (kk * steps + k, 0, j))
    out = jax.ShapeDtypeStruct((rp, n2, oc), F32)
    return pl.pallas_call(
        functools.partial(_filter_mid_kernel, nslab=plan["r"], rh=rh, n2=n2),
        grid=(nj, kh, steps),
        in_specs=[fwd, bwd, fspec, tspec, tspec, wspec, vec, vec],
        out_specs=[ospec, ospec],
        out_shape=[out, out],
        scratch_shapes=[pltpu.VMEM((n2 * rh, LANES), F32) for _ in range(4)],
        compiler_params=_cparams(("parallel", "arbitrary", "arbitrary")),
        name="filter_mid",
    )(h, h, f1k, *plan["tw"], plan["f2"], inv, b0)


def _conv_mid_kernel(x_ref, f_ref, tr_ref, ti_ref, wf_ref, wi_ref, kr_ref, ki_ref, g_ref,
                     y_ref, ar_ref, ai_ref, dr_ref, di_ref, *, nslab, rh, n2):
    k0 = pl.program_id(3) * FFT_SPB

    @pl.when((pl.program_id(2) == 0) & (pl.program_id(3) == 0))
    def _():
        y_ref[...] = jnp.zeros_like(y_ref)

    @pl.when(pl.program_id(3) == 0)
    def _():
        _outer_dft_chunk(x_ref, f_ref, ar_ref, ai_ref, rh, n2, gather=True)

    @pl.when(pl.program_id(2) * rh + k0 < nslab)
    def _():
        for t in range(FFT_SPB):
            rows = pl.ds(k0 + t, n2, stride=rh)
            tr, ti = tr_ref[t], ti_ref[t]
            x2 = jnp.concatenate(_twiddle(ar_ref[rows, :], ai_ref[rows, :], tr, ti), axis=1)
            xr, xi = _cpair(_dot1(wf_ref[...], x2), n2)
            kr, ki = kr_ref[t], ki_ref[t]
            y2 = jnp.concatenate([xr * kr - xi * ki, xr * ki + xi * kr], axis=1)
            cr, ci = _cpair(_dot1(wi_ref[...], y2), n2)
            dr, di = _twiddle(cr, ci, tr, -ti)
            dr_ref[rows, :] = dr
            di_ref[rows, :] = di

    @pl.when(pl.program_id(2) * rh + k0 >= nslab)
    def _():
        for t in range(FFT_SPB):
            rows = pl.ds(k0 + t, n2, stride=rh)
            dr_ref[rows, :] = jnp.zeros((n2, LANES), F32)
            di_ref[rows, :] = jnp.zeros((n2, LANES), F32)

    @pl.when(pl.program_id(3) == pl.num_programs(3) - 1)
    def _():
        n1c = y_ref.shape[0]
        n1r = y_ref.shape[1] // n2
        for g in range(n2 // FFT_GROUP):
            cols = []
            for s in range(FFT_GROUP):
                r0 = (g * FFT_GROUP + s) * rh
                cols.append(jnp.concatenate([dr_ref[r0:r0 + rh, :], di_ref[r0:r0 + rh, :]], axis=0))
            y = _dot1(g_ref[...], jnp.concatenate(cols, axis=1))
            for c in range(n1c):
                for s in range(FFT_GROUP):
                    t0 = (g * FFT_GROUP + s) * n1r
                    y_ref[c, t0:t0 + n1r, :] += y[c * n1r:(c + 1) * n1r, s * LANES:(s + 1) * LANES]


def _conv_mid(x, row_blk0, col_blk0, l, nb, c, kr, ki, order, plan):
    n2, rh, kh = plan["n2"], plan["rh"], plan["kh"]
    f1k, g = plan["f1k"], plan["g"]
    nh = g.shape[1]
    n1r = min(16, nh)
    n1c = nh // n1r
    koff = order * (c // LANES)
    xspec = pl.BlockSpec((l, LANES), lambda b, j, kk, k: (row_blk0 + b, col_blk0 + j))
    fspec = pl.BlockSpec((None,) + f1k.shape[1:], lambda b, j, kk, k: (kk, 0, 0))
    steps = rh // FFT_SPB
    tspec = pl.BlockSpec((FFT_SPB, n2, LANES), lambda b, j, kk, k: (kk * steps + k, 0, 0))
    wspec = pl.BlockSpec((2 * n2, n2), lambda b, j, kk, k: (0, 0))
    kspec = pl.BlockSpec((FFT_SPB, n2, LANES), lambda b, j, kk, k: (kk * steps + k, 0, j + koff))
    gspec = pl.BlockSpec((None, nh, 2 * rh), lambda b, j, kk, k: (kk, 0, 0))
    return pl.pallas_call(
        functools.partial(_conv_mid_kernel, nslab=plan["r"], rh=rh, n2=n2),
        grid=(nb, c // LANES, kh, steps),
        in_specs=[xspec, fspec, tspec, tspec, wspec, wspec, kspec, kspec, gspec],
        out_specs=pl.BlockSpec((None, n1c, n2 * n1r, LANES), lambda b, j, kk, k: (b, 0, 0, j)),
        out_shape=jax.ShapeDtypeStruct((nb, n1c, n2 * n1r, c), F32),
        scratch_shapes=[pltpu.VMEM((n2 * rh, LANES), F32) for _ in range(4)],
        compiler_params=_cparams(("parallel", "parallel", "arbitrary", "arbitrary")),
        name="conv_mid",
    )(x, f1k, *plan["tw"], plan["f2"], plan["f2i"], kr, ki, g)


def _gate_kernel(y_ref, u_ref, g_ref, bias_ref, o_ref, *, n2):
    n1r = y_ref.shape[0] // n2
    for a in range(n1r):
        rows = slice(a * n2, (a + 1) * n2)
        yt = y_ref[pl.ds(a, n2, stride=n1r), :]
        o_ref[rows, :] = (g_ref[rows, :] * (yt + u_ref[rows, :] * bias_ref[...])).astype(o_ref.dtype)


def _gate(y, n2, u, u_row0, u_col0, gate, g_row0, g_col0, bias, out_dtype):
    nb, n1c, yr, c = y.shape
    tr = yr
    return pl.pallas_call(
        functools.partial(_gate_kernel, n2=n2),
        grid=(nb, c // LANES, n1c),
        in_specs=[pl.BlockSpec((None, None, yr, LANES), lambda b, j, q: (b, q, 0, j)),
                  pl.BlockSpec((tr, LANES), lambda b, j, q: (u_row0 // tr + b * n1c + q, u_col0 + j)),
                  pl.BlockSpec((tr, LANES), lambda b, j, q: (g_row0 // tr + b * n1c + q, g_col0 + j)),
                  pl.BlockSpec((1, LANES), lambda b, j, q: (0, j))],
        out_specs=pl.BlockSpec((tr, LANES), lambda b, j, q: (b * n1c + q, j)),
        out_shape=jax.ShapeDtypeStruct((nb * n1c * tr, c), out_dtype),
        compiler_params=_cparams(("parallel", "parallel", "arbitrary")),
        name="hyena_gate",
    )(y, u, gate, bias)


def _np_bf16(a):
    return np.asarray(a, np.float32).astype(BF16)


def _fft_plan(l):
    n = 2 * l
    n2 = FFT_N2
    n1 = n // n2
    r = n1 // 2 + 1
    rh = FFT_RH
    kh = -(-r // rh)
    rp = kh * rh
    kn1 = n1 // 2
    k1 = np.arange(rp, dtype=np.float64)[:, None]
    live = (k1 < r).astype(np.float64)

    ang = 2 * np.pi * k1 * np.arange(kn1)[None, :] / n1
    f1k = np.concatenate([(np.cos(ang) * live).reshape(kh, rh, kn1),
                          (-np.sin(ang) * live).reshape(kh, rh, kn1)], axis=1)

    kk = np.arange(rp, dtype=np.float64)[None, :]
    wgt = np.where((kk == 0) | (kk == n1 // 2), 1.0, 2.0) * (kk < r) / n
    ango = 2 * np.pi * np.arange(n1 // 2)[:, None] * kk / n1
    gre = (np.cos(ango) * wgt).reshape(n1 // 2, kh, rh).transpose(1, 0, 2)
    gim = (-np.sin(ango) * wgt).reshape(n1 // 2, kh, rh).transpose(1, 0, 2)
    g = np.concatenate([gre, gim], axis=2)

    a2 = 2 * np.pi * np.outer(np.arange(n2), np.arange(n2)) / n2
    f2 = np.concatenate([np.cos(a2), -np.sin(a2)], axis=0)
    f2i = np.concatenate([np.cos(a2), np.sin(a2)], axis=0)

    idx = jnp.arange(rp, dtype=jnp.int32)[:, None] * jnp.arange(n2, dtype=jnp.int32)[None, :]
    ang = idx.astype(F32) * F32(2.0 * math.pi / n)
    tw = tuple(jnp.broadcast_to(t[:, :, None], (rp, n2, LANES)) for t in (jnp.cos(ang), -jnp.sin(ang)))
    return dict(n1=n1, n2=n2, r=r, rp=rp, kh=kh, rh=rh, f1k=_np_bf16(f1k), g=_np_bf16(g),
                f2=_np_bf16(f2), f2i=_np_bf16(f2i), tw=tw)


def _filter_features(l):
    t = jnp.linspace(0.0, 1.0, l, dtype=F32)[:, None]
    bands = jnp.linspace(1e-4, B_BANDS - 1, B_BANDS, dtype=F32)[None, :]
    w = 2.0 * math.pi * jnp.arange(l, dtype=F32)[:, None] / l
    return jnp.concatenate([t, jnp.cos(bands * w), -jnp.sin(bands * w)], axis=-1)


def _hyena_filters(plan, l, c, w1, b1, f1, w2, b2, f2, w3, b3):
    n1, n2 = plan["n1"], plan["n2"]
    z = _filter_features(l)
    z = z.reshape(n1 // 2, n2, z.shape[1]).transpose(1, 0, 2).reshape(z.shape)
    e = z.shape[1]
    ep = -(-e // 16) * 16
    z = jnp.pad(z, ((0, 0), (0, ep - e)))
    w1p = jnp.pad(w1, ((0, ep - e), (0, 0))).astype(BF16)
    max_decay = math.log(B_DECAY_TARGET) / B_FAST_DECAY_PCT
    min_decay = math.log(B_DECAY_TARGET) / B_SLOW_DECAY_PCT
    deltas = jnp.abs(jnp.linspace(min_decay, max_decay, c, dtype=F32))
    h, nrm = _filter_mlp(z, w1p, b1, f1, w2.astype(BF16), b2, f2, w3.astype(BF16), b3, deltas,
                         tl=min(512, l))
    oc = h.shape[1] // 2
    inv = 1.0 / (nrm[:, :oc] + nrm[:, oc:])
    b0 = h[0:1, oc:]
    return _filter_mid(h, inv, b0, plan)


def _hyena_conv(plan, kf, order, l, nb, u, u_row0, u_col0, gate, g_row0, g_col0, bias, c, out_dtype):
    n2 = plan["n2"]
    cb = c // LANES
    y = _conv_mid(u, u_row0 // l, u_col0 * cb, l, nb, c, kf[0], kf[1], order, plan)
    return _gate(y, n2, u, u_row0, u_col0 * cb, gate, g_row0, g_col0 * cb, bias.reshape(1, c), out_dtype)


def _hyena_mixer(u, seqs, plans, c, fw, hy_bias):
    outs = []
    for (row0, nb, l) in seqs:
        plan = plans[l]
        kf = _hyena_filters(plan, l, c, *fw)
        z = _hyena_conv(plan, kf, 0, l, nb, u, row0, 0, u, row0, 1, hy_bias[0], c, F32)
        o = _hyena_conv(plan, kf, 1, l, nb, z, 0, 0, u, row0, 2, hy_bias[1], c, BF16)
        outs.append(o)
    return jnp.concatenate(outs, axis=0)


def _rope_tables(pos, hd):
    rot = hd // ROPE_FRACTION
    half = rot // 2
    inv = ROPE_THETA ** (-(jnp.arange(half, dtype=F32) * 2.0 / rot))
    ang = pos[:, None] * inv[None, :]
    cos, sin = jnp.cos(ang), jnp.sin(ang)
    m = pos.shape[0]
    one = jnp.ones((m, hd - rot), F32)
    zero = jnp.zeros((m, hd - rot), F32)
    zh = jnp.zeros((m, half), F32)
    c = jnp.concatenate([cos, cos, one], axis=1)
    s1 = jnp.concatenate([-sin, zh, zero], axis=1)
    s2 = jnp.concatenate([zh, sin, zero], axis=1)
    rep = LANES // hd
    return tuple(jnp.tile(t, (1, rep)) for t in (c, s1, s2)), half


def _trunk(x, bounds, seqs, p):
    m, dm = x.shape
    pos = jnp.concatenate([jnp.tile(jnp.arange(l, dtype=F32), nb) for (_, nb, l) in seqs])
    tabs_a, half_a = _rope_tables(pos, A_HEAD_DIM)
    tabs_c, half_c = _rope_tables(pos, C_HEAD_DIM)
    ident = (jnp.ones((m, LANES), F32), jnp.zeros((m, LANES), F32), jnp.zeros((m, LANES), F32))
    tabs_kv = tuple(jnp.concatenate([a, b], axis=1) for a, b in zip(tabs_a, ident))
    c_hy = dm - A_WIDTH
    plans = {l: _fft_plan(l) for l in sorted({l for (_, _, l) in seqs})}
    xf = x
    xb = x.astype(BF16)
    for i in range(DEPTH):
        j = i // 2
        if i % 2 == 0:
            w_in = p['mix_e_w_in'][j].astype(BF16)
            kv0 = A_WIDTH
            hy0 = A_WIDTH + 2 * A_KV_WIDTH
            q = _matmul_rope(xb, w_in[:, :kv0], tabs_a, half_a, tn=512)
            kv = _matmul_rope(xb, w_in[:, kv0:hy0], tabs_kv, half_a, tn=2 * A_KV_WIDTH)
            u = _matmul_conv(xb, w_in[:, hy0:], p['hy_conv_w'][j], p['hy_conv_b'][j], bounds)
            a_out = _even_attention(q, kv, p['a_sink'][j], bounds)
            fw = (p['hy_w1'][j], p['hy_b1'][j], p['hy_f1'][j], p['hy_w2'][j], p['hy_b2'][j],
                  p['hy_f2'][j], p['hy_w3'][j], p['hy_b3'][j])
            h_out = _hyena_mixer(u, seqs, plans, c_hy, fw, p['hy_bias'][j])
            w_out = p['mix_e_w_out'][j].astype(BF16)
            xf, xb = _matmul_ln([a_out, h_out], [w_out[:A_WIDTH], w_out[A_WIDTH:]], xf,
                                p['ln1_g'][i], p['ln1_b'][i])
        else:
            w_in = p['mix_o_w_in'][j].astype(BF16)
            gw = C_HEADS * C_HEAD_DIM
            ng = len(C_DILATIONS)
            qkv = []
            for g, d in enumerate(C_DILATIONS):
                trio = []
                for part in range(3):
                    c0 = (part * ng + g) * gw
                    trio.append(_odd_proj(xb, w_in, c0, tabs_c, half_c, d, rope=part < 2))
                qkv.append(tuple(trio))
            o = _odd_attention(qkv, bounds, m)
            xf, xb = _matmul_ln([o], [p['mix_o_w_out'][j].astype(BF16)], xf, p['ln1_g'][i], p['ln1_b'][i])
        xf, xb = _ffn_ln(xb, xf, p['ffn_w_gate'][i].astype(BF16), p['ffn_w_up'][i].astype(BF16),
                         p['ffn_w_down'][i].astype(BF16), p['ln2_g'][i], p['ln2_b'][i])
    return xf


def kernel(x_prompt, x_sample, mix_e_w_in, a_sink, hy_conv_w, hy_conv_b, hy_w1, hy_b1, hy_f1, hy_w2, hy_b2,
           hy_f2, hy_w3, hy_b3, hy_bias, mix_e_w_out, mix_o_w_in, mix_o_w_out, ffn_w_gate, ffn_w_up,
           ffn_w_down, ln1_g, ln1_b, ln2_g, ln2_b):
    p = dict(mix_e_w_in=mix_e_w_in, a_sink=a_sink, hy_conv_w=hy_conv_w, hy_conv_b=hy_conv_b,
             hy_w1=hy_w1, hy_b1=hy_b1, hy_f1=hy_f1, hy_w2=hy_w2, hy_b2=hy_b2, hy_f2=hy_f2,
             hy_w3=hy_w3, hy_b3=hy_b3, hy_bias=hy_bias, mix_e_w_out=mix_e_w_out,
             mix_o_w_in=mix_o_w_in, mix_o_w_out=mix_o_w_out, ffn_w_gate=ffn_w_gate,
             ffn_w_up=ffn_w_up, ffn_w_down=ffn_w_down, ln1_g=ln1_g, ln1_b=ln1_b,
             ln2_g=ln2_g, ln2_b=ln2_b)
    dm = x_prompt.shape[-1]
    seqs, bounds, row = [], [0], 0
    for xs in (x_prompt, x_sample):
        nb, l = xs.shape[0], xs.shape[1]
        seqs.append((row, nb, l))
        for _ in range(nb):
            row += l
            bounds.append(row)
    x = jnp.concatenate([x_prompt.reshape(-1, dm), x_sample.reshape(-1, dm)], axis=0)
    y = _trunk(x, tuple(bounds), tuple(seqs), p)
    n_p = x_prompt.shape[0] * x_prompt.shape[1]
    return (y[:n_p].reshape(x_prompt.shape), y[n_p:].reshape(x_sample.shape))
```

```python
import functools
import math

import numpy as np
import jax
import jax.numpy as jnp
from jax import lax
from jax.experimental import pallas as pl
from jax.experimental.pallas import tpu as pltpu

F32 = jnp.float32
BF16 = jnp.bfloat16

DEPTH = 4
A_HEADS, A_KV_HEADS, A_HEAD_DIM, A_RADIUS = 16, 2, 64, 128
A_WIDTH = A_HEADS * A_HEAD_DIM
A_KV_WIDTH = A_KV_HEADS * A_HEAD_DIM
B_SHORT, B_EMB = 3, 33
B_BANDS = (B_EMB - 1) // 2
B_DECAY_TARGET, B_FAST_DECAY_PCT, B_SLOW_DECAY_PCT = 1e-2, 0.3, 1.5
C_HEADS, C_HEAD_DIM = 16, 128
C_DILATIONS = (1, 4, 16)
C_RADIUS = 64
ROPE_THETA, ROPE_FRACTION = 500000.0, 4
ALPHA = (2 * DEPTH) ** 0.25
LN_EPS = 1e-5

LANES = 128
VMEM_LIMIT = 56 * 1024 * 1024
FFT_N2 = 256

ODD_CHUNK = 2048
ATT_TQ = 256


def _cparams(sem):
    return pltpu.CompilerParams(dimension_semantics=sem, vmem_limit_bytes=VMEM_LIMIT)


def _seq_bounds(row, bounds):
    start = jnp.int32(bounds[0])
    end = jnp.int32(bounds[1])
    for b0, b1 in zip(bounds[1:-1], bounds[2:]):
        inside = row >= b0
        start = jnp.where(inside, jnp.int32(b0), start)
        end = jnp.where(inside, jnp.int32(b1), end)
    return start, end


def _rope(a, c, s1, s2, half):
    w = a.shape[-1]
    return a * c + pltpu.roll(a, w - half, 1) * s1 + pltpu.roll(a, half, 1) * s2


def _mm_conv_kernel(xp_ref, xm_ref, xn_ref, w_ref, cw_ref, cb_ref, o_ref, *, bounds):
    i = pl.program_id(0)
    tm, hb = xm_ref.shape[0], xp_ref.shape[0]
    rows = tm + 2 * hb
    row0 = i * tm
    start, end = _seq_bounds(row0, bounds)
    lhs = jnp.concatenate([xp_ref[...], xm_ref[...], xn_ref[...]], axis=0)
    acc = jnp.dot(lhs, w_ref[...], preferred_element_type=F32)
    h0 = acc[hb:hb + tm]
    hm = pltpu.roll(acc, 1, 0)[hb:hb + tm]
    hp = pltpu.roll(acc, rows - 1, 0)[hb:hb + tm]
    ridx = lax.broadcasted_iota(jnp.int32, h0.shape, 0)
    hm = jnp.where((ridx == 0) & (row0 <= start), 0.0, hm)
    hp = jnp.where((ridx == tm - 1) & (row0 + tm >= end), 0.0, hp)
    y = cb_ref[...] + hm * cw_ref[0:1, :]
    y = y + h0 * cw_ref[1:2, :]
    y = y + hp * cw_ref[2:3, :]
    o_ref[...] = y


def _matmul_conv(x, w, cw, cb, bounds, tn=512, tm=1024):
    m, k = x.shape
    n = w.shape[1]
    hb = 16
    per = tm // hb
    last = m // hb - 1
    return pl.pallas_call(
        functools.partial(_mm_conv_kernel, bounds=bounds),
        grid=(m // tm, n // tn),
        in_specs=[pl.BlockSpec((hb, k), lambda i, j: (jnp.maximum(i * per - 1, 0), 0)),
                  pl.BlockSpec((tm, k), lambda i, j: (i, 0)),
                  pl.BlockSpec((hb, k), lambda i, j: (jnp.minimum((i + 1) * per, last), 0)),
                  pl.BlockSpec((k, tn), lambda i, j: (0, j)),
                  pl.BlockSpec((B_SHORT, tn), lambda i, j: (0, j)),
                  pl.BlockSpec((1, tn), lambda i, j: (0, j))],
        out_specs=pl.BlockSpec((tm, tn), lambda i, j: (i, j)),
        out_shape=jax.ShapeDtypeStruct((m, n), F32),
        compiler_params=_cparams(("parallel", "arbitrary")),
        name="matmul_conv",
    )(x, x, x, w, cw, cb.reshape(1, n))


def _mm_rope_kernel(x_ref, w_ref, c_ref, s1_ref, s2_ref, o_ref, *, half):
    acc = jnp.dot(x_ref[...], w_ref[...], preferred_element_type=F32)
    tw = c_ref.shape[1]
    rc = 256
    for r0 in range(0, acc.shape[0], rc):
        rows = slice(r0, r0 + rc)
        for c in range(acc.shape[1] // tw):
            cols = slice(c * tw, (c + 1) * tw)
            o_ref[rows, cols] = _rope(acc[rows, cols], c_ref[rows, :], s1_ref[rows, :], s2_ref[rows, :],
                                      half).astype(o_ref.dtype)


def _matmul_rope(x, w, tabs, half, tn, tm=1024):
    m, k = x.shape
    n = w.shape[1]
    tw = tabs[0].shape[1]
    tab_spec = pl.BlockSpec((tm, tw), lambda i, j: (i, 0))
    return pl.pallas_call(
        functools.partial(_mm_rope_kernel, half=half),
        grid=(m // tm, n // tn),
        in_specs=[pl.BlockSpec((tm, k), lambda i, j: (i, 0)),
                  pl.BlockSpec((k, tn), lambda i, j: (0, j)),
                  tab_spec, tab_spec, tab_spec],
        out_specs=pl.BlockSpec((tm, tn), lambda i, j: (i, j)),
        out_shape=jax.ShapeDtypeStruct((m, n), BF16),
        compiler_params=_cparams(("parallel", "arbitrary")),
        name="matmul_rope",
    )(x, w, *tabs)


def _odd_proj_kernel(x_ref, w_ref, c_ref, s1_ref, s2_ref, o_ref, acc_ref, tmp_ref, *, d, rope, half):
    hps = w_ref.shape[1] // LANES
    tm = x_ref.shape[0]
    t = tm // d
    pair = 2
    rc = 256
    for p in range(hps // pair):
        acc = jnp.dot(x_ref[...], w_ref[:, p * pair * LANES:(p + 1) * pair * LANES], preferred_element_type=F32)
        for h2 in range(pair):
            hh = p * pair + h2
            slot = (p % 2) * pair + h2
            acc_ref[slot, :, :] = acc[:, h2 * LANES:(h2 + 1) * LANES]
            for c0 in range(0, tm, rc):
                a = acc_ref[slot, c0:c0 + rc, :]
                if rope:
                    a = _rope(a, c_ref[c0:c0 + rc, :], s1_ref[c0:c0 + rc, :], s2_ref[c0:c0 + rc, :], half)
                if d == 1:
                    o_ref[hh, 0, c0:c0 + rc, :] = a.astype(BF16)
                elif rope:
                    acc_ref[slot, c0:c0 + rc, :] = a
            if d == 16:
                ts = hh % 2
                for ra in range(4):
                    tmp_ref[ts, ra * (tm // 4):(ra + 1) * (tm // 4), :] = acc_ref[slot, pl.ds(ra, tm // 4, stride=4), :]
                for ra in range(4):
                    for rb in range(4):
                        o_ref[hh, ra + 4 * rb, :, :] = tmp_ref[ts, pl.ds(ra * (tm // 4) + rb, t, stride=4), :].astype(BF16)
            elif d > 1:
                for r in range(d):
                    o_ref[hh, r, :, :] = acc_ref[slot, pl.ds(r, t, stride=d), :].astype(BF16)


def _odd_proj(x, w, col0, tabs, half, d, rope, hps=8):
    m, k = x.shape
    tm = ODD_CHUNK
    cb0 = col0 // (hps * LANES)
    tab_spec = pl.BlockSpec((tm, LANES), lambda i, j: (i, 0))
    return pl.pallas_call(
        functools.partial(_odd_proj_kernel, d=d, rope=rope, half=half),
        grid=(m // tm, C_HEADS // hps),
        in_specs=[pl.BlockSpec((tm, k), lambda i, j: (i, 0)),
                  pl.BlockSpec((k, hps * LANES), lambda i, j: (0, cb0 + j)),
                  tab_spec, tab_spec, tab_spec],
        out_specs=pl.BlockSpec((hps, d, tm // d, LANES), lambda i, j: (j, 0, i, 0)),
        out_shape=jax.ShapeDtypeStruct((C_HEADS, d, m // d, LANES), BF16),
        scratch_shapes=[pltpu.VMEM((4, tm, LANES), F32), pltpu.VMEM((2, tm, LANES), F32)],
        compiler_params=_cparams(("parallel", "arbitrary")),
        name="odd_proj",
    )(x, w, *tabs)


LN_ROWS = 128


def _layer_norm_store(x_ref, acc_ref, g_ref, b_ref, of_ref, ob_ref, row0=0, nrows=None):
    nrows = x_ref.shape[0] if nrows is None else nrows
    for c in range(nrows // LN_ROWS):
        rows = pl.ds(row0 + c * LN_ROWS, LN_ROWS)
        r = ALPHA * x_ref[rows, :] + acc_ref[rows, :]
        mu = jnp.mean(r, axis=-1, keepdims=True)
        xc = r - mu
        var = jnp.mean(xc * xc, axis=-1, keepdims=True)
        y = xc * lax.rsqrt(var + LN_EPS) * g_ref[...] + b_ref[...]
        of_ref[rows, :] = y
        ob_ref[rows, :] = y.astype(BF16)


def _mm_ln_kernel(*refs, n_in):
    ys = refs[:n_in]
    ws = refs[n_in:2 * n_in]
    x_ref, g_ref, b_ref, of_ref, ob_ref, acc_ref = refs[2 * n_in:]
    half = x_ref.shape[0] // 2
    for r0 in (0, half):
        rows = slice(r0, r0 + half)
        acc = jnp.dot(ys[0][rows, :], ws[0][...], preferred_element_type=F32)
        for y_ref, w_ref in zip(ys[1:], ws[1:]):
            acc = acc + jnp.dot(y_ref[rows, :], w_ref[...], preferred_element_type=F32)
        acc_ref[rows, :] = acc
        _layer_norm_store(x_ref, acc_ref, g_ref, b_ref, of_ref, ob_ref, r0, half)


def _matmul_ln(ys, ws, x, g, b, tm=512):
    m, dm = x.shape
    n_in = len(ys)
    in_specs = [pl.BlockSpec((tm, y.shape[1]), lambda i: (i, 0)) for y in ys]
    in_specs += [pl.BlockSpec(w.shape, lambda i: (0, 0)) for w in ws]
    in_specs += [pl.BlockSpec((tm, dm), lambda i: (i, 0)),
                 pl.BlockSpec((1, dm), lambda i: (0, 0)),
                 pl.BlockSpec((1, dm), lambda i: (0, 0))]
    return pl.pallas_call(
        functools.partial(_mm_ln_kernel, n_in=n_in),
        grid=(m // tm,),
        in_specs=in_specs,
        out_specs=[pl.BlockSpec((tm, dm), lambda i: (i, 0)), pl.BlockSpec((tm, dm), lambda i: (i, 0))],
        out_shape=[jax.ShapeDtypeStruct((m, dm), F32), jax.ShapeDtypeStruct((m, dm), BF16)],
        scratch_shapes=[pltpu.VMEM((tm, dm), F32)],
        compiler_params=_cparams(("parallel",)),
        name="matmul_ln",
    )(*ys, *ws, x, g.reshape(1, dm), b.reshape(1, dm))


def _ffn_kernel(xb_ref, xf_ref, wg_ref, wu_ref, wd_ref, g_ref, b_ref, of_ref, ob_ref, acc_ref):
    j = pl.program_id(1)

    @pl.when(j == 0)
    def _():
        acc_ref[...] = jnp.zeros_like(acc_ref)

    xb = xb_ref[...]
    gate = jnp.dot(xb, wg_ref[...], preferred_element_type=F32)
    up = jnp.dot(xb, wu_ref[...], preferred_element_type=F32)
    h = (gate * jax.nn.sigmoid(gate)) * up
    acc_ref[...] += jnp.dot(h.astype(BF16), wd_ref[...], preferred_element_type=F32)

    @pl.when(j == pl.num_programs(1) - 1)
    def _():
        _layer_norm_store(xf_ref, acc_ref, g_ref, b_ref, of_ref, ob_ref)


def _ffn_ln(xb, xf, wg, wu, wd, g, b, tm=512, tf=512):
    m, dm = xf.shape
    dff = wg.shape[1]
    row = lambda i, j: (i, 0)
    return pl.pallas_call(
        _ffn_kernel,
        grid=(m // tm, dff // tf),
        in_specs=[pl.BlockSpec((tm, dm), row), pl.BlockSpec((tm, dm), row),
                  pl.BlockSpec((dm, tf), lambda i, j: (0, j)),
                  pl.BlockSpec((dm, tf), lambda i, j: (0, j)),
                  pl.BlockSpec((tf, dm), lambda i, j: (j, 0)),
                  pl.BlockSpec((1, dm), lambda i, j: (0, 0)),
                  pl.BlockSpec((1, dm), lambda i, j: (0, 0))],
        out_specs=[pl.BlockSpec((tm, dm), row), pl.BlockSpec((tm, dm), row)],
        out_shape=[jax.ShapeDtypeStruct((m, dm), F32), jax.ShapeDtypeStruct((m, dm), BF16)],
        scratch_shapes=[pltpu.VMEM((tm, dm), F32)],
        compiler_params=_cparams(("parallel", "arbitrary")),
        name="ffn_ln",
    )(xb, xf, wg, wu, wd, g.reshape(1, dm), b.reshape(1, dm))


def _band_bias(nq, radius):
    r = np.arange(nq)[:, None]
    c = np.arange(nq + 2 * radius)[None, :]
    return np.where(np.abs(c - radius - r) <= radius, 0.0, -np.inf).astype(np.float32)


def _even_attn_kernel(sink_ref, band_ref, q_ref, kp_ref, km_ref, kn_ref, o_ref, bias_ref, *, bounds):
    i = pl.program_id(0)
    tq = q_ref.shape[0]
    sub = A_RADIUS
    nk = sub + 2 * A_RADIUS
    row0 = i * tq
    start, end = _seq_bounds(row0, bounds)
    group = A_HEADS // A_KV_HEADS
    scale = A_HEAD_DIM ** -0.5
    blocks = [kp_ref, km_ref, kn_ref]
    assert tq == 2 * sub
    for sb in range(tq // sub):
        kv = jnp.concatenate([r[...] for r in blocks[sb:sb + 2]], axis=0)
        rk = row0 + sb * sub - A_RADIUS + lax.broadcasted_iota(jnp.int32, (1, nk), 1)
        bias = band_ref[...] + jnp.where((rk >= start) & (rk < end), 0.0, -jnp.inf)
        bias_ref[0:sub, :] = bias
        bias_ref[sub:2 * sub, :] = bias
        first = lax.broadcasted_iota(jnp.int32, (2 * sub, 1), 0) < sub
        rows = slice(sb * sub, (sb + 1) * sub)
        for j in range(A_KV_HEADS):
            k = kv[:, j * A_HEAD_DIM:(j + 1) * A_HEAD_DIM]
            v = kv[:, A_KV_WIDTH + j * A_HEAD_DIM:A_KV_WIDTH + (j + 1) * A_HEAD_DIM]
            for gq in range(0, group, 2):
                heads = (j * group + gq, j * group + gq + 1)
                cols = [slice(h * A_HEAD_DIM, (h + 1) * A_HEAD_DIM) for h in heads]
                qh = jnp.concatenate([q_ref[rows, c] for c in cols], axis=0)
                s = lax.dot_general(qh, k, (((1,), (1,)), ((), ())), preferred_element_type=F32) * scale
                s = s + bias_ref[...]
                sk = jnp.where(first, sink_ref[heads[0]], sink_ref[heads[1]])
                m = jnp.maximum(jnp.max(s, axis=-1, keepdims=True), sk)
                p = jnp.exp(s - m)
                den = jnp.sum(p, axis=-1, keepdims=True) + jnp.exp(sk - m)
                o = jnp.dot(p.astype(BF16), v, preferred_element_type=F32) / den
                o_ref[rows, cols[0]] = o[0:sub].astype(o_ref.dtype)
                o_ref[rows, cols[1]] = o[sub:2 * sub].astype(o_ref.dtype)


def _even_attention(q, kv, sink, bounds):
    m = q.shape[0]
    tq = ATT_TQ
    hb = A_RADIUS
    per = tq // hb
    last = m // hb - 1
    kvw = kv.shape[1]
    band = _band_bias(A_RADIUS, A_RADIUS)
    return pl.pallas_call(
        functools.partial(_even_attn_kernel, bounds=bounds),
        grid=(m // tq,),
        in_specs=[pl.BlockSpec(memory_space=pltpu.SMEM),
                  pl.BlockSpec(band.shape, lambda i: (0, 0)),
                  pl.BlockSpec((tq, A_WIDTH), lambda i: (i, 0)),
                  pl.BlockSpec((hb, kvw), lambda i: (jnp.maximum(i * per - 1, 0), 0)),
                  pl.BlockSpec((tq, kvw), lambda i: (i, 0)),
                  pl.BlockSpec((hb, kvw), lambda i: (jnp.minimum((i + 1) * per, last), 0))],
        out_specs=pl.BlockSpec((tq, A_WIDTH), lambda i: (i, 0)),
        out_shape=jax.ShapeDtypeStruct((m, A_WIDTH), BF16),
        scratch_shapes=[pltpu.VMEM((2 * band.shape[0], band.shape[1]), F32)],
        compiler_params=_cparams(("parallel",)),
        name="even_attention",
    )(sink, band, q, kv, kv, kv)


def _odd_attn_kernel(*refs, bounds):
    ng = len(C_DILATIONS)
    band_ref = refs[7 * ng]
    o_ref, oacc, lacc = refs[7 * ng + 1:]
    i = pl.program_id(0)
    chunk = ODD_CHUNK
    qb = 128
    nk = qb + 2 * C_RADIUS
    row0 = i * chunk
    start, end = _seq_bounds(row0, bounds)
    cc = lax.broadcasted_iota(jnp.int32, (1, nk), 1)
    scale = C_HEAD_DIM ** -0.5
    for g, d in enumerate(C_DILATIONS):
        q_ref, kp_ref, km_ref, kn_ref, vp_ref, vm_ref, vn_ref = refs[7 * g:7 * g + 7]
        tg = chunk // d
        t_lo, t_hi, t_c0 = start // d, end // d, row0 // d
        for sb in range(tg // qb):
            lo, hi = qb * sb - C_RADIUS, qb * sb + qb + C_RADIUS
            tk = t_c0 + lo + cc
            col = jnp.where((tk >= t_lo) & (tk < t_hi), 0.0, -jnp.inf)
            for r in range(d):
                def window(p_ref, m_ref, n_ref):
                    parts = []
                    if lo < 0:
                        parts.append(p_ref[0, r, :, :])
                    parts.append(m_ref[0, r, max(lo, 0):min(hi, tg), :])
                    if hi > tg:
                        parts.append(n_ref[0, r, :, :])
                    return parts[0] if len(parts) == 1 else jnp.concatenate(parts, axis=0)

                q = q_ref[0, r, qb * sb:qb * (sb + 1), :]
                k = window(kp_ref, km_ref, kn_ref)
                v = window(vp_ref, vm_ref, vn_ref)
                s = lax.dot_general(q, k, (((1,), (1,)), ((), ())), preferred_element_type=F32) * scale
                s = s + band_ref[...] + col
                m = jnp.max(s, axis=-1, keepdims=True)
                p = jnp.exp(s - m)
                den = jnp.sum(p, axis=-1, keepdims=True)
                o = jnp.dot(p.astype(BF16), v, preferred_element_type=F32) / den
                lse = jnp.broadcast_to(m + jnp.log(den), (qb, LANES))
                if d == 1:
                    rows = pl.ds(qb * sb, qb)
                else:
                    rows = pl.ds(r + d * qb * sb, qb, stride=d)
                oacc[g, rows, :] = o
                lacc[g, rows, :] = lse
    ls = [lacc[g] for g in range(ng)]
    mx = functools.reduce(jnp.maximum, ls)
    ws = [jnp.exp(l - mx) for l in ls]
    tot = functools.reduce(lambda a, b: a + b, ws)
    out = functools.reduce(lambda a, b: a + b, [(ws[g] / tot) * oacc[g] for g in range(ng)])
    o_ref[...] = out.astype(o_ref.dtype)


def _odd_attention(qkv, bounds, m):
    chunk = ODD_CHUNK
    hb = C_RADIUS
    operands, in_specs = [], []
    for (q, k, v), d in zip(qkv, C_DILATIONS):
        tg = chunk // d
        per = tg // hb
        last = m // d // hb - 1
        main = pl.BlockSpec((1, d, tg, LANES), lambda i, h: (h, 0, i, 0))
        prev = pl.BlockSpec((1, d, hb, LANES), lambda i, h, per=per: (h, 0, jnp.maximum(i * per - 1, 0), 0))
        nxt = pl.BlockSpec((1, d, hb, LANES), lambda i, h, per=per, last=last: (h, 0, jnp.minimum((i + 1) * per, last), 0))
        operands += [q, k, k, k, v, v, v]
        in_specs += [main, prev, main, nxt, prev, main, nxt]
    band = _band_bias(128, C_RADIUS)
    operands.append(band)
    in_specs.append(pl.BlockSpec(band.shape, lambda i, h: (0, 0)))
    ng = len(C_DILATIONS)
    return pl.pallas_call(
        functools.partial(_odd_attn_kernel, bounds=bounds),
        grid=(m // chunk, C_HEADS),
        in_specs=in_specs,
        out_specs=pl.BlockSpec((chunk, LANES), lambda i, h: (i, h)),
        out_shape=jax.ShapeDtypeStruct((m, C_HEADS * C_HEAD_DIM), BF16),
        scratch_shapes=[pltpu.VMEM((ng, chunk, LANES), F32), pltpu.VMEM((ng, chunk, LANES), F32)],
        compiler_params=_cparams(("parallel", "arbitrary")),
        name="odd_attention",
    )(*operands)


def _filter_mlp_kernel(z_ref, w1_ref, b1_ref, f1_ref, w2_ref, b2_ref, f2_ref, w3_ref, b3_ref, dl_ref,
                       h_ref, nrm_ref):
    i = pl.program_id(0)
    z = z_ref[...]
    h = jnp.sin(f1_ref[...] * (jnp.dot(z.astype(BF16), w1_ref[...], preferred_element_type=F32) + b1_ref[...]))
    h = jnp.sin(f2_ref[...] * (jnp.dot(h.astype(BF16), w2_ref[...], preferred_element_type=F32) + b2_ref[...]))
    h = jnp.dot(h.astype(BF16), w3_ref[...], preferred_element_type=F32) + b3_ref[...]
    decay = jnp.exp(-z[:, 0:1] * dl_ref[...])
    nrep = h.shape[1] // decay.shape[1]
    h = h * jnp.concatenate([decay] * nrep, axis=1)
    h_ref[...] = h

    @pl.when(i == 0)
    def _():
        nrm_ref[...] = jnp.zeros_like(nrm_ref)

    half = h.shape[1] // 2
    col = lax.broadcasted_iota(jnp.int32, h.shape, 1)
    row = lax.broadcasted_iota(jnp.int32, h.shape, 0) + i * h.shape[0]
    a = jnp.where((col >= half) & (row == 0), 0.0, jnp.abs(h))
    nrm_ref[...] += jnp.sum(a, axis=0, keepdims=True)


def _filter_mlp(z, w1, b1, f1, w2, b2, f2, w3, b3, deltas, tl=512):
    l, e = z.shape
    hid = w1.shape[1]
    n = w3.shape[1]
    c = deltas.shape[0]
    full = lambda shape: pl.BlockSpec(shape, lambda i: (0, 0))
    return pl.pallas_call(
        _filter_mlp_kernel,
        grid=(l // tl,),
        in_specs=[pl.BlockSpec((tl, e), lambda i: (i, 0)),
                  full((e, hid)), full((1, hid)), full((1, hid)),
                  full((hid, hid)), full((1, hid)), full((1, hid)),
                  full((hid, n)), full((1, n)), full((1, c))],
        out_specs=[pl.BlockSpec((tl, n), lambda i: (i, 0)), full((1, n))],
        out_shape=[jax.ShapeDtypeStruct((l, n), F32), jax.ShapeDtypeStruct((1, n), F32)],
        compiler_params=_cparams(("arbitrary",)),
        name="filter_mlp",
    )(z, w1, b1.reshape(1, hid), f1.reshape(1, hid), w2, b2.reshape(1, hid), f2.reshape(1, hid),
      w3, b3.reshape(1, n), deltas.reshape(1, c))


def _dot1(ch, x):
    return jnp.dot(ch, x.astype(BF16), preferred_element_type=F32)


FFT_GROUP = 8


FFT_RH = 24
FFT_SPB = 6


def _outer_dft_chunk(x_ref, f_ref, ar_ref, ai_ref, rh, n2, gather):
    kn1 = f_ref.shape[1]
    for g in range(n2 // FFT_GROUP):
        cols = []
        for s in range(FFT_GROUP):
            i2 = g * FFT_GROUP + s
            cols.append(x_ref[pl.ds(i2, kn1, stride=n2), :] if gather else x_ref[i2 * kn1:(i2 + 1) * kn1, :])
        out = _dot1(f_ref[...], jnp.concatenate(cols, axis=1))
        for s in range(FFT_GROUP):
            r0 = (g * FFT_GROUP + s) * rh
            ar_ref[r0:r0 + rh, :] = out[:rh, s * LANES:(s + 1) * LANES]
            ai_ref[r0:r0 + rh, :] = out[rh:, s * LANES:(s + 1) * LANES]


def _cpair(p, n):
    return p[:n, :LANES] - p[n:, LANES:], p[:n, LANES:] + p[n:, :LANES]


def _twiddle(xr, xi, tr, ti):
    return xr * tr - xi * ti, xr * ti + xi * tr


def _filter_mid_kernel(hf_ref, hb_ref, f_ref, tr_ref, ti_ref, w_ref, inv_ref, b0_ref,
                       kr_ref, ki_ref, fr_ref, fi_ref, br_ref, bi_ref, *, nslab, rh, n2):
    k0 = pl.program_id(2) * FFT_SPB

    @pl.when(pl.program_id(2) == 0)
    def _():
        _outer_dft_chunk(hf_ref, f_ref, fr_ref, fi_ref, rh, n2, gather=False)
        _outer_dft_chunk(hb_ref, f_ref, br_ref, bi_ref, rh, n2, gather=False)

    @pl.when(pl.program_id(1) * rh + k0 < nslab)
    def _():
        for t in range(FFT_SPB):
            rows = pl.ds(k0 + t, n2, stride=rh)
            tr, ti = tr_ref[t], ti_ref[t]
            x4 = jnp.concatenate(_twiddle(fr_ref[rows, :], fi_ref[rows, :], tr, ti)
                                 + _twiddle(br_ref[rows, :], bi_ref[rows, :], tr, ti), axis=1)
            p = _dot1(w_ref[...], x4)
            fr, fi = _cpair(p[:, :2 * LANES], n2)
            br, bi = _cpair(p[:, 2 * LANES:], n2)
            kr_ref[t] = (fr + (br - b0_ref[...])) * inv_ref[...]
            ki_ref[t] = (fi - bi) * inv_ref[...]

    @pl.when(pl.program_id(1) * rh + k0 >= nslab)
    def _():
        kr_ref[...] = jnp.zeros_like(kr_ref)
        ki_ref[...] = jnp.zeros_like(ki_ref)


def _filter_mid(h, inv, b0, plan):
    n2, kh, rh, rp = plan["n2"], plan["kh"], plan["rh"], plan["rp"]
    l = h.shape[0]
    oc = h.shape[1] // 2
    nj = oc // LANES
    f1k = plan["f1k"]
    fwd = pl.BlockSpec((l, LANES), lambda j, kk, k: (0, j))
    bwd = pl.BlockSpec((l, LANES), lambda j, kk, k: (0, j + nj))
    fspec = pl.BlockSpec((None,) + f1k.shape[1:], lambda j, kk, k: (kk, 0, 0))
    steps = rh // FFT_SPB
    tspec = pl.BlockSpec((FFT_SPB, n2, LANES), lambda j, kk, k: (kk * steps + k, 0, 0))
    wspec = pl.BlockSpec((2 * n2, n2), lambda j, kk, k: (0, 0))
    vec = pl.BlockSpec((1, LANES), lambda j, kk, k: (0, j))
    ospec = pl.BlockSpec((FFT_SPB, n2, LANES), lambda j, kk, k: (kk * steps + k, 0, j))
    out = jax.ShapeDtypeStruct((rp, n2, oc), F32)
    return pl.pallas_call(
        functools.partial(_filter_mid_kernel, nslab=plan["r"], rh=rh, n2=n2),
        grid=(nj, kh, steps),
        in_specs=[fwd, bwd, fspec, tspec, tspec, wspec, vec, vec],
        out_specs=[ospec, ospec],
        out_shape=[out, out],
        scratch_shapes=[pltpu.VMEM((n2 * rh, LANES), F32) for _ in range(4)],
        compiler_params=_cparams(("parallel", "arbitrary", "arbitrary")),
        name="filter_mid",
    )(h, h, f1k, *plan["tw"], plan["f2"], inv, b0)


def _conv_mid_kernel(x_ref, f_ref, tr_ref, ti_ref, wf_ref, wi_ref, kr_ref, ki_ref, g_ref,
                     y_ref, ar_ref, ai_ref, dr_ref, di_ref, xg_ref, *, nslab, rh, n2):
    k0 = pl.program_id(3) * FFT_SPB

    @pl.when((pl.program_id(2) == 0) & (pl.program_id(3) == 0))
    def _():
        y_ref[...] = jnp.zeros_like(y_ref)

    @pl.when((pl.program_id(2) == 0) & (pl.program_id(3) == 0))
    def _():
        kn1 = f_ref.shape[1]
        for i2 in range(n2):
            xg_ref[i2 * kn1:(i2 + 1) * kn1, :] = x_ref[pl.ds(i2, kn1, stride=n2), :]

    @pl.when(pl.program_id(3) == 0)
    def _():
        _outer_dft_chunk(xg_ref, f_ref, ar_ref, ai_ref, rh, n2, gather=False)

    @pl.when(pl.program_id(2) * rh + k0 < nslab)
    def _():
        for t in range(FFT_SPB):
            rows = pl.ds(k0 + t, n2, stride=rh)
            tr, ti = tr_ref[t], ti_ref[t]
            x2 = jnp.concatenate(_twiddle(ar_ref[rows, :], ai_ref[rows, :], tr, ti), axis=1)
            xr, xi = _cpair(_dot1(wf_ref[...], x2), n2)
            kr, ki = kr_ref[t], ki_ref[t]
            y2 = jnp.concatenate([xr * kr - xi * ki, xr * ki + xi * kr], axis=1)
            cr, ci = _cpair(_dot1(wi_ref[...], y2), n2)
            dr, di = _twiddle(cr, ci, tr, -ti)
            dr_ref[rows, :] = dr
            di_ref[rows, :] = di

    @pl.when(pl.program_id(2) * rh + k0 >= nslab)
    def _():
        for t in range(FFT_SPB):
            rows = pl.ds(k0 + t, n2, stride=rh)
            dr_ref[rows, :] = jnp.zeros((n2, LANES), F32)
            di_ref[rows, :] = jnp.zeros((n2, LANES), F32)

    @pl.when(pl.program_id(3) == pl.num_programs(3) - 1)
    def _():
        n1c = y_ref.shape[0]
        n1r = y_ref.shape[1] // n2
        for g in range(n2 // FFT_GROUP):
            cols = []
            for s in range(FFT_GROUP):
                r0 = (g * FFT_GROUP + s) * rh
                cols.append(jnp.concatenate([dr_ref[r0:r0 + rh, :], di_ref[r0:r0 + rh, :]], axis=0))
            y = _dot1(g_ref[...], jnp.concatenate(cols, axis=1))
            for c in range(n1c):
                for s in range(FFT_GROUP):
                    t0 = (g * FFT_GROUP + s) * n1r
                    y_ref[c, t0:t0 + n1r, :] += y[c * n1r:(c + 1) * n1r, s * LANES:(s + 1) * LANES]


def _conv_mid(x, row_blk0, col_blk0, l, nb, c, kr, ki, order, plan):
    n2, rh, kh = plan["n2"], plan["rh"], plan["kh"]
    f1k, g = plan["f1k"], plan["g"]
    nh = g.shape[1]
    n1r = min(16, nh)
    n1c = nh // n1r
    koff = order * (c // LANES)
    xspec = pl.BlockSpec((l, LANES), lambda b, j, kk, k: (row_blk0 + b, col_blk0 + j),
                         pipeline_mode=pl.Buffered(1))
    fspec = pl.BlockSpec((None,) + f1k.shape[1:], lambda b, j, kk, k: (kk, 0, 0))
    steps = rh // FFT_SPB
    tspec = pl.BlockSpec((FFT_SPB, n2, LANES), lambda b, j, kk, k: (kk * steps + k, 0, 0))
    wspec = pl.BlockSpec((2 * n2, n2), lambda b, j, kk, k: (0, 0))
    kspec = pl.BlockSpec((FFT_SPB, n2, LANES), lambda b, j, kk, k: (kk * steps + k, 0, j + koff))
    gspec = pl.BlockSpec((None, nh, 2 * rh), lambda b, j, kk, k: (kk, 0, 0))
    return pl.pallas_call(
        functools.partial(_conv_mid_kernel, nslab=plan["r"], rh=rh, n2=n2),
        grid=(nb, c // LANES, kh, steps),
        in_specs=[xspec, fspec, tspec, tspec, wspec, wspec, kspec, kspec, gspec],
        out_specs=pl.BlockSpec((None, n1c, n2 * n1r, LANES), lambda b, j, kk, k: (b, 0, 0, j)),
        out_shape=jax.ShapeDtypeStruct((nb, n1c, n2 * n1r, c), F32),
        scratch_shapes=[pltpu.VMEM((n2 * rh, LANES), F32) for _ in range(4)] + [pltpu.VMEM((l, LANES), F32)],
        compiler_params=_cparams(("parallel", "parallel", "arbitrary", "arbitrary")),
        name="conv_mid",
    )(x, f1k, *plan["tw"], plan["f2"], plan["f2i"], kr, ki, g)


def _gate_kernel(y_ref, u_ref, g_ref, bias_ref, o_ref, *, n2):
    n1r = y_ref.shape[0] // n2
    for a in range(n1r):
        rows = slice(a * n2, (a + 1) * n2)
        yt = y_ref[pl.ds(a, n2, stride=n1r), :]
        o_ref[rows, :] = (g_ref[rows, :] * (yt + u_ref[rows, :] * bias_ref[...])).astype(o_ref.dtype)


def _gate(y, n2, u, u_row0, u_col0, gate, g_row0, g_col0, bias, out_dtype):
    nb, n1c, yr, c = y.shape
    tr = yr
    return pl.pallas_call(
        functools.partial(_gate_kernel, n2=n2),
        grid=(nb, c // LANES, n1c),
        in_specs=[pl.BlockSpec((None, None, yr, LANES), lambda b, j, q: (b, q, 0, j)),
                  pl.BlockSpec((tr, LANES), lambda b, j, q: (u_row0 // tr + b * n1c + q, u_col0 + j)),
                  pl.BlockSpec((tr, LANES), lambda b, j, q: (g_row0 // tr + b * n1c + q, g_col0 + j)),
                  pl.BlockSpec((1, LANES), lambda b, j, q: (0, j))],
        out_specs=pl.BlockSpec((tr, LANES), lambda b, j, q: (b * n1c + q, j)),
        out_shape=jax.ShapeDtypeStruct((nb * n1c * tr, c), out_dtype),
        compiler_params=_cparams(("parallel", "parallel", "arbitrary")),
        name="hyena_gate",
    )(y, u, gate, bias)


def _np_bf16(a):
    return np.asarray(a, np.float32).astype(BF16)


def _fft_plan(l):
    n = 2 * l
    n2 = FFT_N2
    n1 = n // n2
    r = n1 // 2 + 1
    rh = FFT_RH
    kh = -(-r // rh)
    rp = kh * rh
    kn1 = n1 // 2
    k1 = np.arange(rp, dtype=np.float64)[:, None]
    live = (k1 < r).astype(np.float64)

    ang = 2 * np.pi * k1 * np.arange(kn1)[None, :] / n1
    f1k = np.concatenate([(np.cos(ang) * live).reshape(kh, rh, kn1),
                          (-np.sin(ang) * live).reshape(kh, rh, kn1)], axis=1)

    kk = np.arange(rp, dtype=np.float64)[None, :]
    wgt = np.where((kk == 0) | (kk == n1 // 2), 1.0, 2.0) * (kk < r) / n
    ango = 2 * np.pi * np.arange(n1 // 2)[:, None] * kk / n1
    gre = (np.cos(ango) * wgt).reshape(n1 // 2, kh, rh).transpose(1, 0, 2)
    gim = (-np.sin(ango) * wgt).reshape(n1 // 2, kh, rh).transpose(1, 0, 2)
    g = np.concatenate([gre, gim], axis=2)

    a2 = 2 * np.pi * np.outer(np.arange(n2), np.arange(n2)) / n2
    f2 = np.concatenate([np.cos(a2), -np.sin(a2)], axis=0)
    f2i = np.concatenate([np.cos(a2), np.sin(a2)], axis=0)

    idx = jnp.arange(rp, dtype=jnp.int32)[:, None] * jnp.arange(n2, dtype=jnp.int32)[None, :]
    ang = idx.astype(F32) * F32(2.0 * math.pi / n)
    tw = tuple(jnp.broadcast_to(t[:, :, None], (rp, n2, LANES)) for t in (jnp.cos(ang), -jnp.sin(ang)))
    return dict(n1=n1, n2=n2, r=r, rp=rp, kh=kh, rh=rh, f1k=_np_bf16(f1k), g=_np_bf16(g),
                f2=_np_bf16(f2), f2i=_np_bf16(f2i), tw=tw)


def _filter_features(l):
    t = jnp.linspace(0.0, 1.0, l, dtype=F32)[:, None]
    bands = jnp.linspace(1e-4, B_BANDS - 1, B_BANDS, dtype=F32)[None, :]
    w = 2.0 * math.pi * jnp.arange(l, dtype=F32)[:, None] / l
    return jnp.concatenate([t, jnp.cos(bands * w), -jnp.sin(bands * w)], axis=-1)


def _hyena_filters(plan, l, c, w1, b1, f1, w2, b2, f2, w3, b3):
    n1, n2 = plan["n1"], plan["n2"]
    z = _filter_features(l)
    z = z.reshape(n1 // 2, n2, z.shape[1]).transpose(1, 0, 2).reshape(z.shape)
    e = z.shape[1]
    ep = -(-e // 16) * 16
    z = jnp.pad(z, ((0, 0), (0, ep - e)))
    w1p = jnp.pad(w1, ((0, ep - e), (0, 0))).astype(BF16)
    max_decay = math.log(B_DECAY_TARGET) / B_FAST_DECAY_PCT
    min_decay = math.log(B_DECAY_TARGET) / B_SLOW_DECAY_PCT
    deltas = jnp.abs(jnp.linspace(min_decay, max_decay, c, dtype=F32))
    h, nrm = _filter_mlp(z, w1p, b1, f1, w2.astype(BF16), b2, f2, w3.astype(BF16), b3, deltas,
                         tl=min(512, l))
    oc = h.shape[1] // 2
    inv = 1.0 / (nrm[:, :oc] + nrm[:, oc:])
    b0 = h[0:1, oc:]
    return _filter_mid(h, inv, b0, plan)


def _hyena_conv(plan, kf, order, l, nb, u, u_row0, u_col0, gate, g_row0, g_col0, bias, c, out_dtype):
    n2 = plan["n2"]
    cb = c // LANES
    y = _conv_mid(u, u_row0 // l, u_col0 * cb, l, nb, c, kf[0], kf[1], order, plan)
    return _gate(y, n2, u, u_row0, u_col0 * cb, gate, g_row0, g_col0 * cb, bias.reshape(1, c), out_dtype)


def _hyena_mixer(u, seqs, plans, c, fw, hy_bias):
    outs = []
    for (row0, nb, l) in seqs:
        plan = plans[l]
        kf = _hyena_filters(plan, l, c, *fw)
        z = _hyena_conv(plan, kf, 0, l, nb, u, row0, 0, u, row0, 1, hy_bias[0], c, F32)
        o = _hyena_conv(plan, kf, 1, l, nb, z, 0, 0, u, row0, 2, hy_bias[1], c, BF16)
        outs.append(o)
    return jnp.concatenate(outs, axis=0)


def _rope_tables(pos, hd):
    rot = hd // ROPE_FRACTION
    half = rot // 2
    inv = ROPE_THETA ** (-(jnp.arange(half, dtype=F32) * 2.0 / rot))
    ang = pos[:, None] * inv[None, :]
    cos, sin = jnp.cos(ang), jnp.sin(ang)
    m = pos.shape[0]
    one = jnp.ones((m, hd - rot), F32)
    zero = jnp.zeros((m, hd - rot), F32)
    zh = jnp.zeros((m, half), F32)
    c = jnp.concatenate([cos, cos, one], axis=1)
    s1 = jnp.concatenate([-sin, zh, zero], axis=1)
    s2 = jnp.concatenate([zh, sin, zero], axis=1)
    rep = LANES // hd
    return tuple(jnp.tile(t, (1, rep)) for t in (c, s1, s2)), half


def _trunk(x, bounds, seqs, p):
    m, dm = x.shape
    pos = jnp.concatenate([jnp.tile(jnp.arange(l, dtype=F32), nb) for (_, nb, l) in seqs])
    tabs_a, half_a = _rope_tables(pos, A_HEAD_DIM)
    tabs_c, half_c = _rope_tables(pos, C_HEAD_DIM)
    ident = (jnp.ones((m, LANES), F32), jnp.zeros((m, LANES), F32), jnp.zeros((m, LANES), F32))
    tabs_kv = tuple(jnp.concatenate([a, b], axis=1) for a, b in zip(tabs_a, ident))
    c_hy = dm - A_WIDTH
    plans = {l: _fft_plan(l) for l in sorted({l for (_, _, l) in seqs})}
    xf = x
    xb = x.astype(BF16)
    for i in range(DEPTH):
        j = i // 2
        if i % 2 == 0:
            w_in = p['mix_e_w_in'][j].astype(BF16)
            kv0 = A_WIDTH
            hy0 = A_WIDTH + 2 * A_KV_WIDTH
            q = _matmul_rope(xb, w_in[:, :kv0], tabs_a, half_a, tn=512)
            kv = _matmul_rope(xb, w_in[:, kv0:hy0], tabs_kv, half_a, tn=2 * A_KV_WIDTH)
            u = _matmul_conv(xb, w_in[:, hy0:], p['hy_conv_w'][j], p['hy_conv_b'][j], bounds)
            a_out = _even_attention(q, kv, p['a_sink'][j], bounds)
            fw = (p['hy_w1'][j], p['hy_b1'][j], p['hy_f1'][j], p['hy_w2'][j], p['hy_b2'][j],
                  p['hy_f2'][j], p['hy_w3'][j], p['hy_b3'][j])
            h_out = _hyena_mixer(u, seqs, plans, c_hy, fw, p['hy_bias'][j])
            w_out = p['mix_e_w_out'][j].astype(BF16)
            xf, xb = _matmul_ln([a_out, h_out], [w_out[:A_WIDTH], w_out[A_WIDTH:]], xf,
                                p['ln1_g'][i], p['ln1_b'][i])
        else:
            w_in = p['mix_o_w_in'][j].astype(BF16)
            gw = C_HEADS * C_HEAD_DIM
            ng = len(C_DILATIONS)
            qkv = []
            for g, d in enumerate(C_DILATIONS):
                trio = []
                for part in range(3):
                    c0 = (part * ng + g) * gw
                    trio.append(_odd_proj(xb, w_in, c0, tabs_c, half_c, d, rope=part < 2))
                qkv.append(tuple(trio))
            o = _odd_attention(qkv, bounds, m)
            xf, xb = _matmul_ln([o], [p['mix_o_w_out'][j].astype(BF16)], xf, p['ln1_g'][i], p['ln1_b'][i])
        xf, xb = _ffn_ln(xb, xf, p['ffn_w_gate'][i].astype(BF16), p['ffn_w_up'][i].astype(BF16),
                         p['ffn_w_down'][i].astype(BF16), p['ln2_g'][i], p['ln2_b'][i])
    return xf


def kernel(x_prompt, x_sample, mix_e_w_in, a_sink, hy_conv_w, hy_conv_b, hy_w1, hy_b1, hy_f1, hy_w2, hy_b2,
           hy_f2, hy_w3, hy_b3, hy_bias, mix_e_w_out, mix_o_w_in, mix_o_w_out, ffn_w_gate, ffn_w_up,
           ffn_w_down, ln1_g, ln1_b, ln2_g, ln2_b):
    p = dict(mix_e_w_in=mix_e_w_in, a_sink=a_sink, hy_conv_w=hy_conv_w, hy_conv_b=hy_conv_b,
             hy_w1=hy_w1, hy_b1=hy_b1, hy_f1=hy_f1, hy_w2=hy_w2, hy_b2=hy_b2, hy_f2=hy_f2,
             hy_w3=hy_w3, hy_b3=hy_b3, hy_bias=hy_bias, mix_e_w_out=mix_e_w_out,
             mix_o_w_in=mix_o_w_in, mix_o_w_out=mix_o_w_out, ffn_w_gate=ffn_w_gate,
             ffn_w_up=ffn_w_up, ffn_w_down=ffn_w_down, ln1_g=ln1_g, ln1_b=ln1_b,
             ln2_g=ln2_g, ln2_b=ln2_b)
    dm = x_prompt.shape[-1]
    seqs, bounds, row = [], [0], 0
    for xs in (x_prompt, x_sample):
        nb, l = xs.shape[0], xs.shape[1]
        seqs.append((row, nb, l))
        for _ in range(nb):
            row += l
            bounds.append(row)
    x = jnp.concatenate([x_prompt.reshape(-1, dm), x_sample.reshape(-1, dm)], axis=0)
    y = _trunk(x, tuple(bounds), tuple(seqs), p)
    n_p = x_prompt.shape[0] * x_prompt.shape[1]
    return (y[:n_p].reshape(x_prompt.shape), y[n_p:].reshape(x_sample.shape))
```
